```python
import jax
import jax.numpy as jnp
from jax import lax
import numpy as np

D_MODEL = 2048
BATCH = 4
SEQ = 2048
DEPTH = 2
DEC_BATCH = 32
DEC_SEQ = 1
PAST_LEN = 16384
PAGE_SIZE = 128

HEAD_DIM = 64
ATTN_WIDTH = D_MODEL // 2
RWKV_WIDTH = D_MODEL - ATTN_WIDTH
N_Q_HEADS = ATTN_WIDTH // HEAD_DIM
N_KV_HEADS = N_Q_HEADS // 4
GQA_REP = N_Q_HEADS // N_KV_HEADS
KV_WIDTH = N_KV_HEADS * HEAD_DIM
N_RWKV_HEADS = RWKV_WIDTH // HEAD_DIM
WINDOW = 128
BLOCK = WINDOW
D_DECAY_LORA = max(32, int(round(1.8 * RWKV_WIDTH ** 0.5 / 32)) * 32)
D_ICLR_LORA = max(32, int(round(1.8 * RWKV_WIDTH ** 0.5 / 32)) * 32)
D_GATE_LORA = max(32, int(round(0.6 * RWKV_WIDTH ** 0.8 / 32)) * 32)
RWKV_PROJ = 3 * RWKV_WIDTH + D_DECAY_LORA + D_ICLR_LORA + D_GATE_LORA
QKV_COLS = ATTN_WIDTH + 2 * KV_WIDTH
IN_COLS = QKV_COLS + RWKV_PROJ
ATTN_SPLITS = (ATTN_WIDTH, ATTN_WIDTH + KV_WIDTH, QKV_COLS)
RWKV_SPLITS = (RWKV_WIDTH, 2 * RWKV_WIDTH, 3 * RWKV_WIDTH, 3 * RWKV_WIDTH + D_DECAY_LORA,
               3 * RWKV_WIDTH + D_DECAY_LORA + D_ICLR_LORA)
FFN_HIDDEN = ((8 * D_MODEL + 3 * 256 - 1) // (3 * 256)) * 256
RMS_EPS = 1e-5
GN_EPS = 64e-5
NEG_INF = -1e30

kernel_name = "hymba_swa_sink_alibi_rwkv7_adaln_decode_step"


def _alibi_slopes(n):
    return jnp.exp2(-8.0 * jnp.arange(1, n + 1, dtype=jnp.float32) / n)


def _rmsnorm(x, g):
    xf = x.astype(jnp.float32)
    y = xf * lax.rsqrt(jnp.mean(xf * xf, axis=-1, keepdims=True) + RMS_EPS)
    return (y * g.astype(jnp.float32)).astype(x.dtype)


def _modulate(h, shift, scale):
    return h * (1.0 + scale[:, None, :]) + shift[:, None, :]


def _banded_attention(q, k, v, q_pos, k_pos, sinks):
    f32 = jnp.float32
    logits = jnp.einsum("bnqgrd,bnkgd->bngrqk", q.astype(f32), k.astype(f32)) * (HEAD_DIM ** -0.5)
    dist = q_pos[:, :, None] - k_pos[:, None, :]
    valid = (dist >= 0) & (dist <= WINDOW) & (k_pos[:, None, :] >= 0)
    slopes = _alibi_slopes(N_Q_HEADS).reshape(N_KV_HEADS, GQA_REP)
    logits = logits - slopes[None, None, :, :, None, None] * dist.astype(f32)[None, :, None, None]
    logits = jnp.where(valid[None, :, None, None], logits, NEG_INF)
    sink = jnp.broadcast_to(sinks.astype(f32).reshape(N_KV_HEADS, GQA_REP)[None, None, :, :, None, None],
                            logits.shape[:-1] + (1,))
    probs = jax.nn.softmax(jnp.concatenate([logits, sink], axis=-1), axis=-1)[..., :-1]
    out = jnp.einsum("bngrqk,bnkgd->bnqgrd", probs, v.astype(f32))
    return out.astype(q.dtype)


def _wkv7_scan(S0, r, w, k, v, a, b):
    def step(S, inp):
        r_t, w_t, k_t, v_t, a_t, b_t = inp
        sa = jnp.einsum("bhij,bhj->bhi", S, a_t)
        S = S * w_t[:, :, None, :] + sa[..., None] * b_t[:, :, None, :] + v_t[..., None] * k_t[:, :, None, :]
        return S, jnp.einsum("bhij,bhj->bhi", S, r_t)
    xs = tuple(jnp.swapaxes(t, 0, 1) for t in (r, w, k, v, a, b))
    S, ys = lax.scan(step, S0, xs)
    return jnp.swapaxes(ys, 0, 1), S


def _rwkv7_mix(p_cur, p_prev_row, S0, mix, w0, decay_up, a0, iclr_up, gate_up, k_k, k_a, r_k, ln_w, ln_b):
    f32 = jnp.float32
    B, T = p_cur.shape[:2]
    pc = p_cur.astype(f32)
    pp = jnp.concatenate([p_prev_row.astype(f32)[:, None], pc[:, :-1]], axis=1)
    p = pc + (pp - pc) * mix.astype(f32)
    r, k, v, wd, ad, gd = jnp.split(p, RWKV_SPLITS, axis=-1)
    w = w0.astype(f32) + jnp.tanh(wd) @ decay_up.astype(f32)
    decay = jnp.exp(-jnp.exp(-jax.nn.softplus(-w) - 0.5))
    a = jax.nn.sigmoid(a0.astype(f32) + ad @ iclr_up.astype(f32))
    g = jax.nn.sigmoid(gd) @ gate_up.astype(f32)
    hs = lambda t: t.reshape(B, T, N_RWKV_HEADS, HEAD_DIM)
    kk = hs(k * k_k.astype(f32))
    kk = kk / jnp.maximum(jnp.sqrt(jnp.sum(kk * kk, axis=-1, keepdims=True)), 1e-12)
    k = k * (1.0 + (a - 1.0) * k_a.astype(f32))
    r_h, k_h, v_h = hs(r), hs(k), hs(v)
    y, S = _wkv7_scan(S0.astype(f32), r_h, hs(decay), k_h, v_h, -kk, kk * hs(a))
    mu = jnp.mean(y, axis=-1, keepdims=True)
    var = jnp.mean(jnp.square(y - mu), axis=-1, keepdims=True)
    y = ((y - mu) * lax.rsqrt(var + GN_EPS)).reshape(B, T, RWKV_WIDTH) * ln_w.astype(f32) + ln_b.astype(f32)
    bonus = jnp.sum(r_h * k_h * r_k.astype(f32), axis=-1, keepdims=True) * v_h
    y = (y + bonus.reshape(B, T, RWKV_WIDTH)) * g
    return y.astype(p_cur.dtype), S, p_cur[:, -1]


def setup_inputs(seed: int = 0) -> dict:
    key = jax.random.key(seed)
    ks = iter(jax.random.split(key, 40))
    f32 = jnp.float32
    nrm = lambda shape, s: jax.random.normal(next(ks), shape, f32) * s
    L = DEPTH
    n_buf = min(WINDOW, PAST_LEN)
    d = {}
    d["x_prompt"] = nrm((BATCH, SEQ, D_MODEL), 1.0)
    d["x_sample"] = nrm((DEC_BATCH, DEC_SEQ, D_MODEL), 1.0)
    d["cache_k"] = nrm((L, DEC_BATCH, n_buf, N_KV_HEADS, HEAD_DIM), 1.0)
    d["cache_v"] = nrm((L, DEC_BATCH, n_buf, N_KV_HEADS, HEAD_DIM), 1.0)
    d["state_wkv"] = nrm((L, DEC_BATCH, N_RWKV_HEADS, HEAD_DIM, HEAD_DIM), 0.5)
    d["state_shift"] = nrm((L, DEC_BATCH, RWKV_PROJ), 1.0)
    d["c_prompt"] = nrm((BATCH, D_MODEL), 1.0)
    d["c_sample"] = nrm((DEC_BATCH, D_MODEL), 1.0)
    d["w_ada"] = nrm((L, D_MODEL, 6 * D_MODEL), 0.5 * D_MODEL ** -0.5)
    d["b_ada"] = nrm((L, 6 * D_MODEL), 0.02)
    d["g_norm_mix"] = 1.0 + nrm((L, D_MODEL), 0.02)
    d["g_norm_ffn"] = 1.0 + nrm((L, D_MODEL), 0.02)
    d["w_in"] = nrm((L, D_MODEL, IN_COLS), D_MODEL ** -0.5)
    d["w_out"] = nrm((L, D_MODEL, D_MODEL), D_MODEL ** -0.5)
    d["attn_sinks"] = nrm((L, N_Q_HEADS), 1.0)
    d["mix_shift"] = jax.random.uniform(next(ks), (L, RWKV_PROJ), f32)
    d["decay_w0"] = jax.random.uniform(next(ks), (L, RWKV_WIDTH), f32, -5.0, 0.0)
    d["decay_up"] = nrm((L, D_DECAY_LORA, RWKV_WIDTH), 0.1)
    d["iclr_a0"] = nrm((L, RWKV_WIDTH), 0.1)
    d["iclr_up"] = nrm((L, D_ICLR_LORA, RWKV_WIDTH), 0.1)
    d["gate_up"] = nrm((L, D_GATE_LORA, RWKV_WIDTH), D_GATE_LORA ** -0.5)
    d["k_k"] = 0.85 + nrm((L, RWKV_WIDTH), 0.02)
    d["k_a"] = 1.0 + nrm((L, RWKV_WIDTH), 0.02)
    d["r_k"] = nrm((L, N_RWKV_HEADS, HEAD_DIM), 0.1)
    d["ln_x_w"] = 1.0 + nrm((L, RWKV_WIDTH), 0.02)
    d["ln_x_b"] = nrm((L, RWKV_WIDTH), 0.02)
    d["w_ffn_in"] = nrm((L, D_MODEL, 2 * FFN_HIDDEN), D_MODEL ** -0.5)
    d["w_ffn_out"] = nrm((L, FFN_HIDDEN, D_MODEL), FFN_HIDDEN ** -0.5)
    d["g_norm_final"] = 1.0 + nrm((D_MODEL,), 0.02)
    return d


def reference(x_prompt, x_sample, cache_k, cache_v, state_wkv, state_shift, c_prompt, c_sample,
              w_ada, b_ada, g_norm_mix, g_norm_ffn, w_in, w_out, attn_sinks, mix_shift, decay_w0,
              decay_up, iclr_a0, iclr_up, gate_up, k_k, k_a, r_k, ln_x_w, ln_x_b, w_ffn_in, w_ffn_out,
              g_norm_final):

    def layer(l, x, c, buf_k, buf_v, S0, shift0):
        B, T = x.shape[:2]
        mod = jax.nn.silu(c) @ w_ada[l] + b_ada[l]
        sh1, sc1, gt1, sh2, sc2, gt2 = jnp.split(mod, 6, axis=-1)
        h = _modulate(_rmsnorm(x, g_norm_mix[l]), sh1, sc1)
        proj = h @ w_in[l]
        q, k, v, p_rwkv = jnp.split(proj, ATTN_SPLITS, axis=-1)
        q = q.reshape(B, T, N_KV_HEADS, GQA_REP, HEAD_DIM)
        k = k.reshape(B, T, N_KV_HEADS, HEAD_DIM)
        v = v.reshape(B, T, N_KV_HEADS, HEAD_DIM)
        if buf_k is None:
            nb = T // BLOCK
            kb = k.reshape(B, nb, BLOCK, N_KV_HEADS, HEAD_DIM)
            vb = v.reshape(B, nb, BLOCK, N_KV_HEADS, HEAD_DIM)
            band = lambda t: jnp.concatenate(
                [jnp.concatenate([jnp.zeros_like(t[:, :1]), t[:, :-1]], axis=1), t], axis=2)
            q_pos = jnp.arange(T).reshape(nb, BLOCK)
            k_pos = (jnp.arange(nb)[:, None] - 1) * BLOCK + jnp.arange(2 * BLOCK)[None, :]
            att = _banded_attention(q.reshape(B, nb, BLOCK, N_KV_HEADS, GQA_REP, HEAD_DIM),
                                    band(kb), band(vb), q_pos, k_pos, attn_sinks[l])
            new_k, new_v = k[:, T - WINDOW:], v[:, T - WINDOW:]
        else:
            n_buf = buf_k.shape[1]
            k_all = jnp.concatenate([buf_k.astype(k.dtype), k], axis=1)
            v_all = jnp.concatenate([buf_v.astype(v.dtype), v], axis=1)
            q_pos = (PAST_LEN + jnp.arange(T))[None]
            k_pos = (PAST_LEN - n_buf + jnp.arange(n_buf + T))[None]
            att = _banded_attention(q[:, None], k_all[:, None], v_all[:, None], q_pos, k_pos, attn_sinks[l])
            new_k, new_v = k_all[:, T:], v_all[:, T:]
        att = att.reshape(B, T, ATTN_WIDTH)
        rw, S, last_row = _rwkv7_mix(p_rwkv, shift0, S0, mix_shift[l], decay_w0[l], decay_up[l], iclr_a0[l],
                                     iclr_up[l], gate_up[l], k_k[l], k_a[l], r_k[l], ln_x_w[l], ln_x_b[l])
        x = x + gt1[:, None, :] * (jnp.concatenate([att, rw], axis=-1) @ w_out[l])
        h2 = _modulate(_rmsnorm(x, g_norm_ffn[l]), sh2, sc2)
        gate, up = jnp.split(h2 @ w_ffn_in[l], 2, axis=-1)
        x = x + gt2[:, None, :] * ((jax.nn.silu(gate) * up) @ w_ffn_out[l])
        return x, new_k, new_v, S, last_row

    xp = x_prompt
    bp = x_prompt.shape[0]
    pk, pv, pS, pR = [], [], [], []
    for l in range(DEPTH):
        xp, nk, nv, S, row = layer(l, xp, c_prompt, None, None,
                                   jnp.zeros((bp, N_RWKV_HEADS, HEAD_DIM, HEAD_DIM), jnp.float32),
                                   jnp.zeros((bp, RWKV_PROJ), x_prompt.dtype))
        pk.append(nk); pv.append(nv); pS.append(S); pR.append(row)
    y_prompt = _rmsnorm(xp, g_norm_final)

    xs = x_sample
    sk, sv, sS, sR = [], [], [], []
    for l in range(DEPTH):
        xs, nk, nv, S, row = layer(l, xs, c_sample, cache_k[l], cache_v[l], state_wkv[l], state_shift[l])
        sk.append(nk); sv.append(nv); sS.append(S); sR.append(row)
    y_sample = _rmsnorm(xs, g_norm_final)

    return (y_prompt, y_sample,
            jnp.stack(pk, 0), jnp.stack(pv, 0), jnp.stack(pS, 0), jnp.stack(pR, 0),
            jnp.stack(sk, 0), jnp.stack(sv, 0), jnp.stack(sS, 0), jnp.stack(sR, 0))
```

```python
import functools
import math

import numpy as np
import jax
import jax.numpy as jnp
from jax import lax
from jax.experimental import pallas as pl
from jax.experimental.pallas import tpu as pltpu

F32 = jnp.float32
BF16 = jnp.bfloat16

D_MODEL = 2048
HEAD_DIM = 64
ATTN_WIDTH = D_MODEL // 2
RWKV_WIDTH = D_MODEL - ATTN_WIDTH
N_Q_HEADS = ATTN_WIDTH // HEAD_DIM
N_KV_HEADS = N_Q_HEADS // 4
GQA_REP = N_Q_HEADS // N_KV_HEADS
KV_WIDTH = N_KV_HEADS * HEAD_DIM
N_RWKV_HEADS = RWKV_WIDTH // HEAD_DIM
WINDOW = 128
D_DECAY_LORA = 64
D_ICLR_LORA = 64
D_GATE_LORA = 160
RWKV_PROJ = 3 * RWKV_WIDTH + D_DECAY_LORA + D_ICLR_LORA + D_GATE_LORA
QKV_COLS = ATTN_WIDTH + 2 * KV_WIDTH
FFN_HIDDEN = 5632
RMS_EPS = 1e-5
GN_EPS = 64e-5
NEG_INF = -1e30

LANES = 128
MXU_DIM = 256
VMEM_LIMIT_BYTES = 56 * 1024 * 1024

LORA_OFF = 3 * RWKV_WIDTH
GATE_OFF = LORA_OFF + D_DECAY_LORA + D_ICLR_LORA
GATE_PAD = 2 * LANES
RWKV_PROJ_PAD = 3584
HEADS_PER_GROUP = MXU_DIM // HEAD_DIM
N_GROUPS = N_RWKV_HEADS // HEADS_PER_GROUP

ALIBI_SLOPES = tuple(float(s) for s in np.exp2(-8.0 * np.arange(1, N_Q_HEADS + 1, dtype=np.float32) / N_Q_HEADS))


def _params(n_axes):
    return pltpu.CompilerParams(dimension_semantics=("arbitrary",) * n_axes,
                                vmem_limit_bytes=VMEM_LIMIT_BYTES)


def _dot(a, b):
    return jnp.dot(a, b, preferred_element_type=F32)


def _dot_nt(a, b):
    return lax.dot_general(a, b, (((1,), (1,)), ((), ())), preferred_element_type=F32)


def _dot_tn(a, b):
    return lax.dot_general(a, b, (((0,), (0,)), ((), ())), preferred_element_type=F32)


def _split(x, parts):
    out = []
    for _ in range(parts):
        p = x.astype(BF16)
        out.append(p)
        x = x - p.astype(F32)
    return out


def _rmsnorm_mod(x, g, shift, scale):
    ms = jnp.mean(x * x, axis=-1, keepdims=True)
    y = x * lax.rsqrt(ms + RMS_EPS) * g
    return y * (1.0 + scale) + shift


def _ada_kernel(c_ref, w_ref, b_ref, o_ref):
    c = c_ref[...]
    s = (c * jax.nn.sigmoid(c)).astype(BF16)
    o_ref[...] = _dot(s, w_ref[...].astype(BF16)) + b_ref[...]


def _ada(c_all, w_ada, b_ada):
    n_layers, d, n = w_ada.shape
    rows = c_all.shape[0]
    tn = 1024
    return pl.pallas_call(
        _ada_kernel,
        grid=(n_layers, n // tn),
        in_specs=[pl.BlockSpec((rows, d), lambda l, j: (0, 0)),
                  pl.BlockSpec((None, d, tn), lambda l, j: (l, 0, j)),
                  pl.BlockSpec((None, 1, tn), lambda l, j: (l, 0, j))],
        out_specs=pl.BlockSpec((None, rows, tn), lambda l, j: (l, 0, j)),
        out_shape=jax.ShapeDtypeStruct((n_layers, rows, n), F32),
        compiler_params=_params(2),
        name="ada_mod",
    )(c_all, w_ada, b_ada.reshape(n_layers, 1, n))


def _norm_proj_kernel(x_ref, sh_ref, sc_ref, g_ref, w_ref, o_ref, h_scr):
    @pl.when(pl.program_id(2) == 0)
    def _():
        h_scr[...] = _rmsnorm_mod(x_ref[0], g_ref[...], sh_ref[0], sc_ref[0]).astype(BF16)

    o_ref[0] = _dot(h_scr[...], w_ref[...])


def _norm_proj(x, shift, scale, g, w, layer, *, tm, tn):
    b, t, d = x.shape
    n = w.shape[2]
    r = shift.shape[1]
    return pl.pallas_call(
        _norm_proj_kernel,
        grid=(b, t // tm, n // tn),
        in_specs=[pl.BlockSpec((1, tm, d), lambda i, m, j: (i, m, 0)),
                  pl.BlockSpec((1, r, d), lambda i, m, j: (i, 0, 0)),
                  pl.BlockSpec((1, r, d), lambda i, m, j: (i, 0, 0)),
                  pl.BlockSpec((None, 1, d), lambda i, m, j: (layer, 0, 0)),
                  pl.BlockSpec((None, d, tn), lambda i, m, j: (layer, 0, j))],
        out_specs=pl.BlockSpec((1, tm, tn), lambda i, m, j: (i, m, j)),
        out_shape=jax.ShapeDtypeStruct((b, t, n), F32),
        scratch_shapes=[pltpu.VMEM((tm, d), BF16)],
        compiler_params=_params(3),
        name="norm_proj",
    )(x, shift, scale, g, w)


def _attn_prompt_kernel(sink_ref, q_ref, kc_ref, vc_ref, kp_ref, vp_ref, o_ref, *, layer):
    blk = pl.program_id(1)
    q = q_ref[0] * (HEAD_DIM ** -0.5)
    kc, vc, kp, vp = kc_ref[0], vc_ref[0], kp_ref[0], vp_ref[0]
    t = lax.broadcasted_iota(jnp.int32, (WINDOW, 2 * WINDOW), 0)
    j = lax.broadcasted_iota(jnp.int32, (WINDOW, 2 * WINDOW), 1)
    dist = t + WINDOW - j
    first_key = jnp.where(blk > 0, 0, WINDOW)
    valid = (dist >= 0) & (dist <= WINDOW) & (j >= first_key)
    distf = dist.astype(F32)
    outs = []
    for g in range(N_KV_HEADS):
        ksl = slice(g * HEAD_DIM, (g + 1) * HEAD_DIM)
        k_band = jnp.concatenate([kp[:, ksl], kc[:, ksl]], axis=0).astype(BF16)
        v_band = jnp.concatenate([vp[:, ksl], vc[:, ksl]], axis=0).astype(BF16)
        q_rep = jnp.concatenate(
            [q[:, (g * GQA_REP + r) * HEAD_DIM:(g * GQA_REP + r + 1) * HEAD_DIM] for r in range(GQA_REP)],
            axis=0).astype(BF16)
        logits = _dot_nt(q_rep, k_band)
        for r in range(GQA_REP):
            h = g * GQA_REP + r
            lg = logits[r * WINDOW:(r + 1) * WINDOW] - ALIBI_SLOPES[h] * distf
            lg = jnp.where(valid, lg, NEG_INF)
            sink = sink_ref[layer * N_Q_HEADS + h]
            m = jnp.maximum(jnp.max(lg, axis=-1, keepdims=True), sink)
            p = jnp.exp(lg - m)
            den = jnp.sum(p, axis=-1, keepdims=True) + jnp.exp(sink - m)
            outs.append(_dot((p / den).astype(BF16), v_band))
    o_ref[0] = jnp.concatenate(outs, axis=1).astype(BF16)


def _attn_prompt(qkv, sinks_flat, layer):
    b, t, _ = qkv.shape
    kcol = ATTN_WIDTH // KV_WIDTH
    prev = lambda i, m: (i, jnp.maximum(m - 1, 0), kcol)
    prev_v = lambda i, m: (i, jnp.maximum(m - 1, 0), kcol + 1)
    return pl.pallas_call(
        functools.partial(_attn_prompt_kernel, layer=layer),
        grid=(b, t // WINDOW),
        in_specs=[pl.BlockSpec(memory_space=pltpu.SMEM),
                  pl.BlockSpec((1, WINDOW, ATTN_WIDTH), lambda i, m: (i, m, 0)),
                  pl.BlockSpec((1, WINDOW, KV_WIDTH), lambda i, m: (i, m, kcol)),
                  pl.BlockSpec((1, WINDOW, KV_WIDTH), lambda i, m: (i, m, kcol + 1)),
                  pl.BlockSpec((1, WINDOW, KV_WIDTH), prev),
                  pl.BlockSpec((1, WINDOW, KV_WIDTH), prev_v)],
        out_specs=pl.BlockSpec((1, WINDOW, ATTN_WIDTH), lambda i, m: (i, m, 0)),
        out_shape=jax.ShapeDtypeStruct((b, t, ATTN_WIDTH), BF16),
        compiler_params=_params(2),
        name="attn_prompt",
    )(sinks_flat, qkv, qkv, qkv, qkv, qkv)


def _attn_sample_kernel(row_ref, kc_ref, vc_ref, sink_ref, slope_ref, spread_ref, gather_ref,
                        o_ref, nk_ref, nv_ref):
    row = row_ref[0]
    q = row[:, :ATTN_WIDTH] * (HEAD_DIM ** -0.5)
    k_new = row[:, ATTN_WIDTH:ATTN_WIDTH + KV_WIDTH]
    v_new = row[:, ATTN_WIDTH + KV_WIDTH:]
    kc, vc = kc_ref[0], vc_ref[0]
    head_row = lax.broadcasted_iota(jnp.int32, (N_Q_HEADS, ATTN_WIDTH), 0)
    head_lane = lax.broadcasted_iota(jnp.int32, (N_Q_HEADS, ATTN_WIDTH), 1) // HEAD_DIM
    own = head_row == head_lane
    q_rows = jnp.where(own, jnp.broadcast_to(q, (N_Q_HEADS, ATTN_WIDTH)), 0.0).astype(BF16)
    q_grp = _dot(q_rows, spread_ref[...])
    k_new_b = k_new.astype(BF16).astype(F32)
    lg = _dot_nt(q_grp.astype(BF16), kc.astype(BF16))
    lg_new = jnp.sum(q_grp * k_new_b, axis=-1, keepdims=True)
    dist = (WINDOW - lax.broadcasted_iota(jnp.int32, (N_Q_HEADS, WINDOW), 1)).astype(F32)
    lg = lg - slope_ref[...] * dist
    sink = sink_ref[...]
    m = jnp.maximum(jnp.maximum(jnp.max(lg, axis=-1, keepdims=True), lg_new), sink)
    p = jnp.exp(lg - m)
    p_new = jnp.exp(lg_new - m)
    den = jnp.sum(p, axis=-1, keepdims=True) + p_new + jnp.exp(sink - m)
    out = _dot((p / den).astype(BF16), vc.astype(BF16))
    out = out + (p_new / den).astype(BF16).astype(F32) * v_new.astype(BF16).astype(F32)
    grp_row = lax.broadcasted_iota(jnp.int32, (N_Q_HEADS, KV_WIDTH), 0) // GQA_REP
    grp_lane = lax.broadcasted_iota(jnp.int32, (N_Q_HEADS, KV_WIDTH), 1) // HEAD_DIM
    out = jnp.where(grp_row == grp_lane, out, 0.0).astype(BF16)
    full = _dot(out, gather_ref[...])
    o_ref[0] = jnp.sum(jnp.where(own, full, 0.0), axis=0, keepdims=True).astype(BF16)
    last = lax.broadcasted_iota(jnp.int32, (WINDOW, KV_WIDTH), 0) == WINDOW - 1
    nk_ref[0] = jnp.where(last, k_new, pltpu.roll(kc, WINDOW - 1, axis=0))
    nv_ref[0] = jnp.where(last, v_new, pltpu.roll(vc, WINDOW - 1, axis=0))


def _attn_sample(qkv_rows, cache_k, cache_v, sinks_col, layer):
    s = qkv_rows.shape[0]
    spread = np.zeros((ATTN_WIDTH, KV_WIDTH), np.float32)
    for h in range(N_Q_HEADS):
        for d in range(HEAD_DIM):
            spread[h * HEAD_DIM + d, (h // GQA_REP) * HEAD_DIM + d] = 1.0
    slopes = np.asarray(ALIBI_SLOPES, np.float32).reshape(N_Q_HEADS, 1)
    win = (1, WINDOW, KV_WIDTH)
    return pl.pallas_call(
        _attn_sample_kernel,
        grid=(s,),
        in_specs=[pl.BlockSpec((1, 1, QKV_COLS), lambda i: (i, 0, 0)),
                  pl.BlockSpec((None,) + win, lambda i: (layer, i, 0, 0)),
                  pl.BlockSpec((None,) + win, lambda i: (layer, i, 0, 0)),
                  pl.BlockSpec((None, N_Q_HEADS, 1), lambda i: (layer, 0, 0)),
                  pl.BlockSpec((N_Q_HEADS, 1), lambda i: (0, 0)),
                  pl.BlockSpec((ATTN_WIDTH, KV_WIDTH), lambda i: (0, 0)),
                  pl.BlockSpec((KV_WIDTH, ATTN_WIDTH), lambda i: (0, 0))],
        out_specs=[pl.BlockSpec((1, 1, ATTN_WIDTH), lambda i: (i, 0, 0)),
                   pl.BlockSpec(win, lambda i: (i, 0, 0)),
                   pl.BlockSpec(win, lambda i: (i, 0, 0))],
        out_shape=[jax.ShapeDtypeStruct((s, 1, ATTN_WIDTH), BF16),
                   jax.ShapeDtypeStruct((s, WINDOW, KV_WIDTH), F32),
                   jax.ShapeDtypeStruct((s, WINDOW, KV_WIDTH), F32)],
        compiler_params=_params(1),
        name="attn_sample",
    )(qkv_rows, cache_k, cache_v, sinks_col, jnp.asarray(slopes),
      jnp.asarray(spread, BF16), jnp.asarray(spread.T, BF16))


def _head_blocks(x, rows):
    lane_head = lax.broadcasted_iota(jnp.int32, (rows, MXU_DIM), 1) // HEAD_DIM
    return jnp.concatenate([jnp.where(lane_head == h, x, 0.0) for h in range(HEADS_PER_GROUP)], axis=0)


def _segment_sum(x, ones_bd):
    cols = []
    for grp in range(N_GROUPS):
        xs = x[:, grp * MXU_DIM:(grp + 1) * MXU_DIM]
        cols.append(sum(_dot(part, ones_bd) for part in _split(xs, 2)))
    return jnp.concatenate(cols, axis=1)


def _wkv_kernel(p_ref, shift0_ref, s0_ref, mix_ref, w0_ref, dup_ref, a0_ref, iup_ref, gup_ref,
                kk_ref, ka_ref, rk_ref, lnw_ref, lnb_ref, rw_ref, sout_ref, s_scr, prev_scr,
                *, chunk, t_valid, t_total):
    c = pl.program_id(1)
    n_chunks = pl.num_programs(1)
    bd_rows = HEADS_PER_GROUP * chunk
    row_head = lax.broadcasted_iota(jnp.int32, (MXU_DIM, MXU_DIM), 0) // HEAD_DIM
    lane_head = lax.broadcasted_iota(jnp.int32, (MXU_DIM, MXU_DIM), 1) // HEAD_DIM
    same_head = row_head == lane_head
    ones_bd = jnp.where(same_head, 1.0, 0.0).astype(BF16)
    tile_sel = jnp.where(lax.broadcasted_iota(jnp.int32, (HEAD_DIM, MXU_DIM), 0)
                         == lax.broadcasted_iota(jnp.int32, (HEAD_DIM, MXU_DIM), 1) % HEAD_DIM, 1.0, 0.0).astype(BF16)
    fold_sel = jnp.where(lax.broadcasted_iota(jnp.int32, (MXU_DIM, HEAD_DIM), 0) % HEAD_DIM
                         == lax.broadcasted_iota(jnp.int32, (MXU_DIM, HEAD_DIM), 1), 1.0, 0.0).astype(BF16)

    @pl.when(c == 0)
    def _():
        prev_scr[...] = shift0_ref[0]
        for grp in range(N_GROUPS):
            rows = s0_ref[0, grp * MXU_DIM:(grp + 1) * MXU_DIM, :]
            tiled = sum(_dot(part, tile_sel) for part in _split(rows, 3))
            s_scr[grp] = jnp.where(same_head, tiled, 0.0)

    pc = p_ref[0]
    row = lax.broadcasted_iota(jnp.int32, (chunk, 1), 0)
    pp = jnp.where(row == 0, prev_scr[...], pltpu.roll(pc, 1, axis=0))
    prev_scr[...] = pc[chunk - 1:chunk, :]
    p = pc + (pp - pc) * mix_ref[...]

    r = p[:, 0:RWKV_WIDTH]
    k = p[:, RWKV_WIDTH:2 * RWKV_WIDTH]
    v = p[:, 2 * RWKV_WIDTH:3 * RWKV_WIDTH]
    lora_in = p[:, LORA_OFF:LORA_OFF + LANES]
    gate_in = p[:, GATE_OFF:GATE_OFF + GATE_PAD]
    w = w0_ref[...] + _dot(jnp.tanh(lora_in).astype(BF16), dup_ref[...])
    logw = -math.exp(-0.5) * jax.nn.sigmoid(w)
    a = jax.nn.sigmoid(a0_ref[...] + _dot(lora_in.astype(BF16), iup_ref[...]))
    gate = _dot(jax.nn.sigmoid(gate_in).astype(BF16), gup_ref[...])
    kk = k * kk_ref[...]
    kk = kk / jnp.maximum(jnp.sqrt(_segment_sum(kk * kk, ones_bd)), 1e-12)
    k = k * (1.0 + (a - 1.0) * ka_ref[...])
    if t_valid < t_total:
        live = (c * chunk + row) < t_valid
        logw = jnp.where(live, logw, 0.0)
        kk = jnp.where(live, kk, 0.0)
        k = jnp.where(live, k, 0.0)
        v = jnp.where(live, v, 0.0)

    tri = lax.broadcasted_iota(jnp.int32, (chunk, chunk), 0) >= lax.broadcasted_iota(jnp.int32, (chunk, chunk), 1)
    tri = jnp.where(tri, 1.0, 0.0).astype(BF16)
    cum = sum(_dot(tri, part) for part in _split(logw, 3))
    e_pos = jnp.exp(cum)
    e_neg = jnp.exp(-cum)
    a_t = -kk * jnp.exp(cum - logw)
    b_t = kk * a * e_neg
    k_t = k * e_neg
    r_t = r * e_pos
    total = e_pos[chunk - 1:chunk, :]
    b_end = b_t * total
    k_end = k_t * total

    t_cat = lax.broadcasted_iota(jnp.int32, (chunk, bd_rows), 0)
    s_cat = lax.broadcasted_iota(jnp.int32, (chunk, bd_rows), 1) % chunk
    strict_cat = s_cat < t_cat
    lower_cat = s_cat <= t_cat
    strict_bd = (lax.broadcasted_iota(jnp.int32, (bd_rows, bd_rows), 1) % chunk
                 < lax.broadcasted_iota(jnp.int32, (bd_rows, bd_rows), 0) % chunk)

    ys = []
    for grp in range(N_GROUPS):
        sl = slice(grp * MXU_DIM, (grp + 1) * MXU_DIM)
        a_g, b_g, k_g, r_g, v_g = a_t[:, sl], b_t[:, sl], k_t[:, sl], r_t[:, sl], v[:, sl]
        a_b, r_b = a_g.astype(BF16), r_g.astype(BF16)
        xb = _head_blocks(b_g, chunk).astype(BF16)
        xk = _head_blocks(k_g, chunk).astype(BF16)
        xa = _head_blocks(a_g, chunk).astype(BF16)
        v_bd = _head_blocks(v_g, chunk).astype(BF16)
        s_b = s_scr[grp].astype(BF16)
        n_bd = jnp.where(strict_bd, _dot_nt(xa, xb), 0.0)
        e_bd, m_bd = n_bd, n_bd
        for _ in range(int(math.log2(chunk)) - 1):
            m_b = m_bd.astype(BF16)
            m_bd = _dot(m_b, m_b)
            e_bd = e_bd + m_bd + _dot(e_bd.astype(BF16), m_bd.astype(BF16))
        e_cat = sum(e_bd[h * chunk:(h + 1) * chunk] for h in range(HEADS_PER_GROUP))
        a_ak = jnp.where(strict_cat, _dot_nt(a_b, xk), 0.0)
        a_rb = jnp.where(lower_cat, _dot_nt(r_b, xb), 0.0)
        a_rk = jnp.where(lower_cat, _dot_nt(r_b, xk), 0.0)
        rhs = _dot_nt(a_b, s_b) + _dot(a_ak.astype(BF16), v_bd)
        p_g = rhs + _dot(e_cat.astype(BF16), _head_blocks(rhs, chunk).astype(BF16))
        p_bd = _head_blocks(p_g, chunk).astype(BF16)
        ys.append(_dot_nt(r_b, s_b) + _dot(a_rb.astype(BF16), p_bd) + _dot(a_rk.astype(BF16), v_bd))
        upd = _dot_tn(jnp.concatenate([p_g, v_g], axis=0).astype(BF16),
                      jnp.concatenate([b_end[:, sl], k_end[:, sl]], axis=0).astype(BF16))
        s_scr[grp] = s_scr[grp] * total[:, sl] + jnp.where(same_head, upd, 0.0)

    y = jnp.concatenate(ys, axis=1)
    inv_n = 1.0 / HEAD_DIM
    mu = _segment_sum(y, ones_bd) * inv_n
    dev = y - mu
    var = _segment_sum(dev * dev, ones_bd) * inv_n
    y = dev * lax.rsqrt(var + GN_EPS) * lnw_ref[...] + lnb_ref[...]
    bonus = _segment_sum(r * k * rk_ref[...], ones_bd) * v
    rw_ref[0] = ((y + bonus) * gate).astype(BF16)

    @pl.when(c == n_chunks - 1)
    def _():
        for grp in range(N_GROUPS):
            folded = sum(_dot(part, fold_sel) for part in _split(s_scr[grp], 3))
            sout_ref[0, grp * MXU_DIM:(grp + 1) * MXU_DIM, :] = folded


def _wkv(prw, shift0, s0, wp, layer, *, chunk, t_valid):
    b, t, _ = prw.shape
    lay = lambda *blk: pl.BlockSpec((None,) + blk, lambda i, c: (layer,) + (0,) * len(blk))
    vec = lay(1, RWKV_WIDTH)
    return pl.pallas_call(
        functools.partial(_wkv_kernel, chunk=chunk, t_valid=t_valid, t_total=t),
        grid=(b, t // chunk),
        in_specs=[pl.BlockSpec((1, chunk, RWKV_PROJ_PAD), lambda i, c: (i, c, 0)),
                  pl.BlockSpec((1, 1, RWKV_PROJ_PAD), lambda i, c: (i, 0, 0)),
                  pl.BlockSpec((1, RWKV_WIDTH, HEAD_DIM), lambda i, c: (i, 0, 0)),
                  lay(1, RWKV_PROJ_PAD), vec, lay(LANES, RWKV_WIDTH), vec, lay(LANES, RWKV_WIDTH),
                  lay(GATE_PAD, RWKV_WIDTH), vec, vec, vec, vec, vec],
        out_specs=[pl.BlockSpec((1, chunk, RWKV_WIDTH), lambda i, c: (i, c, 0)),
                   pl.BlockSpec((1, RWKV_WIDTH, HEAD_DIM), lambda i, c: (i, 0, 0))],
        out_shape=[jax.ShapeDtypeStruct((b, t, RWKV_WIDTH), BF16),
                   jax.ShapeDtypeStruct((b, RWKV_WIDTH, HEAD_DIM), F32)],
        scratch_shapes=[pltpu.VMEM((N_GROUPS, MXU_DIM, MXU_DIM), F32),
                        pltpu.VMEM((1, RWKV_PROJ_PAD), F32)],
        compiler_params=_params(2),
        name="wkv7",
    )(prw, shift0, s0, wp["mix"], wp["w0"], wp["decay_up"], wp["a0"], wp["iclr_up"], wp["gate_up"],
      wp["k_k"], wp["k_a"], wp["r_k"], wp["ln_w"], wp["ln_b"])


def _out_proj_kernel(att_ref, rw_ref, x_ref, gt_ref, sh_ref, sc_ref, g_ref, w_ref, x1_ref, h_ref):
    y = _dot(att_ref[0], w_ref[0:ATTN_WIDTH, :]) + _dot(rw_ref[0], w_ref[ATTN_WIDTH:, :])
    x1 = x_ref[0] + gt_ref[0] * y
    x1_ref[0] = x1
    h_ref[0] = _rmsnorm_mod(x1, g_ref[...], sh_ref[0], sc_ref[0]).astype(BF16)


def _out_proj(att, rw, x, gate, shift, scale, g, w, layer, *, tm):
    b, t, d = x.shape
    r = gate.shape[1]
    row = lambda width: pl.BlockSpec((1, tm, width), lambda i, m: (i, m, 0))
    mod = pl.BlockSpec((1, r, d), lambda i, m: (i, 0, 0))
    return pl.pallas_call(
        _out_proj_kernel,
        grid=(b, t // tm),
        in_specs=[row(ATTN_WIDTH), row(RWKV_WIDTH), row(d), mod, mod, mod,
                  pl.BlockSpec((None, 1, d), lambda i, m: (layer, 0, 0)),
                  pl.BlockSpec((None, d, d), lambda i, m: (layer, 0, 0))],
        out_specs=[row(d), row(d)],
        out_shape=[jax.ShapeDtypeStruct((b, t, d), F32), jax.ShapeDtypeStruct((b, t, d), BF16)],
        compiler_params=_params(2),
        name="out_proj",
    )(att, rw, x, gate, shift, scale, g, w)


def _ffn_kernel(h_ref, x_ref, gt_ref, wg_ref, wu_ref, wo_ref, gf_ref, o_ref, acc_ref, *, final_norm):
    j = pl.program_id(2)
    h = h_ref[0]
    gate = _dot(h, wg_ref[...])
    up = _dot(h, wu_ref[...])
    act = (gate * jax.nn.sigmoid(gate) * up).astype(BF16)
    part = _dot(act, wo_ref[...])

    @pl.when(j == 0)
    def _():
        acc_ref[...] = part

    @pl.when(j > 0)
    def _():
        acc_ref[...] += part

    @pl.when(j == pl.num_programs(2) - 1)
    def _():
        x2 = x_ref[0] + gt_ref[0] * acc_ref[...]
        if final_norm:
            ms = jnp.mean(x2 * x2, axis=-1, keepdims=True)
            x2 = x2 * lax.rsqrt(ms + RMS_EPS) * gf_ref[...]
        o_ref[0] = x2


def _ffn(h, x, gate, w_in, w_out, g_final, layer, *, tm, th, final_norm):
    b, t, d = x.shape
    r = gate.shape[1]
    n_h = FFN_HIDDEN // th
    return pl.pallas_call(
        functools.partial(_ffn_kernel, final_norm=final_norm),
        grid=(b, t // tm, n_h),
        in_specs=[pl.BlockSpec((1, tm, d), lambda i, m, j: (i, m, 0)),
                  pl.BlockSpec((1, tm, d), lambda i, m, j: (i, m, 0)),
                  pl.BlockSpec((1, r, d), lambda i, m, j: (i, 0, 0)),
                  pl.BlockSpec((None, d, th), lambda i, m, j: (layer, 0, j)),
                  pl.BlockSpec((None, d, th), lambda i, m, j: (layer, 0, j + n_h)),
                  pl.BlockSpec((None, th, d), lambda i, m, j: (layer, j, 0)),
                  pl.BlockSpec((1, d), lambda i, m, j: (0, 0))],
        out_specs=pl.BlockSpec((1, tm, d), lambda i, m, j: (i, m, 0)),
        out_shape=jax.ShapeDtypeStruct((b, t, d), F32),
        scratch_shapes=[pltpu.VMEM((tm, d), F32)],
        compiler_params=_params(3),
        name="ffn",
    )(h, x, gate, w_in, w_in, w_out, g_final)


def _pad_cols(x, width):
    return jnp.pad(x, [(0, 0)] * (x.ndim - 1) + [(0, width - x.shape[-1])])


def _pad_rows(x, rows, offset=0):
    return jnp.pad(x, [(0, 0), (offset, rows - offset - x.shape[1]), (0, 0)])


def kernel(x_prompt, x_sample, cache_k, cache_v, state_wkv, state_shift, c_prompt, c_sample, w_ada, b_ada, g_norm_mix, g_norm_ffn, w_in, w_out, attn_sinks, mix_shift, decay_w0, decay_up, iclr_a0, iclr_up, gate_up, k_k, k_a, r_k, ln_x_w, ln_x_b, w_ffn_in, w_ffn_out, g_norm_final):
    n_layers = w_in.shape[0]
    bp, tp, d = x_prompt.shape
    bs = x_sample.shape[0]
    sample_chunk = 32

    w_qkv = w_in[:, :, :QKV_COLS].astype(BF16)
    w_rwkv = _pad_cols(w_in[:, :, QKV_COLS:], RWKV_PROJ_PAD).astype(BF16)
    w_out_b = w_out.astype(BF16)
    w_ffn_in_b = w_ffn_in.astype(BF16)
    w_ffn_out_b = w_ffn_out.astype(BF16)
    vec = lambda x: x.reshape(n_layers, 1, -1)
    wp = dict(
        mix=vec(_pad_cols(mix_shift, RWKV_PROJ_PAD)), w0=vec(decay_w0), a0=vec(iclr_a0),
        decay_up=_pad_rows(decay_up, LANES).astype(BF16),
        iclr_up=_pad_rows(iclr_up, LANES, D_DECAY_LORA).astype(BF16),
        gate_up=_pad_rows(gate_up, GATE_PAD).astype(BF16),
        k_k=vec(k_k), k_a=vec(k_a), r_k=vec(r_k), ln_w=vec(ln_x_w), ln_b=vec(ln_x_b))
    g_mix, g_ffn = vec(g_norm_mix), vec(g_norm_ffn)
    g_final = g_norm_final.reshape(1, d)
    sinks_flat = attn_sinks.reshape(-1)
    sinks_col = attn_sinks.reshape(n_layers, N_Q_HEADS, 1)
    cache_k2 = cache_k.reshape(n_layers, bs, WINDOW, KV_WIDTH)
    cache_v2 = cache_v.reshape(n_layers, bs, WINDOW, KV_WIDTH)

    mod = _ada(jnp.concatenate([c_prompt, c_sample], axis=0), w_ada, b_ada)

    def mods(layer):
        parts = [mod[layer, :, i * d:(i + 1) * d] for i in range(6)]
        return ([p[:bp].reshape(bp, 1, d) for p in parts], [p[bp:].reshape(1, bs, d) for p in parts])

    xp = x_prompt
    xs = x_sample.reshape(1, bs, d)
    outs_p = [[] for _ in range(4)]
    outs_s = [[] for _ in range(4)]
    for layer in range(n_layers):
        last = layer == n_layers - 1
        (sh1, sc1, gt1, sh2, sc2, gt2), (ssh1, ssc1, sgt1, ssh2, ssc2, sgt2) = mods(layer)

        qkv = _norm_proj(xp, sh1, sc1, g_mix, w_qkv, layer, tm=1024, tn=512)
        prw = _norm_proj(xp, sh1, sc1, g_mix, w_rwkv, layer, tm=1024, tn=512)
        att = _attn_prompt(qkv, sinks_flat, layer)
        rw, state = _wkv(prw, jnp.zeros((bp, 1, RWKV_PROJ_PAD), F32),
                         jnp.zeros((bp, RWKV_WIDTH, HEAD_DIM), F32), wp, layer, chunk=64, t_valid=tp)
        x1, h2 = _out_proj(att, rw, xp, gt1, sh2, sc2, g_ffn, w_out_b, layer, tm=256)
        xp = _ffn(h2, x1, gt2, w_ffn_in_b, w_ffn_out_b, g_final, layer, tm=512, th=512, final_norm=last)
        outs_p[0].append(qkv[:, tp - WINDOW:, ATTN_WIDTH:ATTN_WIDTH + KV_WIDTH].reshape(bp, WINDOW, N_KV_HEADS, HEAD_DIM))
        outs_p[1].append(qkv[:, tp - WINDOW:, ATTN_WIDTH + KV_WIDTH:].reshape(bp, WINDOW, N_KV_HEADS, HEAD_DIM))
        outs_p[2].append(state.reshape(bp, N_RWKV_HEADS, HEAD_DIM, HEAD_DIM))
        outs_p[3].append(prw[:, tp - 1, :RWKV_PROJ])

        qkv_s = _norm_proj(xs, ssh1, ssc1, g_mix, w_qkv, layer, tm=bs, tn=512)
        prw_s = _norm_proj(xs, ssh1, ssc1, g_mix, w_rwkv, layer, tm=bs, tn=512)
        att_s, nk, nv = _attn_sample(qkv_s.reshape(bs, 1, QKV_COLS), cache_k2, cache_v2, sinks_col, layer)
        prw_tok = prw_s.reshape(bs, 1, RWKV_PROJ_PAD)
        prw_pad = jnp.pad(prw_tok, [(0, 0), (0, sample_chunk - 1), (0, 0)])
        rw_s, state_s = _wkv(prw_pad, _pad_cols(state_shift[layer], RWKV_PROJ_PAD).reshape(bs, 1, RWKV_PROJ_PAD),
                             state_wkv[layer].reshape(bs, RWKV_WIDTH, HEAD_DIM), wp, layer,
                             chunk=sample_chunk, t_valid=1)
        x1_s, h2_s = _out_proj(att_s.reshape(1, bs, ATTN_WIDTH), rw_s[:, 0, :].reshape(1, bs, RWKV_WIDTH), xs,
                               sgt1, ssh2, ssc2, g_ffn, w_out_b, layer, tm=bs)
        xs = _ffn(h2_s, x1_s, sgt2, w_ffn_in_b, w_ffn_out_b, g_final, layer, tm=bs, th=512, final_norm=last)
        outs_s[0].append(nk.reshape(bs, WINDOW, N_KV_HEADS, HEAD_DIM))
        outs_s[1].append(nv.reshape(bs, WINDOW, N_KV_HEADS, HEAD_DIM))
        outs_s[2].append(state_s.reshape(bs, N_RWKV_HEADS, HEAD_DIM, HEAD_DIM))
        outs_s[3].append(prw_tok[:, 0, :RWKV_PROJ])

    stack = lambda xs_: jnp.stack(xs_, axis=0)
    return (xp, xs.reshape(bs, 1, d),
            stack(outs_p[0]), stack(outs_p[1]), stack(outs_p[2]), stack(outs_p[3]),
            stack(outs_s[0]), stack(outs_s[1]), stack(outs_s[2]), stack(outs_s[3]))
```

```python
import functools
import math

import numpy as np
import jax
import jax.numpy as jnp
from jax import lax
from jax.experimental import pallas as pl
from jax.experimental.pallas import tpu as pltpu

F32 = jnp.float32
BF16 = jnp.bfloat16

D_MODEL = 2048
HEAD_DIM = 64
ATTN_WIDTH = D_MODEL // 2
RWKV_WIDTH = D_MODEL - ATTN_WIDTH
N_Q_HEADS = ATTN_WIDTH // HEAD_DIM
N_KV_HEADS = N_Q_HEADS // 4
GQA_REP = N_Q_HEADS // N_KV_HEADS
KV_WIDTH = N_KV_HEADS * HEAD_DIM
N_RWKV_HEADS = RWKV_WIDTH // HEAD_DIM
WINDOW = 128
D_DECAY_LORA = 64
D_ICLR_LORA = 64
D_GATE_LORA = 160
RWKV_PROJ = 3 * RWKV_WIDTH + D_DECAY_LORA + D_ICLR_LORA + D_GATE_LORA
QKV_COLS = ATTN_WIDTH + 2 * KV_WIDTH
FFN_HIDDEN = 5632
RMS_EPS = 1e-5
GN_EPS = 64e-5
NEG_INF = -1e30

LANES = 128
MXU_DIM = 256
VMEM_LIMIT_BYTES = 56 * 1024 * 1024

LORA_OFF = 3 * RWKV_WIDTH
GATE_OFF = LORA_OFF + D_DECAY_LORA + D_ICLR_LORA
GATE_PAD = 2 * LANES
RWKV_PROJ_PAD = 3584
HEADS_PER_GROUP = MXU_DIM // HEAD_DIM
N_GROUPS = N_RWKV_HEADS // HEADS_PER_GROUP

ALIBI_SLOPES = tuple(float(s) for s in np.exp2(-8.0 * np.arange(1, N_Q_HEADS + 1, dtype=np.float32) / N_Q_HEADS))


def _params(n_axes):
    return pltpu.CompilerParams(dimension_semantics=("arbitrary",) * n_axes,
                                vmem_limit_bytes=VMEM_LIMIT_BYTES)


def _dot(a, b):
    return jnp.dot(a, b, preferred_element_type=F32)


def _dot_nt(a, b):
    return lax.dot_general(a, b, (((1,), (1,)), ((), ())), preferred_element_type=F32)


def _dot_tn(a, b):
    return lax.dot_general(a, b, (((0,), (0,)), ((), ())), preferred_element_type=F32)


def _split(x, parts):
    out = []
    for _ in range(parts):
        p = x.astype(BF16)
        out.append(p)
        x = x - p.astype(F32)
    return out


def _rmsnorm_mod(x, g, shift, scale):
    ms = jnp.mean(x * x, axis=-1, keepdims=True)
    y = x * lax.rsqrt(ms + RMS_EPS) * g
    return y * (1.0 + scale) + shift


def _ada_kernel(c_ref, w_ref, b_ref, o_ref):
    c = c_ref[...]
    s = (c * jax.nn.sigmoid(c)).astype(BF16)
    o_ref[...] = _dot(s, w_ref[...].astype(BF16)) + b_ref[...]


def _ada(c_all, w_ada, b_ada):
    n_layers, d, n = w_ada.shape
    rows = c_all.shape[0]
    tn = 1024
    return pl.pallas_call(
        _ada_kernel,
        grid=(n_layers, n // tn),
        in_specs=[pl.BlockSpec((rows, d), lambda l, j: (0, 0)),
                  pl.BlockSpec((None, d, tn), lambda l, j: (l, 0, j)),
                  pl.BlockSpec((None, 1, tn), lambda l, j: (l, 0, j))],
        out_specs=pl.BlockSpec((None, rows, tn), lambda l, j: (l, 0, j)),
        out_shape=jax.ShapeDtypeStruct((n_layers, rows, n), F32),
        compiler_params=_params(2),
        name="ada_mod",
    )(c_all, w_ada, b_ada.reshape(n_layers, 1, n))


def _norm_proj_kernel(x_ref, sh_ref, sc_ref, g_ref, w_ref, o_ref, h_scr):
    @pl.when(pl.program_id(2) == 0)
    def _():
        h_scr[...] = _rmsnorm_mod(x_ref[0], g_ref[...], sh_ref[0], sc_ref[0]).astype(BF16)

    o_ref[0] = _dot(h_scr[...], w_ref[...])


def _norm_proj(x, shift, scale, g, w, layer, *, tm, tn):
    b, t, d = x.shape
    n = w.shape[2]
    r = shift.shape[1]
    return pl.pallas_call(
        _norm_proj_kernel,
        grid=(b, t // tm, n // tn),
        in_specs=[pl.BlockSpec((1, tm, d), lambda i, m, j: (i, m, 0)),
                  pl.BlockSpec((1, r, d), lambda i, m, j: (i, 0, 0)),
                  pl.BlockSpec((1, r, d), lambda i, m, j: (i, 0, 0)),
                  pl.BlockSpec((None, 1, d), lambda i, m, j: (layer, 0, 0)),
                  pl.BlockSpec((None, d, tn), lambda i, m, j: (layer, 0, j))],
        out_specs=pl.BlockSpec((1, tm, tn), lambda i, m, j: (i, m, j)),
        out_shape=jax.ShapeDtypeStruct((b, t, n), F32),
        scratch_shapes=[pltpu.VMEM((tm, d), BF16)],
        compiler_params=_params(3),
        name="norm_proj",
    )(x, shift, scale, g, w)


def _attn_prompt_kernel(sink_ref, q_ref, kc_ref, vc_ref, kp_ref, vp_ref, o_ref, *, layer):
    blk = pl.program_id(1)
    q = q_ref[0] * (HEAD_DIM ** -0.5)
    kc, vc, kp, vp = kc_ref[0], vc_ref[0], kp_ref[0], vp_ref[0]
    t = lax.broadcasted_iota(jnp.int32, (WINDOW, 2 * WINDOW), 0)
    j = lax.broadcasted_iota(jnp.int32, (WINDOW, 2 * WINDOW), 1)
    dist = t + WINDOW - j
    first_key = jnp.where(blk > 0, 0, WINDOW)
    valid = (dist >= 0) & (dist <= WINDOW) & (j >= first_key)
    distf = dist.astype(F32)
    outs = []
    for g in range(N_KV_HEADS):
        ksl = slice(g * HEAD_DIM, (g + 1) * HEAD_DIM)
        k_band = jnp.concatenate([kp[:, ksl], kc[:, ksl]], axis=0).astype(BF16)
        v_band = jnp.concatenate([vp[:, ksl], vc[:, ksl]], axis=0).astype(BF16)
        q_rep = jnp.concatenate(
            [q[:, (g * GQA_REP + r) * HEAD_DIM:(g * GQA_REP + r + 1) * HEAD_DIM] for r in range(GQA_REP)],
            axis=0).astype(BF16)
        logits = _dot_nt(q_rep, k_band)
        for r in range(GQA_REP):
            h = g * GQA_REP + r
            lg = logits[r * WINDOW:(r + 1) * WINDOW] - ALIBI_SLOPES[h] * distf
            lg = jnp.where(valid, lg, NEG_INF)
            sink = sink_ref[layer * N_Q_HEADS + h]
            m = jnp.maximum(jnp.max(lg, axis=-1, keepdims=True), sink)
            p = jnp.exp(lg - m)
            den = jnp.sum(p, axis=-1, keepdims=True) + jnp.exp(sink - m)
            outs.append(_dot((p / den).astype(BF16), v_band))
    o_ref[0] = jnp.concatenate(outs, axis=1).astype(BF16)


def _attn_prompt(qkv, sinks_flat, layer):
    b, t, _ = qkv.shape
    kcol = ATTN_WIDTH // KV_WIDTH
    prev = lambda i, m: (i, jnp.maximum(m - 1, 0), kcol)
    prev_v = lambda i, m: (i, jnp.maximum(m - 1, 0), kcol + 1)
    return pl.pallas_call(
        functools.partial(_attn_prompt_kernel, layer=layer),
        grid=(b, t // WINDOW),
        in_specs=[pl.BlockSpec(memory_space=pltpu.SMEM),
                  pl.BlockSpec((1, WINDOW, ATTN_WIDTH), lambda i, m: (i, m, 0)),
                  pl.BlockSpec((1, WINDOW, KV_WIDTH), lambda i, m: (i, m, kcol)),
                  pl.BlockSpec((1, WINDOW, KV_WIDTH), lambda i, m: (i, m, kcol + 1)),
                  pl.BlockSpec((1, WINDOW, KV_WIDTH), prev),
                  pl.BlockSpec((1, WINDOW, KV_WIDTH), prev_v)],
        out_specs=pl.BlockSpec((1, WINDOW, ATTN_WIDTH), lambda i, m: (i, m, 0)),
        out_shape=jax.ShapeDtypeStruct((b, t, ATTN_WIDTH), BF16),
        compiler_params=_params(2),
        name="attn_prompt",
    )(sinks_flat, qkv, qkv, qkv, qkv, qkv)


def _attn_sample_kernel(row_ref, kc_ref, vc_ref, sink_ref, slope_ref, spread_ref, gather_ref,
                        o_ref, nk_ref, nv_ref):
    row = row_ref[0]
    q = row[:, :ATTN_WIDTH] * (HEAD_DIM ** -0.5)
    k_new = row[:, ATTN_WIDTH:ATTN_WIDTH + KV_WIDTH]
    v_new = row[:, ATTN_WIDTH + KV_WIDTH:]
    kc, vc = kc_ref[0], vc_ref[0]
    head_row = lax.broadcasted_iota(jnp.int32, (N_Q_HEADS, ATTN_WIDTH), 0)
    head_lane = lax.broadcasted_iota(jnp.int32, (N_Q_HEADS, ATTN_WIDTH), 1) // HEAD_DIM
    own = head_row == head_lane
    q_rows = jnp.where(own, jnp.broadcast_to(q, (N_Q_HEADS, ATTN_WIDTH)), 0.0).astype(BF16)
    q_grp = _dot(q_rows, spread_ref[...])
    k_new_b = k_new.astype(BF16).astype(F32)
    lg = _dot_nt(q_grp.astype(BF16), kc.astype(BF16))
    lg_new = jnp.sum(q_grp * k_new_b, axis=-1, keepdims=True)
    dist = (WINDOW - lax.broadcasted_iota(jnp.int32, (N_Q_HEADS, WINDOW), 1)).astype(F32)
    lg = lg - slope_ref[...] * dist
    sink = sink_ref[...]
    m = jnp.maximum(jnp.maximum(jnp.max(lg, axis=-1, keepdims=True), lg_new), sink)
    p = jnp.exp(lg - m)
    p_new = jnp.exp(lg_new - m)
    den = jnp.sum(p, axis=-1, keepdims=True) + p_new + jnp.exp(sink - m)
    out = _dot((p / den).astype(BF16), vc.astype(BF16))
    out = out + (p_new / den).astype(BF16).astype(F32) * v_new.astype(BF16).astype(F32)
    grp_row = lax.broadcasted_iota(jnp.int32, (N_Q_HEADS, KV_WIDTH), 0) // GQA_REP
    grp_lane = lax.broadcasted_iota(jnp.int32, (N_Q_HEADS, KV_WIDTH), 1) // HEAD_DIM
    out = jnp.where(grp_row == grp_lane, out, 0.0).astype(BF16)
    full = _dot(out, gather_ref[...])
    o_ref[0] = jnp.sum(jnp.where(own, full, 0.0), axis=0, keepdims=True).astype(BF16)
    last = lax.broadcasted_iota(jnp.int32, (WINDOW, KV_WIDTH), 0) == WINDOW - 1
    nk_ref[0] = jnp.where(last, k_new, pltpu.roll(kc, WINDOW - 1, axis=0))
    nv_ref[0] = jnp.where(last, v_new, pltpu.roll(vc, WINDOW - 1, axis=0))


def _attn_sample(qkv_rows, cache_k, cache_v, sinks_col, layer):
    s = qkv_rows.shape[0]
    spread = np.zeros((ATTN_WIDTH, KV_WIDTH), np.float32)
    for h in range(N_Q_HEADS):
        for d in range(HEAD_DIM):
            spread[h * HEAD_DIM + d, (h // GQA_REP) * HEAD_DIM + d] = 1.0
    slopes = np.asarray(ALIBI_SLOPES, np.float32).reshape(N_Q_HEADS, 1)
    win = (1, WINDOW, KV_WIDTH)
    return pl.pallas_call(
        _attn_sample_kernel,
        grid=(s,),
        in_specs=[pl.BlockSpec((1, 1, QKV_COLS), lambda i: (i, 0, 0)),
                  pl.BlockSpec((None,) + win, lambda i: (layer, i, 0, 0)),
                  pl.BlockSpec((None,) + win, lambda i: (layer, i, 0, 0)),
                  pl.BlockSpec((None, N_Q_HEADS, 1), lambda i: (layer, 0, 0)),
                  pl.BlockSpec((N_Q_HEADS, 1), lambda i: (0, 0)),
                  pl.BlockSpec((ATTN_WIDTH, KV_WIDTH), lambda i: (0, 0)),
                  pl.BlockSpec((KV_WIDTH, ATTN_WIDTH), lambda i: (0, 0))],
        out_specs=[pl.BlockSpec((1, 1, ATTN_WIDTH), lambda i: (i, 0, 0)),
                   pl.BlockSpec(win, lambda i: (i, 0, 0)),
                   pl.BlockSpec(win, lambda i: (i, 0, 0))],
        out_shape=[jax.ShapeDtypeStruct((s, 1, ATTN_WIDTH), BF16),
                   jax.ShapeDtypeStruct((s, WINDOW, KV_WIDTH), F32),
                   jax.ShapeDtypeStruct((s, WINDOW, KV_WIDTH), F32)],
        compiler_params=_params(1),
        name="attn_sample",
    )(qkv_rows, cache_k, cache_v, sinks_col, jnp.asarray(slopes),
      jnp.asarray(spread, BF16), jnp.asarray(spread.T, BF16))


def _wkv_kernel(p_ref, shift0_ref, s0_ref, mix_ref, w0_ref, dup_ref, a0_ref, iup_ref, gup_ref,
                kk_ref, ka_ref, rk_ref, lnw_ref, lnb_ref, rw_ref, sout_ref, s_scr, prev_scr,
                *, chunk, t_valid, t_total):
    c = pl.program_id(1)
    n_chunks = pl.num_programs(1)
    bd_rows = HEADS_PER_GROUP * chunk
    groups = range(N_GROUPS)
    cols = [slice(g * MXU_DIM, (g + 1) * MXU_DIM) for g in groups]
    iota = lambda shape, dim: lax.broadcasted_iota(jnp.int32, shape, dim)
    same_head = iota((MXU_DIM, MXU_DIM), 0) // HEAD_DIM == iota((MXU_DIM, MXU_DIM), 1) // HEAD_DIM
    ones_bd = jnp.where(same_head, 1.0, 0.0).astype(BF16)
    tile_sel = jnp.where(iota((HEAD_DIM, MXU_DIM), 0) == iota((HEAD_DIM, MXU_DIM), 1) % HEAD_DIM, 1.0, 0.0).astype(BF16)
    fold_sel = jnp.where(iota((MXU_DIM, HEAD_DIM), 0) % HEAD_DIM == iota((MXU_DIM, HEAD_DIM), 1), 1.0, 0.0).astype(BF16)

    @pl.when(c == 0)
    def _():
        prev_scr[...] = shift0_ref[0]
        for g in groups:
            rows = s0_ref[0, cols[g], :]
            tiled = sum(_dot(part, tile_sel) for part in _split(rows, 3))
            s_scr[g] = jnp.where(same_head, tiled, 0.0)

    row = iota((chunk, 1), 0)

    def shifted(lo, width):
        cur = p_ref[0, :, lo:lo + width]
        prev = jnp.where(row == 0, prev_scr[:, lo:lo + width], pltpu.roll(cur, 1, axis=0))
        return cur + (prev - cur) * mix_ref[:, lo:lo + width]

    def seg_sum(x):
        return sum(_dot(part, ones_bd) for part in _split(x, 2))

    head_mask = [jnp.where(iota((chunk, MXU_DIM), 1) // HEAD_DIM == h, 1.0, 0.0).astype(BF16)
                 for h in range(HEADS_PER_GROUP)]

    def head_blocks(x):
        return jnp.concatenate([x * m for m in head_mask], axis=0)

    lora_in = shifted(LORA_OFF, LANES)
    lora_tanh = jnp.tanh(lora_in).astype(BF16)
    lora_lin = lora_in.astype(BF16)
    gate_sig = jax.nn.sigmoid(shifted(GATE_OFF, GATE_PAD)).astype(BF16)
    r = [shifted(g * MXU_DIM, MXU_DIM) for g in groups]
    k = [shifted(RWKV_WIDTH + g * MXU_DIM, MXU_DIM) for g in groups]
    v = [shifted(2 * RWKV_WIDTH + g * MXU_DIM, MXU_DIM) for g in groups]
    prev_scr[...] = p_ref[0, chunk - 1:chunk, :]

    logw = [-math.exp(-0.5) * jax.nn.sigmoid(w0_ref[:, cols[g]] + _dot(lora_tanh, dup_ref[:, cols[g]])) for g in groups]
    a = [jax.nn.sigmoid(a0_ref[:, cols[g]] + _dot(lora_lin, iup_ref[:, cols[g]])) for g in groups]
    gate = [_dot(gate_sig, gup_ref[:, cols[g]]) for g in groups]
    kk = [k[g] * kk_ref[:, cols[g]] for g in groups]
    kk = [kk[g] / jnp.maximum(jnp.sqrt(seg_sum(kk[g] * kk[g])), 1e-12) for g in groups]
    k = [k[g] * (1.0 + (a[g] - 1.0) * ka_ref[:, cols[g]]) for g in groups]
    if t_valid < t_total:
        live = (c * chunk + row) < t_valid
        logw = [jnp.where(live, x, 0.0) for x in logw]
        kk = [jnp.where(live, x, 0.0) for x in kk]
        k = [jnp.where(live, x, 0.0) for x in k]
        v = [jnp.where(live, x, 0.0) for x in v]

    tri = jnp.where(iota((chunk, chunk), 0) >= iota((chunk, chunk), 1), 1.0, 0.0).astype(BF16)
    cum = [sum(_dot(tri, part) for part in _split(logw[g], 3)) for g in groups]
    e_pos = [jnp.exp(x) for x in cum]
    e_neg = [jnp.exp(-x) for x in cum]
    a_t = [-kk[g] * jnp.exp(cum[g] - logw[g]) for g in groups]
    b_t = [kk[g] * a[g] * e_neg[g] for g in groups]
    k_t = [k[g] * e_neg[g] for g in groups]
    r_t = [r[g] * e_pos[g] for g in groups]
    total = [x[chunk - 1:chunk, :] for x in e_pos]
    ar_b = [jnp.concatenate([a_t[g], r_t[g]], axis=0).astype(BF16) for g in groups]
    xb = [head_blocks(x.astype(BF16)) for x in b_t]
    xk = [head_blocks(x.astype(BF16)) for x in k_t]
    v_bd = [head_blocks(x.astype(BF16)) for x in v]

    scores = [_dot_nt(ar_b[g], jnp.concatenate([xb[g], xk[g]], axis=0)) for g in groups]
    t_idx = iota((chunk, 2 * bd_rows), 0)
    s_idx = iota((chunk, 2 * bd_rows), 1) % chunk
    bd_row, bd_col = iota((bd_rows, bd_rows), 0), iota((bd_rows, bd_rows), 1)
    strict_bd = (bd_row // chunk == bd_col // chunk) & (bd_col % chunk < bd_row % chunk)
    eye_b = jnp.where(bd_row == bd_col, 1.0, 0.0).astype(BF16)
    a_ak = [jnp.where((s_idx < t_idx)[:, :bd_rows], x[:chunk, bd_rows:], 0.0).astype(BF16) for x in scores]
    a_r = [jnp.where(s_idx <= t_idx, x[chunk:], 0.0).astype(BF16) for x in scores]

    e_bd = [jnp.where(strict_bd, jnp.concatenate([x[:chunk, :bd_rows]] * HEADS_PER_GROUP, axis=0), 0.0)
            for x in scores]
    m_b = [x.astype(BF16) for x in e_bd]
    m_b = [_dot(x, x).astype(BF16) for x in m_b]
    levels = int(math.log2(chunk)) - 1
    for lvl in range(levels):
        t_b = [x.astype(BF16) + eye_b for x in e_bd]
        e_bd = [e_bd[g] + _dot(m_b[g], t_b[g]) for g in groups]
        if lvl < levels - 1:
            m_b = [_dot(x, x).astype(BF16) for x in m_b]
    e_cat = [sum(x[h * chunk:(h + 1) * chunk] for h in range(HEADS_PER_GROUP)).astype(BF16) for x in e_bd]

    s_b = [s_scr[g].astype(BF16) for g in groups]
    uy = [_dot_nt(ar_b[g], s_b[g]) for g in groups]
    rhs = [uy[g][:chunk] + _dot(a_ak[g], v_bd[g]) for g in groups]
    p = [rhs[g] + _dot(e_cat[g], head_blocks(rhs[g].astype(BF16))) for g in groups]
    p_b = [x.astype(BF16) for x in p]
    y = [uy[g][chunk:] + _dot(a_r[g], jnp.concatenate([head_blocks(p_b[g]), v_bd[g]], axis=0)) for g in groups]
    for g in groups:
        upd = _dot_tn(jnp.concatenate([p_b[g], v[g].astype(BF16)], axis=0),
                      jnp.concatenate([b_t[g] * total[g], k_t[g] * total[g]], axis=0).astype(BF16))
        s_scr[g] = s_scr[g] * total[g] + jnp.where(same_head, upd, 0.0)

    inv_n = 1.0 / HEAD_DIM
    mu = [seg_sum(x) * inv_n for x in y]
    dev = [y[g] - mu[g] for g in groups]
    var = [seg_sum(x * x) * inv_n for x in dev]
    bonus = [seg_sum(r[g] * k[g] * rk_ref[:, cols[g]]) * v[g] for g in groups]
    for g in groups:
        yn = dev[g] * lax.rsqrt(var[g] + GN_EPS) * lnw_ref[:, cols[g]] + lnb_ref[:, cols[g]]
        rw_ref[0, :, cols[g]] = ((yn + bonus[g]) * gate[g]).astype(BF16)

    @pl.when(c == n_chunks - 1)
    def _():
        for g in groups:
            folded = sum(_dot(part, fold_sel) for part in _split(s_scr[g], 3))
            sout_ref[0, cols[g], :] = folded


def _wkv(prw, shift0, s0, wp, layer, *, chunk, t_valid):
    b, t, _ = prw.shape
    lay = lambda *blk: pl.BlockSpec((None,) + blk, lambda i, c: (layer,) + (0,) * len(blk))
    vec = lay(1, RWKV_WIDTH)
    return pl.pallas_call(
        functools.partial(_wkv_kernel, chunk=chunk, t_valid=t_valid, t_total=t),
        grid=(b, t // chunk),
        in_specs=[pl.BlockSpec((1, chunk, RWKV_PROJ_PAD), lambda i, c: (i, c, 0)),
                  pl.BlockSpec((1, 1, RWKV_PROJ_PAD), lambda i, c: (i, 0, 0)),
                  pl.BlockSpec((1, RWKV_WIDTH, HEAD_DIM), lambda i, c: (i, 0, 0)),
                  lay(1, RWKV_PROJ_PAD), vec, lay(LANES, RWKV_WIDTH), vec, lay(LANES, RWKV_WIDTH),
                  lay(GATE_PAD, RWKV_WIDTH), vec, vec, vec, vec, vec],
        out_specs=[pl.BlockSpec((1, chunk, RWKV_WIDTH), lambda i, c: (i, c, 0)),
                   pl.BlockSpec((1, RWKV_WIDTH, HEAD_DIM), lambda i, c: (i, 0, 0))],
        out_shape=[jax.ShapeDtypeStruct((b, t, RWKV_WIDTH), BF16),
                   jax.ShapeDtypeStruct((b, RWKV_WIDTH, HEAD_DIM), F32)],
        scratch_shapes=[pltpu.VMEM((N_GROUPS, MXU_DIM, MXU_DIM), F32),
                        pltpu.VMEM((1, RWKV_PROJ_PAD), F32)],
        compiler_params=_params(2),
        name="wkv7",
    )(prw, shift0, s0, wp["mix"], wp["w0"], wp["decay_up"], wp["a0"], wp["iclr_up"], wp["gate_up"],
      wp["k_k"], wp["k_a"], wp["r_k"], wp["ln_w"], wp["ln_b"])


def _out_proj_kernel(att_ref, rw_ref, x_ref, gt_ref, sh_ref, sc_ref, g_ref, w_ref, x1_ref, h_ref):
    y = _dot(att_ref[0], w_ref[0:ATTN_WIDTH, :]) + _dot(rw_ref[0], w_ref[ATTN_WIDTH:, :])
    x1 = x_ref[0] + gt_ref[0] * y
    x1_ref[0] = x1
    h_ref[0] = _rmsnorm_mod(x1, g_ref[...], sh_ref[0], sc_ref[0]).astype(BF16)


def _out_proj(att, rw, x, gate, shift, scale, g, w, layer, *, tm):
    b, t, d = x.shape
    r = gate.shape[1]
    row = lambda width: pl.BlockSpec((1, tm, width), lambda i, m: (i, m, 0))
    mod = pl.BlockSpec((1, r, d), lambda i, m: (i, 0, 0))
    return pl.pallas_call(
        _out_proj_kernel,
        grid=(b, t // tm),
        in_specs=[row(ATTN_WIDTH), row(RWKV_WIDTH), row(d), mod, mod, mod,
                  pl.BlockSpec((None, 1, d), lambda i, m: (layer, 0, 0)),
                  pl.BlockSpec((None, d, d), lambda i, m: (layer, 0, 0))],
        out_specs=[row(d), row(d)],
        out_shape=[jax.ShapeDtypeStruct((b, t, d), F32), jax.ShapeDtypeStruct((b, t, d), BF16)],
        compiler_params=_params(2),
        name="out_proj",
    )(att, rw, x, gate, shift, scale, g, w)


def _ffn_kernel(h_ref, x_ref, gt_ref, wg_ref, wu_ref, wo_ref, gf_ref, o_ref, acc_ref, *, final_norm):
    j = pl.program_id(2)
    h = h_ref[0]
    gate = _dot(h, wg_ref[...])
    up = _dot(h, wu_ref[...])
    act = (gate * jax.nn.sigmoid(gate) * up).astype(BF16)
    part = _dot(act, wo_ref[...])

    @pl.when(j == 0)
    def _():
        acc_ref[...] = part

    @pl.when(j > 0)
    def _():
        acc_ref[...] += part

    @pl.when(j == pl.num_programs(2) - 1)
    def _():
        x2 = x_ref[0] + gt_ref[0] * acc_ref[...]
        if final_norm:
            ms = jnp.mean(x2 * x2, axis=-1, keepdims=True)
            x2 = x2 * lax.rsqrt(ms + RMS_EPS) * gf_ref[...]
        o_ref[0] = x2


def _ffn(h, x, gate, w_in, w_out, g_final, layer, *, tm, th, final_norm):
    b, t, d = x.shape
    r = gate.shape[1]
    n_h = FFN_HIDDEN // th
    return pl.pallas_call(
        functools.partial(_ffn_kernel, final_norm=final_norm),
        grid=(b, t // tm, n_h),
        in_specs=[pl.BlockSpec((1, tm, d), lambda i, m, j: (i, m, 0)),
                  pl.BlockSpec((1, tm, d), lambda i, m, j: (i, m, 0)),
                  pl.BlockSpec((1, r, d), lambda i, m, j: (i, 0, 0)),
                  pl.BlockSpec((None, d, th), lambda i, m, j: (layer, 0, j)),
                  pl.BlockSpec((None, d, th), lambda i, m, j: (layer, 0, j + n_h)),
                  pl.BlockSpec((None, th, d), lambda i, m, j: (layer, j, 0)),
                  pl.BlockSpec((1, d), lambda i, m, j: (0, 0))],
        out_specs=pl.BlockSpec((1, tm, d), lambda i, m, j: (i, m, 0)),
        out_shape=jax.ShapeDtypeStruct((b, t, d), F32),
        scratch_shapes=[pltpu.VMEM((tm, d), F32)],
        compiler_params=_params(3),
        name="ffn",
    )(h, x, gate, w_in, w_in, w_out, g_final)


def _pad_cols(x, width):
    return jnp.pad(x, [(0, 0)] * (x.ndim - 1) + [(0, width - x.shape[-1])])


def _pad_rows(x, rows, offset=0):
    return jnp.pad(x, [(0, 0), (offset, rows - offset - x.shape[1]), (0, 0)])


def kernel(x_prompt, x_sample, cache_k, cache_v, state_wkv, state_shift, c_prompt, c_sample, w_ada, b_ada, g_norm_mix, g_norm_ffn, w_in, w_out, attn_sinks, mix_shift, decay_w0, decay_up, iclr_a0, iclr_up, gate_up, k_k, k_a, r_k, ln_x_w, ln_x_b, w_ffn_in, w_ffn_out, g_norm_final):
    n_layers = w_in.shape[0]
    bp, tp, d = x_prompt.shape
    bs = x_sample.shape[0]
    sample_chunk = 32

    w_qkv = w_in[:, :, :QKV_COLS].astype(BF16)
    w_rwkv = _pad_cols(w_in[:, :, QKV_COLS:], RWKV_PROJ_PAD).astype(BF16)
    w_out_b = w_out.astype(BF16)
    w_ffn_in_b = w_ffn_in.astype(BF16)
    w_ffn_out_b = w_ffn_out.astype(BF16)
    vec = lambda x: x.reshape(n_layers, 1, -1)
    wp = dict(
        mix=vec(_pad_cols(mix_shift, RWKV_PROJ_PAD)), w0=vec(decay_w0), a0=vec(iclr_a0),
        decay_up=_pad_rows(decay_up, LANES).astype(BF16),
        iclr_up=_pad_rows(iclr_up, LANES, D_DECAY_LORA).astype(BF16),
        gate_up=_pad_rows(gate_up, GATE_PAD).astype(BF16),
        k_k=vec(k_k), k_a=vec(k_a), r_k=vec(r_k), ln_w=vec(ln_x_w), ln_b=vec(ln_x_b))
    g_mix, g_ffn = vec(g_norm_mix), vec(g_norm_ffn)
    g_final = g_norm_final.reshape(1, d)
    sinks_flat = attn_sinks.reshape(-1)
    sinks_col = attn_sinks.reshape(n_layers, N_Q_HEADS, 1)
    cache_k2 = cache_k.reshape(n_layers, bs, WINDOW, KV_WIDTH)
    cache_v2 = cache_v.reshape(n_layers, bs, WINDOW, KV_WIDTH)

    mod = _ada(jnp.concatenate([c_prompt, c_sample], axis=0), w_ada, b_ada)

    def mods(layer):
        parts = [mod[layer, :, i * d:(i + 1) * d] for i in range(6)]
        return ([p[:bp].reshape(bp, 1, d) for p in parts], [p[bp:].reshape(1, bs, d) for p in parts])

    xp = x_prompt
    xs = x_sample.reshape(1, bs, d)
    outs_p = [[] for _ in range(4)]
    outs_s = [[] for _ in range(4)]
    for layer in range(n_layers):
        last = layer == n_layers - 1
        (sh1, sc1, gt1, sh2, sc2, gt2), (ssh1, ssc1, sgt1, ssh2, ssc2, sgt2) = mods(layer)

        qkv = _norm_proj(xp, sh1, sc1, g_mix, w_qkv, layer, tm=1024, tn=512)
        prw = _norm_proj(xp, sh1, sc1, g_mix, w_rwkv, layer, tm=1024, tn=512)
        att = _attn_prompt(qkv, sinks_flat, layer)
        rw, state = _wkv(prw, jnp.zeros((bp, 1, RWKV_PROJ_PAD), F32),
                         jnp.zeros((bp, RWKV_WIDTH, HEAD_DIM), F32), wp, layer, chunk=64, t_valid=tp)
        x1, h2 = _out_proj(att, rw, xp, gt1, sh2, sc2, g_ffn, w_out_b, layer, tm=256)
        xp = _ffn(h2, x1, gt2, w_ffn_in_b, w_ffn_out_b, g_final, layer, tm=512, th=512, final_norm=last)
        outs_p[0].append(qkv[:, tp - WINDOW:, ATTN_WIDTH:ATTN_WIDTH + KV_WIDTH].reshape(bp, WINDOW, N_KV_HEADS, HEAD_DIM))
        outs_p[1].append(qkv[:, tp - WINDOW:, ATTN_WIDTH + KV_WIDTH:].reshape(bp, WINDOW, N_KV_HEADS, HEAD_DIM))
        outs_p[2].append(state.reshape(bp, N_RWKV_HEADS, HEAD_DIM, HEAD_DIM))
        outs_p[3].append(prw[:, tp - 1, :RWKV_PROJ])

        qkv_s = _norm_proj(xs, ssh1, ssc1, g_mix, w_qkv, layer, tm=bs, tn=512)
        prw_s = _norm_proj(xs, ssh1, ssc1, g_mix, w_rwkv, layer, tm=bs, tn=512)
        att_s, nk, nv = _attn_sample(qkv_s.reshape(bs, 1, QKV_COLS), cache_k2, cache_v2, sinks_col, layer)
        prw_tok = prw_s.reshape(bs, 1, RWKV_PROJ_PAD)
        prw_pad = jnp.pad(prw_tok, [(0, 0), (0, sample_chunk - 1), (0, 0)])
        rw_s, state_s = _wkv(prw_pad, _pad_cols(state_shift[layer], RWKV_PROJ_PAD).reshape(bs, 1, RWKV_PROJ_PAD),
                             state_wkv[layer].reshape(bs, RWKV_WIDTH, HEAD_DIM), wp, layer,
                             chunk=sample_chunk, t_valid=1)
        x1_s, h2_s = _out_proj(att_s.reshape(1, bs, ATTN_WIDTH), rw_s[:, 0, :].reshape(1, bs, RWKV_WIDTH), xs,
                               sgt1, ssh2, ssc2, g_ffn, w_out_b, layer, tm=bs)
        xs = _ffn(h2_s, x1_s, sgt2, w_ffn_in_b, w_ffn_out_b, g_final, layer, tm=bs, th=512, final_norm=last)
        outs_s[0].append(nk.reshape(bs, WINDOW, N_KV_HEADS, HEAD_DIM))
        outs_s[1].append(nv.reshape(bs, WINDOW, N_KV_HEADS, HEAD_DIM))
        outs_s[2].append(state_s.reshape(bs, N_RWKV_HEADS, HEAD_DIM, HEAD_DIM))
        outs_s[3].append(prw_tok[:, 0, :RWKV_PROJ])

    stack = lambda xs_: jnp.stack(xs_, axis=0)
    return (xp, xs.reshape(bs, 1, d),
            stack(outs_p[0]), stack(outs_p[1]), stack(outs_p[2]), stack(outs_p[3]),
            stack(outs_s[0]), stack(outs_s[1]), stack(outs_s[2]), stack(outs_s[3]))
```

```python
import functools
import math

import numpy as np
import jax
import jax.numpy as jnp
from jax import lax
from jax.experimental import pallas as pl
from jax.experimental.pallas import tpu as pltpu

F32 = jnp.float32
BF16 = jnp.bfloat16

D_MODEL = 2048
HEAD_DIM = 64
ATTN_WIDTH = D_MODEL // 2
RWKV_WIDTH = D_MODEL - ATTN_WIDTH
N_Q_HEADS = ATTN_WIDTH // HEAD_DIM
N_KV_HEADS = N_Q_HEADS // 4
GQA_REP = N_Q_HEADS // N_KV_HEADS
KV_WIDTH = N_KV_HEADS * HEAD_DIM
N_RWKV_HEADS = RWKV_WIDTH // HEAD_DIM
WINDOW = 128
D_DECAY_LORA = 64
D_ICLR_LORA = 64
D_GATE_LORA = 160
RWKV_PROJ = 3 * RWKV_WIDTH + D_DECAY_LORA + D_ICLR_LORA + D_GATE_LORA
QKV_COLS = ATTN_WIDTH + 2 * KV_WIDTH
FFN_HIDDEN = 5632
RMS_EPS = 1e-5
GN_EPS = 64e-5
NEG_INF = -1e30

LANES = 128
MXU_DIM = 256
VMEM_LIMIT_BYTES = 56 * 1024 * 1024

LORA_OFF = 3 * RWKV_WIDTH
GATE_OFF = LORA_OFF + D_DECAY_LORA + D_ICLR_LORA
GATE_PAD = 2 * LANES
RWKV_PROJ_PAD = 3584
HEADS_PER_GROUP = MXU_DIM // HEAD_DIM
N_GROUPS = N_RWKV_HEADS // HEADS_PER_GROUP

ALIBI_SLOPES = tuple(float(s) for s in np.exp2(-8.0 * np.arange(1, N_Q_HEADS + 1, dtype=np.float32) / N_Q_HEADS))


def _params(n_axes):
    return pltpu.CompilerParams(dimension_semantics=("arbitrary",) * n_axes,
                                vmem_limit_bytes=VMEM_LIMIT_BYTES)


def _dot(a, b):
    return jnp.dot(a, b, preferred_element_type=F32)


def _dot_nt(a, b):
    return lax.dot_general(a, b, (((1,), (1,)), ((), ())), preferred_element_type=F32)


def _dot_tn(a, b):
    return lax.dot_general(a, b, (((0,), (0,)), ((), ())), preferred_element_type=F32)


def _split(x, parts):
    out = []
    for _ in range(parts):
        p = x.astype(BF16)
        out.append(p)
        x = x - p.astype(F32)
    return out


def _rmsnorm_mod(x, g, shift, scale):
    ms = jnp.mean(x * x, axis=-1, keepdims=True)
    y = x * lax.rsqrt(ms + RMS_EPS) * g
    return y * (1.0 + scale) + shift


def _ada_kernel(c_ref, w_ref, b_ref, o_ref):
    c = c_ref[...]
    s = (c * jax.nn.sigmoid(c)).astype(BF16)
    o_ref[...] = _dot(s, w_ref[...].astype(BF16)) + b_ref[...]


def _ada(c_all, w_ada, b_ada):
    n_layers, d, n = w_ada.shape
    rows = c_all.shape[0]
    tn = 1024
    return pl.pallas_call(
        _ada_kernel,
        grid=(n_layers, n // tn),
        in_specs=[pl.BlockSpec((rows, d), lambda l, j: (0, 0)),
                  pl.BlockSpec((None, d, tn), lambda l, j: (l, 0, j)),
                  pl.BlockSpec((None, 1, tn), lambda l, j: (l, 0, j))],
        out_specs=pl.BlockSpec((None, rows, tn), lambda l, j: (l, 0, j)),
        out_shape=jax.ShapeDtypeStruct((n_layers, rows, n), F32),
        compiler_params=_params(2),
        name="ada_mod",
    )(c_all, w_ada, b_ada.reshape(n_layers, 1, n))


NORM_STRIP = 32


def _norm_proj_kernel(x_ref, sh_ref, sc_ref, g_ref, w_ref, qkv_ref, prw_ref, h_scr, *, n_qkv_tiles):
    j = pl.program_id(2)
    tm = h_scr.shape[0]

    @pl.when(j == 0)
    def _():
        gain = g_ref[...] * (1.0 + sc_ref[0])
        shift = sh_ref[0]
        if gain.shape[0] == 1:
            def strip(i, carry):
                rows = pl.ds(pl.multiple_of(i * NORM_STRIP, NORM_STRIP), NORM_STRIP)
                x = x_ref[0, rows, :]
                ms = jnp.mean(x * x, axis=-1, keepdims=True)
                h_scr[rows, :] = (x * lax.rsqrt(ms + RMS_EPS) * gain + shift).astype(BF16)
                return carry
            lax.fori_loop(0, tm // NORM_STRIP, strip, 0, unroll=4)
        else:
            x = x_ref[0]
            ms = jnp.mean(x * x, axis=-1, keepdims=True)
            h_scr[...] = (x * lax.rsqrt(ms + RMS_EPS) * gain + shift).astype(BF16)

    @pl.when(j < n_qkv_tiles)
    def _():
        qkv_ref[0] = _dot(h_scr[...], w_ref[...])

    @pl.when(j >= n_qkv_tiles)
    def _():
        prw_ref[0] = _dot(h_scr[...], w_ref[...])


def _norm_proj(x, shift, scale, g, w, layer, *, tm, tn):
    b, t, d = x.shape
    r = shift.shape[1]
    n_qkv_tiles = QKV_COLS // tn
    n_tiles = (QKV_COLS + RWKV_PROJ_PAD) // tn
    return pl.pallas_call(
        functools.partial(_norm_proj_kernel, n_qkv_tiles=n_qkv_tiles),
        grid=(b, t // tm, n_tiles),
        in_specs=[pl.BlockSpec((1, tm, d), lambda i, m, j: (i, m, 0)),
                  pl.BlockSpec((1, r, d), lambda i, m, j: (i, 0, 0)),
                  pl.BlockSpec((1, r, d), lambda i, m, j: (i, 0, 0)),
                  pl.BlockSpec((None, 1, d), lambda i, m, j: (layer, 0, 0)),
                  pl.BlockSpec((None, d, tn), lambda i, m, j: (layer, 0, j))],
        out_specs=[pl.BlockSpec((1, tm, tn), lambda i, m, j: (i, m, jnp.minimum(j, n_qkv_tiles - 1))),
                   pl.BlockSpec((1, tm, tn), lambda i, m, j: (i, m, jnp.maximum(j - n_qkv_tiles, 0)))],
        out_shape=[jax.ShapeDtypeStruct((b, t, QKV_COLS), F32),
                   jax.ShapeDtypeStruct((b, t, RWKV_PROJ_PAD), F32)],
        scratch_shapes=[pltpu.VMEM((tm, d), BF16)],
        compiler_params=_params(3),
        name="norm_proj",
    )(x, shift, scale, g, w)


def _attn_prompt_kernel(sink_ref, q_ref, kc_ref, vc_ref, kp_ref, vp_ref, o_ref, *, layer):
    blk = pl.program_id(1)
    q = q_ref[0] * (HEAD_DIM ** -0.5)
    kc, vc, kp, vp = kc_ref[0], vc_ref[0], kp_ref[0], vp_ref[0]
    t = lax.broadcasted_iota(jnp.int32, (WINDOW, 2 * WINDOW), 0)
    j = lax.broadcasted_iota(jnp.int32, (WINDOW, 2 * WINDOW), 1)
    dist = t + WINDOW - j
    first_key = jnp.where(blk > 0, 0, WINDOW)
    valid = (dist >= 0) & (dist <= WINDOW) & (j >= first_key)
    distf = dist.astype(F32)
    outs = []
    for g in range(N_KV_HEADS):
        ksl = slice(g * HEAD_DIM, (g + 1) * HEAD_DIM)
        k_band = jnp.concatenate([kp[:, ksl], kc[:, ksl]], axis=0).astype(BF16)
        v_band = jnp.concatenate([vp[:, ksl], vc[:, ksl]], axis=0).astype(BF16)
        q_rep = jnp.concatenate(
            [q[:, (g * GQA_REP + r) * HEAD_DIM:(g * GQA_REP + r + 1) * HEAD_DIM] for r in range(GQA_REP)],
            axis=0).astype(BF16)
        logits = _dot_nt(q_rep, k_band)
        for r in range(GQA_REP):
            h = g * GQA_REP + r
            lg = logits[r * WINDOW:(r + 1) * WINDOW] - ALIBI_SLOPES[h] * distf
            lg = jnp.where(valid, lg, NEG_INF)
            sink = sink_ref[layer * N_Q_HEADS + h]
            m = jnp.maximum(jnp.max(lg, axis=-1, keepdims=True), sink)
            p = jnp.exp(lg - m)
            den = jnp.sum(p, axis=-1, keepdims=True) + jnp.exp(sink - m)
            outs.append(_dot((p / den).astype(BF16), v_band))
    o_ref[0] = jnp.concatenate(outs, axis=1).astype(BF16)


def _attn_prompt(qkv, sinks_flat, layer):
    b, t, _ = qkv.shape
    kcol = ATTN_WIDTH // KV_WIDTH
    prev = lambda i, m: (i, jnp.maximum(m - 1, 0), kcol)
    prev_v = lambda i, m: (i, jnp.maximum(m - 1, 0), kcol + 1)
    return pl.pallas_call(
        functools.partial(_attn_prompt_kernel, layer=layer),
        grid=(b, t // WINDOW),
        in_specs=[pl.BlockSpec(memory_space=pltpu.SMEM),
                  pl.BlockSpec((1, WINDOW, ATTN_WIDTH), lambda i, m: (i, m, 0)),
                  pl.BlockSpec((1, WINDOW, KV_WIDTH), lambda i, m: (i, m, kcol)),
                  pl.BlockSpec((1, WINDOW, KV_WIDTH), lambda i, m: (i, m, kcol + 1)),
                  pl.BlockSpec((1, WINDOW, KV_WIDTH), prev),
                  pl.BlockSpec((1, WINDOW, KV_WIDTH), prev_v)],
        out_specs=pl.BlockSpec((1, WINDOW, ATTN_WIDTH), lambda i, m: (i, m, 0)),
        out_shape=jax.ShapeDtypeStruct((b, t, ATTN_WIDTH), BF16),
        compiler_params=_params(2),
        name="attn_prompt",
    )(sinks_flat, qkv, qkv, qkv, qkv, qkv)


def _attn_sample_kernel(row_ref, kc_ref, vc_ref, sink_ref, slope_ref, spread_ref, gather_ref,
                        o_ref, nk_ref, nv_ref):
    row = row_ref[0]
    q = row[:, :ATTN_WIDTH] * (HEAD_DIM ** -0.5)
    k_new = row[:, ATTN_WIDTH:ATTN_WIDTH + KV_WIDTH]
    v_new = row[:, ATTN_WIDTH + KV_WIDTH:]
    kc, vc = kc_ref[0], vc_ref[0]
    head_row = lax.broadcasted_iota(jnp.int32, (N_Q_HEADS, ATTN_WIDTH), 0)
    head_lane = lax.broadcasted_iota(jnp.int32, (N_Q_HEADS, ATTN_WIDTH), 1) // HEAD_DIM
    own = head_row == head_lane
    q_rows = jnp.where(own, jnp.broadcast_to(q, (N_Q_HEADS, ATTN_WIDTH)), 0.0).astype(BF16)
    q_grp = _dot(q_rows, spread_ref[...])
    k_new_b = k_new.astype(BF16).astype(F32)
    lg = _dot_nt(q_grp.astype(BF16), kc.astype(BF16))
    lg_new = jnp.sum(q_grp * k_new_b, axis=-1, keepdims=True)
    dist = (WINDOW - lax.broadcasted_iota(jnp.int32, (N_Q_HEADS, WINDOW), 1)).astype(F32)
    lg = lg - slope_ref[...] * dist
    sink = sink_ref[...]
    m = jnp.maximum(jnp.maximum(jnp.max(lg, axis=-1, keepdims=True), lg_new), sink)
    p = jnp.exp(lg - m)
    p_new = jnp.exp(lg_new - m)
    den = jnp.sum(p, axis=-1, keepdims=True) + p_new + jnp.exp(sink - m)
    out = _dot((p / den).astype(BF16), vc.astype(BF16))
    out = out + (p_new / den).astype(BF16).astype(F32) * v_new.astype(BF16).astype(F32)
    grp_row = lax.broadcasted_iota(jnp.int32, (N_Q_HEADS, KV_WIDTH), 0) // GQA_REP
    grp_lane = lax.broadcasted_iota(jnp.int32, (N_Q_HEADS, KV_WIDTH), 1) // HEAD_DIM
    out = jnp.where(grp_row == grp_lane, out, 0.0).astype(BF16)
    full = _dot(out, gather_ref[...])
    o_ref[0] = jnp.sum(jnp.where(own, full, 0.0), axis=0, keepdims=True).astype(BF16)
    last = lax.broadcasted_iota(jnp.int32, (WINDOW, KV_WIDTH), 0) == WINDOW - 1
    nk_ref[0] = jnp.where(last, k_new, pltpu.roll(kc, WINDOW - 1, axis=0))
    nv_ref[0] = jnp.where(last, v_new, pltpu.roll(vc, WINDOW - 1, axis=0))


def _attn_sample(qkv_rows, cache_k, cache_v, sinks_col, layer):
    s = qkv_rows.shape[0]
    spread = np.zeros((ATTN_WIDTH, KV_WIDTH), np.float32)
    for h in range(N_Q_HEADS):
        for d in range(HEAD_DIM):
            spread[h * HEAD_DIM + d, (h // GQA_REP) * HEAD_DIM + d] = 1.0
    slopes = np.asarray(ALIBI_SLOPES, np.float32).reshape(N_Q_HEADS, 1)
    win = (1, WINDOW, KV_WIDTH)
    return pl.pallas_call(
        _attn_sample_kernel,
        grid=(s,),
        in_specs=[pl.BlockSpec((1, 1, QKV_COLS), lambda i: (i, 0, 0)),
                  pl.BlockSpec((None,) + win, lambda i: (layer, i, 0, 0)),
                  pl.BlockSpec((None,) + win, lambda i: (layer, i, 0, 0)),
                  pl.BlockSpec((None, N_Q_HEADS, 1), lambda i: (layer, 0, 0)),
                  pl.BlockSpec((N_Q_HEADS, 1), lambda i: (0, 0)),
                  pl.BlockSpec((ATTN_WIDTH, KV_WIDTH), lambda i: (0, 0)),
                  pl.BlockSpec((KV_WIDTH, ATTN_WIDTH), lambda i: (0, 0))],
        out_specs=[pl.BlockSpec((1, 1, ATTN_WIDTH), lambda i: (i, 0, 0)),
                   pl.BlockSpec(win, lambda i: (i, 0, 0)),
                   pl.BlockSpec(win, lambda i: (i, 0, 0))],
        out_shape=[jax.ShapeDtypeStruct((s, 1, ATTN_WIDTH), BF16),
                   jax.ShapeDtypeStruct((s, WINDOW, KV_WIDTH), F32),
                   jax.ShapeDtypeStruct((s, WINDOW, KV_WIDTH), F32)],
        compiler_params=_params(1),
        name="attn_sample",
    )(qkv_rows, cache_k, cache_v, sinks_col, jnp.asarray(slopes),
      jnp.asarray(spread, BF16), jnp.asarray(spread.T, BF16))


def _wkv_kernel(p_ref, shift0_ref, s0_ref, mix_ref, w0_ref, dup_ref, a0_ref, iup_ref, gup_ref,
                kk_ref, ka_ref, rk_ref, lnw_ref, lnb_ref, rw_ref, sout_ref, s_scr, prev_scr,
                *, chunk):
    c = pl.program_id(1)
    n_chunks = pl.num_programs(1)
    bd_rows = HEADS_PER_GROUP * chunk
    groups = range(N_GROUPS)
    cols = [slice(g * MXU_DIM, (g + 1) * MXU_DIM) for g in groups]
    iota = lambda shape, dim: lax.broadcasted_iota(jnp.int32, shape, dim)
    same_head = iota((MXU_DIM, MXU_DIM), 0) // HEAD_DIM == iota((MXU_DIM, MXU_DIM), 1) // HEAD_DIM
    ones_bd = jnp.where(same_head, 1.0, 0.0).astype(BF16)
    tile_sel = jnp.where(iota((HEAD_DIM, MXU_DIM), 0) == iota((HEAD_DIM, MXU_DIM), 1) % HEAD_DIM, 1.0, 0.0).astype(BF16)
    fold_sel = jnp.where(iota((MXU_DIM, HEAD_DIM), 0) % HEAD_DIM == iota((MXU_DIM, HEAD_DIM), 1), 1.0, 0.0).astype(BF16)

    @pl.when(c == 0)
    def _():
        prev_scr[...] = shift0_ref[0]
        for g in groups:
            rows = s0_ref[0, cols[g], :]
            tiled = sum(_dot(part, tile_sel) for part in _split(rows, 3))
            s_scr[g] = jnp.where(same_head, tiled, 0.0)

    row = iota((chunk, 1), 0)

    def shifted(lo, width):
        cur = p_ref[0, :, lo:lo + width]
        prev = jnp.where(row == 0, prev_scr[:, lo:lo + width], pltpu.roll(cur, 1, axis=0))
        return cur + (prev - cur) * mix_ref[:, lo:lo + width]

    def seg_sum(x):
        return sum(_dot(part, ones_bd) for part in _split(x, 2))

    head_mask = [jnp.where(iota((chunk, MXU_DIM), 1) // HEAD_DIM == h, 1.0, 0.0).astype(BF16)
                 for h in range(HEADS_PER_GROUP)]

    def head_blocks(x):
        return jnp.concatenate([x * m for m in head_mask], axis=0)

    lora_in = shifted(LORA_OFF, LANES)
    lora_tanh = jnp.tanh(lora_in).astype(BF16)
    lora_lin = lora_in.astype(BF16)
    gate_sig = jax.nn.sigmoid(shifted(GATE_OFF, GATE_PAD)).astype(BF16)
    r = [shifted(g * MXU_DIM, MXU_DIM) for g in groups]
    k = [shifted(RWKV_WIDTH + g * MXU_DIM, MXU_DIM) for g in groups]
    v = [shifted(2 * RWKV_WIDTH + g * MXU_DIM, MXU_DIM) for g in groups]
    prev_scr[...] = p_ref[0, chunk - 1:chunk, :]

    logw = [-math.exp(-0.5) * jax.nn.sigmoid(w0_ref[:, cols[g]] + _dot(lora_tanh, dup_ref[:, cols[g]])) for g in groups]
    a = [jax.nn.sigmoid(a0_ref[:, cols[g]] + _dot(lora_lin, iup_ref[:, cols[g]])) for g in groups]
    gate = [_dot(gate_sig, gup_ref[:, cols[g]]) for g in groups]
    kk = [k[g] * kk_ref[:, cols[g]] for g in groups]
    kk = [kk[g] / jnp.maximum(jnp.sqrt(seg_sum(kk[g] * kk[g])), 1e-12) for g in groups]
    k = [k[g] * (1.0 + (a[g] - 1.0) * ka_ref[:, cols[g]]) for g in groups]

    tri = jnp.where(iota((chunk, chunk), 0) >= iota((chunk, chunk), 1), 1.0, 0.0).astype(BF16)
    cum = [sum(_dot(tri, part) for part in _split(logw[g], 3)) for g in groups]
    e_pos = [jnp.exp(x) for x in cum]
    e_neg = [jnp.exp(-x) for x in cum]
    a_t = [-kk[g] * jnp.exp(cum[g] - logw[g]) for g in groups]
    b_t = [kk[g] * a[g] * e_neg[g] for g in groups]
    k_t = [k[g] * e_neg[g] for g in groups]
    r_t = [r[g] * e_pos[g] for g in groups]
    total = [x[chunk - 1:chunk, :] for x in e_pos]
    ar_b = [jnp.concatenate([a_t[g], r_t[g]], axis=0).astype(BF16) for g in groups]
    xb = [head_blocks(x.astype(BF16)) for x in b_t]
    xk = [head_blocks(x.astype(BF16)) for x in k_t]
    v_bd = [head_blocks(x.astype(BF16)) for x in v]

    scores = [_dot_nt(ar_b[g], jnp.concatenate([xb[g], xk[g]], axis=0)) for g in groups]
    t_idx = iota((chunk, 2 * bd_rows), 0)
    s_idx = iota((chunk, 2 * bd_rows), 1) % chunk
    bd_row, bd_col = iota((bd_rows, bd_rows), 0), iota((bd_rows, bd_rows), 1)
    strict_bd = (bd_row // chunk == bd_col // chunk) & (bd_col % chunk < bd_row % chunk)
    eye_b = jnp.where(bd_row == bd_col, 1.0, 0.0).astype(BF16)
    a_ak = [jnp.where((s_idx < t_idx)[:, :bd_rows], x[:chunk, bd_rows:], 0.0).astype(BF16) for x in scores]
    a_r = [jnp.where(s_idx <= t_idx, x[chunk:], 0.0).astype(BF16) for x in scores]

    e_bd = [jnp.where(strict_bd, jnp.concatenate([x[:chunk, :bd_rows]] * HEADS_PER_GROUP, axis=0), 0.0)
            for x in scores]
    m_b = [x.astype(BF16) for x in e_bd]
    m_b = [_dot(x, x).astype(BF16) for x in m_b]
    levels = int(math.log2(chunk)) - 1
    for lvl in range(levels):
        t_b = [x.astype(BF16) + eye_b for x in e_bd]
        e_bd = [e_bd[g] + _dot(m_b[g], t_b[g]) for g in groups]
        if lvl < levels - 1:
            m_b = [_dot(x, x).astype(BF16) for x in m_b]
    e_cat = [sum(x[h * chunk:(h + 1) * chunk] for h in range(HEADS_PER_GROUP)).astype(BF16) for x in e_bd]

    s_b = [s_scr[g].astype(BF16) for g in groups]
    uy = [_dot_nt(ar_b[g], s_b[g]) for g in groups]
    rhs = [uy[g][:chunk] + _dot(a_ak[g], v_bd[g]) for g in groups]
    p = [rhs[g] + _dot(e_cat[g], head_blocks(rhs[g].astype(BF16))) for g in groups]
    p_b = [x.astype(BF16) for x in p]
    y = [uy[g][chunk:] + _dot(a_r[g], jnp.concatenate([head_blocks(p_b[g]), v_bd[g]], axis=0)) for g in groups]
    for g in groups:
        upd = _dot_tn(jnp.concatenate([p_b[g], v[g].astype(BF16)], axis=0),
                      jnp.concatenate([b_t[g] * total[g], k_t[g] * total[g]], axis=0).astype(BF16))
        s_scr[g] = s_scr[g] * total[g] + jnp.where(same_head, upd, 0.0)

    inv_n = 1.0 / HEAD_DIM
    mu = [seg_sum(x) * inv_n for x in y]
    dev = [y[g] - mu[g] for g in groups]
    var = [seg_sum(x * x) * inv_n for x in dev]
    bonus = [seg_sum(r[g] * k[g] * rk_ref[:, cols[g]]) * v[g] for g in groups]
    for g in groups:
        yn = dev[g] * lax.rsqrt(var[g] + GN_EPS) * lnw_ref[:, cols[g]] + lnb_ref[:, cols[g]]
        rw_ref[0, :, cols[g]] = ((yn + bonus[g]) * gate[g]).astype(BF16)

    @pl.when(c == n_chunks - 1)
    def _():
        for g in groups:
            folded = sum(_dot(part, fold_sel) for part in _split(s_scr[g], 3))
            sout_ref[0, cols[g], :] = folded


def _wkv(prw, shift0, s0, wp, layer, *, chunk):
    b, t, _ = prw.shape
    lay = lambda *blk: pl.BlockSpec((None,) + blk, lambda i, c: (layer,) + (0,) * len(blk))
    vec = lay(1, RWKV_WIDTH)
    return pl.pallas_call(
        functools.partial(_wkv_kernel, chunk=chunk),
        grid=(b, t // chunk),
        in_specs=[pl.BlockSpec((1, chunk, RWKV_PROJ_PAD), lambda i, c: (i, c, 0)),
                  pl.BlockSpec((1, 1, RWKV_PROJ_PAD), lambda i, c: (i, 0, 0)),
                  pl.BlockSpec((1, RWKV_WIDTH, HEAD_DIM), lambda i, c: (i, 0, 0)),
                  lay(1, RWKV_PROJ_PAD), vec, lay(LANES, RWKV_WIDTH), vec, lay(LANES, RWKV_WIDTH),
                  lay(GATE_PAD, RWKV_WIDTH), vec, vec, vec, vec, vec],
        out_specs=[pl.BlockSpec((1, chunk, RWKV_WIDTH), lambda i, c: (i, c, 0)),
                   pl.BlockSpec((1, RWKV_WIDTH, HEAD_DIM), lambda i, c: (i, 0, 0))],
        out_shape=[jax.ShapeDtypeStruct((b, t, RWKV_WIDTH), BF16),
                   jax.ShapeDtypeStruct((b, RWKV_WIDTH, HEAD_DIM), F32)],
        scratch_shapes=[pltpu.VMEM((N_GROUPS, MXU_DIM, MXU_DIM), F32),
                        pltpu.VMEM((1, RWKV_PROJ_PAD), F32)],
        compiler_params=_params(2),
        name="wkv7",
    )(prw, shift0, s0, wp["mix"], wp["w0"], wp["decay_up"], wp["a0"], wp["iclr_up"], wp["gate_up"],
      wp["k_k"], wp["k_a"], wp["r_k"], wp["ln_w"], wp["ln_b"])


SUBLANES = 8


def _wkv_step_kernel(p_ref, shift_ref, s_ref, mix_ref, w0_ref, dup_ref, a0_ref, iup_ref, gup_ref,
                     kk_ref, ka_ref, rk_ref, lnw_ref, lnb_ref, fold_ref, rw_ref, sout_ref):
    iota = lambda shape, dim: lax.broadcasted_iota(jnp.int32, shape, dim)
    groups = range(N_GROUPS)
    cols = [slice(g * MXU_DIM, (g + 1) * MXU_DIM) for g in groups]
    ones_bd = jnp.where(iota((MXU_DIM, MXU_DIM), 0) // HEAD_DIM == iota((MXU_DIM, MXU_DIM), 1) // HEAD_DIM,
                        1.0, 0.0).astype(BF16)

    def seg_sum(x):
        return jnp.concatenate([sum(_dot(part, ones_bd) for part in _split(x[:, cols[g]], 2)) for g in groups], axis=1)

    cur = jnp.broadcast_to(p_ref[0], (SUBLANES, RWKV_PROJ_PAD))
    prev = jnp.broadcast_to(shift_ref[0], (SUBLANES, RWKV_PROJ_PAD))
    p = cur + (prev - cur) * mix_ref[...]
    r = p[:, 0:RWKV_WIDTH]
    k = p[:, RWKV_WIDTH:2 * RWKV_WIDTH]
    v = p[:, 2 * RWKV_WIDTH:3 * RWKV_WIDTH]
    lora_in = p[:, LORA_OFF:LORA_OFF + LANES]
    gate_in = p[:, GATE_OFF:GATE_OFF + GATE_PAD]
    w = w0_ref[...] + _dot(jnp.tanh(lora_in).astype(BF16), dup_ref[...])
    decay = jnp.exp(-math.exp(-0.5) * jax.nn.sigmoid(w))
    a = jax.nn.sigmoid(a0_ref[...] + _dot(lora_in.astype(BF16), iup_ref[...]))
    gate = _dot(jax.nn.sigmoid(gate_in).astype(BF16), gup_ref[...])
    kk = k * kk_ref[...]
    kk = kk / jnp.maximum(jnp.sqrt(seg_sum(kk * kk)), 1e-12)
    k = k * (1.0 + (a - 1.0) * ka_ref[...])

    own = iota((N_RWKV_HEADS, RWKV_WIDTH), 0) == iota((N_RWKV_HEADS, RWKV_WIDTH), 1) // HEAD_DIM
    spread = lambda x: jnp.where(own, jnp.broadcast_to(x[0:1], (N_RWKV_HEADS, RWKV_WIDTH)), 0.0)
    stacked = jnp.concatenate([spread(decay), spread(-kk), spread(kk * a), spread(k), spread(r)], axis=0)
    per_head = sum(_dot(part, fold_ref[...]) for part in _split(stacked, 3))
    w_h, a_h, b_h, k_h, r_h = (per_head[i * N_RWKV_HEADS:(i + 1) * N_RWKV_HEADS] for i in range(5))
    rows = lambda x: jnp.concatenate(
        [jnp.broadcast_to(x[h:h + 1, :], (HEAD_DIM, HEAD_DIM)) for h in range(N_RWKV_HEADS)], axis=0)

    eye = iota((LANES, LANES), 0) == iota((LANES, LANES), 1)
    ones_cols = jnp.ones((LANES, HEAD_DIM), BF16)
    v_col = jnp.concatenate(
        [sum(_dot(part, ones_cols) for part in
             _split(jnp.where(eye, jnp.broadcast_to(v[0:1, t * LANES:(t + 1) * LANES], (LANES, LANES)), 0.0), 3))
         for t in range(RWKV_WIDTH // LANES)], axis=0)

    s = s_ref[0]
    sa = jnp.sum(s * rows(a_h), axis=-1, keepdims=True)
    s_new = s * rows(w_h) + sa * rows(b_h) + v_col * rows(k_h)
    sout_ref[0] = s_new
    y_heads = _dot_nt(r_h.astype(BF16), s_new.astype(BF16))
    y = jnp.broadcast_to(jnp.sum(jnp.where(own, y_heads, 0.0), axis=0, keepdims=True), (SUBLANES, RWKV_WIDTH))

    inv_n = 1.0 / HEAD_DIM
    dev = y - seg_sum(y) * inv_n
    var = seg_sum(dev * dev) * inv_n
    y = dev * lax.rsqrt(var + GN_EPS) * lnw_ref[...] + lnb_ref[...]
    bonus = seg_sum(r * k * rk_ref[...]) * v
    rw_ref[0] = ((y + bonus) * gate)[0:1].astype(BF16)


def _wkv_step(prw_rows, shift_rows, state, wp, layer):
    s = prw_rows.shape[0]
    lay = lambda *blk: pl.BlockSpec((None,) + blk, lambda i: (layer,) + (0,) * len(blk))
    vec = lay(1, RWKV_WIDTH)
    fold = np.tile(np.eye(HEAD_DIM, dtype=np.float32), (N_RWKV_HEADS, 1))
    return pl.pallas_call(
        _wkv_step_kernel,
        grid=(s,),
        in_specs=[pl.BlockSpec((1, 1, RWKV_PROJ_PAD), lambda i: (i, 0, 0)),
                  pl.BlockSpec((None, 1, 1, RWKV_PROJ_PAD), lambda i: (layer, i, 0, 0)),
                  pl.BlockSpec((None, 1, RWKV_WIDTH, HEAD_DIM), lambda i: (layer, i, 0, 0)),
                  lay(1, RWKV_PROJ_PAD), vec, lay(LANES, RWKV_WIDTH), vec, lay(LANES, RWKV_WIDTH),
                  lay(GATE_PAD, RWKV_WIDTH), vec, vec, vec, vec, vec,
                  pl.BlockSpec((RWKV_WIDTH, HEAD_DIM), lambda i: (0, 0))],
        out_specs=[pl.BlockSpec((1, 1, RWKV_WIDTH), lambda i: (i, 0, 0)),
                   pl.BlockSpec((1, RWKV_WIDTH, HEAD_DIM), lambda i: (i, 0, 0))],
        out_shape=[jax.ShapeDtypeStruct((s, 1, RWKV_WIDTH), BF16),
                   jax.ShapeDtypeStruct((s, RWKV_WIDTH, HEAD_DIM), F32)],
        compiler_params=_params(1),
        name="wkv7_step",
    )(prw_rows, shift_rows, state, wp["mix"], wp["w0"], wp["decay_up"], wp["a0"], wp["iclr_up"], wp["gate_up"],
      wp["k_k"], wp["k_a"], wp["r_k"], wp["ln_w"], wp["ln_b"], jnp.asarray(fold, BF16))


def _out_proj_kernel(att_ref, rw_ref, x_ref, gt_ref, sh_ref, sc_ref, g_ref, w_ref, x1_ref, h_ref):
    y = _dot(att_ref[0], w_ref[0:ATTN_WIDTH, :]) + _dot(rw_ref[0], w_ref[ATTN_WIDTH:, :])
    x1 = x_ref[0] + gt_ref[0] * y
    x1_ref[0] = x1
    h_ref[0] = _rmsnorm_mod(x1, g_ref[...], sh_ref[0], sc_ref[0]).astype(BF16)


def _out_proj(att, rw, x, gate, shift, scale, g, w, layer, *, tm):
    b, t, d = x.shape
    r = gate.shape[1]
    row = lambda width: pl.BlockSpec((1, tm, width), lambda i, m: (i, m, 0))
    mod = pl.BlockSpec((1, r, d), lambda i, m: (i, 0, 0))
    return pl.pallas_call(
        _out_proj_kernel,
        grid=(b, t // tm),
        in_specs=[row(ATTN_WIDTH), row(RWKV_WIDTH), row(d), mod, mod, mod,
                  pl.BlockSpec((None, 1, d), lambda i, m: (layer, 0, 0)),
                  pl.BlockSpec((None, d, d), lambda i, m: (layer, 0, 0))],
        out_specs=[row(d), row(d)],
        out_shape=[jax.ShapeDtypeStruct((b, t, d), F32), jax.ShapeDtypeStruct((b, t, d), BF16)],
        compiler_params=_params(2),
        name="out_proj",
    )(att, rw, x, gate, shift, scale, g, w)


def _ffn_kernel(h_ref, x_ref, gt_ref, wg_ref, wu_ref, wo_ref, gf_ref, o_ref, acc_ref, *, final_norm):
    j = pl.program_id(2)

    @pl.when(j == 0)
    def _():
        acc_ref[...] = jnp.zeros_like(acc_ref)

    h = h_ref[0]
    gate = _dot(h, wg_ref[...])
    up = _dot(h, wu_ref[...])
    act = (gate * jax.nn.sigmoid(gate) * up).astype(BF16)
    acc_ref[...] += _dot(act, wo_ref[...])

    @pl.when(j == pl.num_programs(2) - 1)
    def _():
        x2 = x_ref[0] + gt_ref[0] * acc_ref[...]
        if final_norm:
            ms = jnp.mean(x2 * x2, axis=-1, keepdims=True)
            x2 = x2 * lax.rsqrt(ms + RMS_EPS) * gf_ref[...]
        o_ref[0] = x2


def _ffn(h, x, gate, w_in, w_out, g_final, layer, *, tm, th, final_norm):
    b, t, d = x.shape
    r = gate.shape[1]
    n_h = FFN_HIDDEN // th
    return pl.pallas_call(
        functools.partial(_ffn_kernel, final_norm=final_norm),
        grid=(b, t // tm, n_h),
        in_specs=[pl.BlockSpec((1, tm, d), lambda i, m, j: (i, m, 0)),
                  pl.BlockSpec((1, tm, d), lambda i, m, j: (i, m, 0)),
                  pl.BlockSpec((1, r, d), lambda i, m, j: (i, 0, 0)),
                  pl.BlockSpec((None, d, th), lambda i, m, j: (layer, 0, j)),
                  pl.BlockSpec((None, d, th), lambda i, m, j: (layer, 0, j + n_h)),
                  pl.BlockSpec((None, th, d), lambda i, m, j: (layer, j, 0)),
                  pl.BlockSpec((1, d), lambda i, m, j: (0, 0))],
        out_specs=pl.BlockSpec((1, tm, d), lambda i, m, j: (i, m, 0)),
        out_shape=jax.ShapeDtypeStruct((b, t, d), F32),
        scratch_shapes=[pltpu.VMEM((tm, d), F32)],
        compiler_params=_params(3),
        name="ffn",
    )(h, x, gate, w_in, w_in, w_out, g_final)


def _tiles(t_prompt, n_sample):
    tiles = dict(proj_tm=1024, proj_tn=512, out_tm=512, ffn_tm=512, ffn_th=512, wkv_chunk=64)
    assert t_prompt % tiles["proj_tm"] == 0 and t_prompt % tiles["out_tm"] == 0 and t_prompt % tiles["ffn_tm"] == 0
    assert t_prompt % tiles["wkv_chunk"] == 0 and t_prompt % WINDOW == 0
    assert QKV_COLS % tiles["proj_tn"] == 0 and RWKV_PROJ_PAD % tiles["proj_tn"] == 0
    assert FFN_HIDDEN % tiles["ffn_th"] == 0 and n_sample % SUBLANES == 0
    return tiles


def _pad_cols(x, width):
    return jnp.pad(x, [(0, 0)] * (x.ndim - 1) + [(0, width - x.shape[-1])])


def _pad_rows(x, rows, offset=0):
    return jnp.pad(x, [(0, 0), (offset, rows - offset - x.shape[1]), (0, 0)])


def kernel(x_prompt, x_sample, cache_k, cache_v, state_wkv, state_shift, c_prompt, c_sample, w_ada, b_ada, g_norm_mix, g_norm_ffn, w_in, w_out, attn_sinks, mix_shift, decay_w0, decay_up, iclr_a0, iclr_up, gate_up, k_k, k_a, r_k, ln_x_w, ln_x_b, w_ffn_in, w_ffn_out, g_norm_final):
    n_layers = w_in.shape[0]
    bp, tp, d = x_prompt.shape
    bs = x_sample.shape[0]
    tiles = _tiles(tp, bs)

    w_in_b = _pad_cols(w_in, QKV_COLS + RWKV_PROJ_PAD).astype(BF16)
    w_out_b = w_out.astype(BF16)
    w_ffn_in_b = w_ffn_in.astype(BF16)
    w_ffn_out_b = w_ffn_out.astype(BF16)
    vec = lambda x: x.reshape(n_layers, 1, -1)
    wp = dict(
        mix=vec(_pad_cols(mix_shift, RWKV_PROJ_PAD)), w0=vec(decay_w0), a0=vec(iclr_a0),
        decay_up=_pad_rows(decay_up, LANES).astype(BF16),
        iclr_up=_pad_rows(iclr_up, LANES, D_DECAY_LORA).astype(BF16),
        gate_up=_pad_rows(gate_up, GATE_PAD).astype(BF16),
        k_k=vec(k_k), k_a=vec(k_a), r_k=vec(r_k), ln_w=vec(ln_x_w), ln_b=vec(ln_x_b))
    g_mix, g_ffn = vec(g_norm_mix), vec(g_norm_ffn)
    g_final = g_norm_final.reshape(1, d)
    sinks_flat = attn_sinks.reshape(-1)
    sinks_col = attn_sinks.reshape(n_layers, N_Q_HEADS, 1)
    cache_k2 = cache_k.reshape(n_layers, bs, WINDOW, KV_WIDTH)
    cache_v2 = cache_v.reshape(n_layers, bs, WINDOW, KV_WIDTH)
    shift_rows = _pad_cols(state_shift, RWKV_PROJ_PAD).reshape(n_layers, bs, 1, RWKV_PROJ_PAD)
    state_rows = state_wkv.reshape(n_layers, bs, RWKV_WIDTH, HEAD_DIM)
    zero_shift = jnp.zeros((bp, 1, RWKV_PROJ_PAD), F32)
    zero_state = jnp.zeros((bp, RWKV_WIDTH, HEAD_DIM), F32)

    mod = _ada(jnp.concatenate([c_prompt, c_sample], axis=0), w_ada, b_ada)

    def mods(layer):
        parts = [mod[layer, :, i * d:(i + 1) * d] for i in range(6)]
        return ([p[:bp].reshape(bp, 1, d) for p in parts], [p[bp:].reshape(1, bs, d) for p in parts])

    xp = x_prompt
    xs = x_sample.reshape(1, bs, d)
    outs_p = [[] for _ in range(4)]
    outs_s = [[] for _ in range(4)]
    for layer in range(n_layers):
        last = layer == n_layers - 1
        (sh1, sc1, gt1, sh2, sc2, gt2), (ssh1, ssc1, sgt1, ssh2, ssc2, sgt2) = mods(layer)

        qkv, prw = _norm_proj(xp, sh1, sc1, g_mix, w_in_b, layer, tm=tiles["proj_tm"], tn=tiles["proj_tn"])
        att = _attn_prompt(qkv, sinks_flat, layer)
        rw, state = _wkv(prw, zero_shift, zero_state, wp, layer, chunk=tiles["wkv_chunk"])
        x1, h2 = _out_proj(att, rw, xp, gt1, sh2, sc2, g_ffn, w_out_b, layer, tm=tiles["out_tm"])
        xp = _ffn(h2, x1, gt2, w_ffn_in_b, w_ffn_out_b, g_final, layer, tm=tiles["ffn_tm"], th=tiles["ffn_th"],
                  final_norm=last)
        outs_p[0].append(qkv[:, tp - WINDOW:, ATTN_WIDTH:ATTN_WIDTH + KV_WIDTH].reshape(bp, WINDOW, N_KV_HEADS, HEAD_DIM))
        outs_p[1].append(qkv[:, tp - WINDOW:, ATTN_WIDTH + KV_WIDTH:].reshape(bp, WINDOW, N_KV_HEADS, HEAD_DIM))
        outs_p[2].append(state.reshape(bp, N_RWKV_HEADS, HEAD_DIM, HEAD_DIM))
        outs_p[3].append(prw[:, tp - 1, :RWKV_PROJ])

        qkv_s, prw_s = _norm_proj(xs, ssh1, ssc1, g_mix, w_in_b, layer, tm=bs, tn=tiles["proj_tn"])
        att_s, nk, nv = _attn_sample(qkv_s.reshape(bs, 1, QKV_COLS), cache_k2, cache_v2, sinks_col, layer)
        prw_tok = prw_s.reshape(bs, 1, RWKV_PROJ_PAD)
        rw_s, state_s = _wkv_step(prw_tok, shift_rows, state_rows, wp, layer)
        x1_s, h2_s = _out_proj(att_s.reshape(1, bs, ATTN_WIDTH), rw_s.reshape(1, bs, RWKV_WIDTH), xs,
                               sgt1, ssh2, ssc2, g_ffn, w_out_b, layer, tm=bs)
        xs = _ffn(h2_s, x1_s, sgt2, w_ffn_in_b, w_ffn_out_b, g_final, layer, tm=bs, th=tiles["ffn_th"],
                  final_norm=last)
        outs_s[0].append(nk.reshape(bs, WINDOW, N_KV_HEADS, HEAD_DIM))
        outs_s[1].append(nv.reshape(bs, WINDOW, N_KV_HEADS, HEAD_DIM))
        outs_s[2].append(state_s.reshape(bs, N_RWKV_HEADS, HEAD_DIM, HEAD_DIM))
        outs_s[3].append(prw_tok[:, 0, :RWKV_PROJ])

    stack = lambda xs_: jnp.stack(xs_, axis=0)
    return (xp, xs.reshape(bs, 1, d),
            stack(outs_p[0]), stack(outs_p[1]), stack(outs_p[2]), stack(outs_p[3]),
            stack(outs_s[0]), stack(outs_s[1]), stack(outs_s[2]), stack(outs_s[3]))
```

```python
import functools
import math

import numpy as np
import jax
import jax.numpy as jnp
from jax import lax
from jax.experimental import pallas as pl
from jax.experimental.pallas import tpu as pltpu

F32 = jnp.float32
BF16 = jnp.bfloat16

D_MODEL = 2048
HEAD_DIM = 64
ATTN_WIDTH = D_MODEL // 2
RWKV_WIDTH = D_MODEL - ATTN_WIDTH
N_Q_HEADS = ATTN_WIDTH // HEAD_DIM
N_KV_HEADS = N_Q_HEADS // 4
GQA_REP = N_Q_HEADS // N_KV_HEADS
KV_WIDTH = N_KV_HEADS * HEAD_DIM
N_RWKV_HEADS = RWKV_WIDTH // HEAD_DIM
WINDOW = 128
D_DECAY_LORA = 64
D_ICLR_LORA = 64
D_GATE_LORA = 160
RWKV_PROJ = 3 * RWKV_WIDTH + D_DECAY_LORA + D_ICLR_LORA + D_GATE_LORA
QKV_COLS = ATTN_WIDTH + 2 * KV_WIDTH
FFN_HIDDEN = 5632
RMS_EPS = 1e-5
GN_EPS = 64e-5
NEG_INF = -1e30

LANES = 128
MXU_DIM = 256
VMEM_LIMIT_BYTES = 56 * 1024 * 1024

LORA_OFF = 3 * RWKV_WIDTH
GATE_OFF = LORA_OFF + D_DECAY_LORA + D_ICLR_LORA
GATE_PAD = 2 * LANES
RWKV_PROJ_PAD = 3584
HEADS_PER_GROUP = MXU_DIM // HEAD_DIM
N_GROUPS = N_RWKV_HEADS // HEADS_PER_GROUP

ALIBI_SLOPES = tuple(float(s) for s in np.exp2(-8.0 * np.arange(1, N_Q_HEADS + 1, dtype=np.float32) / N_Q_HEADS))


def _params(n_axes):
    return pltpu.CompilerParams(dimension_semantics=("arbitrary",) * n_axes,
                                vmem_limit_bytes=VMEM_LIMIT_BYTES)


def _dot(a, b):
    return jnp.dot(a, b, preferred_element_type=F32)


def _dot_nt(a, b):
    return lax.dot_general(a, b, (((1,), (1,)), ((), ())), preferred_element_type=F32)


def _dot_tn(a, b):
    return lax.dot_general(a, b, (((0,), (0,)), ((), ())), preferred_element_type=F32)


def _split(x, parts):
    out = []
    for _ in range(parts):
        p = x.astype(BF16)
        out.append(p)
        x = x - p.astype(F32)
    return out


def _rmsnorm_mod(x, g, shift, scale):
    ms = jnp.mean(x * x, axis=-1, keepdims=True)
    y = x * lax.rsqrt(ms + RMS_EPS) * g
    return y * (1.0 + scale) + shift


def _ada_kernel(c_ref, w_ref, b_ref, o_ref):
    c = c_ref[...]
    s = (c * jax.nn.sigmoid(c)).astype(BF16)
    o_ref[...] = _dot(s, w_ref[...].astype(BF16)) + b_ref[...]


def _ada(c_all, w_ada, b_ada):
    n_layers, d, n = w_ada.shape
    rows = c_all.shape[0]
    tn = 1024
    return pl.pallas_call(
        _ada_kernel,
        grid=(n_layers, n // tn),
        in_specs=[pl.BlockSpec((rows, d), lambda l, j: (0, 0)),
                  pl.BlockSpec((None, d, tn), lambda l, j: (l, 0, j)),
                  pl.BlockSpec((None, 1, tn), lambda l, j: (l, 0, j))],
        out_specs=pl.BlockSpec((None, rows, tn), lambda l, j: (l, 0, j)),
        out_shape=jax.ShapeDtypeStruct((n_layers, rows, n), F32),
        compiler_params=_params(2),
        name="ada_mod",
    )(c_all, w_ada, b_ada.reshape(n_layers, 1, n))


NORM_STRIP = 32


def _norm_proj_kernel(x_ref, sh_ref, sc_ref, g_ref, w_ref, qkv_ref, prw_ref, h_scr, *, n_qkv_tiles):
    j = pl.program_id(2)
    tm = h_scr.shape[0]

    @pl.when(j == 0)
    def _():
        gain = g_ref[...] * (1.0 + sc_ref[0])
        shift = sh_ref[0]
        if gain.shape[0] == 1:
            def strip(i, carry):
                rows = pl.ds(pl.multiple_of(i * NORM_STRIP, NORM_STRIP), NORM_STRIP)
                x = x_ref[0, rows, :]
                ms = jnp.mean(x * x, axis=-1, keepdims=True)
                h_scr[rows, :] = (x * lax.rsqrt(ms + RMS_EPS) * gain + shift).astype(BF16)
                return carry
            lax.fori_loop(0, tm // NORM_STRIP, strip, 0, unroll=4)
        else:
            x = x_ref[0]
            ms = jnp.mean(x * x, axis=-1, keepdims=True)
            h_scr[...] = (x * lax.rsqrt(ms + RMS_EPS) * gain + shift).astype(BF16)

    @pl.when(j < n_qkv_tiles)
    def _():
        qkv_ref[0] = _dot(h_scr[...], w_ref[...])

    @pl.when(j >= n_qkv_tiles)
    def _():
        prw_ref[0] = _dot(h_scr[...], w_ref[...])


def _norm_proj(x, shift, scale, g, w, layer, *, tm, tn):
    b, t, d = x.shape
    r = shift.shape[1]
    n_qkv_tiles = QKV_COLS // tn
    n_tiles = (QKV_COLS + RWKV_PROJ_PAD) // tn
    return pl.pallas_call(
        functools.partial(_norm_proj_kernel, n_qkv_tiles=n_qkv_tiles),
        grid=(b, t // tm, n_tiles),
        in_specs=[pl.BlockSpec((1, tm, d), lambda i, m, j: (i, m, 0)),
                  pl.BlockSpec((1, r, d), lambda i, m, j: (i, 0, 0)),
                  pl.BlockSpec((1, r, d), lambda i, m, j: (i, 0, 0)),
                  pl.BlockSpec((None, 1, d), lambda i, m, j: (layer, 0, 0)),
                  pl.BlockSpec((None, d, tn), lambda i, m, j: (layer, 0, j))],
        out_specs=[pl.BlockSpec((1, tm, tn), lambda i, m, j: (i, m, jnp.minimum(j, n_qkv_tiles - 1))),
                   pl.BlockSpec((1, tm, tn), lambda i, m, j: (i, m, jnp.maximum(j - n_qkv_tiles, 0)))],
        out_shape=[jax.ShapeDtypeStruct((b, t, QKV_COLS), F32),
                   jax.ShapeDtypeStruct((b, t, RWKV_PROJ_PAD), F32)],
        scratch_shapes=[pltpu.VMEM((tm, d), BF16)],
        compiler_params=_params(3),
        name="norm_proj",
    )(x, shift, scale, g, w)


LOG2E = math.log2(math.e)


def _attn_bias_table():
    t = np.arange(WINDOW)[:, None]
    j = np.arange(2 * WINDOW)[None, :]
    dist = t + WINDOW - j
    valid = (dist >= 0) & (dist <= WINDOW)
    slopes = np.asarray(ALIBI_SLOPES, np.float32)[:, None, None]
    bias = np.where(valid[None], -slopes * dist[None].astype(np.float32) * np.float32(LOG2E), np.float32(NEG_INF))
    first = np.where((j >= WINDOW)[None], bias, np.float32(NEG_INF))
    return np.stack([first, bias]).astype(np.float32)


def _attn_prompt_kernel(sink_ref, bias_ref, q_ref, kc_ref, vc_ref, kp_ref, vp_ref, o_ref, *, layer):
    q = q_ref[0] * (HEAD_DIM ** -0.5 * LOG2E)
    kc, vc, kp, vp = kc_ref[0], vc_ref[0], kp_ref[0], vp_ref[0]
    outs = []
    for g in range(N_KV_HEADS):
        ksl = slice(g * HEAD_DIM, (g + 1) * HEAD_DIM)
        k_band = jnp.concatenate([kp[:, ksl], kc[:, ksl]], axis=0).astype(BF16)
        v_band = jnp.concatenate([vp[:, ksl], vc[:, ksl]], axis=0).astype(BF16)
        q_rep = jnp.concatenate(
            [q[:, (g * GQA_REP + r) * HEAD_DIM:(g * GQA_REP + r + 1) * HEAD_DIM] for r in range(GQA_REP)],
            axis=0).astype(BF16)
        logits = _dot_nt(q_rep, k_band)
        for r in range(GQA_REP):
            h = g * GQA_REP + r
            lg = logits[r * WINDOW:(r + 1) * WINDOW] + bias_ref[0, h]
            sink = sink_ref[layer * N_Q_HEADS + h] * LOG2E
            m = jnp.maximum(jnp.max(lg, axis=-1, keepdims=True), sink)
            p = jnp.exp2(lg - m)
            den = jnp.sum(p, axis=-1, keepdims=True) + jnp.exp2(sink - m)
            outs.append(_dot((p * (1.0 / den)).astype(BF16), v_band))
    o_ref[0] = jnp.concatenate(outs, axis=1).astype(BF16)


def _attn_prompt(qkv, sinks_flat, layer):
    b, t, _ = qkv.shape
    kcol = ATTN_WIDTH // KV_WIDTH
    prev = lambda i, m: (i, jnp.maximum(m - 1, 0), kcol)
    prev_v = lambda i, m: (i, jnp.maximum(m - 1, 0), kcol + 1)
    return pl.pallas_call(
        functools.partial(_attn_prompt_kernel, layer=layer),
        grid=(b, t // WINDOW),
        in_specs=[pl.BlockSpec(memory_space=pltpu.SMEM),
                  pl.BlockSpec((1, N_Q_HEADS, WINDOW, 2 * WINDOW), lambda i, m: (jnp.minimum(m, 1), 0, 0, 0)),
                  pl.BlockSpec((1, WINDOW, ATTN_WIDTH), lambda i, m: (i, m, 0)),
                  pl.BlockSpec((1, WINDOW, KV_WIDTH), lambda i, m: (i, m, kcol)),
                  pl.BlockSpec((1, WINDOW, KV_WIDTH), lambda i, m: (i, m, kcol + 1)),
                  pl.BlockSpec((1, WINDOW, KV_WIDTH), prev),
                  pl.BlockSpec((1, WINDOW, KV_WIDTH), prev_v)],
        out_specs=pl.BlockSpec((1, WINDOW, ATTN_WIDTH), lambda i, m: (i, m, 0)),
        out_shape=jax.ShapeDtypeStruct((b, t, ATTN_WIDTH), BF16),
        compiler_params=_params(2),
        name="attn_prompt",
    )(sinks_flat, jnp.asarray(_attn_bias_table()), qkv, qkv, qkv, qkv, qkv)


def _attn_sample_kernel(row_ref, kc_ref, vc_ref, sink_ref, slope_ref, spread_ref, gather_ref,
                        o_ref, nk_ref, nv_ref):
    row = row_ref[0]
    q = row[:, :ATTN_WIDTH] * (HEAD_DIM ** -0.5)
    k_new = row[:, ATTN_WIDTH:ATTN_WIDTH + KV_WIDTH]
    v_new = row[:, ATTN_WIDTH + KV_WIDTH:]
    kc, vc = kc_ref[0], vc_ref[0]
    head_row = lax.broadcasted_iota(jnp.int32, (N_Q_HEADS, ATTN_WIDTH), 0)
    head_lane = lax.broadcasted_iota(jnp.int32, (N_Q_HEADS, ATTN_WIDTH), 1) // HEAD_DIM
    own = head_row == head_lane
    q_rows = jnp.where(own, jnp.broadcast_to(q, (N_Q_HEADS, ATTN_WIDTH)), 0.0).astype(BF16)
    q_grp = _dot(q_rows, spread_ref[...])
    k_new_b = k_new.astype(BF16).astype(F32)
    lg = _dot_nt(q_grp.astype(BF16), kc.astype(BF16))
    lg_new = jnp.sum(q_grp * k_new_b, axis=-1, keepdims=True)
    dist = (WINDOW - lax.broadcasted_iota(jnp.int32, (N_Q_HEADS, WINDOW), 1)).astype(F32)
    lg = lg - slope_ref[...] * dist
    sink = sink_ref[...]
    m = jnp.maximum(jnp.maximum(jnp.max(lg, axis=-1, keepdims=True), lg_new), sink)
    p = jnp.exp(lg - m)
    p_new = jnp.exp(lg_new - m)
    den = jnp.sum(p, axis=-1, keepdims=True) + p_new + jnp.exp(sink - m)
    out = _dot((p / den).astype(BF16), vc.astype(BF16))
    out = out + (p_new / den).astype(BF16).astype(F32) * v_new.astype(BF16).astype(F32)
    grp_row = lax.broadcasted_iota(jnp.int32, (N_Q_HEADS, KV_WIDTH), 0) // GQA_REP
    grp_lane = lax.broadcasted_iota(jnp.int32, (N_Q_HEADS, KV_WIDTH), 1) // HEAD_DIM
    out = jnp.where(grp_row == grp_lane, out, 0.0).astype(BF16)
    full = _dot(out, gather_ref[...])
    o_ref[0] = jnp.sum(jnp.where(own, full, 0.0), axis=0, keepdims=True).astype(BF16)
    last = lax.broadcasted_iota(jnp.int32, (WINDOW, KV_WIDTH), 0) == WINDOW - 1
    nk_ref[0] = jnp.where(last, k_new, pltpu.roll(kc, WINDOW - 1, axis=0))
    nv_ref[0] = jnp.where(last, v_new, pltpu.roll(vc, WINDOW - 1, axis=0))


def _attn_sample(qkv_rows, cache_k, cache_v, sinks_col, layer):
    s = qkv_rows.shape[0]
    spread = np.zeros((ATTN_WIDTH, KV_WIDTH), np.float32)
    for h in range(N_Q_HEADS):
        for d in range(HEAD_DIM):
            spread[h * HEAD_DIM + d, (h // GQA_REP) * HEAD_DIM + d] = 1.0
    slopes = np.asarray(ALIBI_SLOPES, np.float32).reshape(N_Q_HEADS, 1)
    win = (1, WINDOW, KV_WIDTH)
    return pl.pallas_call(
        _attn_sample_kernel,
        grid=(s,),
        in_specs=[pl.BlockSpec((1, 1, QKV_COLS), lambda i: (i, 0, 0)),
                  pl.BlockSpec((None,) + win, lambda i: (layer, i, 0, 0)),
                  pl.BlockSpec((None,) + win, lambda i: (layer, i, 0, 0)),
                  pl.BlockSpec((None, N_Q_HEADS, 1), lambda i: (layer, 0, 0)),
                  pl.BlockSpec((N_Q_HEADS, 1), lambda i: (0, 0)),
                  pl.BlockSpec((ATTN_WIDTH, KV_WIDTH), lambda i: (0, 0)),
                  pl.BlockSpec((KV_WIDTH, ATTN_WIDTH), lambda i: (0, 0))],
        out_specs=[pl.BlockSpec((1, 1, ATTN_WIDTH), lambda i: (i, 0, 0)),
                   pl.BlockSpec(win, lambda i: (i, 0, 0)),
                   pl.BlockSpec(win, lambda i: (i, 0, 0))],
        out_shape=[jax.ShapeDtypeStruct((s, 1, ATTN_WIDTH), BF16),
                   jax.ShapeDtypeStruct((s, WINDOW, KV_WIDTH), F32),
                   jax.ShapeDtypeStruct((s, WINDOW, KV_WIDTH), F32)],
        compiler_params=_params(1),
        name="attn_sample",
    )(qkv_rows, cache_k, cache_v, sinks_col, jnp.asarray(slopes),
      jnp.asarray(spread, BF16), jnp.asarray(spread.T, BF16))


def _wkv_kernel(p_ref, shift0_ref, s0_ref, mix_ref, w0_ref, dup_ref, a0_ref, iup_ref, gup_ref,
                kk_ref, ka_ref, rk_ref, lnw_ref, lnb_ref, rw_ref, sout_ref, s_scr, prev_scr,
                arb_scr, aak_scr, ar_scr, ecat_scr, vbd_scr, vb_scr, bk_scr, total_scr, gate_scr, bonus_scr,
                *, chunk):
    step = pl.program_id(1)
    n_steps = pl.num_programs(1)
    stage = (arb_scr, aak_scr, ar_scr, ecat_scr, vbd_scr, vb_scr, bk_scr, total_scr, gate_scr, bonus_scr)
    bd_rows = HEADS_PER_GROUP * chunk
    groups = range(N_GROUPS)
    cols = [slice(g * MXU_DIM, (g + 1) * MXU_DIM) for g in groups]
    iota = lambda shape, dim: lax.broadcasted_iota(jnp.int32, shape, dim)
    same_head = iota((MXU_DIM, MXU_DIM), 0) // HEAD_DIM == iota((MXU_DIM, MXU_DIM), 1) // HEAD_DIM
    ones_bd = jnp.where(same_head, 1.0, 0.0).astype(BF16)
    tile_sel = jnp.where(iota((HEAD_DIM, MXU_DIM), 0) == iota((HEAD_DIM, MXU_DIM), 1) % HEAD_DIM, 1.0, 0.0).astype(BF16)
    fold_sel = jnp.where(iota((MXU_DIM, HEAD_DIM), 0) % HEAD_DIM == iota((MXU_DIM, HEAD_DIM), 1), 1.0, 0.0).astype(BF16)

    @pl.when(step == 0)
    def _():
        prev_scr[...] = shift0_ref[0]
        for g in groups:
            rows = s0_ref[0, cols[g], :]
            tiled = sum(_dot(part, tile_sel) for part in _split(rows, 3))
            s_scr[g] = jnp.where(same_head, tiled, 0.0)
        for ref in stage:
            ref[...] = jnp.zeros_like(ref)
        total_scr[...] = jnp.ones_like(total_scr)

    row = iota((chunk, 1), 0)

    def shifted(lo, width):
        cur = p_ref[0, :, lo:lo + width]
        prev = jnp.where(row == 0, prev_scr[:, lo:lo + width], pltpu.roll(cur, 1, axis=0))
        return cur + (prev - cur) * mix_ref[:, lo:lo + width]

    def seg_sum(x):
        return _dot(x.astype(BF16), ones_bd)

    head_mask = [jnp.where(iota((chunk, MXU_DIM), 1) // HEAD_DIM == h, 1.0, 0.0).astype(BF16)
                 for h in range(HEADS_PER_GROUP)]

    def head_blocks(x):
        return jnp.concatenate([x * m for m in head_mask], axis=0)

    st_ar, st_ak, st_r, st_e, st_vbd, st_v, st_bk, st_total, st_gate, st_bonus = (
        [ref[g] for g in groups] for ref in stage)
    s_b = [s_scr[g].astype(BF16) for g in groups]
    uy = [_dot_nt(st_ar[g], s_b[g]) for g in groups]

    lora_in = shifted(LORA_OFF, LANES)
    lora_tanh = jnp.tanh(lora_in).astype(BF16)
    lora_lin = lora_in.astype(BF16)
    gate_sig = jax.nn.sigmoid(shifted(GATE_OFF, GATE_PAD)).astype(BF16)
    r = [shifted(g * MXU_DIM, MXU_DIM) for g in groups]

    rhs = [uy[g][:chunk] + _dot(st_ak[g], st_vbd[g]) for g in groups]

    k = [shifted(RWKV_WIDTH + g * MXU_DIM, MXU_DIM) for g in groups]
    v = [shifted(2 * RWKV_WIDTH + g * MXU_DIM, MXU_DIM) for g in groups]
    prev_scr[...] = p_ref[0, chunk - 1:chunk, :]

    p = [rhs[g] + _dot(st_e[g], head_blocks(rhs[g].astype(BF16))) for g in groups]
    p_b = [x.astype(BF16) for x in p]

    logw = [-math.exp(-0.5) * jax.nn.sigmoid(w0_ref[:, cols[g]] + _dot(lora_tanh, dup_ref[:, cols[g]])) for g in groups]
    a = [jax.nn.sigmoid(a0_ref[:, cols[g]] + _dot(lora_lin, iup_ref[:, cols[g]])) for g in groups]
    gate = [_dot(gate_sig, gup_ref[:, cols[g]]) for g in groups]

    y = [uy[g][chunk:] + _dot(st_r[g], jnp.concatenate([head_blocks(p_b[g]), st_vbd[g]], axis=0)) for g in groups]
    for g in groups:
        upd = _dot_tn(jnp.concatenate([p_b[g], st_v[g]], axis=0), st_bk[g])
        s_scr[g] = s_scr[g] * st_total[g] + jnp.where(same_head, upd, 0.0)

    kk = [k[g] * kk_ref[:, cols[g]] for g in groups]
    kk = [kk[g] / jnp.maximum(jnp.sqrt(seg_sum(kk[g] * kk[g])), 1e-12) for g in groups]
    k = [k[g] * (1.0 + (a[g] - 1.0) * ka_ref[:, cols[g]]) for g in groups]

    inv_n = 1.0 / HEAD_DIM
    dev = [y[g] - seg_sum(y[g]) * inv_n for g in groups]

    tri = jnp.where(iota((chunk, chunk), 0) >= iota((chunk, chunk), 1), 1.0, 0.0).astype(BF16)
    cum = [sum(_dot(tri, part) for part in _split(logw[g], 3)) for g in groups]

    var = [seg_sum(x * x) * inv_n for x in dev]

    e_pos = [jnp.exp(x) for x in cum]
    e_neg = [jnp.exp(-x) for x in cum]
    a_t = [-kk[g] * jnp.exp(cum[g] - logw[g]) for g in groups]
    b_t = [kk[g] * a[g] * e_neg[g] for g in groups]

    for g in groups:
        yn = dev[g] * lax.rsqrt(var[g] + GN_EPS) * lnw_ref[:, cols[g]] + lnb_ref[:, cols[g]]
        rw_ref[0, :, cols[g]] = ((yn + st_bonus[g]) * st_gate[g]).astype(BF16)

    k_t = [k[g] * e_neg[g] for g in groups]
    r_t = [r[g] * e_pos[g] for g in groups]
    total = [x[chunk - 1:chunk, :] for x in e_pos]
    ar_b = [jnp.concatenate([a_t[g], r_t[g]], axis=0).astype(BF16) for g in groups]
    xb = [head_blocks(x.astype(BF16)) for x in b_t]
    xk = [head_blocks(x.astype(BF16)) for x in k_t]
    v_bd = [head_blocks(x.astype(BF16)) for x in v]

    scores = [_dot_nt(ar_b[g], jnp.concatenate([xb[g], xk[g]], axis=0)) for g in groups]
    t_idx = iota((chunk, 2 * bd_rows), 0)
    s_idx = iota((chunk, 2 * bd_rows), 1) % chunk
    bd_row, bd_col = iota((bd_rows, bd_rows), 0), iota((bd_rows, bd_rows), 1)
    strict_bd = (bd_row // chunk == bd_col // chunk) & (bd_col % chunk < bd_row % chunk)
    eye_b = jnp.where(bd_row == bd_col, 1.0, 0.0).astype(BF16)
    a_ak = [jnp.where((s_idx < t_idx)[:, :bd_rows], x[:chunk, bd_rows:], 0.0).astype(BF16) for x in scores]
    a_r = [jnp.where(s_idx <= t_idx, x[chunk:], 0.0).astype(BF16) for x in scores]

    e_bd = [jnp.where(strict_bd, jnp.concatenate([x[:chunk, :bd_rows]] * HEADS_PER_GROUP, axis=0), 0.0)
            for x in scores]
    m_b = [x.astype(BF16) for x in e_bd]
    m_b = [_dot(x, x).astype(BF16) for x in m_b]
    levels = int(math.log2(chunk)) - 1
    for lvl in range(levels):
        t_b = [x.astype(BF16) + eye_b for x in e_bd]
        e_bd = [e_bd[g] + _dot(m_b[g], t_b[g]) for g in groups]
        if lvl < levels - 1:
            m_b = [_dot(x, x).astype(BF16) for x in m_b]
    e_cat = [sum(x[h * chunk:(h + 1) * chunk] for h in range(HEADS_PER_GROUP)).astype(BF16) for x in e_bd]
    bonus = [seg_sum(r[g] * k[g] * rk_ref[:, cols[g]]) * v[g] for g in groups]
    for g in groups:
        arb_scr[g], aak_scr[g], ar_scr[g], ecat_scr[g], vbd_scr[g] = ar_b[g], a_ak[g], a_r[g], e_cat[g], v_bd[g]
        vb_scr[g] = v[g].astype(BF16)
        bk_scr[g] = jnp.concatenate([b_t[g] * total[g], k_t[g] * total[g]], axis=0).astype(BF16)
        total_scr[g], gate_scr[g], bonus_scr[g] = total[g], gate[g], bonus[g]

    @pl.when(step == n_steps - 1)
    def _():
        for g in groups:
            folded = sum(_dot(part, fold_sel) for part in _split(s_scr[g], 3))
            sout_ref[0, cols[g], :] = folded


def _wkv(prw, shift0, s0, wp, layer, *, chunk):
    b, t, _ = prw.shape
    n_chunks = t // chunk
    bd_rows = HEADS_PER_GROUP * chunk
    lay = lambda *blk: pl.BlockSpec((None,) + blk, lambda i, c: (layer,) + (0,) * len(blk))
    vec = lay(1, RWKV_WIDTH)
    grp = lambda rows, width, dtype: pltpu.VMEM((N_GROUPS, rows, width), dtype)
    return pl.pallas_call(
        functools.partial(_wkv_kernel, chunk=chunk),
        grid=(b, n_chunks + 1),
        in_specs=[pl.BlockSpec((1, chunk, RWKV_PROJ_PAD), lambda i, c: (i, jnp.minimum(c, n_chunks - 1), 0)),
                  pl.BlockSpec((1, 1, RWKV_PROJ_PAD), lambda i, c: (i, 0, 0)),
                  pl.BlockSpec((1, RWKV_WIDTH, HEAD_DIM), lambda i, c: (i, 0, 0)),
                  lay(1, RWKV_PROJ_PAD), vec, lay(LANES, RWKV_WIDTH), vec, lay(LANES, RWKV_WIDTH),
                  lay(GATE_PAD, RWKV_WIDTH), vec, vec, vec, vec, vec],
        out_specs=[pl.BlockSpec((1, chunk, RWKV_WIDTH), lambda i, c: (i, jnp.maximum(c - 1, 0), 0)),
                   pl.BlockSpec((1, RWKV_WIDTH, HEAD_DIM), lambda i, c: (i, 0, 0))],
        out_shape=[jax.ShapeDtypeStruct((b, t, RWKV_WIDTH), BF16),
                   jax.ShapeDtypeStruct((b, RWKV_WIDTH, HEAD_DIM), F32)],
        scratch_shapes=[pltpu.VMEM((N_GROUPS, MXU_DIM, MXU_DIM), F32),
                        pltpu.VMEM((1, RWKV_PROJ_PAD), F32),
                        grp(2 * chunk, MXU_DIM, BF16), grp(chunk, bd_rows, BF16), grp(chunk, 2 * bd_rows, BF16),
                        grp(chunk, bd_rows, BF16), grp(bd_rows, MXU_DIM, BF16), grp(chunk, MXU_DIM, BF16),
                        grp(2 * chunk, MXU_DIM, BF16), grp(1, MXU_DIM, F32), grp(chunk, MXU_DIM, F32),
                        grp(chunk, MXU_DIM, F32)],
        compiler_params=_params(2),
        name="wkv7",
    )(prw, shift0, s0, wp["mix"], wp["w0"], wp["decay_up"], wp["a0"], wp["iclr_up"], wp["gate_up"],
      wp["k_k"], wp["k_a"], wp["r_k"], wp["ln_w"], wp["ln_b"])


SUBLANES = 8


def _wkv_step_kernel(p_ref, shift_ref, s_ref, mix_ref, w0_ref, dup_ref, a0_ref, iup_ref, gup_ref,
                     kk_ref, ka_ref, rk_ref, lnw_ref, lnb_ref, fold_ref, rw_ref, sout_ref):
    iota = lambda shape, dim: lax.broadcasted_iota(jnp.int32, shape, dim)
    groups = range(N_GROUPS)
    cols = [slice(g * MXU_DIM, (g + 1) * MXU_DIM) for g in groups]
    ones_bd = jnp.where(iota((MXU_DIM, MXU_DIM), 0) // HEAD_DIM == iota((MXU_DIM, MXU_DIM), 1) // HEAD_DIM,
                        1.0, 0.0).astype(BF16)

    def seg_sum(x):
        return jnp.concatenate([_dot(x[:, cols[g]].astype(BF16), ones_bd) for g in groups], axis=1)

    cur = jnp.broadcast_to(p_ref[0], (SUBLANES, RWKV_PROJ_PAD))
    prev = jnp.broadcast_to(shift_ref[0], (SUBLANES, RWKV_PROJ_PAD))
    p = cur + (prev - cur) * mix_ref[...]
    r = p[:, 0:RWKV_WIDTH]
    k = p[:, RWKV_WIDTH:2 * RWKV_WIDTH]
    v = p[:, 2 * RWKV_WIDTH:3 * RWKV_WIDTH]
    lora_in = p[:, LORA_OFF:LORA_OFF + LANES]
    gate_in = p[:, GATE_OFF:GATE_OFF + GATE_PAD]
    w = w0_ref[...] + _dot(jnp.tanh(lora_in).astype(BF16), dup_ref[...])
    decay = jnp.exp(-math.exp(-0.5) * jax.nn.sigmoid(w))
    a = jax.nn.sigmoid(a0_ref[...] + _dot(lora_in.astype(BF16), iup_ref[...]))
    gate = _dot(jax.nn.sigmoid(gate_in).astype(BF16), gup_ref[...])
    kk = k * kk_ref[...]
    kk = kk / jnp.maximum(jnp.sqrt(seg_sum(kk * kk)), 1e-12)
    k = k * (1.0 + (a - 1.0) * ka_ref[...])

    own = iota((N_RWKV_HEADS, RWKV_WIDTH), 0) == iota((N_RWKV_HEADS, RWKV_WIDTH), 1) // HEAD_DIM
    spread = lambda x: jnp.where(own, jnp.broadcast_to(x[0:1], (N_RWKV_HEADS, RWKV_WIDTH)), 0.0)
    stacked = jnp.concatenate([spread(decay), spread(-kk), spread(kk * a), spread(k), spread(r)], axis=0)
    per_head = sum(_dot(part, fold_ref[...]) for part in _split(stacked, 3))
    w_h, a_h, b_h, k_h, r_h = (per_head[i * N_RWKV_HEADS:(i + 1) * N_RWKV_HEADS] for i in range(5))
    rows = lambda x: jnp.concatenate(
        [jnp.broadcast_to(x[h:h + 1, :], (HEAD_DIM, HEAD_DIM)) for h in range(N_RWKV_HEADS)], axis=0)

    eye = iota((LANES, LANES), 0) == iota((LANES, LANES), 1)
    ones_cols = jnp.ones((LANES, HEAD_DIM), BF16)
    v_col = jnp.concatenate(
        [sum(_dot(part, ones_cols) for part in
             _split(jnp.where(eye, jnp.broadcast_to(v[0:1, t * LANES:(t + 1) * LANES], (LANES, LANES)), 0.0), 3))
         for t in range(RWKV_WIDTH // LANES)], axis=0)

    s = s_ref[0]
    sa = jnp.sum(s * rows(a_h), axis=-1, keepdims=True)
    s_new = s * rows(w_h) + sa * rows(b_h) + v_col * rows(k_h)
    sout_ref[0] = s_new
    y_heads = _dot_nt(r_h.astype(BF16), s_new.astype(BF16))
    y = jnp.broadcast_to(jnp.sum(jnp.where(own, y_heads, 0.0), axis=0, keepdims=True), (SUBLANES, RWKV_WIDTH))

    inv_n = 1.0 / HEAD_DIM
    dev = y - seg_sum(y) * inv_n
    var = seg_sum(dev * dev) * inv_n
    y = dev * lax.rsqrt(var + GN_EPS) * lnw_ref[...] + lnb_ref[...]
    bonus = seg_sum(r * k * rk_ref[...]) * v
    rw_ref[0] = ((y + bonus) * gate)[0:1].astype(BF16)


def _wkv_step(prw_rows, shift_rows, state, wp, layer):
    s = prw_rows.shape[0]
    lay = lambda *blk: pl.BlockSpec((None,) + blk, lambda i: (layer,) + (0,) * len(blk))
    vec = lay(1, RWKV_WIDTH)
    fold = np.tile(np.eye(HEAD_DIM, dtype=np.float32), (N_RWKV_HEADS, 1))
    return pl.pallas_call(
        _wkv_step_kernel,
        grid=(s,),
        in_specs=[pl.BlockSpec((1, 1, RWKV_PROJ_PAD), lambda i: (i, 0, 0)),
                  pl.BlockSpec((None, 1, 1, RWKV_PROJ_PAD), lambda i: (layer, i, 0, 0)),
                  pl.BlockSpec((None, 1, RWKV_WIDTH, HEAD_DIM), lambda i: (layer, i, 0, 0)),
                  lay(1, RWKV_PROJ_PAD), vec, lay(LANES, RWKV_WIDTH), vec, lay(LANES, RWKV_WIDTH),
                  lay(GATE_PAD, RWKV_WIDTH), vec, vec, vec, vec, vec,
                  pl.BlockSpec((RWKV_WIDTH, HEAD_DIM), lambda i: (0, 0))],
        out_specs=[pl.BlockSpec((1, 1, RWKV_WIDTH), lambda i: (i, 0, 0)),
                   pl.BlockSpec((1, RWKV_WIDTH, HEAD_DIM), lambda i: (i, 0, 0))],
        out_shape=[jax.ShapeDtypeStruct((s, 1, RWKV_WIDTH), BF16),
                   jax.ShapeDtypeStruct((s, RWKV_WIDTH, HEAD_DIM), F32)],
        compiler_params=_params(1),
        name="wkv7_step",
    )(prw_rows, shift_rows, state, wp["mix"], wp["w0"], wp["decay_up"], wp["a0"], wp["iclr_up"], wp["gate_up"],
      wp["k_k"], wp["k_a"], wp["r_k"], wp["ln_w"], wp["ln_b"], jnp.asarray(fold, BF16))


def _out_proj_kernel(att_ref, rw_ref, x_ref, gt_ref, sh_ref, sc_ref, g_ref, w_ref, x1_ref, h_ref):
    y = _dot(att_ref[0], w_ref[0:ATTN_WIDTH, :]) + _dot(rw_ref[0], w_ref[ATTN_WIDTH:, :])
    x1 = x_ref[0] + gt_ref[0] * y
    x1_ref[0] = x1
    h_ref[0] = _rmsnorm_mod(x1, g_ref[...], sh_ref[0], sc_ref[0]).astype(BF16)


def _out_proj(att, rw, x, gate, shift, scale, g, w, layer, *, tm):
    b, t, d = x.shape
    r = gate.shape[1]
    row = lambda width: pl.BlockSpec((1, tm, width), lambda i, m: (i, m, 0))
    mod = pl.BlockSpec((1, r, d), lambda i, m: (i, 0, 0))
    return pl.pallas_call(
        _out_proj_kernel,
        grid=(b, t // tm),
        in_specs=[row(ATTN_WIDTH), row(RWKV_WIDTH), row(d), mod, mod, mod,
                  pl.BlockSpec((None, 1, d), lambda i, m: (layer, 0, 0)),
                  pl.BlockSpec((None, d, d), lambda i, m: (layer, 0, 0))],
        out_specs=[row(d), row(d)],
        out_shape=[jax.ShapeDtypeStruct((b, t, d), F32), jax.ShapeDtypeStruct((b, t, d), BF16)],
        compiler_params=_params(2),
        name="out_proj",
    )(att, rw, x, gate, shift, scale, g, w)


def _ffn_kernel(h_ref, x_ref, gt_ref, wg_ref, wu_ref, wo_ref, gf_ref, o_ref, acc_ref, *, final_norm):
    j = pl.program_id(2)

    @pl.when(j == 0)
    def _():
        acc_ref[...] = jnp.zeros_like(acc_ref)

    h = h_ref[0]
    gate = _dot(h, wg_ref[...])
    up = _dot(h, wu_ref[...])
    act = (gate * jax.nn.sigmoid(gate) * up).astype(BF16)
    acc_ref[...] += _dot(act, wo_ref[...])

    @pl.when(j == pl.num_programs(2) - 1)
    def _():
        x2 = x_ref[0] + gt_ref[0] * acc_ref[...]
        if final_norm:
            ms = jnp.mean(x2 * x2, axis=-1, keepdims=True)
            x2 = x2 * lax.rsqrt(ms + RMS_EPS) * gf_ref[...]
        o_ref[0] = x2


def _ffn(h, x, gate, w_in, w_out, g_final, layer, *, tm, th, final_norm):
    b, t, d = x.shape
    r = gate.shape[1]
    n_h = FFN_HIDDEN // th
    return pl.pallas_call(
        functools.partial(_ffn_kernel, final_norm=final_norm),
        grid=(b, t // tm, n_h),
        in_specs=[pl.BlockSpec((1, tm, d), lambda i, m, j: (i, m, 0)),
                  pl.BlockSpec((1, tm, d), lambda i, m, j: (i, m, 0)),
                  pl.BlockSpec((1, r, d), lambda i, m, j: (i, 0, 0)),
                  pl.BlockSpec((None, d, th), lambda i, m, j: (layer, 0, j)),
                  pl.BlockSpec((None, d, th), lambda i, m, j: (layer, 0, j + n_h)),
                  pl.BlockSpec((None, th, d), lambda i, m, j: (layer, j, 0)),
                  pl.BlockSpec((1, d), lambda i, m, j: (0, 0))],
        out_specs=pl.BlockSpec((1, tm, d), lambda i, m, j: (i, m, 0)),
        out_shape=jax.ShapeDtypeStruct((b, t, d), F32),
        scratch_shapes=[pltpu.VMEM((tm, d), F32)],
        compiler_params=_params(3),
        name="ffn",
    )(h, x, gate, w_in, w_in, w_out, g_final)


def _tiles(t_prompt, n_sample):
    tiles = dict(proj_tm=1024, proj_tn=512, out_tm=512, ffn_tm=512, ffn_th=512, wkv_chunk=64)
    assert t_prompt % tiles["proj_tm"] == 0 and t_prompt % tiles["out_tm"] == 0 and t_prompt % tiles["ffn_tm"] == 0
    assert t_prompt % tiles["wkv_chunk"] == 0 and t_prompt % WINDOW == 0
    assert QKV_COLS % tiles["proj_tn"] == 0 and RWKV_PROJ_PAD % tiles["proj_tn"] == 0
    assert FFN_HIDDEN % tiles["ffn_th"] == 0 and n_sample % SUBLANES == 0
    return tiles


def _pad_cols(x, width):
    return jnp.pad(x, [(0, 0)] * (x.ndim - 1) + [(0, width - x.shape[-1])])


def _pad_rows(x, rows, offset=0):
    return jnp.pad(x, [(0, 0), (offset, rows - offset - x.shape[1]), (0, 0)])


def kernel(x_prompt, x_sample, cache_k, cache_v, state_wkv, state_shift, c_prompt, c_sample, w_ada, b_ada, g_norm_mix, g_norm_ffn, w_in, w_out, attn_sinks, mix_shift, decay_w0, decay_up, iclr_a0, iclr_up, gate_up, k_k, k_a, r_k, ln_x_w, ln_x_b, w_ffn_in, w_ffn_out, g_norm_final):
    n_layers = w_in.shape[0]
    bp, tp, d = x_prompt.shape
    bs = x_sample.shape[0]
    tiles = _tiles(tp, bs)

    w_in_b = _pad_cols(w_in, QKV_COLS + RWKV_PROJ_PAD).astype(BF16)
    w_out_b = w_out.astype(BF16)
    w_ffn_in_b = w_ffn_in.astype(BF16)
    w_ffn_out_b = w_ffn_out.astype(BF16)
    vec = lambda x: x.reshape(n_layers, 1, -1)
    wp = dict(
        mix=vec(_pad_cols(mix_shift, RWKV_PROJ_PAD)), w0=vec(decay_w0), a0=vec(iclr_a0),
        decay_up=_pad_rows(decay_up, LANES).astype(BF16),
        iclr_up=_pad_rows(iclr_up, LANES, D_DECAY_LORA).astype(BF16),
        gate_up=_pad_rows(gate_up, GATE_PAD).astype(BF16),
        k_k=vec(k_k), k_a=vec(k_a), r_k=vec(r_k), ln_w=vec(ln_x_w), ln_b=vec(ln_x_b))
    g_mix, g_ffn = vec(g_norm_mix), vec(g_norm_ffn)
    g_final = g_norm_final.reshape(1, d)
    sinks_flat = attn_sinks.reshape(-1)
    sinks_col = attn_sinks.reshape(n_layers, N_Q_HEADS, 1)
    cache_k2 = cache_k.reshape(n_layers, bs, WINDOW, KV_WIDTH)
    cache_v2 = cache_v.reshape(n_layers, bs, WINDOW, KV_WIDTH)
    shift_rows = _pad_cols(state_shift, RWKV_PROJ_PAD).reshape(n_layers, bs, 1, RWKV_PROJ_PAD)
    state_rows = state_wkv.reshape(n_layers, bs, RWKV_WIDTH, HEAD_DIM)
    zero_shift = jnp.zeros((bp, 1, RWKV_PROJ_PAD), F32)
    zero_state = jnp.zeros((bp, RWKV_WIDTH, HEAD_DIM), F32)

    mod = _ada(jnp.concatenate([c_prompt, c_sample], axis=0), w_ada, b_ada)

    def mods(layer):
        parts = [mod[layer, :, i * d:(i + 1) * d] for i in range(6)]
        return ([p[:bp].reshape(bp, 1, d) for p in parts], [p[bp:].reshape(1, bs, d) for p in parts])

    xp = x_prompt
    xs = x_sample.reshape(1, bs, d)
    outs_p = [[] for _ in range(4)]
    outs_s = [[] for _ in range(4)]
    for layer in range(n_layers):
        last = layer == n_layers - 1
        (sh1, sc1, gt1, sh2, sc2, gt2), (ssh1, ssc1, sgt1, ssh2, ssc2, sgt2) = mods(layer)

        qkv, prw = _norm_proj(xp, sh1, sc1, g_mix, w_in_b, layer, tm=tiles["proj_tm"], tn=tiles["proj_tn"])
        att = _attn_prompt(qkv, sinks_flat, layer)
        rw, state = _wkv(prw, zero_shift, zero_state, wp, layer, chunk=tiles["wkv_chunk"])
        x1, h2 = _out_proj(att, rw, xp, gt1, sh2, sc2, g_ffn, w_out_b, layer, tm=tiles["out_tm"])
        xp = _ffn(h2, x1, gt2, w_ffn_in_b, w_ffn_out_b, g_final, layer, tm=tiles["ffn_tm"], th=tiles["ffn_th"],
                  final_norm=last)
        outs_p[0].append(qkv[:, tp - WINDOW:, ATTN_WIDTH:ATTN_WIDTH + KV_WIDTH].reshape(bp, WINDOW, N_KV_HEADS, HEAD_DIM))
        outs_p[1].append(qkv[:, tp - WINDOW:, ATTN_WIDTH + KV_WIDTH:].reshape(bp, WINDOW, N_KV_HEADS, HEAD_DIM))
        outs_p[2].append(state.reshape(bp, N_RWKV_HEADS, HEAD_DIM, HEAD_DIM))
        outs_p[3].append(prw[:, tp - 1, :RWKV_PROJ])

        qkv_s, prw_s = _norm_proj(xs, ssh1, ssc1, g_mix, w_in_b, layer, tm=bs, tn=tiles["proj_tn"])
        att_s, nk, nv = _attn_sample(qkv_s.reshape(bs, 1, QKV_COLS), cache_k2, cache_v2, sinks_col, layer)
        prw_tok = prw_s.reshape(bs, 1, RWKV_PROJ_PAD)
        rw_s, state_s = _wkv_step(prw_tok, shift_rows, state_rows, wp, layer)
        x1_s, h2_s = _out_proj(att_s.reshape(1, bs, ATTN_WIDTH), rw_s.reshape(1, bs, RWKV_WIDTH), xs,
                               sgt1, ssh2, ssc2, g_ffn, w_out_b, layer, tm=bs)
        xs = _ffn(h2_s, x1_s, sgt2, w_ffn_in_b, w_ffn_out_b, g_final, layer, tm=bs, th=tiles["ffn_th"],
                  final_norm=last)
        outs_s[0].append(nk.reshape(bs, WINDOW, N_KV_HEADS, HEAD_DIM))
        outs_s[1].append(nv.reshape(bs, WINDOW, N_KV_HEADS, HEAD_DIM))
        outs_s[2].append(state_s.reshape(bs, N_RWKV_HEADS, HEAD_DIM, HEAD_DIM))
        outs_s[3].append(prw_tok[:, 0, :RWKV_PROJ])

    stack = lambda xs_: jnp.stack(xs_, axis=0)
    return (xp, xs.reshape(bs, 1, d),
            stack(outs_p[0]), stack(outs_p[1]), stack(outs_p[2]), stack(outs_p[3]),
            stack(outs_s[0]), stack(outs_s[1]), stack(outs_s[2]), stack(outs_s[3]))
```

```python
import functools
import math

import numpy as np
import jax
import jax.numpy as jnp
from jax import lax
from jax.experimental import pallas as pl
from jax.experimental.pallas import tpu as pltpu

F32 = jnp.float32
BF16 = jnp.bfloat16

D_MODEL = 2048
HEAD_DIM = 64
ATTN_WIDTH = D_MODEL // 2
RWKV_WIDTH = D_MODEL - ATTN_WIDTH
N_Q_HEADS = ATTN_WIDTH // HEAD_DIM
N_KV_HEADS = N_Q_HEADS // 4
GQA_REP = N_Q_HEADS // N_KV_HEADS
KV_WIDTH = N_KV_HEADS * HEAD_DIM
N_RWKV_HEADS = RWKV_WIDTH // HEAD_DIM
WINDOW = 128
D_DECAY_LORA = 64
D_ICLR_LORA = 64
D_GATE_LORA = 160
RWKV_PROJ = 3 * RWKV_WIDTH + D_DECAY_LORA + D_ICLR_LORA + D_GATE_LORA
QKV_COLS = ATTN_WIDTH + 2 * KV_WIDTH
FFN_HIDDEN = 5632
RMS_EPS = 1e-5
GN_EPS = 64e-5
NEG_INF = -1e30

LANES = 128
MXU_DIM = 256
VMEM_LIMIT_BYTES = 56 * 1024 * 1024

LORA_OFF = 3 * RWKV_WIDTH
GATE_OFF = LORA_OFF + D_DECAY_LORA + D_ICLR_LORA
GATE_PAD = 2 * LANES
RWKV_PROJ_PAD = 3584
HEADS_PER_GROUP = MXU_DIM // HEAD_DIM
N_GROUPS = N_RWKV_HEADS // HEADS_PER_GROUP

ALIBI_SLOPES = tuple(float(s) for s in np.exp2(-8.0 * np.arange(1, N_Q_HEADS + 1, dtype=np.float32) / N_Q_HEADS))


def _params(n_axes):
    return pltpu.CompilerParams(dimension_semantics=("arbitrary",) * n_axes,
                                vmem_limit_bytes=VMEM_LIMIT_BYTES)


def _dot(a, b):
    return jnp.dot(a, b, preferred_element_type=F32)


def _dot_nt(a, b):
    return lax.dot_general(a, b, (((1,), (1,)), ((), ())), preferred_element_type=F32)


def _dot_tn(a, b):
    return lax.dot_general(a, b, (((0,), (0,)), ((), ())), preferred_element_type=F32)


def _split(x, parts):
    out = []
    for _ in range(parts):
        p = x.astype(BF16)
        out.append(p)
        x = x - p.astype(F32)
    return out


def _rmsnorm_mod(x, g, shift, scale):
    ms = jnp.mean(x * x, axis=-1, keepdims=True)
    y = x * lax.rsqrt(ms + RMS_EPS) * g
    return y * (1.0 + scale) + shift


def _ada_kernel(c_ref, w_ref, b_ref, o_ref):
    c = c_ref[...]
    s = (c * jax.nn.sigmoid(c)).astype(BF16)
    o_ref[...] = _dot(s, w_ref[...].astype(BF16)) + b_ref[...]


def _ada(c_all, w_ada, b_ada):
    n_layers, d, n = w_ada.shape
    rows = c_all.shape[0]
    tn = 1024
    return pl.pallas_call(
        _ada_kernel,
        grid=(n_layers, n // tn),
        in_specs=[pl.BlockSpec((rows, d), lambda l, j: (0, 0)),
                  pl.BlockSpec((None, d, tn), lambda l, j: (l, 0, j)),
                  pl.BlockSpec((None, 1, tn), lambda l, j: (l, 0, j))],
        out_specs=pl.BlockSpec((None, rows, tn), lambda l, j: (l, 0, j)),
        out_shape=jax.ShapeDtypeStruct((n_layers, rows, n), F32),
        compiler_params=_params(2),
        name="ada_mod",
    )(c_all, w_ada, b_ada.reshape(n_layers, 1, n))


NORM_STRIP = 32


def _norm_proj_kernel(x_ref, sh_ref, sc_ref, g_ref, w_ref, qkv_ref, prw_ref, h_scr, *, n_qkv_tiles):
    j = pl.program_id(2)
    tm = h_scr.shape[0]

    @pl.when(j == 0)
    def _():
        gain = g_ref[...] * (1.0 + sc_ref[0])
        shift = sh_ref[0]
        if gain.shape[0] == 1:
            def strip(i, carry):
                rows = pl.ds(pl.multiple_of(i * NORM_STRIP, NORM_STRIP), NORM_STRIP)
                x = x_ref[0, rows, :]
                ms = jnp.mean(x * x, axis=-1, keepdims=True)
                h_scr[rows, :] = (x * lax.rsqrt(ms + RMS_EPS) * gain + shift).astype(BF16)
                return carry
            lax.fori_loop(0, tm // NORM_STRIP, strip, 0, unroll=4)
        else:
            x = x_ref[0]
            ms = jnp.mean(x * x, axis=-1, keepdims=True)
            h_scr[...] = (x * lax.rsqrt(ms + RMS_EPS) * gain + shift).astype(BF16)

    @pl.when(j < n_qkv_tiles)
    def _():
        qkv_ref[0] = _dot(h_scr[...], w_ref[...])

    @pl.when(j >= n_qkv_tiles)
    def _():
        prw_ref[0] = _dot(h_scr[...], w_ref[...])


def _norm_proj(x, shift, scale, g, w, layer, *, tm, tn):
    b, t, d = x.shape
    r = shift.shape[1]
    n_qkv_tiles = QKV_COLS // tn
    n_tiles = (QKV_COLS + RWKV_PROJ_PAD) // tn
    return pl.pallas_call(
        functools.partial(_norm_proj_kernel, n_qkv_tiles=n_qkv_tiles),
        grid=(b, t // tm, n_tiles),
        in_specs=[pl.BlockSpec((1, tm, d), lambda i, m, j: (i, m, 0)),
                  pl.BlockSpec((1, r, d), lambda i, m, j: (i, 0, 0)),
                  pl.BlockSpec((1, r, d), lambda i, m, j: (i, 0, 0)),
                  pl.BlockSpec((None, 1, d), lambda i, m, j: (layer, 0, 0)),
                  pl.BlockSpec((None, d, tn), lambda i, m, j: (layer, 0, j))],
        out_specs=[pl.BlockSpec((1, tm, tn), lambda i, m, j: (i, m, jnp.minimum(j, n_qkv_tiles - 1))),
                   pl.BlockSpec((1, tm, tn), lambda i, m, j: (i, m, jnp.maximum(j - n_qkv_tiles, 0)))],
        out_shape=[jax.ShapeDtypeStruct((b, t, QKV_COLS), F32),
                   jax.ShapeDtypeStruct((b, t, RWKV_PROJ_PAD), F32)],
        scratch_shapes=[pltpu.VMEM((tm, d), BF16)],
        compiler_params=_params(3),
        name="norm_proj",
    )(x, shift, scale, g, w)


LOG2E = math.log2(math.e)


def _attn_bias_table():
    t = np.arange(WINDOW)[:, None]
    j = np.arange(2 * WINDOW)[None, :]
    dist = t + WINDOW - j
    valid = (dist >= 0) & (dist <= WINDOW)
    slopes = np.asarray(ALIBI_SLOPES, np.float32)[:, None, None]
    bias = np.where(valid[None], -slopes * dist[None].astype(np.float32) * np.float32(LOG2E), np.float32(NEG_INF))
    first = np.where((j >= WINDOW)[None], bias, np.float32(NEG_INF))
    return np.stack([first, bias]).astype(np.float32)


def _attn_prompt_kernel(sink_ref, bias_ref, q_ref, kc_ref, vc_ref, kp_ref, vp_ref, o_ref, *, layer):
    q = q_ref[0] * (HEAD_DIM ** -0.5 * LOG2E)
    kc, vc, kp, vp = kc_ref[0], vc_ref[0], kp_ref[0], vp_ref[0]
    kv_heads, heads = range(N_KV_HEADS), range(N_Q_HEADS)
    ksl = [slice(g * HEAD_DIM, (g + 1) * HEAD_DIM) for g in kv_heads]
    k_band = [jnp.concatenate([kp[:, ksl[g]], kc[:, ksl[g]]], axis=0).astype(BF16) for g in kv_heads]
    v_band = [jnp.concatenate([vp[:, ksl[g]], vc[:, ksl[g]]], axis=0).astype(BF16) for g in kv_heads]
    lg = [_dot_nt(q[:, h * HEAD_DIM:(h + 1) * HEAD_DIM].astype(BF16), k_band[h // GQA_REP]) + bias_ref[0, h]
          for h in heads]
    sink = [sink_ref[layer * N_Q_HEADS + h] * LOG2E for h in heads]
    m = [jnp.maximum(jnp.max(lg[h], axis=-1, keepdims=True), sink[h]) for h in heads]
    p = [jnp.exp2(lg[h] - m[h]).astype(BF16) for h in heads]
    ones = jnp.ones((2 * WINDOW, HEAD_DIM), BF16)
    den = [_dot(p[h], ones) + jnp.exp2(sink[h] - m[h]) for h in heads]
    outs = [_dot(p[h], v_band[h // GQA_REP]) * (1.0 / den[h]) for h in heads]
    o_ref[0] = jnp.concatenate(outs, axis=1).astype(BF16)


def _attn_prompt(qkv, sinks_flat, layer):
    b, t, _ = qkv.shape
    kcol = ATTN_WIDTH // KV_WIDTH
    prev = lambda i, m: (i, jnp.maximum(m - 1, 0), kcol)
    prev_v = lambda i, m: (i, jnp.maximum(m - 1, 0), kcol + 1)
    return pl.pallas_call(
        functools.partial(_attn_prompt_kernel, layer=layer),
        grid=(b, t // WINDOW),
        in_specs=[pl.BlockSpec(memory_space=pltpu.SMEM),
                  pl.BlockSpec((1, N_Q_HEADS, WINDOW, 2 * WINDOW), lambda i, m: (jnp.minimum(m, 1), 0, 0, 0)),
                  pl.BlockSpec((1, WINDOW, ATTN_WIDTH), lambda i, m: (i, m, 0)),
                  pl.BlockSpec((1, WINDOW, KV_WIDTH), lambda i, m: (i, m, kcol)),
                  pl.BlockSpec((1, WINDOW, KV_WIDTH), lambda i, m: (i, m, kcol + 1)),
                  pl.BlockSpec((1, WINDOW, KV_WIDTH), prev),
                  pl.BlockSpec((1, WINDOW, KV_WIDTH), prev_v)],
        out_specs=pl.BlockSpec((1, WINDOW, ATTN_WIDTH), lambda i, m: (i, m, 0)),
        out_shape=jax.ShapeDtypeStruct((b, t, ATTN_WIDTH), BF16),
        compiler_params=_params(2),
        name="attn_prompt",
    )(sinks_flat, jnp.asarray(_attn_bias_table()), qkv, qkv, qkv, qkv, qkv)


def _attn_sample_kernel(row_ref, kc_ref, vc_ref, sink_ref, slope_ref, spread_ref, gather_ref,
                        o_ref, nk_ref, nv_ref):
    row = row_ref[0]
    q = row[:, :ATTN_WIDTH] * (HEAD_DIM ** -0.5)
    k_new = row[:, ATTN_WIDTH:ATTN_WIDTH + KV_WIDTH]
    v_new = row[:, ATTN_WIDTH + KV_WIDTH:]
    kc, vc = kc_ref[0], vc_ref[0]
    head_row = lax.broadcasted_iota(jnp.int32, (N_Q_HEADS, ATTN_WIDTH), 0)
    head_lane = lax.broadcasted_iota(jnp.int32, (N_Q_HEADS, ATTN_WIDTH), 1) // HEAD_DIM
    own = head_row == head_lane
    q_rows = jnp.where(own, jnp.broadcast_to(q, (N_Q_HEADS, ATTN_WIDTH)), 0.0).astype(BF16)
    q_grp = _dot(q_rows, spread_ref[...])
    k_new_b = k_new.astype(BF16).astype(F32)
    lg = _dot_nt(q_grp.astype(BF16), kc.astype(BF16))
    lg_new = jnp.sum(q_grp * k_new_b, axis=-1, keepdims=True)
    dist = (WINDOW - lax.broadcasted_iota(jnp.int32, (N_Q_HEADS, WINDOW), 1)).astype(F32)
    lg = lg - slope_ref[...] * dist
    sink = sink_ref[...]
    m = jnp.maximum(jnp.maximum(jnp.max(lg, axis=-1, keepdims=True), lg_new), sink)
    p = jnp.exp(lg - m)
    p_new = jnp.exp(lg_new - m)
    den = jnp.sum(p, axis=-1, keepdims=True) + p_new + jnp.exp(sink - m)
    out = _dot((p / den).astype(BF16), vc.astype(BF16))
    out = out + (p_new / den).astype(BF16).astype(F32) * v_new.astype(BF16).astype(F32)
    grp_row = lax.broadcasted_iota(jnp.int32, (N_Q_HEADS, KV_WIDTH), 0) // GQA_REP
    grp_lane = lax.broadcasted_iota(jnp.int32, (N_Q_HEADS, KV_WIDTH), 1) // HEAD_DIM
    out = jnp.where(grp_row == grp_lane, out, 0.0).astype(BF16)
    full = _dot(out, gather_ref[...])
    o_ref[0] = jnp.sum(jnp.where(own, full, 0.0), axis=0, keepdims=True).astype(BF16)
    last = lax.broadcasted_iota(jnp.int32, (WINDOW, KV_WIDTH), 0) == WINDOW - 1
    nk_ref[0] = jnp.where(last, k_new, pltpu.roll(kc, WINDOW - 1, axis=0))
    nv_ref[0] = jnp.where(last, v_new, pltpu.roll(vc, WINDOW - 1, axis=0))


def _attn_sample(qkv_rows, cache_k, cache_v, sinks_col, layer):
    s = qkv_rows.shape[0]
    spread = np.zeros((ATTN_WIDTH, KV_WIDTH), np.float32)
    for h in range(N_Q_HEADS):
        for d in range(HEAD_DIM):
            spread[h * HEAD_DIM + d, (h // GQA_REP) * HEAD_DIM + d] = 1.0
    slopes = np.asarray(ALIBI_SLOPES, np.float32).reshape(N_Q_HEADS, 1)
    win = (1, WINDOW, KV_WIDTH)
    return pl.pallas_call(
        _attn_sample_kernel,
        grid=(s,),
        in_specs=[pl.BlockSpec((1, 1, QKV_COLS), lambda i: (i, 0, 0)),
                  pl.BlockSpec((None,) + win, lambda i: (layer, i, 0, 0)),
                  pl.BlockSpec((None,) + win, lambda i: (layer, i, 0, 0)),
                  pl.BlockSpec((None, N_Q_HEADS, 1), lambda i: (layer, 0, 0)),
                  pl.BlockSpec((N_Q_HEADS, 1), lambda i: (0, 0)),
                  pl.BlockSpec((ATTN_WIDTH, KV_WIDTH), lambda i: (0, 0)),
                  pl.BlockSpec((KV_WIDTH, ATTN_WIDTH), lambda i: (0, 0))],
        out_specs=[pl.BlockSpec((1, 1, ATTN_WIDTH), lambda i: (i, 0, 0)),
                   pl.BlockSpec(win, lambda i: (i, 0, 0)),
                   pl.BlockSpec(win, lambda i: (i, 0, 0))],
        out_shape=[jax.ShapeDtypeStruct((s, 1, ATTN_WIDTH), BF16),
                   jax.ShapeDtypeStruct((s, WINDOW, KV_WIDTH), F32),
                   jax.ShapeDtypeStruct((s, WINDOW, KV_WIDTH), F32)],
        compiler_params=_params(1),
        name="attn_sample",
    )(qkv_rows, cache_k, cache_v, sinks_col, jnp.asarray(slopes),
      jnp.asarray(spread, BF16), jnp.asarray(spread.T, BF16))


def _wkv_kernel(p_ref, shift0_ref, s0_ref, mix_ref, w0_ref, dup_ref, a0_ref, iup_ref, gup_ref,
                kk_ref, ka_ref, rk_ref, lnw_ref, lnb_ref, rw_ref, sout_ref, s_scr, prev_scr,
                arb_scr, aak_scr, ar_scr, ecat_scr, vbd_scr, vb_scr, bk_scr, total_scr, gate_scr, bonus_scr,
                *, chunk):
    step = pl.program_id(1)
    n_steps = pl.num_programs(1)
    stage = (arb_scr, aak_scr, ar_scr, ecat_scr, vbd_scr, vb_scr, bk_scr, total_scr, gate_scr, bonus_scr)
    bd_rows = HEADS_PER_GROUP * chunk
    groups = range(N_GROUPS)
    cols = [slice(g * MXU_DIM, (g + 1) * MXU_DIM) for g in groups]
    iota = lambda shape, dim: lax.broadcasted_iota(jnp.int32, shape, dim)
    same_head = iota((MXU_DIM, MXU_DIM), 0) // HEAD_DIM == iota((MXU_DIM, MXU_DIM), 1) // HEAD_DIM
    ones_bd = jnp.where(same_head, 1.0, 0.0).astype(BF16)
    tile_sel = jnp.where(iota((HEAD_DIM, MXU_DIM), 0) == iota((HEAD_DIM, MXU_DIM), 1) % HEAD_DIM, 1.0, 0.0).astype(BF16)
    fold_sel = jnp.where(iota((MXU_DIM, HEAD_DIM), 0) % HEAD_DIM == iota((MXU_DIM, HEAD_DIM), 1), 1.0, 0.0).astype(BF16)

    @pl.when(step == 0)
    def _():
        prev_scr[...] = shift0_ref[0]
        for g in groups:
            rows = s0_ref[0, cols[g], :]
            tiled = sum(_dot(part, tile_sel) for part in _split(rows, 3))
            s_scr[g] = jnp.where(same_head, tiled, 0.0)
        for ref in stage:
            ref[...] = jnp.zeros_like(ref)
        total_scr[...] = jnp.ones_like(total_scr)

    row = iota((chunk, 1), 0)

    def shifted(lo, width):
        cur = p_ref[0, :, lo:lo + width]
        prev = jnp.where(row == 0, prev_scr[:, lo:lo + width], pltpu.roll(cur, 1, axis=0))
        return cur + (prev - cur) * mix_ref[:, lo:lo + width]

    def seg_sum(x):
        return _dot(x.astype(BF16), ones_bd)

    head_mask = [jnp.where(iota((chunk, MXU_DIM), 1) // HEAD_DIM == h, 1.0, 0.0).astype(BF16)
                 for h in range(HEADS_PER_GROUP)]

    def head_blocks(x):
        return jnp.concatenate([x * m for m in head_mask], axis=0)

    st_ar, st_ak, st_r, st_e, st_vbd, st_v, st_bk, st_total, st_gate, st_bonus = (
        [ref[g] for g in groups] for ref in stage)
    s_b = [s_scr[g].astype(BF16) for g in groups]
    uy = [_dot_nt(st_ar[g], s_b[g]) for g in groups]

    lora_in = shifted(LORA_OFF, LANES)
    lora_tanh = jnp.tanh(lora_in).astype(BF16)
    lora_lin = lora_in.astype(BF16)
    gate_sig = jax.nn.sigmoid(shifted(GATE_OFF, GATE_PAD)).astype(BF16)
    r = [shifted(g * MXU_DIM, MXU_DIM) for g in groups]

    rhs = [uy[g][:chunk] + _dot(st_ak[g], st_vbd[g]) for g in groups]

    k = [shifted(RWKV_WIDTH + g * MXU_DIM, MXU_DIM) for g in groups]
    v = [shifted(2 * RWKV_WIDTH + g * MXU_DIM, MXU_DIM) for g in groups]
    prev_scr[...] = p_ref[0, chunk - 1:chunk, :]

    p = [rhs[g] + _dot(st_e[g], head_blocks(rhs[g].astype(BF16))) for g in groups]
    p_b = [x.astype(BF16) for x in p]

    logw = [-math.exp(-0.5) * jax.nn.sigmoid(w0_ref[:, cols[g]] + _dot(lora_tanh, dup_ref[:, cols[g]])) for g in groups]
    a = [jax.nn.sigmoid(a0_ref[:, cols[g]] + _dot(lora_lin, iup_ref[:, cols[g]])) for g in groups]
    gate = [_dot(gate_sig, gup_ref[:, cols[g]]) for g in groups]

    y = [uy[g][chunk:] + _dot(st_r[g], jnp.concatenate([head_blocks(p_b[g]), st_vbd[g]], axis=0)) for g in groups]
    for g in groups:
        upd = _dot_tn(jnp.concatenate([p_b[g], st_v[g]], axis=0), st_bk[g])
        s_scr[g] = s_scr[g] * st_total[g] + jnp.where(same_head, upd, 0.0)

    kk = [k[g] * kk_ref[:, cols[g]] for g in groups]
    kk = [kk[g] / jnp.maximum(jnp.sqrt(seg_sum(kk[g] * kk[g])), 1e-12) for g in groups]
    k = [k[g] * (1.0 + (a[g] - 1.0) * ka_ref[:, cols[g]]) for g in groups]

    inv_n = 1.0 / HEAD_DIM
    dev = [y[g] - seg_sum(y[g]) * inv_n for g in groups]

    tri = jnp.where(iota((chunk, chunk), 0) >= iota((chunk, chunk), 1), 1.0, 0.0).astype(BF16)
    cum = [sum(_dot(tri, part) for part in _split(logw[g], 3)) for g in groups]

    var = [seg_sum(x * x) * inv_n for x in dev]

    e_pos = [jnp.exp(x) for x in cum]
    e_neg = [jnp.exp(-x) for x in cum]
    a_t = [-kk[g] * jnp.exp(cum[g] - logw[g]) for g in groups]
    b_t = [kk[g] * a[g] * e_neg[g] for g in groups]

    for g in groups:
        yn = dev[g] * lax.rsqrt(var[g] + GN_EPS) * lnw_ref[:, cols[g]] + lnb_ref[:, cols[g]]
        rw_ref[0, :, cols[g]] = ((yn + st_bonus[g]) * st_gate[g]).astype(BF16)

    k_t = [k[g] * e_neg[g] for g in groups]
    r_t = [r[g] * e_pos[g] for g in groups]
    total = [x[chunk - 1:chunk, :] for x in e_pos]
    ar_b = [jnp.concatenate([a_t[g], r_t[g]], axis=0).astype(BF16) for g in groups]
    xb = [head_blocks(x.astype(BF16)) for x in b_t]
    xk = [head_blocks(x.astype(BF16)) for x in k_t]
    v_bd = [head_blocks(x.astype(BF16)) for x in v]

    scores = [_dot_nt(ar_b[g], jnp.concatenate([xb[g], xk[g]], axis=0)) for g in groups]
    t_idx = iota((chunk, 2 * bd_rows), 0)
    s_idx = iota((chunk, 2 * bd_rows), 1) % chunk
    strict = (s_idx < t_idx)[:, :bd_rows]
    a_ak = [jnp.where(strict, x[:chunk, bd_rows:], 0.0).astype(BF16) for x in scores]
    a_r = [jnp.where(s_idx <= t_idx, x[chunk:], 0.0).astype(BF16) for x in scores]

    same_block = jnp.where(iota((bd_rows, bd_rows), 0) // chunk == iota((bd_rows, bd_rows), 1) // chunk,
                           1.0, 0.0).astype(BF16)
    eye_cat = jnp.where(t_idx == s_idx, 1.0, 0.0)[:, :bd_rows].astype(BF16)
    block_diag = lambda x: jnp.concatenate([x] * HEADS_PER_GROUP, axis=0) * same_block
    e_cat = [jnp.where(strict, x[:chunk, :bd_rows], 0.0) for x in scores]
    m_b = [x.astype(BF16) for x in e_cat]
    m_b = [_dot(x, block_diag(x)).astype(BF16) for x in m_b]
    levels = int(math.log2(chunk)) - 1
    for lvl in range(levels):
        t_bd = [block_diag(x.astype(BF16) + eye_cat) for x in e_cat]
        if lvl < levels - 1:
            both = [_dot(m_b[g], jnp.concatenate([t_bd[g], block_diag(m_b[g])], axis=1)) for g in groups]
            e_cat = [e_cat[g] + both[g][:, :bd_rows] for g in groups]
            m_b = [x[:, bd_rows:].astype(BF16) for x in both]
        else:
            e_cat = [e_cat[g] + _dot(m_b[g], t_bd[g]) for g in groups]
    e_cat = [x.astype(BF16) for x in e_cat]
    bonus =[seg_sum(r[g] * k[g] * rk_ref[:, cols[g]]) * v[g] for g in groups]
    for g in groups:
        arb_scr[g], aak_scr[g], ar_scr[g], ecat_scr[g], vbd_scr[g] = ar_b[g], a_ak[g], a_r[g], e_cat[g], v_bd[g]
        vb_scr[g] = v[g].astype(BF16)
        bk_scr[g] = jnp.concatenate([b_t[g] * total[g], k_t[g] * total[g]], axis=0).astype(BF16)
        total_scr[g], gate_scr[g], bonus_scr[g] = total[g], gate[g], bonus[g]

    @pl.when(step == n_steps - 1)
    def _():
        for g in groups:
            folded = sum(_dot(part, fold_sel) for part in _split(s_scr[g], 3))
            sout_ref[0, cols[g], :] = folded


def _wkv(prw, shift0, s0, wp, layer, *, chunk):
    b, t, _ = prw.shape
    n_chunks = t // chunk
    bd_rows = HEADS_PER_GROUP * chunk
    lay = lambda *blk: pl.BlockSpec((None,) + blk, lambda i, c: (layer,) + (0,) * len(blk))
    vec = lay(1, RWKV_WIDTH)
    grp = lambda rows, width, dtype: pltpu.VMEM((N_GROUPS, rows, width), dtype)
    return pl.pallas_call(
        functools.partial(_wkv_kernel, chunk=chunk),
        grid=(b, n_chunks + 1),
        in_specs=[pl.BlockSpec((1, chunk, RWKV_PROJ_PAD), lambda i, c: (i, jnp.minimum(c, n_chunks - 1), 0)),
                  pl.BlockSpec((1, 1, RWKV_PROJ_PAD), lambda i, c: (i, 0, 0)),
                  pl.BlockSpec((1, RWKV_WIDTH, HEAD_DIM), lambda i, c: (i, 0, 0)),
                  lay(1, RWKV_PROJ_PAD), vec, lay(LANES, RWKV_WIDTH), vec, lay(LANES, RWKV_WIDTH),
                  lay(GATE_PAD, RWKV_WIDTH), vec, vec, vec, vec, vec],
        out_specs=[pl.BlockSpec((1, chunk, RWKV_WIDTH), lambda i, c: (i, jnp.maximum(c - 1, 0), 0)),
                   pl.BlockSpec((1, RWKV_WIDTH, HEAD_DIM), lambda i, c: (i, 0, 0))],
        out_shape=[jax.ShapeDtypeStruct((b, t, RWKV_WIDTH), BF16),
                   jax.ShapeDtypeStruct((b, RWKV_WIDTH, HEAD_DIM), F32)],
        scratch_shapes=[pltpu.VMEM((N_GROUPS, MXU_DIM, MXU_DIM), F32),
                        pltpu.VMEM((1, RWKV_PROJ_PAD), F32),
                        grp(2 * chunk, MXU_DIM, BF16), grp(chunk, bd_rows, BF16), grp(chunk, 2 * bd_rows, BF16),
                        grp(chunk, bd_rows, BF16), grp(bd_rows, MXU_DIM, BF16), grp(chunk, MXU_DIM, BF16),
                        grp(2 * chunk, MXU_DIM, BF16), grp(1, MXU_DIM, F32), grp(chunk, MXU_DIM, F32),
                        grp(chunk, MXU_DIM, F32)],
        compiler_params=_params(2),
        name="wkv7",
    )(prw, shift0, s0, wp["mix"], wp["w0"], wp["decay_up"], wp["a0"], wp["iclr_up"], wp["gate_up"],
      wp["k_k"], wp["k_a"], wp["r_k"], wp["ln_w"], wp["ln_b"])


SUBLANES = 8


def _wkv_step_kernel(p_ref, shift_ref, s_ref, mix_ref, w0_ref, dup_ref, a0_ref, iup_ref, gup_ref,
                     kk_ref, ka_ref, rk_ref, lnw_ref, lnb_ref, fold_ref, rw_ref, sout_ref):
    iota = lambda shape, dim: lax.broadcasted_iota(jnp.int32, shape, dim)
    groups = range(N_GROUPS)
    cols = [slice(g * MXU_DIM, (g + 1) * MXU_DIM) for g in groups]
    ones_bd = jnp.where(iota((MXU_DIM, MXU_DIM), 0) // HEAD_DIM == iota((MXU_DIM, MXU_DIM), 1) // HEAD_DIM,
                        1.0, 0.0).astype(BF16)

    def seg_sum(x):
        return jnp.concatenate([_dot(x[:, cols[g]].astype(BF16), ones_bd) for g in groups], axis=1)

    cur = jnp.broadcast_to(p_ref[0], (SUBLANES, RWKV_PROJ_PAD))
    prev = jnp.broadcast_to(shift_ref[0], (SUBLANES, RWKV_PROJ_PAD))
    p = cur + (prev - cur) * mix_ref[...]
    r = p[:, 0:RWKV_WIDTH]
    k = p[:, RWKV_WIDTH:2 * RWKV_WIDTH]
    v = p[:, 2 * RWKV_WIDTH:3 * RWKV_WIDTH]
    lora_in = p[:, LORA_OFF:LORA_OFF + LANES]
    gate_in = p[:, GATE_OFF:GATE_OFF + GATE_PAD]
    w = w0_ref[...] + _dot(jnp.tanh(lora_in).astype(BF16), dup_ref[...])
    decay = jnp.exp(-math.exp(-0.5) * jax.nn.sigmoid(w))
    a = jax.nn.sigmoid(a0_ref[...] + _dot(lora_in.astype(BF16), iup_ref[...]))
    gate = _dot(jax.nn.sigmoid(gate_in).astype(BF16), gup_ref[...])
    kk = k * kk_ref[...]
    kk = kk / jnp.maximum(jnp.sqrt(seg_sum(kk * kk)), 1e-12)
    k = k * (1.0 + (a - 1.0) * ka_ref[...])

    own = iota((N_RWKV_HEADS, RWKV_WIDTH), 0) == iota((N_RWKV_HEADS, RWKV_WIDTH), 1) // HEAD_DIM
    spread = lambda x: jnp.where(own, jnp.broadcast_to(x[0:1], (N_RWKV_HEADS, RWKV_WIDTH)), 0.0)
    stacked = jnp.concatenate([spread(decay), spread(-kk), spread(kk * a), spread(k), spread(r)], axis=0)
    per_head = sum(_dot(part, fold_ref[...]) for part in _split(stacked, 3))
    w_h, a_h, b_h, k_h, r_h = (per_head[i * N_RWKV_HEADS:(i + 1) * N_RWKV_HEADS] for i in range(5))
    rows = lambda x: jnp.concatenate(
        [jnp.broadcast_to(x[h:h + 1, :], (HEAD_DIM, HEAD_DIM)) for h in range(N_RWKV_HEADS)], axis=0)

    eye = iota((LANES, LANES), 0) == iota((LANES, LANES), 1)
    ones_cols = jnp.ones((LANES, HEAD_DIM), BF16)
    v_col = jnp.concatenate(
        [sum(_dot(part, ones_cols) for part in
             _split(jnp.where(eye, jnp.broadcast_to(v[0:1, t * LANES:(t + 1) * LANES], (LANES, LANES)), 0.0), 3))
         for t in range(RWKV_WIDTH // LANES)], axis=0)

    s = s_ref[0]
    sa = jnp.sum(s * rows(a_h), axis=-1, keepdims=True)
    s_new = s * rows(w_h) + sa * rows(b_h) + v_col * rows(k_h)
    sout_ref[0] = s_new
    y_heads = _dot_nt(r_h.astype(BF16), s_new.astype(BF16))
    y = jnp.broadcast_to(jnp.sum(jnp.where(own, y_heads, 0.0), axis=0, keepdims=True), (SUBLANES, RWKV_WIDTH))

    inv_n = 1.0 / HEAD_DIM
    dev = y - seg_sum(y) * inv_n
    var = seg_sum(dev * dev) * inv_n
    y = dev * lax.rsqrt(var + GN_EPS) * lnw_ref[...] + lnb_ref[...]
    bonus = seg_sum(r * k * rk_ref[...]) * v
    rw_ref[0] = ((y + bonus) * gate)[0:1].astype(BF16)


def _wkv_step(prw_rows, shift_rows, state, wp, layer):
    s = prw_rows.shape[0]
    lay = lambda *blk: pl.BlockSpec((None,) + blk, lambda i: (layer,) + (0,) * len(blk))
    vec = lay(1, RWKV_WIDTH)
    fold = np.tile(np.eye(HEAD_DIM, dtype=np.float32), (N_RWKV_HEADS, 1))
    return pl.pallas_call(
        _wkv_step_kernel,
        grid=(s,),
        in_specs=[pl.BlockSpec((1, 1, RWKV_PROJ_PAD), lambda i: (i, 0, 0)),
                  pl.BlockSpec((None, 1, 1, RWKV_PROJ_PAD), lambda i: (layer, i, 0, 0)),
                  pl.BlockSpec((None, 1, RWKV_WIDTH, HEAD_DIM), lambda i: (layer, i, 0, 0)),
                  lay(1, RWKV_PROJ_PAD), vec, lay(LANES, RWKV_WIDTH), vec, lay(LANES, RWKV_WIDTH),
                  lay(GATE_PAD, RWKV_WIDTH), vec, vec, vec, vec, vec,
                  pl.BlockSpec((RWKV_WIDTH, HEAD_DIM), lambda i: (0, 0))],
        out_specs=[pl.BlockSpec((1, 1, RWKV_WIDTH), lambda i: (i, 0, 0)),
                   pl.BlockSpec((1, RWKV_WIDTH, HEAD_DIM), lambda i: (i, 0, 0))],
        out_shape=[jax.ShapeDtypeStruct((s, 1, RWKV_WIDTH), BF16),
                   jax.ShapeDtypeStruct((s, RWKV_WIDTH, HEAD_DIM), F32)],
        compiler_params=_params(1),
        name="wkv7_step",
    )(prw_rows, shift_rows, state, wp["mix"], wp["w0"], wp["decay_up"], wp["a0"], wp["iclr_up"], wp["gate_up"],
      wp["k_k"], wp["k_a"], wp["r_k"], wp["ln_w"], wp["ln_b"], jnp.asarray(fold, BF16))


def _out_proj_kernel(att_ref, rw_ref, x_ref, gt_ref, sh_ref, sc_ref, g_ref, w_ref, x1_ref, h_ref):
    y = _dot(att_ref[0], w_ref[0:ATTN_WIDTH, :]) + _dot(rw_ref[0], w_ref[ATTN_WIDTH:, :])
    x1 = x_ref[0] + gt_ref[0] * y
    x1_ref[0] = x1
    h_ref[0] = _rmsnorm_mod(x1, g_ref[...], sh_ref[0], sc_ref[0]).astype(BF16)


def _out_proj(att, rw, x, gate, shift, scale, g, w, layer, *, tm):
    b, t, d = x.shape
    r = gate.shape[1]
    row = lambda width: pl.BlockSpec((1, tm, width), lambda i, m: (i, m, 0))
    mod = pl.BlockSpec((1, r, d), lambda i, m: (i, 0, 0))
    return pl.pallas_call(
        _out_proj_kernel,
        grid=(b, t // tm),
        in_specs=[row(ATTN_WIDTH), row(RWKV_WIDTH), row(d), mod, mod, mod,
                  pl.BlockSpec((None, 1, d), lambda i, m: (layer, 0, 0)),
                  pl.BlockSpec((None, d, d), lambda i, m: (layer, 0, 0))],
        out_specs=[row(d), row(d)],
        out_shape=[jax.ShapeDtypeStruct((b, t, d), F32), jax.ShapeDtypeStruct((b, t, d), BF16)],
        compiler_params=_params(2),
        name="out_proj",
    )(att, rw, x, gate, shift, scale, g, w)


def _ffn_kernel(h_ref, x_ref, gt_ref, wg_ref, wu_ref, wo_ref, gf_ref, o_ref, acc_ref, *, final_norm):
    j = pl.program_id(2)

    @pl.when(j == 0)
    def _():
        acc_ref[...] = jnp.zeros_like(acc_ref)

    h = h_ref[0]
    gate = _dot(h, wg_ref[...])
    up = _dot(h, wu_ref[...])
    act = (gate * jax.nn.sigmoid(gate) * up).astype(BF16)
    acc_ref[...] += _dot(act, wo_ref[...])

    @pl.when(j == pl.num_programs(2) - 1)
    def _():
        x2 = x_ref[0] + gt_ref[0] * acc_ref[...]
        if final_norm:
            ms = jnp.mean(x2 * x2, axis=-1, keepdims=True)
            x2 = x2 * lax.rsqrt(ms + RMS_EPS) * gf_ref[...]
        o_ref[0] = x2


def _ffn(h, x, gate, w_in, w_out, g_final, layer, *, tm, th, final_norm):
    b, t, d = x.shape
    r = gate.shape[1]
    n_h = FFN_HIDDEN // th
    return pl.pallas_call(
        functools.partial(_ffn_kernel, final_norm=final_norm),
        grid=(b, t // tm, n_h),
        in_specs=[pl.BlockSpec((1, tm, d), lambda i, m, j: (i, m, 0)),
                  pl.BlockSpec((1, tm, d), lambda i, m, j: (i, m, 0)),
                  pl.BlockSpec((1, r, d), lambda i, m, j: (i, 0, 0)),
                  pl.BlockSpec((None, d, th), lambda i, m, j: (layer, 0, j)),
                  pl.BlockSpec((None, d, th), lambda i, m, j: (layer, 0, j + n_h)),
                  pl.BlockSpec((None, th, d), lambda i, m, j: (layer, j, 0)),
                  pl.BlockSpec((1, d), lambda i, m, j: (0, 0))],
        out_specs=pl.BlockSpec((1, tm, d), lambda i, m, j: (i, m, 0)),
        out_shape=jax.ShapeDtypeStruct((b, t, d), F32),
        scratch_shapes=[pltpu.VMEM((tm, d), F32)],
        compiler_params=_params(3),
        name="ffn",
    )(h, x, gate, w_in, w_in, w_out, g_final)


def _tiles(t_prompt, n_sample):
    tiles = dict(proj_tm=1024, proj_tn=512, out_tm=512, ffn_tm=512, ffn_th=512, wkv_chunk=64)
    assert t_prompt % tiles["proj_tm"] == 0 and t_prompt % tiles["out_tm"] == 0 and t_prompt % tiles["ffn_tm"] == 0
    assert t_prompt % tiles["wkv_chunk"] == 0 and t_prompt % WINDOW == 0
    assert QKV_COLS % tiles["proj_tn"] == 0 and RWKV_PROJ_PAD % tiles["proj_tn"] == 0
    assert FFN_HIDDEN % tiles["ffn_th"] == 0 and n_sample % SUBLANES == 0
    return tiles


def _pad_cols(x, width):
    return jnp.pad(x, [(0, 0)] * (x.ndim - 1) + [(0, width - x.shape[-1])])


def _pad_rows(x, rows, offset=0):
    return jnp.pad(x, [(0, 0), (offset, rows - offset - x.shape[1]), (0, 0)])


def kernel(x_prompt, x_sample, cache_k, cache_v, state_wkv, state_shift, c_prompt, c_sample, w_ada, b_ada, g_norm_mix, g_norm_ffn, w_in, w_out, attn_sinks, mix_shift, decay_w0, decay_up, iclr_a0, iclr_up, gate_up, k_k, k_a, r_k, ln_x_w, ln_x_b, w_ffn_in, w_ffn_out, g_norm_final):
    n_layers = w_in.shape[0]
    bp, tp, d = x_prompt.shape
    bs = x_sample.shape[0]
    tiles = _tiles(tp, bs)

    w_in_b = _pad_cols(w_in, QKV_COLS + RWKV_PROJ_PAD).astype(BF16)
    w_out_b = w_out.astype(BF16)
    w_ffn_in_b = w_ffn_in.astype(BF16)
    w_ffn_out_b = w_ffn_out.astype(BF16)
    vec = lambda x: x.reshape(n_layers, 1, -1)
    wp = dict(
        mix=vec(_pad_cols(mix_shift, RWKV_PROJ_PAD)), w0=vec(decay_w0), a0=vec(iclr_a0),
        decay_up=_pad_rows(decay_up, LANES).astype(BF16),
        iclr_up=_pad_rows(iclr_up, LANES, D_DECAY_LORA).astype(BF16),
        gate_up=_pad_rows(gate_up, GATE_PAD).astype(BF16),
        k_k=vec(k_k), k_a=vec(k_a), r_k=vec(r_k), ln_w=vec(ln_x_w), ln_b=vec(ln_x_b))
    g_mix, g_ffn = vec(g_norm_mix), vec(g_norm_ffn)
    g_final = g_norm_final.reshape(1, d)
    sinks_flat = attn_sinks.reshape(-1)
    sinks_col = attn_sinks.reshape(n_layers, N_Q_HEADS, 1)
    cache_k2 = cache_k.reshape(n_layers, bs, WINDOW, KV_WIDTH)
    cache_v2 = cache_v.reshape(n_layers, bs, WINDOW, KV_WIDTH)
    shift_rows = _pad_cols(state_shift, RWKV_PROJ_PAD).reshape(n_layers, bs, 1, RWKV_PROJ_PAD)
    state_rows = state_wkv.reshape(n_layers, bs, RWKV_WIDTH, HEAD_DIM)
    zero_shift = jnp.zeros((bp, 1, RWKV_PROJ_PAD), F32)
    zero_state = jnp.zeros((bp, RWKV_WIDTH, HEAD_DIM), F32)

    mod = _ada(jnp.concatenate([c_prompt, c_sample], axis=0), w_ada, b_ada)

    def mods(layer):
        parts = [mod[layer, :, i * d:(i + 1) * d] for i in range(6)]
        return ([p[:bp].reshape(bp, 1, d) for p in parts], [p[bp:].reshape(1, bs, d) for p in parts])

    xp = x_prompt
    xs = x_sample.reshape(1, bs, d)
    outs_p = [[] for _ in range(4)]
    outs_s = [[] for _ in range(4)]
    for layer in range(n_layers):
        last = layer == n_layers - 1
        (sh1, sc1, gt1, sh2, sc2, gt2), (ssh1, ssc1, sgt1, ssh2, ssc2, sgt2) = mods(layer)

        qkv, prw = _norm_proj(xp, sh1, sc1, g_mix, w_in_b, layer, tm=tiles["proj_tm"], tn=tiles["proj_tn"])
        att = _attn_prompt(qkv, sinks_flat, layer)
        rw, state = _wkv(prw, zero_shift, zero_state, wp, layer, chunk=tiles["wkv_chunk"])
        x1, h2 = _out_proj(att, rw, xp, gt1, sh2, sc2, g_ffn, w_out_b, layer, tm=tiles["out_tm"])
        xp = _ffn(h2, x1, gt2, w_ffn_in_b, w_ffn_out_b, g_final, layer, tm=tiles["ffn_tm"], th=tiles["ffn_th"],
                  final_norm=last)
        outs_p[0].append(qkv[:, tp - WINDOW:, ATTN_WIDTH:ATTN_WIDTH + KV_WIDTH].reshape(bp, WINDOW, N_KV_HEADS, HEAD_DIM))
        outs_p[1].append(qkv[:, tp - WINDOW:, ATTN_WIDTH + KV_WIDTH:].reshape(bp, WINDOW, N_KV_HEADS, HEAD_DIM))
        outs_p[2].append(state.reshape(bp, N_RWKV_HEADS, HEAD_DIM, HEAD_DIM))
        outs_p[3].append(prw[:, tp - 1, :RWKV_PROJ])

        qkv_s, prw_s = _norm_proj(xs, ssh1, ssc1, g_mix, w_in_b, layer, tm=bs, tn=tiles["proj_tn"])
        att_s, nk, nv = _attn_sample(qkv_s.reshape(bs, 1, QKV_COLS), cache_k2, cache_v2, sinks_col, layer)
        prw_tok = prw_s.reshape(bs, 1, RWKV_PROJ_PAD)
        rw_s, state_s = _wkv_step(prw_tok, shift_rows, state_rows, wp, layer)
        x1_s, h2_s = _out_proj(att_s.reshape(1, bs, ATTN_WIDTH), rw_s.reshape(1, bs, RWKV_WIDTH), xs,
                               sgt1, ssh2, ssc2, g_ffn, w_out_b, layer, tm=bs)
        xs = _ffn(h2_s, x1_s, sgt2, w_ffn_in_b, w_ffn_out_b, g_final, layer, tm=bs, th=tiles["ffn_th"],
                  final_norm=last)
        outs_s[0].append(nk.reshape(bs, WINDOW, N_KV_HEADS, HEAD_DIM))
        outs_s[1].append(nv.reshape(bs, WINDOW, N_KV_HEADS, HEAD_DIM))
        outs_s[2].append(state_s.reshape(bs, N_RWKV_HEADS, HEAD_DIM, HEAD_DIM))
        outs_s[3].append(prw_tok[:, 0, :RWKV_PROJ])

    stack = lambda xs_: jnp.stack(xs_, axis=0)
    return (xp, xs.reshape(bs, 1, d),
            stack(outs_p[0]), stack(outs_p[1]), stack(outs_p[2]), stack(outs_p[3]),
            stack(outs_s[0]), stack(outs_s[1]), stack(outs_s[2]), stack(outs_s[3]))
```

```python
import functools
import math

import numpy as np
import jax
import jax.numpy as jnp
from jax import lax
from jax.experimental import pallas as pl
from jax.experimental.pallas import tpu as pltpu

F32 = jnp.float32
BF16 = jnp.bfloat16

D_MODEL = 2048
HEAD_DIM = 64
ATTN_WIDTH = D_MODEL // 2
RWKV_WIDTH = D_MODEL - ATTN_WIDTH
N_Q_HEADS = ATTN_WIDTH // HEAD_DIM
N_KV_HEADS = N_Q_HEADS // 4
GQA_REP = N_Q_HEADS // N_KV_HEADS
KV_WIDTH = N_KV_HEADS * HEAD_DIM
N_RWKV_HEADS = RWKV_WIDTH // HEAD_DIM
WINDOW = 128
D_DECAY_LORA = 64
D_ICLR_LORA = 64
D_GATE_LORA = 160
RWKV_PROJ = 3 * RWKV_WIDTH + D_DECAY_LORA + D_ICLR_LORA + D_GATE_LORA
QKV_COLS = ATTN_WIDTH + 2 * KV_WIDTH
FFN_HIDDEN = 5632
RMS_EPS = 1e-5
GN_EPS = 64e-5
NEG_INF = -1e30

LANES = 128
MXU_DIM = 256
VMEM_LIMIT_BYTES = 56 * 1024 * 1024

LORA_OFF = 3 * RWKV_WIDTH
GATE_OFF = LORA_OFF + D_DECAY_LORA + D_ICLR_LORA
GATE_PAD = 2 * LANES
RWKV_PROJ_PAD = 3584
HEADS_PER_GROUP = MXU_DIM // HEAD_DIM
N_GROUPS = N_RWKV_HEADS // HEADS_PER_GROUP

ALIBI_SLOPES = tuple(float(s) for s in np.exp2(-8.0 * np.arange(1, N_Q_HEADS + 1, dtype=np.float32) / N_Q_HEADS))


def _params(n_axes):
    return pltpu.CompilerParams(dimension_semantics=("arbitrary",) * n_axes,
                                vmem_limit_bytes=VMEM_LIMIT_BYTES)


def _dot(a, b):
    return jnp.dot(a, b, preferred_element_type=F32)


def _dot_nt(a, b):
    return lax.dot_general(a, b, (((1,), (1,)), ((), ())), preferred_element_type=F32)


def _dot_tn(a, b):
    return lax.dot_general(a, b, (((0,), (0,)), ((), ())), preferred_element_type=F32)


def _split(x, parts):
    out = []
    for _ in range(parts):
        p = x.astype(BF16)
        out.append(p)
        x = x - p.astype(F32)
    return out


def _rmsnorm_mod(x, g, shift, scale):
    ms = jnp.mean(x * x, axis=-1, keepdims=True)
    y = x * lax.rsqrt(ms + RMS_EPS) * g
    return y * (1.0 + scale) + shift


def _ada_kernel(c_ref, w_ref, b_ref, o_ref):
    c = c_ref[...]
    s = (c * jax.nn.sigmoid(c)).astype(BF16)
    o_ref[...] = _dot(s, w_ref[...].astype(BF16)) + b_ref[...]


def _ada(c_all, w_ada, b_ada):
    n_layers, d, n = w_ada.shape
    rows = c_all.shape[0]
    tn = 1024
    return pl.pallas_call(
        _ada_kernel,
        grid=(n_layers, n // tn),
        in_specs=[pl.BlockSpec((rows, d), lambda l, j: (0, 0)),
                  pl.BlockSpec((None, d, tn), lambda l, j: (l, 0, j)),
                  pl.BlockSpec((None, 1, tn), lambda l, j: (l, 0, j))],
        out_specs=pl.BlockSpec((None, rows, tn), lambda l, j: (l, 0, j)),
        out_shape=jax.ShapeDtypeStruct((n_layers, rows, n), F32),
        compiler_params=_params(2),
        name="ada_mod",
    )(c_all, w_ada, b_ada.reshape(n_layers, 1, n))


NORM_STRIP = 32


def _norm_proj_kernel(x_ref, sh_ref, sc_ref, g_ref, w_ref, qkv_ref, prw_ref, h_scr, *, n_qkv_tiles):
    j = pl.program_id(2)
    tm = h_scr.shape[0]

    @pl.when(j == 0)
    def _():
        gain = g_ref[...] * (1.0 + sc_ref[0])
        shift = sh_ref[0]
        if gain.shape[0] == 1:
            def strip(i, carry):
                rows = pl.ds(pl.multiple_of(i * NORM_STRIP, NORM_STRIP), NORM_STRIP)
                x = x_ref[0, rows, :]
                ms = jnp.mean(x * x, axis=-1, keepdims=True)
                h_scr[rows, :] = (x * lax.rsqrt(ms + RMS_EPS) * gain + shift).astype(BF16)
                return carry
            lax.fori_loop(0, tm // NORM_STRIP, strip, 0, unroll=4)
        else:
            x = x_ref[0]
            ms = jnp.mean(x * x, axis=-1, keepdims=True)
            h_scr[...] = (x * lax.rsqrt(ms + RMS_EPS) * gain + shift).astype(BF16)

    @pl.when(j < n_qkv_tiles)
    def _():
        qkv_ref[0] = _dot(h_scr[...], w_ref[...])

    @pl.when(j >= n_qkv_tiles)
    def _():
        prw_ref[0] = _dot(h_scr[...], w_ref[...])


def _norm_proj(x, shift, scale, g, w, layer, *, tm, tn):
    b, t, d = x.shape
    r = shift.shape[1]
    n_qkv_tiles = QKV_COLS // tn
    n_tiles = (QKV_COLS + RWKV_PROJ_PAD) // tn
    return pl.pallas_call(
        functools.partial(_norm_proj_kernel, n_qkv_tiles=n_qkv_tiles),
        grid=(b, t // tm, n_tiles),
        in_specs=[pl.BlockSpec((1, tm, d), lambda i, m, j: (i, m, 0)),
                  pl.BlockSpec((1, r, d), lambda i, m, j: (i, 0, 0)),
                  pl.BlockSpec((1, r, d), lambda i, m, j: (i, 0, 0)),
                  pl.BlockSpec((None, 1, d), lambda i, m, j: (layer, 0, 0)),
                  pl.BlockSpec((None, d, tn), lambda i, m, j: (layer, 0, j))],
        out_specs=[pl.BlockSpec((1, tm, tn), lambda i, m, j: (i, m, jnp.minimum(j, n_qkv_tiles - 1))),
                   pl.BlockSpec((1, tm, tn), lambda i, m, j: (i, m, jnp.maximum(j - n_qkv_tiles, 0)))],
        out_shape=[jax.ShapeDtypeStruct((b, t, QKV_COLS), F32),
                   jax.ShapeDtypeStruct((b, t, RWKV_PROJ_PAD), F32)],
        scratch_shapes=[pltpu.VMEM((tm, d), BF16)],
        compiler_params=_params(3),
        name="norm_proj",
    )(x, shift, scale, g, w)


LOG2E = math.log2(math.e)


def _attn_bias_table():
    t = np.arange(WINDOW)[:, None]
    j = np.arange(2 * WINDOW)[None, :]
    dist = t + WINDOW - j
    valid = (dist >= 0) & (dist <= WINDOW)
    slopes = np.asarray(ALIBI_SLOPES, np.float32)[:, None, None]
    bias = np.where(valid[None], -slopes * dist[None].astype(np.float32) * np.float32(LOG2E), np.float32(NEG_INF))
    first = np.where((j >= WINDOW)[None], bias, np.float32(NEG_INF))
    return np.stack([first, bias]).astype(np.float32)


def _attn_prompt_kernel(sink_ref, bias_ref, q_ref, kc_ref, vc_ref, kp_ref, vp_ref, o_ref, *, layer):
    q = q_ref[0] * (HEAD_DIM ** -0.5 * LOG2E)
    kc, vc, kp, vp = kc_ref[0], vc_ref[0], kp_ref[0], vp_ref[0]
    kv_heads, heads = range(N_KV_HEADS), range(N_Q_HEADS)
    ksl = [slice(g * HEAD_DIM, (g + 1) * HEAD_DIM) for g in kv_heads]
    k_band = [jnp.concatenate([kp[:, ksl[g]], kc[:, ksl[g]]], axis=0).astype(BF16) for g in kv_heads]
    v_band = [jnp.concatenate([vp[:, ksl[g]], vc[:, ksl[g]]], axis=0).astype(BF16) for g in kv_heads]
    lg = [_dot_nt(q[:, h * HEAD_DIM:(h + 1) * HEAD_DIM].astype(BF16), k_band[h // GQA_REP]) + bias_ref[0, h]
          for h in heads]
    sink = [sink_ref[layer * N_Q_HEADS + h] * LOG2E for h in heads]
    m = [jnp.maximum(jnp.max(lg[h], axis=-1, keepdims=True), sink[h]) for h in heads]
    p = [jnp.exp2(lg[h] - m[h]).astype(BF16) for h in heads]
    ones = jnp.ones((2 * WINDOW, HEAD_DIM), BF16)
    den = [_dot(p[h], ones) + jnp.exp2(sink[h] - m[h]) for h in heads]
    outs = [_dot(p[h], v_band[h // GQA_REP]) * (1.0 / den[h]) for h in heads]
    o_ref[0] = jnp.concatenate(outs, axis=1).astype(BF16)


def _attn_prompt(qkv, sinks_flat, layer):
    b, t, _ = qkv.shape
    kcol = ATTN_WIDTH // KV_WIDTH
    prev = lambda i, m: (i, jnp.maximum(m - 1, 0), kcol)
    prev_v = lambda i, m: (i, jnp.maximum(m - 1, 0), kcol + 1)
    return pl.pallas_call(
        functools.partial(_attn_prompt_kernel, layer=layer),
        grid=(b, t // WINDOW),
        in_specs=[pl.BlockSpec(memory_space=pltpu.SMEM),
                  pl.BlockSpec((1, N_Q_HEADS, WINDOW, 2 * WINDOW), lambda i, m: (jnp.minimum(m, 1), 0, 0, 0)),
                  pl.BlockSpec((1, WINDOW, ATTN_WIDTH), lambda i, m: (i, m, 0)),
                  pl.BlockSpec((1, WINDOW, KV_WIDTH), lambda i, m: (i, m, kcol)),
                  pl.BlockSpec((1, WINDOW, KV_WIDTH), lambda i, m: (i, m, kcol + 1)),
                  pl.BlockSpec((1, WINDOW, KV_WIDTH), prev),
                  pl.BlockSpec((1, WINDOW, KV_WIDTH), prev_v)],
        out_specs=pl.BlockSpec((1, WINDOW, ATTN_WIDTH), lambda i, m: (i, m, 0)),
        out_shape=jax.ShapeDtypeStruct((b, t, ATTN_WIDTH), BF16),
        compiler_params=_params(2),
        name="attn_prompt",
    )(sinks_flat, jnp.asarray(_attn_bias_table()), qkv, qkv, qkv, qkv, qkv)


def _attn_sample_kernel(row_ref, kc_ref, vc_ref, sink_ref, slope_ref, spread_ref, gather_ref,
                        o_ref, nk_ref, nv_ref):
    row = row_ref[0]
    q = row[:, :ATTN_WIDTH] * (HEAD_DIM ** -0.5)
    k_new = row[:, ATTN_WIDTH:ATTN_WIDTH + KV_WIDTH]
    v_new = row[:, ATTN_WIDTH + KV_WIDTH:]
    kc, vc = kc_ref[0], vc_ref[0]
    head_row = lax.broadcasted_iota(jnp.int32, (N_Q_HEADS, ATTN_WIDTH), 0)
    head_lane = lax.broadcasted_iota(jnp.int32, (N_Q_HEADS, ATTN_WIDTH), 1) // HEAD_DIM
    own = head_row == head_lane
    q_rows = jnp.where(own, jnp.broadcast_to(q, (N_Q_HEADS, ATTN_WIDTH)), 0.0).astype(BF16)
    q_grp = _dot(q_rows, spread_ref[...])
    k_new_b = k_new.astype(BF16).astype(F32)
    lg = _dot_nt(q_grp.astype(BF16), kc.astype(BF16))
    lg_new = jnp.sum(q_grp * k_new_b, axis=-1, keepdims=True)
    dist = (WINDOW - lax.broadcasted_iota(jnp.int32, (N_Q_HEADS, WINDOW), 1)).astype(F32)
    lg = lg - slope_ref[...] * dist
    sink = sink_ref[...]
    m = jnp.maximum(jnp.maximum(jnp.max(lg, axis=-1, keepdims=True), lg_new), sink)
    p = jnp.exp(lg - m)
    p_new = jnp.exp(lg_new - m)
    den = jnp.sum(p, axis=-1, keepdims=True) + p_new + jnp.exp(sink - m)
    out = _dot((p / den).astype(BF16), vc.astype(BF16))
    out = out + (p_new / den).astype(BF16).astype(F32) * v_new.astype(BF16).astype(F32)
    grp_row = lax.broadcasted_iota(jnp.int32, (N_Q_HEADS, KV_WIDTH), 0) // GQA_REP
    grp_lane = lax.broadcasted_iota(jnp.int32, (N_Q_HEADS, KV_WIDTH), 1) // HEAD_DIM
    out = jnp.where(grp_row == grp_lane, out, 0.0).astype(BF16)
    full = _dot(out, gather_ref[...])
    o_ref[0] = jnp.sum(jnp.where(own, full, 0.0), axis=0, keepdims=True).astype(BF16)
    last = lax.broadcasted_iota(jnp.int32, (WINDOW, KV_WIDTH), 0) == WINDOW - 1
    nk_ref[0] = jnp.where(last, k_new, pltpu.roll(kc, WINDOW - 1, axis=0))
    nv_ref[0] = jnp.where(last, v_new, pltpu.roll(vc, WINDOW - 1, axis=0))


def _attn_sample(qkv_rows, cache_k, cache_v, sinks_col, layer):
    s = qkv_rows.shape[0]
    spread = np.zeros((ATTN_WIDTH, KV_WIDTH), np.float32)
    for h in range(N_Q_HEADS):
        for d in range(HEAD_DIM):
            spread[h * HEAD_DIM + d, (h // GQA_REP) * HEAD_DIM + d] = 1.0
    slopes = np.asarray(ALIBI_SLOPES, np.float32).reshape(N_Q_HEADS, 1)
    win = (1, WINDOW, KV_WIDTH)
    return pl.pallas_call(
        _attn_sample_kernel,
        grid=(s,),
        in_specs=[pl.BlockSpec((1, 1, QKV_COLS), lambda i: (i, 0, 0)),
                  pl.BlockSpec((None,) + win, lambda i: (layer, i, 0, 0)),
                  pl.BlockSpec((None,) + win, lambda i: (layer, i, 0, 0)),
                  pl.BlockSpec((None, N_Q_HEADS, 1), lambda i: (layer, 0, 0)),
                  pl.BlockSpec((N_Q_HEADS, 1), lambda i: (0, 0)),
                  pl.BlockSpec((ATTN_WIDTH, KV_WIDTH), lambda i: (0, 0)),
                  pl.BlockSpec((KV_WIDTH, ATTN_WIDTH), lambda i: (0, 0))],
        out_specs=[pl.BlockSpec((1, 1, ATTN_WIDTH), lambda i: (i, 0, 0)),
                   pl.BlockSpec(win, lambda i: (i, 0, 0)),
                   pl.BlockSpec(win, lambda i: (i, 0, 0))],
        out_shape=[jax.ShapeDtypeStruct((s, 1, ATTN_WIDTH), BF16),
                   jax.ShapeDtypeStruct((s, WINDOW, KV_WIDTH), F32),
                   jax.ShapeDtypeStruct((s, WINDOW, KV_WIDTH), F32)],
        compiler_params=_params(1),
        name="attn_sample",
    )(qkv_rows, cache_k, cache_v, sinks_col, jnp.asarray(slopes),
      jnp.asarray(spread, BF16), jnp.asarray(spread.T, BF16))


def _wkv_kernel(p_ref, shift0_ref, s0_ref, mix_ref, w0_ref, dup_ref, a0_ref, iup_ref, gup_ref,
                kk_ref, ka_ref, rk_ref, lnw_ref, lnb_ref, rw_ref, sout_ref, s_scr, prev_scr,
                arb_scr, aak_scr, ar_scr, vbd_scr, vb_scr, bk_scr, total_scr, gate_scr, bonus_scr, ncat_scr, ecat_scr,
                *, chunk):
    step = pl.program_id(1)
    n_steps = pl.num_programs(1)
    staged = (arb_scr, aak_scr, ar_scr, vbd_scr, vb_scr, bk_scr, total_scr, gate_scr, bonus_scr)
    bd_rows = HEADS_PER_GROUP * chunk
    groups = range(N_GROUPS)
    cols = [slice(g * MXU_DIM, (g + 1) * MXU_DIM) for g in groups]
    iota = lambda shape, dim: lax.broadcasted_iota(jnp.int32, shape, dim)
    same_head = iota((MXU_DIM, MXU_DIM), 0) // HEAD_DIM == iota((MXU_DIM, MXU_DIM), 1) // HEAD_DIM
    ones_bd = jnp.where(same_head, 1.0, 0.0).astype(BF16)
    tile_sel = jnp.where(iota((HEAD_DIM, MXU_DIM), 0) == iota((HEAD_DIM, MXU_DIM), 1) % HEAD_DIM, 1.0, 0.0).astype(BF16)
    fold_sel = jnp.where(iota((MXU_DIM, HEAD_DIM), 0) % HEAD_DIM == iota((MXU_DIM, HEAD_DIM), 1), 1.0, 0.0).astype(BF16)

    @pl.when(step == 0)
    def _():
        prev_scr[...] = shift0_ref[0]
        for g in groups:
            rows = s0_ref[0, cols[g], :]
            tiled = sum(_dot(part, tile_sel) for part in _split(rows, 3))
            s_scr[g] = jnp.where(same_head, tiled, 0.0)
        for ref in staged + (ncat_scr, ecat_scr):
            ref[...] = jnp.zeros_like(ref)
        total_scr[...] = jnp.ones_like(total_scr)

    row = iota((chunk, 1), 0)

    def shifted(lo, width):
        cur = p_ref[0, :, lo:lo + width]
        prev = jnp.where(row == 0, prev_scr[:, lo:lo + width], pltpu.roll(cur, 1, axis=0))
        return cur + (prev - cur) * mix_ref[:, lo:lo + width]

    def seg_sum(x):
        return _dot(x.astype(BF16), ones_bd)

    head_mask = [jnp.where(iota((chunk, MXU_DIM), 1) // HEAD_DIM == h, 1.0, 0.0).astype(BF16)
                 for h in range(HEADS_PER_GROUP)]

    def head_blocks(x):
        return jnp.concatenate([x * m for m in head_mask], axis=0)

    slot = step % 2
    inv_n = 1.0 / HEAD_DIM
    t_idx = iota((chunk, 2 * bd_rows), 0)
    s_idx = iota((chunk, 2 * bd_rows), 1) % chunk
    strict = (s_idx < t_idx)[:, :bd_rows]
    same_block = jnp.where(iota((bd_rows, bd_rows), 0) // chunk == iota((bd_rows, bd_rows), 1) // chunk,
                           1.0, 0.0).astype(BF16)
    eye_cat = jnp.where(t_idx == s_idx, 1.0, 0.0)[:, :bd_rows].astype(BF16)
    block_diag = lambda x: jnp.concatenate([x] * HEADS_PER_GROUP, axis=0) * same_block


    st_ar, st_ak, st_r, st_vbd, st_v, st_bk, st_total, st_gate, st_bonus = (
        [ref[slot, g] for g in groups] for ref in staged)
    st_e = [ecat_scr[g] for g in groups]
    s_b = [s_scr[g].astype(BF16) for g in groups]
    uy = [_dot_nt(st_ar[g], s_b[g]) for g in groups]

    e_cat = [ncat_scr[g] for g in groups]
    m_b = [x.astype(BF16) for x in e_cat]
    m_b = [_dot(x, block_diag(x)).astype(BF16) for x in m_b]
    levels = int(math.log2(chunk)) - 1

    def inverse_level(lvl, e_cat, m_b):
        t_bd = [block_diag(x.astype(BF16) + eye_cat) for x in e_cat]
        if lvl < levels - 1:
            both = [_dot(m_b[g], jnp.concatenate([t_bd[g], block_diag(m_b[g])], axis=1)) for g in groups]
            return ([e_cat[g] + both[g][:, :bd_rows] for g in groups], [x[:, bd_rows:].astype(BF16) for x in both])
        return [e_cat[g] + _dot(m_b[g], t_bd[g]) for g in groups], None

    lora_in = shifted(LORA_OFF, LANES)
    lora_tanh = jnp.tanh(lora_in).astype(BF16)
    lora_lin = lora_in.astype(BF16)
    gate_sig = jax.nn.sigmoid(shifted(GATE_OFF, GATE_PAD)).astype(BF16)
    r = [shifted(g * MXU_DIM, MXU_DIM) for g in groups]

    rhs = [uy[g][:chunk] + _dot(st_ak[g], st_vbd[g]) for g in groups]
    e_cat, m_b = inverse_level(0, e_cat, m_b)

    k = [shifted(RWKV_WIDTH + g * MXU_DIM, MXU_DIM) for g in groups]
    v = [shifted(2 * RWKV_WIDTH + g * MXU_DIM, MXU_DIM) for g in groups]
    prev_scr[...] = p_ref[0, chunk - 1:chunk, :]

    p = [rhs[g] + _dot(st_e[g], head_blocks(rhs[g].astype(BF16))) for g in groups]
    p_b = [x.astype(BF16) for x in p]
    if levels > 1:
        e_cat, m_b = inverse_level(1, e_cat, m_b)

    logw = [-math.exp(-0.5) * jax.nn.sigmoid(w0_ref[:, cols[g]] + _dot(lora_tanh, dup_ref[:, cols[g]])) for g in groups]
    a = [jax.nn.sigmoid(a0_ref[:, cols[g]] + _dot(lora_lin, iup_ref[:, cols[g]])) for g in groups]
    gate = [_dot(gate_sig, gup_ref[:, cols[g]]) for g in groups]

    y = [uy[g][chunk:] + _dot(st_r[g], jnp.concatenate([head_blocks(p_b[g]), st_vbd[g]], axis=0)) for g in groups]
    for g in groups:
        upd = _dot_tn(jnp.concatenate([p_b[g], st_v[g]], axis=0), st_bk[g])
        s_scr[g] = s_scr[g] * st_total[g] + jnp.where(same_head, upd, 0.0)
    if levels > 2:
        e_cat, m_b = inverse_level(2, e_cat, m_b)

    kk = [k[g] * kk_ref[:, cols[g]] for g in groups]
    kk = [kk[g] / jnp.maximum(jnp.sqrt(seg_sum(kk[g] * kk[g])), 1e-12) for g in groups]
    k = [k[g] * (1.0 + (a[g] - 1.0) * ka_ref[:, cols[g]]) for g in groups]

    dev = [y[g] - seg_sum(y[g]) * inv_n for g in groups]
    if levels > 3:
        e_cat, m_b = inverse_level(3, e_cat, m_b)

    tri = jnp.where(iota((chunk, chunk), 0) >= iota((chunk, chunk), 1), 1.0, 0.0).astype(BF16)
    cum = [sum(_dot(tri, part) for part in _split(logw[g], 3)) for g in groups]

    var = [seg_sum(x * x) * inv_n for x in dev]
    for lvl in range(4, levels):
        e_cat, m_b = inverse_level(lvl, e_cat, m_b)

    e_pos = [jnp.exp(x) for x in cum]
    e_neg = [jnp.exp(-x) for x in cum]
    a_t = [-kk[g] * jnp.exp(cum[g] - logw[g]) for g in groups]
    b_t = [kk[g] * a[g] * e_neg[g] for g in groups]

    for g in groups:
        yn = dev[g] * lax.rsqrt(var[g] + GN_EPS) * lnw_ref[:, cols[g]] + lnb_ref[:, cols[g]]
        rw_ref[0, :, cols[g]] = ((yn + st_bonus[g]) * st_gate[g]).astype(BF16)

    k_t = [k[g] * e_neg[g] for g in groups]
    r_t = [r[g] * e_pos[g] for g in groups]
    total = [x[chunk - 1:chunk, :] for x in e_pos]
    ar_b = [jnp.concatenate([a_t[g], r_t[g]], axis=0).astype(BF16) for g in groups]
    xb =[head_blocks(x.astype(BF16)) for x in b_t]
    xk = [head_blocks(x.astype(BF16)) for x in k_t]
    v_bd = [head_blocks(x.astype(BF16)) for x in v]

    scores = [_dot_nt(ar_b[g], jnp.concatenate([xb[g], xk[g]], axis=0)) for g in groups]
    bonus = [seg_sum(r[g] * k[g] * rk_ref[:, cols[g]]) * v[g] for g in groups]
    for g in groups:
        ecat_scr[g] = e_cat[g].astype(BF16)
        ncat_scr[g] = jnp.where(strict, scores[g][:chunk, :bd_rows], 0.0)
        arb_scr[slot, g] = ar_b[g]
        aak_scr[slot, g] = jnp.where(strict, scores[g][:chunk, bd_rows:], 0.0).astype(BF16)
        ar_scr[slot, g] = jnp.where(s_idx <= t_idx, scores[g][chunk:], 0.0).astype(BF16)
        vbd_scr[slot, g] = v_bd[g]
        vb_scr[slot, g] = v[g].astype(BF16)
        bk_scr[slot, g] = jnp.concatenate([b_t[g] * total[g], k_t[g] * total[g]], axis=0).astype(BF16)
        total_scr[slot, g], gate_scr[slot, g], bonus_scr[slot, g] = total[g], gate[g], bonus[g]

    @pl.when(step == n_steps - 1)
    def _():
        for g in groups:
            folded = sum(_dot(part, fold_sel) for part in _split(s_scr[g], 3))
            sout_ref[0, cols[g], :] = folded


def _wkv(prw, shift0, s0, wp, layer, *, chunk):
    b, t, _ = prw.shape
    n_chunks = t // chunk
    bd_rows = HEADS_PER_GROUP * chunk
    lay = lambda *blk: pl.BlockSpec((None,) + blk, lambda i, c: (layer,) + (0,) * len(blk))
    vec = lay(1, RWKV_WIDTH)
    grp = lambda rows, width, dtype: pltpu.VMEM((N_GROUPS, rows, width), dtype)
    two = lambda rows, width, dtype: pltpu.VMEM((2, N_GROUPS, rows, width), dtype)
    return pl.pallas_call(
        functools.partial(_wkv_kernel, chunk=chunk),
        grid=(b, n_chunks + 2),
        in_specs=[pl.BlockSpec((1, chunk, RWKV_PROJ_PAD), lambda i, c: (i, jnp.minimum(c, n_chunks - 1), 0)),
                  pl.BlockSpec((1, 1, RWKV_PROJ_PAD), lambda i, c: (i, 0, 0)),
                  pl.BlockSpec((1, RWKV_WIDTH, HEAD_DIM), lambda i, c: (i, 0, 0)),
                  lay(1, RWKV_PROJ_PAD), vec, lay(LANES, RWKV_WIDTH), vec, lay(LANES, RWKV_WIDTH),
                  lay(GATE_PAD, RWKV_WIDTH), vec, vec, vec, vec, vec],
        out_specs=[pl.BlockSpec((1, chunk, RWKV_WIDTH), lambda i, c: (i, jnp.maximum(c - 2, 0), 0)),
                   pl.BlockSpec((1, RWKV_WIDTH, HEAD_DIM), lambda i, c: (i, 0, 0))],
        out_shape=[jax.ShapeDtypeStruct((b, t, RWKV_WIDTH), BF16),
                   jax.ShapeDtypeStruct((b, RWKV_WIDTH, HEAD_DIM), F32)],
        scratch_shapes=[pltpu.VMEM((N_GROUPS, MXU_DIM, MXU_DIM), F32),
                        pltpu.VMEM((1, RWKV_PROJ_PAD), F32),
                        two(2 * chunk, MXU_DIM, BF16), two(chunk, bd_rows, BF16), two(chunk, 2 * bd_rows, BF16),
                        two(bd_rows, MXU_DIM, BF16), two(chunk, MXU_DIM, BF16), two(2 * chunk, MXU_DIM, BF16),
                        two(1, MXU_DIM, F32), two(chunk, MXU_DIM, F32), two(chunk, MXU_DIM, F32),
                        grp(chunk, bd_rows, F32), grp(chunk, bd_rows, BF16)],
        compiler_params=_params(2),
        name="wkv7",
    )(prw, shift0, s0, wp["mix"], wp["w0"], wp["decay_up"], wp["a0"], wp["iclr_up"], wp["gate_up"],
      wp["k_k"], wp["k_a"], wp["r_k"], wp["ln_w"], wp["ln_b"])


SUBLANES = 8


def _wkv_step_kernel(p_ref, shift_ref, s_ref, mix_ref, w0_ref, dup_ref, a0_ref, iup_ref, gup_ref,
                     kk_ref, ka_ref, rk_ref, lnw_ref, lnb_ref, fold_ref, rw_ref, sout_ref):
    iota = lambda shape, dim: lax.broadcasted_iota(jnp.int32, shape, dim)
    groups = range(N_GROUPS)
    cols = [slice(g * MXU_DIM, (g + 1) * MXU_DIM) for g in groups]
    ones_bd = jnp.where(iota((MXU_DIM, MXU_DIM), 0) // HEAD_DIM == iota((MXU_DIM, MXU_DIM), 1) // HEAD_DIM,
                        1.0, 0.0).astype(BF16)

    def seg_sum(x):
        return jnp.concatenate([_dot(x[:, cols[g]].astype(BF16), ones_bd) for g in groups], axis=1)

    cur = jnp.broadcast_to(p_ref[0], (SUBLANES, RWKV_PROJ_PAD))
    prev = jnp.broadcast_to(shift_ref[0], (SUBLANES, RWKV_PROJ_PAD))
    p = cur + (prev - cur) * mix_ref[...]
    r = p[:, 0:RWKV_WIDTH]
    k = p[:, RWKV_WIDTH:2 * RWKV_WIDTH]
    v = p[:, 2 * RWKV_WIDTH:3 * RWKV_WIDTH]
    lora_in = p[:, LORA_OFF:LORA_OFF + LANES]
    gate_in = p[:, GATE_OFF:GATE_OFF + GATE_PAD]
    w = w0_ref[...] + _dot(jnp.tanh(lora_in).astype(BF16), dup_ref[...])
    decay = jnp.exp(-math.exp(-0.5) * jax.nn.sigmoid(w))
    a = jax.nn.sigmoid(a0_ref[...] + _dot(lora_in.astype(BF16), iup_ref[...]))
    gate = _dot(jax.nn.sigmoid(gate_in).astype(BF16), gup_ref[...])
    kk = k * kk_ref[...]
    kk = kk / jnp.maximum(jnp.sqrt(seg_sum(kk * kk)), 1e-12)
    k = k * (1.0 + (a - 1.0) * ka_ref[...])

    own = iota((N_RWKV_HEADS, RWKV_WIDTH), 0) == iota((N_RWKV_HEADS, RWKV_WIDTH), 1) // HEAD_DIM
    spread = lambda x: jnp.where(own, jnp.broadcast_to(x[0:1], (N_RWKV_HEADS, RWKV_WIDTH)), 0.0)
    stacked = jnp.concatenate([spread(decay), spread(-kk), spread(kk * a), spread(k), spread(r)], axis=0)
    per_head = sum(_dot(part, fold_ref[...]) for part in _split(stacked, 3))
    w_h, a_h, b_h, k_h, r_h = (per_head[i * N_RWKV_HEADS:(i + 1) * N_RWKV_HEADS] for i in range(5))
    rows = lambda x: jnp.concatenate(
        [jnp.broadcast_to(x[h:h + 1, :], (HEAD_DIM, HEAD_DIM)) for h in range(N_RWKV_HEADS)], axis=0)

    eye = iota((LANES, LANES), 0) == iota((LANES, LANES), 1)
    ones_cols = jnp.ones((LANES, HEAD_DIM), BF16)
    v_col = jnp.concatenate(
        [sum(_dot(part, ones_cols) for part in
             _split(jnp.where(eye, jnp.broadcast_to(v[0:1, t * LANES:(t + 1) * LANES], (LANES, LANES)), 0.0), 3))
         for t in range(RWKV_WIDTH // LANES)], axis=0)

    s = s_ref[0]
    sa = jnp.sum(s * rows(a_h), axis=-1, keepdims=True)
    s_new = s * rows(w_h) + sa * rows(b_h) + v_col * rows(k_h)
    sout_ref[0] = s_new
    y_heads = _dot_nt(r_h.astype(BF16), s_new.astype(BF16))
    y = jnp.broadcast_to(jnp.sum(jnp.where(own, y_heads, 0.0), axis=0, keepdims=True), (SUBLANES, RWKV_WIDTH))

    inv_n = 1.0 / HEAD_DIM
    dev = y - seg_sum(y) * inv_n
    var = seg_sum(dev * dev) * inv_n
    y = dev * lax.rsqrt(var + GN_EPS) * lnw_ref[...] + lnb_ref[...]
    bonus = seg_sum(r * k * rk_ref[...]) * v
    rw_ref[0] = ((y + bonus) * gate)[0:1].astype(BF16)


def _wkv_step(prw_rows, shift_rows, state, wp, layer):
    s = prw_rows.shape[0]
    lay = lambda *blk: pl.BlockSpec((None,) + blk, lambda i: (layer,) + (0,) * len(blk))
    vec = lay(1, RWKV_WIDTH)
    fold = np.tile(np.eye(HEAD_DIM, dtype=np.float32), (N_RWKV_HEADS, 1))
    return pl.pallas_call(
        _wkv_step_kernel,
        grid=(s,),
        in_specs=[pl.BlockSpec((1, 1, RWKV_PROJ_PAD), lambda i: (i, 0, 0)),
                  pl.BlockSpec((None, 1, 1, RWKV_PROJ_PAD), lambda i: (layer, i, 0, 0)),
                  pl.BlockSpec((None, 1, RWKV_WIDTH, HEAD_DIM), lambda i: (layer, i, 0, 0)),
                  lay(1, RWKV_PROJ_PAD), vec, lay(LANES, RWKV_WIDTH), vec, lay(LANES, RWKV_WIDTH),
                  lay(GATE_PAD, RWKV_WIDTH), vec, vec, vec, vec, vec,
                  pl.BlockSpec((RWKV_WIDTH, HEAD_DIM), lambda i: (0, 0))],
        out_specs=[pl.BlockSpec((1, 1, RWKV_WIDTH), lambda i: (i, 0, 0)),
                   pl.BlockSpec((1, RWKV_WIDTH, HEAD_DIM), lambda i: (i, 0, 0))],
        out_shape=[jax.ShapeDtypeStruct((s, 1, RWKV_WIDTH), BF16),
                   jax.ShapeDtypeStruct((s, RWKV_WIDTH, HEAD_DIM), F32)],
        compiler_params=_params(1),
        name="wkv7_step",
    )(prw_rows, shift_rows, state, wp["mix"], wp["w0"], wp["decay_up"], wp["a0"], wp["iclr_up"], wp["gate_up"],
      wp["k_k"], wp["k_a"], wp["r_k"], wp["ln_w"], wp["ln_b"], jnp.asarray(fold, BF16))


def _out_proj_kernel(att_ref, rw_ref, x_ref, gt_ref, sh_ref, sc_ref, g_ref, w_ref, x1_ref, h_ref):
    y = _dot(att_ref[0], w_ref[0:ATTN_WIDTH, :]) + _dot(rw_ref[0], w_ref[ATTN_WIDTH:, :])
    x1 = x_ref[0] + gt_ref[0] * y
    x1_ref[0] = x1
    h_ref[0] = _rmsnorm_mod(x1, g_ref[...], sh_ref[0], sc_ref[0]).astype(BF16)


def _out_proj(att, rw, x, gate, shift, scale, g, w, layer, *, tm):
    b, t, d = x.shape
    r = gate.shape[1]
    row = lambda width: pl.BlockSpec((1, tm, width), lambda i, m: (i, m, 0))
    mod = pl.BlockSpec((1, r, d), lambda i, m: (i, 0, 0))
    return pl.pallas_call(
        _out_proj_kernel,
        grid=(b, t // tm),
        in_specs=[row(ATTN_WIDTH), row(RWKV_WIDTH), row(d), mod, mod, mod,
                  pl.BlockSpec((None, 1, d), lambda i, m: (layer, 0, 0)),
                  pl.BlockSpec((None, d, d), lambda i, m: (layer, 0, 0))],
        out_specs=[row(d), row(d)],
        out_shape=[jax.ShapeDtypeStruct((b, t, d), F32), jax.ShapeDtypeStruct((b, t, d), BF16)],
        compiler_params=_params(2),
        name="out_proj",
    )(att, rw, x, gate, shift, scale, g, w)


def _ffn_kernel(h_ref, x_ref, gt_ref, wg_ref, wu_ref, wo_ref, gf_ref, o_ref, acc_ref, *, final_norm):
    j = pl.program_id(2)

    @pl.when(j == 0)
    def _():
        acc_ref[...] = jnp.zeros_like(acc_ref)

    h = h_ref[0]
    gate = _dot(h, wg_ref[...])
    up = _dot(h, wu_ref[...])
    act = (gate * jax.nn.sigmoid(gate) * up).astype(BF16)
    acc_ref[...] += _dot(act, wo_ref[...])

    @pl.when(j == pl.num_programs(2) - 1)
    def _():
        x2 = x_ref[0] + gt_ref[0] * acc_ref[...]
        if final_norm:
            ms = jnp.mean(x2 * x2, axis=-1, keepdims=True)
            x2 = x2 * lax.rsqrt(ms + RMS_EPS) * gf_ref[...]
        o_ref[0] = x2


def _ffn(h, x, gate, w_in, w_out, g_final, layer, *, tm, th, final_norm):
    b, t, d = x.shape
    r = gate.shape[1]
    n_h = FFN_HIDDEN // th
    return pl.pallas_call(
        functools.partial(_ffn_kernel, final_norm=final_norm),
        grid=(b, t // tm, n_h),
        in_specs=[pl.BlockSpec((1, tm, d), lambda i, m, j: (i, m, 0)),
                  pl.BlockSpec((1, tm, d), lambda i, m, j: (i, m, 0)),
                  pl.BlockSpec((1, r, d), lambda i, m, j: (i, 0, 0)),
                  pl.BlockSpec((None, d, th), lambda i, m, j: (layer, 0, j)),
                  pl.BlockSpec((None, d, th), lambda i, m, j: (layer, 0, j + n_h)),
                  pl.BlockSpec((None, th, d), lambda i, m, j: (layer, j, 0)),
                  pl.BlockSpec((1, d), lambda i, m, j: (0, 0))],
        out_specs=pl.BlockSpec((1, tm, d), lambda i, m, j: (i, m, 0)),
        out_shape=jax.ShapeDtypeStruct((b, t, d), F32),
        scratch_shapes=[pltpu.VMEM((tm, d), F32)],
        compiler_params=_params(3),
        name="ffn",
    )(h, x, gate, w_in, w_in, w_out, g_final)


def _cast_pad_kernel(w_ref, o_ref):
    n = w_ref.shape[1]
    o_ref[:, :n] = w_ref[...].astype(BF16)
    o_ref[:, n:] = jnp.zeros((o_ref.shape[0], o_ref.shape[1] - n), BF16)


def _cast_pad(w, width, *, rows):
    n_layers, k, n = w.shape
    return pl.pallas_call(
        _cast_pad_kernel,
        grid=(n_layers, k // rows),
        in_specs=[pl.BlockSpec((None, rows, n), lambda l, i: (l, i, 0))],
        out_specs=pl.BlockSpec((None, rows, width), lambda l, i: (l, i, 0)),
        out_shape=jax.ShapeDtypeStruct((n_layers, k, width), BF16),
        compiler_params=_params(2),
        name="cast_pad",
    )(w)


def _tiles(t_prompt, n_sample):
    tiles = dict(proj_tm=1024, proj_tn=512, out_tm=512, ffn_tm=512, ffn_th=512, wkv_chunk=64, cast_rows=256)
    assert t_prompt % tiles["proj_tm"] == 0 and t_prompt % tiles["out_tm"] == 0 and t_prompt % tiles["ffn_tm"] == 0
    assert t_prompt % tiles["wkv_chunk"] == 0 and t_prompt % WINDOW == 0
    assert QKV_COLS % tiles["proj_tn"] == 0 and RWKV_PROJ_PAD % tiles["proj_tn"] == 0
    assert FFN_HIDDEN % tiles["ffn_th"] == 0 and n_sample % SUBLANES == 0
    return tiles


def _pad_cols(x, width):
    return jnp.pad(x, [(0, 0)] * (x.ndim - 1) + [(0, width - x.shape[-1])])


def _pad_rows(x, rows, offset=0):
    return jnp.pad(x, [(0, 0), (offset, rows - offset - x.shape[1]), (0, 0)])


def kernel(x_prompt, x_sample, cache_k, cache_v, state_wkv, state_shift, c_prompt, c_sample, w_ada, b_ada, g_norm_mix, g_norm_ffn, w_in, w_out, attn_sinks, mix_shift, decay_w0, decay_up, iclr_a0, iclr_up, gate_up, k_k, k_a, r_k, ln_x_w, ln_x_b, w_ffn_in, w_ffn_out, g_norm_final):
    n_layers = w_in.shape[0]
    bp, tp, d = x_prompt.shape
    bs = x_sample.shape[0]
    tiles = _tiles(tp, bs)

    w_in_b = _cast_pad(w_in, QKV_COLS + RWKV_PROJ_PAD, rows=tiles["cast_rows"])
    w_out_b = w_out.astype(BF16)
    w_ffn_in_b = w_ffn_in.astype(BF16)
    w_ffn_out_b = w_ffn_out.astype(BF16)
    vec = lambda x: x.reshape(n_layers, 1, -1)
    wp = dict(
        mix=vec(_pad_cols(mix_shift, RWKV_PROJ_PAD)), w0=vec(decay_w0), a0=vec(iclr_a0),
        decay_up=_pad_rows(decay_up, LANES).astype(BF16),
        iclr_up=_pad_rows(iclr_up, LANES, D_DECAY_LORA).astype(BF16),
        gate_up=_pad_rows(gate_up, GATE_PAD).astype(BF16),
        k_k=vec(k_k), k_a=vec(k_a), r_k=vec(r_k), ln_w=vec(ln_x_w), ln_b=vec(ln_x_b))
    g_mix, g_ffn = vec(g_norm_mix), vec(g_norm_ffn)
    g_final = g_norm_final.reshape(1, d)
    sinks_flat = attn_sinks.reshape(-1)
    sinks_col = attn_sinks.reshape(n_layers, N_Q_HEADS, 1)
    cache_k2 = cache_k.reshape(n_layers, bs, WINDOW, KV_WIDTH)
    cache_v2 = cache_v.reshape(n_layers, bs, WINDOW, KV_WIDTH)
    shift_rows = _pad_cols(state_shift, RWKV_PROJ_PAD).reshape(n_layers, bs, 1, RWKV_PROJ_PAD)
    state_rows = state_wkv.reshape(n_layers, bs, RWKV_WIDTH, HEAD_DIM)
    zero_shift = jnp.zeros((bp, 1, RWKV_PROJ_PAD), F32)
    zero_state = jnp.zeros((bp, RWKV_WIDTH, HEAD_DIM), F32)

    mod = _ada(jnp.concatenate([c_prompt, c_sample], axis=0), w_ada, b_ada)

    def mods(layer):
        parts = [mod[layer, :, i * d:(i + 1) * d] for i in range(6)]
        return ([p[:bp].reshape(bp, 1, d) for p in parts], [p[bp:].reshape(1, bs, d) for p in parts])

    xp = x_prompt
    xs = x_sample.reshape(1, bs, d)
    outs_p = [[] for _ in range(4)]
    outs_s = [[] for _ in range(4)]
    for layer in range(n_layers):
        last = layer == n_layers - 1
        (sh1, sc1, gt1, sh2, sc2, gt2), (ssh1, ssc1, sgt1, ssh2, ssc2, sgt2) = mods(layer)

        qkv, prw = _norm_proj(xp, sh1, sc1, g_mix, w_in_b, layer, tm=tiles["proj_tm"], tn=tiles["proj_tn"])
        att = _attn_prompt(qkv, sinks_flat, layer)
        rw, state = _wkv(prw, zero_shift, zero_state, wp, layer, chunk=tiles["wkv_chunk"])
        x1, h2 = _out_proj(att, rw, xp, gt1, sh2, sc2, g_ffn, w_out_b, layer, tm=tiles["out_tm"])
        xp = _ffn(h2, x1, gt2, w_ffn_in_b, w_ffn_out_b, g_final, layer, tm=tiles["ffn_tm"], th=tiles["ffn_th"],
                  final_norm=last)
        outs_p[0].append(qkv[:, tp - WINDOW:, ATTN_WIDTH:ATTN_WIDTH + KV_WIDTH].reshape(bp, WINDOW, N_KV_HEADS, HEAD_DIM))
        outs_p[1].append(qkv[:, tp - WINDOW:, ATTN_WIDTH + KV_WIDTH:].reshape(bp, WINDOW, N_KV_HEADS, HEAD_DIM))
        outs_p[2].append(state.reshape(bp, N_RWKV_HEADS, HEAD_DIM, HEAD_DIM))
        outs_p[3].append(prw[:, tp - 1, :RWKV_PROJ])

        qkv_s, prw_s = _norm_proj(xs, ssh1, ssc1, g_mix, w_in_b, layer, tm=bs, tn=tiles["proj_tn"])
        att_s, nk, nv = _attn_sample(qkv_s.reshape(bs, 1, QKV_COLS), cache_k2, cache_v2, sinks_col, layer)
        prw_tok = prw_s.reshape(bs, 1, RWKV_PROJ_PAD)
        rw_s, state_s = _wkv_step(prw_tok, shift_rows, state_rows, wp, layer)
        x1_s, h2_s = _out_proj(att_s.reshape(1, bs, ATTN_WIDTH), rw_s.reshape(1, bs, RWKV_WIDTH), xs,
                               sgt1, ssh2, ssc2, g_ffn, w_out_b, layer, tm=bs)
        xs = _ffn(h2_s, x1_s, sgt2, w_ffn_in_b, w_ffn_out_b, g_final, layer, tm=bs, th=tiles["ffn_th"],
                  final_norm=last)
        outs_s[0].append(nk.reshape(bs, WINDOW, N_KV_HEADS, HEAD_DIM))
        outs_s[1].append(nv.reshape(bs, WINDOW, N_KV_HEADS, HEAD_DIM))
        outs_s[2].append(state_s.reshape(bs, N_RWKV_HEADS, HEAD_DIM, HEAD_DIM))
        outs_s[3].append(prw_tok[:, 0, :RWKV_PROJ])

    stack = lambda xs_: jnp.stack(xs_, axis=0)
    return (xp, xs.reshape(bs, 1, d),
            stack(outs_p[0]), stack(outs_p[1]), stack(outs_p[2]), stack(outs_p[3]),
            stack(outs_s[0]), stack(outs_s[1]), stack(outs_s[2]), stack(outs_s[3]))
```

```python
import functools
import math

import numpy as np
import jax
import jax.numpy as jnp
from jax import lax
from jax.experimental import pallas as pl
from jax.experimental.pallas import tpu as pltpu

F32 = jnp.float32
BF16 = jnp.bfloat16

D_MODEL = 2048
HEAD_DIM = 64
ATTN_WIDTH = D_MODEL // 2
RWKV_WIDTH = D_MODEL - ATTN_WIDTH
N_Q_HEADS = ATTN_WIDTH // HEAD_DIM
N_KV_HEADS = N_Q_HEADS // 4
GQA_REP = N_Q_HEADS // N_KV_HEADS
KV_WIDTH = N_KV_HEADS * HEAD_DIM
N_RWKV_HEADS = RWKV_WIDTH // HEAD_DIM
WINDOW = 128
D_DECAY_LORA = 64
D_ICLR_LORA = 64
D_GATE_LORA = 160
RWKV_PROJ = 3 * RWKV_WIDTH + D_DECAY_LORA + D_ICLR_LORA + D_GATE_LORA
QKV_COLS = ATTN_WIDTH + 2 * KV_WIDTH
FFN_HIDDEN = 5632
RMS_EPS = 1e-5
GN_EPS = 64e-5
NEG_INF = -1e30

LANES = 128
MXU_DIM = 256
VMEM_LIMIT_BYTES = 56 * 1024 * 1024

LORA_OFF = 3 * RWKV_WIDTH
GATE_OFF = LORA_OFF + D_DECAY_LORA + D_ICLR_LORA
GATE_PAD = 2 * LANES
RWKV_PROJ_PAD = 3584
HEADS_PER_GROUP = MXU_DIM // HEAD_DIM
N_GROUPS = N_RWKV_HEADS // HEADS_PER_GROUP

ALIBI_SLOPES = tuple(float(s) for s in np.exp2(-8.0 * np.arange(1, N_Q_HEADS + 1, dtype=np.float32) / N_Q_HEADS))


def _params(n_axes):
    return pltpu.CompilerParams(dimension_semantics=("arbitrary",) * n_axes,
                                vmem_limit_bytes=VMEM_LIMIT_BYTES)


def _dot(a, b):
    return jnp.dot(a, b, preferred_element_type=F32)


def _dot_nt(a, b):
    return lax.dot_general(a, b, (((1,), (1,)), ((), ())), preferred_element_type=F32)


def _dot_tn(a, b):
    return lax.dot_general(a, b, (((0,), (0,)), ((), ())), preferred_element_type=F32)


def _split(x, parts):
    out = []
    for _ in range(parts):
        p = x.astype(BF16)
        out.append(p)
        x = x - p.astype(F32)
    return out


def _rmsnorm_mod(x, g, shift, scale):
    ms = jnp.mean(x * x, axis=-1, keepdims=True)
    y = x * lax.rsqrt(ms + RMS_EPS) * g
    return y * (1.0 + scale) + shift


def _ada_kernel(c_ref, w_ref, b_ref, o_ref):
    c = c_ref[...]
    s = (c * jax.nn.sigmoid(c)).astype(BF16)
    o_ref[...] = _dot(s, w_ref[...].astype(BF16)) + b_ref[...]


def _ada(c_all, w_ada, b_ada):
    n_layers, d, n = w_ada.shape
    rows = c_all.shape[0]
    tn = 1024
    return pl.pallas_call(
        _ada_kernel,
        grid=(n_layers, n // tn),
        in_specs=[pl.BlockSpec((rows, d), lambda l, j: (0, 0)),
                  pl.BlockSpec((None, d, tn), lambda l, j: (l, 0, j)),
                  pl.BlockSpec((None, 1, tn), lambda l, j: (l, 0, j))],
        out_specs=pl.BlockSpec((None, rows, tn), lambda l, j: (l, 0, j)),
        out_shape=jax.ShapeDtypeStruct((n_layers, rows, n), F32),
        compiler_params=_params(2),
        name="ada_mod",
    )(c_all, w_ada, b_ada.reshape(n_layers, 1, n))


NORM_STRIP = 32


def _norm_proj_kernel(x_ref, sh_ref, sc_ref, g_ref, w_ref, qkv_ref, prw_ref, h_scr, *, n_qkv_tiles):
    j = pl.program_id(2)
    tm = h_scr.shape[0]

    @pl.when(j == 0)
    def _():
        gain = g_ref[...] * (1.0 + sc_ref[0])
        shift = sh_ref[0]
        if gain.shape[0] == 1:
            def strip(i, carry):
                rows = pl.ds(pl.multiple_of(i * NORM_STRIP, NORM_STRIP), NORM_STRIP)
                x = x_ref[0, rows, :]
                ms = jnp.mean(x * x, axis=-1, keepdims=True)
                h_scr[rows, :] = (x * lax.rsqrt(ms + RMS_EPS) * gain + shift).astype(BF16)
                return carry
            lax.fori_loop(0, tm // NORM_STRIP, strip, 0, unroll=4)
        else:
            x = x_ref[0]
            ms = jnp.mean(x * x, axis=-1, keepdims=True)
            h_scr[...] = (x * lax.rsqrt(ms + RMS_EPS) * gain + shift).astype(BF16)

    @pl.when(j < n_qkv_tiles)
    def _():
        qkv_ref[0] = _dot_nt(h_scr[...], w_ref[...])

    @pl.when(j >= n_qkv_tiles)
    def _():
        prw_ref[0] = _dot_nt(h_scr[...], w_ref[...])


def _norm_proj(x, shift, scale, g, w, layer, *, tm, tn):
    b, t, d = x.shape
    r = shift.shape[1]
    n_qkv_tiles = QKV_COLS // tn
    n_tiles = (QKV_COLS + RWKV_PROJ_PAD) // tn
    return pl.pallas_call(
        functools.partial(_norm_proj_kernel, n_qkv_tiles=n_qkv_tiles),
        grid=(b, t // tm, n_tiles),
        in_specs=[pl.BlockSpec((1, tm, d), lambda i, m, j: (i, m, 0)),
                  pl.BlockSpec((1, r, d), lambda i, m, j: (i, 0, 0)),
                  pl.BlockSpec((1, r, d), lambda i, m, j: (i, 0, 0)),
                  pl.BlockSpec((None, 1, d), lambda i, m, j: (layer, 0, 0)),
                  pl.BlockSpec((None, tn, d), lambda i, m, j: (layer, j, 0))],
        out_specs=[pl.BlockSpec((1, tm, tn), lambda i, m, j: (i, m, jnp.minimum(j, n_qkv_tiles - 1))),
                   pl.BlockSpec((1, tm, tn), lambda i, m, j: (i, m, jnp.maximum(j - n_qkv_tiles, 0)))],
        out_shape=[jax.ShapeDtypeStruct((b, t, QKV_COLS), F32),
                   jax.ShapeDtypeStruct((b, t, RWKV_PROJ_PAD), F32)],
        scratch_shapes=[pltpu.VMEM((tm, d), BF16)],
        compiler_params=_params(3),
        name="norm_proj",
    )(x, shift, scale, g, w)


LOG2E = math.log2(math.e)


def _attn_bias_table():
    t = np.arange(WINDOW)[:, None]
    j = np.arange(2 * WINDOW)[None, :]
    dist = t + WINDOW - j
    valid = (dist >= 0) & (dist <= WINDOW)
    slopes = np.asarray(ALIBI_SLOPES, np.float32)[:, None, None]
    bias = np.where(valid[None], -slopes * dist[None].astype(np.float32) * np.float32(LOG2E), np.float32(NEG_INF))
    first = np.where((j >= WINDOW)[None], bias, np.float32(NEG_INF))
    return np.stack([first, bias]).astype(np.float32)


def _attn_prompt_kernel(sink_ref, bias_ref, q_ref, kc_ref, vc_ref, kp_ref, vp_ref, o_ref, *, layer):
    q = q_ref[0] * (HEAD_DIM ** -0.5 * LOG2E)
    kc, vc, kp, vp = kc_ref[0], vc_ref[0], kp_ref[0], vp_ref[0]
    kv_heads, heads = range(N_KV_HEADS), range(N_Q_HEADS)
    ksl = [slice(g * HEAD_DIM, (g + 1) * HEAD_DIM) for g in kv_heads]
    k_band = [jnp.concatenate([kp[:, ksl[g]], kc[:, ksl[g]]], axis=0).astype(BF16) for g in kv_heads]
    v_band = [jnp.concatenate([vp[:, ksl[g]], vc[:, ksl[g]]], axis=0).astype(BF16) for g in kv_heads]
    lg = [_dot_nt(q[:, h * HEAD_DIM:(h + 1) * HEAD_DIM].astype(BF16), k_band[h // GQA_REP]) + bias_ref[0, h]
          for h in heads]
    sink = [sink_ref[layer * N_Q_HEADS + h] * LOG2E for h in heads]
    m = [jnp.maximum(jnp.max(lg[h], axis=-1, keepdims=True), sink[h]) for h in heads]
    p = [jnp.exp2(lg[h] - m[h]).astype(BF16) for h in heads]
    ones = jnp.ones((2 * WINDOW, HEAD_DIM), BF16)
    den = [_dot(p[h], ones) + jnp.exp2(sink[h] - m[h]) for h in heads]
    outs = [_dot(p[h], v_band[h // GQA_REP]) * (1.0 / den[h]) for h in heads]
    o_ref[0] = jnp.concatenate(outs, axis=1).astype(BF16)


def _attn_prompt(qkv, sinks_flat, layer):
    b, t, _ = qkv.shape
    kcol = ATTN_WIDTH // KV_WIDTH
    prev = lambda i, m: (i, jnp.maximum(m - 1, 0), kcol)
    prev_v = lambda i, m: (i, jnp.maximum(m - 1, 0), kcol + 1)
    return pl.pallas_call(
        functools.partial(_attn_prompt_kernel, layer=layer),
        grid=(b, t // WINDOW),
        in_specs=[pl.BlockSpec(memory_space=pltpu.SMEM),
                  pl.BlockSpec((1, N_Q_HEADS, WINDOW, 2 * WINDOW), lambda i, m: (jnp.minimum(m, 1), 0, 0, 0)),
                  pl.BlockSpec((1, WINDOW, ATTN_WIDTH), lambda i, m: (i, m, 0)),
                  pl.BlockSpec((1, WINDOW, KV_WIDTH), lambda i, m: (i, m, kcol)),
                  pl.BlockSpec((1, WINDOW, KV_WIDTH), lambda i, m: (i, m, kcol + 1)),
                  pl.BlockSpec((1, WINDOW, KV_WIDTH), prev),
                  pl.BlockSpec((1, WINDOW, KV_WIDTH), prev_v)],
        out_specs=pl.BlockSpec((1, WINDOW, ATTN_WIDTH), lambda i, m: (i, m, 0)),
        out_shape=jax.ShapeDtypeStruct((b, t, ATTN_WIDTH), BF16),
        compiler_params=_params(2),
        name="attn_prompt",
    )(sinks_flat, jnp.asarray(_attn_bias_table()), qkv, qkv, qkv, qkv, qkv)


def _attn_sample_kernel(row_ref, kc_ref, vc_ref, sink_ref, slope_ref, spread_ref, gather_ref,
                        o_ref, nk_ref, nv_ref):
    row = row_ref[0]
    q = row[:, :ATTN_WIDTH] * (HEAD_DIM ** -0.5)
    k_new = row[:, ATTN_WIDTH:ATTN_WIDTH + KV_WIDTH]
    v_new = row[:, ATTN_WIDTH + KV_WIDTH:]
    kc, vc = kc_ref[0], vc_ref[0]
    head_row = lax.broadcasted_iota(jnp.int32, (N_Q_HEADS, ATTN_WIDTH), 0)
    head_lane = lax.broadcasted_iota(jnp.int32, (N_Q_HEADS, ATTN_WIDTH), 1) // HEAD_DIM
    own = head_row == head_lane
    q_rows = jnp.where(own, jnp.broadcast_to(q, (N_Q_HEADS, ATTN_WIDTH)), 0.0).astype(BF16)
    q_grp = _dot(q_rows, spread_ref[...])
    k_new_b = k_new.astype(BF16).astype(F32)
    lg = _dot_nt(q_grp.astype(BF16), kc.astype(BF16))
    lg_new = jnp.sum(q_grp * k_new_b, axis=-1, keepdims=True)
    dist = (WINDOW - lax.broadcasted_iota(jnp.int32, (N_Q_HEADS, WINDOW), 1)).astype(F32)
    lg = lg - slope_ref[...] * dist
    sink = sink_ref[...]
    m = jnp.maximum(jnp.maximum(jnp.max(lg, axis=-1, keepdims=True), lg_new), sink)
    p = jnp.exp(lg - m)
    p_new = jnp.exp(lg_new - m)
    den = jnp.sum(p, axis=-1, keepdims=True) + p_new + jnp.exp(sink - m)
    out = _dot((p / den).astype(BF16), vc.astype(BF16))
    out = out + (p_new / den).astype(BF16).astype(F32) * v_new.astype(BF16).astype(F32)
    grp_row = lax.broadcasted_iota(jnp.int32, (N_Q_HEADS, KV_WIDTH), 0) // GQA_REP
    grp_lane = lax.broadcasted_iota(jnp.int32, (N_Q_HEADS, KV_WIDTH), 1) // HEAD_DIM
    out = jnp.where(grp_row == grp_lane, out, 0.0).astype(BF16)
    full = _dot(out, gather_ref[...])
    o_ref[0] = jnp.sum(jnp.where(own, full, 0.0), axis=0, keepdims=True).astype(BF16)
    last = lax.broadcasted_iota(jnp.int32, (WINDOW, KV_WIDTH), 0) == WINDOW - 1
    nk_ref[0] = jnp.where(last, k_new, pltpu.roll(kc, WINDOW - 1, axis=0))
    nv_ref[0] = jnp.where(last, v_new, pltpu.roll(vc, WINDOW - 1, axis=0))


def _attn_sample(qkv_rows, cache_k, cache_v, sinks_col, layer):
    s = qkv_rows.shape[0]
    spread = np.zeros((ATTN_WIDTH, KV_WIDTH), np.float32)
    for h in range(N_Q_HEADS):
        for d in range(HEAD_DIM):
            spread[h * HEAD_DIM + d, (h // GQA_REP) * HEAD_DIM + d] = 1.0
    slopes = np.asarray(ALIBI_SLOPES, np.float32).reshape(N_Q_HEADS, 1)
    win = (1, WINDOW, KV_WIDTH)
    return pl.pallas_call(
        _attn_sample_kernel,
        grid=(s,),
        in_specs=[pl.BlockSpec((1, 1, QKV_COLS), lambda i: (i, 0, 0)),
                  pl.BlockSpec((None,) + win, lambda i: (layer, i, 0, 0)),
                  pl.BlockSpec((None,) + win, lambda i: (layer, i, 0, 0)),
                  pl.BlockSpec((None, N_Q_HEADS, 1), lambda i: (layer, 0, 0)),
                  pl.BlockSpec((N_Q_HEADS, 1), lambda i: (0, 0)),
                  pl.BlockSpec((ATTN_WIDTH, KV_WIDTH), lambda i: (0, 0)),
                  pl.BlockSpec((KV_WIDTH, ATTN_WIDTH), lambda i: (0, 0))],
        out_specs=[pl.BlockSpec((1, 1, ATTN_WIDTH), lambda i: (i, 0, 0)),
                   pl.BlockSpec(win, lambda i: (i, 0, 0)),
                   pl.BlockSpec(win, lambda i: (i, 0, 0))],
        out_shape=[jax.ShapeDtypeStruct((s, 1, ATTN_WIDTH), BF16),
                   jax.ShapeDtypeStruct((s, WINDOW, KV_WIDTH), F32),
                   jax.ShapeDtypeStruct((s, WINDOW, KV_WIDTH), F32)],
        compiler_params=_params(1),
        name="attn_sample",
    )(qkv_rows, cache_k, cache_v, sinks_col, jnp.asarray(slopes),
      jnp.asarray(spread, BF16), jnp.asarray(spread.T, BF16))


def _wkv_kernel(p_ref, shift0_ref, s0_ref, mix_ref, w0_ref, dup_ref, a0_ref, iup_ref, gup_ref,
                kk_ref, ka_ref, rk_ref, lnw_ref, lnb_ref, rw_ref, sout_ref, s_scr, prev_scr,
                arb_scr, aak_scr, ar_scr, vbd_scr, vb_scr, bk_scr, total_scr, gate_scr, bonus_scr, ncat_scr, ecat_scr,
                *, chunk):
    step = pl.program_id(1)
    n_steps = pl.num_programs(1)
    staged = (arb_scr, aak_scr, ar_scr, vbd_scr, vb_scr, bk_scr, total_scr, gate_scr, bonus_scr)
    bd_rows = HEADS_PER_GROUP * chunk
    groups = range(N_GROUPS)
    cols = [slice(g * MXU_DIM, (g + 1) * MXU_DIM) for g in groups]
    iota = lambda shape, dim: lax.broadcasted_iota(jnp.int32, shape, dim)
    same_head = iota((MXU_DIM, MXU_DIM), 0) // HEAD_DIM == iota((MXU_DIM, MXU_DIM), 1) // HEAD_DIM
    ones_bd = jnp.where(same_head, 1.0, 0.0).astype(BF16)
    tile_sel = jnp.where(iota((HEAD_DIM, MXU_DIM), 0) == iota((HEAD_DIM, MXU_DIM), 1) % HEAD_DIM, 1.0, 0.0).astype(BF16)
    fold_sel = jnp.where(iota((MXU_DIM, HEAD_DIM), 0) % HEAD_DIM == iota((MXU_DIM, HEAD_DIM), 1), 1.0, 0.0).astype(BF16)

    @pl.when(step == 0)
    def _():
        prev_scr[...] = shift0_ref[0]
        for g in groups:
            rows = s0_ref[0, cols[g], :]
            tiled = sum(_dot(part, tile_sel) for part in _split(rows, 3))
            s_scr[g] = jnp.where(same_head, tiled, 0.0)
        for ref in staged + (ncat_scr, ecat_scr):
            ref[...] = jnp.zeros_like(ref)
        total_scr[...] = jnp.ones_like(total_scr)

    row = iota((chunk, 1), 0)

    def shifted(lo, width):
        cur = p_ref[0, :, lo:lo + width]
        prev = jnp.where(row == 0, prev_scr[:, lo:lo + width], pltpu.roll(cur, 1, axis=0))
        return cur + (prev - cur) * mix_ref[:, lo:lo + width]

    def seg_sum(x):
        return _dot(x.astype(BF16), ones_bd)

    head_mask = [jnp.where(iota((chunk, MXU_DIM), 1) // HEAD_DIM == h, 1.0, 0.0).astype(BF16)
                 for h in range(HEADS_PER_GROUP)]

    def head_blocks(x):
        return jnp.concatenate([x * m for m in head_mask], axis=0)

    slot = step % 2
    inv_n = 1.0 / HEAD_DIM
    t_idx = iota((chunk, 2 * bd_rows), 0)
    s_idx = iota((chunk, 2 * bd_rows), 1) % chunk
    strict = (s_idx < t_idx)[:, :bd_rows]
    same_block = jnp.where(iota((bd_rows, bd_rows), 0) // chunk == iota((bd_rows, bd_rows), 1) // chunk,
                           1.0, 0.0).astype(BF16)
    eye_cat = jnp.where(t_idx == s_idx, 1.0, 0.0)[:, :bd_rows].astype(BF16)
    block_diag = lambda x: jnp.concatenate([x] * HEADS_PER_GROUP, axis=0) * same_block


    st_ar, st_ak, st_r, st_vbd, st_v, st_bk, st_total, st_gate, st_bonus = (
        [ref[slot, g] for g in groups] for ref in staged)
    st_e = [ecat_scr[g] for g in groups]
    s_b = [s_scr[g].astype(BF16) for g in groups]
    uy = [_dot_nt(st_ar[g], s_b[g]) for g in groups]

    e_cat = [ncat_scr[g] for g in groups]
    m_b = [x.astype(BF16) for x in e_cat]
    m_b = [_dot(x, block_diag(x)).astype(BF16) for x in m_b]
    levels = int(math.log2(chunk)) - 1

    def inverse_level(lvl, e_cat, m_b):
        t_bd = [block_diag(x.astype(BF16) + eye_cat) for x in e_cat]
        if lvl < levels - 1:
            both = [_dot(m_b[g], jnp.concatenate([t_bd[g], block_diag(m_b[g])], axis=1)) for g in groups]
            return ([e_cat[g] + both[g][:, :bd_rows] for g in groups], [x[:, bd_rows:].astype(BF16) for x in both])
        return [e_cat[g] + _dot(m_b[g], t_bd[g]) for g in groups], None

    lora_in = shifted(LORA_OFF, LANES)
    lora_tanh = jnp.tanh(lora_in).astype(BF16)
    lora_lin = lora_in.astype(BF16)
    gate_sig = jax.nn.sigmoid(shifted(GATE_OFF, GATE_PAD)).astype(BF16)
    r = [shifted(g * MXU_DIM, MXU_DIM) for g in groups]

    rhs = [uy[g][:chunk] + _dot(st_ak[g], st_vbd[g]) for g in groups]
    e_cat, m_b = inverse_level(0, e_cat, m_b)

    k = [shifted(RWKV_WIDTH + g * MXU_DIM, MXU_DIM) for g in groups]
    v = [shifted(2 * RWKV_WIDTH + g * MXU_DIM, MXU_DIM) for g in groups]
    prev_scr[...] = p_ref[0, chunk - 1:chunk, :]

    p = [rhs[g] + _dot(st_e[g], head_blocks(rhs[g].astype(BF16))) for g in groups]
    p_b = [x.astype(BF16) for x in p]
    if levels > 1:
        e_cat, m_b = inverse_level(1, e_cat, m_b)

    logw = [-math.exp(-0.5) * jax.nn.sigmoid(w0_ref[:, cols[g]] + _dot(lora_tanh, dup_ref[:, cols[g]])) for g in groups]
    a = [jax.nn.sigmoid(a0_ref[:, cols[g]] + _dot(lora_lin, iup_ref[:, cols[g]])) for g in groups]
    gate = [_dot(gate_sig, gup_ref[:, cols[g]]) for g in groups]

    y = [uy[g][chunk:] + _dot(st_r[g], jnp.concatenate([head_blocks(p_b[g]), st_vbd[g]], axis=0)) for g in groups]
    for g in groups:
        upd = _dot_tn(jnp.concatenate([p_b[g], st_v[g]], axis=0), st_bk[g])
        s_scr[g] = s_scr[g] * st_total[g] + jnp.where(same_head, upd, 0.0)
    if levels > 2:
        e_cat, m_b = inverse_level(2, e_cat, m_b)

    kk = [k[g] * kk_ref[:, cols[g]] for g in groups]
    kk = [kk[g] / jnp.maximum(jnp.sqrt(seg_sum(kk[g] * kk[g])), 1e-12) for g in groups]
    k = [k[g] * (1.0 + (a[g] - 1.0) * ka_ref[:, cols[g]]) for g in groups]

    dev = [y[g] - seg_sum(y[g]) * inv_n for g in groups]
    if levels > 3:
        e_cat, m_b = inverse_level(3, e_cat, m_b)

    tri = jnp.where(iota((chunk, chunk), 0) >= iota((chunk, chunk), 1), 1.0, 0.0).astype(BF16)
    cum = [sum(_dot(tri, part) for part in _split(logw[g], 3)) for g in groups]

    var = [seg_sum(x * x) * inv_n for x in dev]
    for lvl in range(4, levels):
        e_cat, m_b = inverse_level(lvl, e_cat, m_b)

    e_pos = [jnp.exp(x) for x in cum]
    e_neg = [jnp.exp(-x) for x in cum]
    a_t = [-kk[g] * jnp.exp(cum[g] - logw[g]) for g in groups]
    b_t = [kk[g] * a[g] * e_neg[g] for g in groups]

    for g in groups:
        yn = dev[g] * lax.rsqrt(var[g] + GN_EPS) * lnw_ref[:, cols[g]] + lnb_ref[:, cols[g]]
        rw_ref[0, :, cols[g]] = ((yn + st_bonus[g]) * st_gate[g]).astype(BF16)

    k_t = [k[g] * e_neg[g] for g in groups]
    r_t = [r[g] * e_pos[g] for g in groups]
    total = [x[chunk - 1:chunk, :] for x in e_pos]
    ar_b = [jnp.concatenate([a_t[g], r_t[g]], axis=0).astype(BF16) for g in groups]
    xb =[head_blocks(x.astype(BF16)) for x in b_t]
    xk = [head_blocks(x.astype(BF16)) for x in k_t]
    v_bd = [head_blocks(x.astype(BF16)) for x in v]

    scores = [_dot_nt(ar_b[g], jnp.concatenate([xb[g], xk[g]], axis=0)) for g in groups]
    bonus = [seg_sum(r[g] * k[g] * rk_ref[:, cols[g]]) * v[g] for g in groups]
    for g in groups:
        ecat_scr[g] = e_cat[g].astype(BF16)
        ncat_scr[g] = jnp.where(strict, scores[g][:chunk, :bd_rows], 0.0)
        arb_scr[slot, g] = ar_b[g]
        aak_scr[slot, g] = jnp.where(strict, scores[g][:chunk, bd_rows:], 0.0).astype(BF16)
        ar_scr[slot, g] = jnp.where(s_idx <= t_idx, scores[g][chunk:], 0.0).astype(BF16)
        vbd_scr[slot, g] = v_bd[g]
        vb_scr[slot, g] = v[g].astype(BF16)
        bk_scr[slot, g] = jnp.concatenate([b_t[g] * total[g], k_t[g] * total[g]], axis=0).astype(BF16)
        total_scr[slot, g], gate_scr[slot, g], bonus_scr[slot, g] = total[g], gate[g], bonus[g]

    @pl.when(step == n_steps - 1)
    def _():
        for g in groups:
            folded = sum(_dot(part, fold_sel) for part in _split(s_scr[g], 3))
            sout_ref[0, cols[g], :] = folded


def _wkv(prw, shift0, s0, wp, layer, *, chunk):
    b, t, _ = prw.shape
    n_chunks = t // chunk
    bd_rows = HEADS_PER_GROUP * chunk
    lay = lambda *blk: pl.BlockSpec((None,) + blk, lambda i, c: (layer,) + (0,) * len(blk))
    vec = lay(1, RWKV_WIDTH)
    grp = lambda rows, width, dtype: pltpu.VMEM((N_GROUPS, rows, width), dtype)
    two = lambda rows, width, dtype: pltpu.VMEM((2, N_GROUPS, rows, width), dtype)
    return pl.pallas_call(
        functools.partial(_wkv_kernel, chunk=chunk),
        grid=(b, n_chunks + 2),
        in_specs=[pl.BlockSpec((1, chunk, RWKV_PROJ_PAD), lambda i, c: (i, jnp.minimum(c, n_chunks - 1), 0)),
                  pl.BlockSpec((1, 1, RWKV_PROJ_PAD), lambda i, c: (i, 0, 0)),
                  pl.BlockSpec((1, RWKV_WIDTH, HEAD_DIM), lambda i, c: (i, 0, 0)),
                  lay(1, RWKV_PROJ_PAD), vec, lay(LANES, RWKV_WIDTH), vec, lay(LANES, RWKV_WIDTH),
                  lay(GATE_PAD, RWKV_WIDTH), vec, vec, vec, vec, vec],
        out_specs=[pl.BlockSpec((1, chunk, RWKV_WIDTH), lambda i, c: (i, jnp.maximum(c - 2, 0), 0)),
                   pl.BlockSpec((1, RWKV_WIDTH, HEAD_DIM), lambda i, c: (i, 0, 0))],
        out_shape=[jax.ShapeDtypeStruct((b, t, RWKV_WIDTH), BF16),
                   jax.ShapeDtypeStruct((b, RWKV_WIDTH, HEAD_DIM), F32)],
        scratch_shapes=[pltpu.VMEM((N_GROUPS, MXU_DIM, MXU_DIM), F32),
                        pltpu.VMEM((1, RWKV_PROJ_PAD), F32),
                        two(2 * chunk, MXU_DIM, BF16), two(chunk, bd_rows, BF16), two(chunk, 2 * bd_rows, BF16),
                        two(bd_rows, MXU_DIM, BF16), two(chunk, MXU_DIM, BF16), two(2 * chunk, MXU_DIM, BF16),
                        two(1, MXU_DIM, F32), two(chunk, MXU_DIM, F32), two(chunk, MXU_DIM, F32),
                        grp(chunk, bd_rows, F32), grp(chunk, bd_rows, BF16)],
        compiler_params=_params(2),
        name="wkv7",
    )(prw, shift0, s0, wp["mix"], wp["w0"], wp["decay_up"], wp["a0"], wp["iclr_up"], wp["gate_up"],
      wp["k_k"], wp["k_a"], wp["r_k"], wp["ln_w"], wp["ln_b"])


SUBLANES = 8


def _wkv_step_kernel(p_ref, shift_ref, s_ref, mix_ref, w0_ref, dup_ref, a0_ref, iup_ref, gup_ref,
                     kk_ref, ka_ref, rk_ref, lnw_ref, lnb_ref, fold_ref, rw_ref, sout_ref):
    iota = lambda shape, dim: lax.broadcasted_iota(jnp.int32, shape, dim)
    groups = range(N_GROUPS)
    cols = [slice(g * MXU_DIM, (g + 1) * MXU_DIM) for g in groups]
    ones_bd = jnp.where(iota((MXU_DIM, MXU_DIM), 0) // HEAD_DIM == iota((MXU_DIM, MXU_DIM), 1) // HEAD_DIM,
                        1.0, 0.0).astype(BF16)

    def seg_sum(x):
        return jnp.concatenate([_dot(x[:, cols[g]].astype(BF16), ones_bd) for g in groups], axis=1)

    cur = jnp.broadcast_to(p_ref[0], (SUBLANES, RWKV_PROJ_PAD))
    prev = jnp.broadcast_to(shift_ref[0], (SUBLANES, RWKV_PROJ_PAD))
    p = cur + (prev - cur) * mix_ref[...]
    r = p[:, 0:RWKV_WIDTH]
    k = p[:, RWKV_WIDTH:2 * RWKV_WIDTH]
    v = p[:, 2 * RWKV_WIDTH:3 * RWKV_WIDTH]
    lora_in = p[:, LORA_OFF:LORA_OFF + LANES]
    gate_in = p[:, GATE_OFF:GATE_OFF + GATE_PAD]
    w = w0_ref[...] + _dot(jnp.tanh(lora_in).astype(BF16), dup_ref[...])
    decay = jnp.exp(-math.exp(-0.5) * jax.nn.sigmoid(w))
    a = jax.nn.sigmoid(a0_ref[...] + _dot(lora_in.astype(BF16), iup_ref[...]))
    gate = _dot(jax.nn.sigmoid(gate_in).astype(BF16), gup_ref[...])
    kk = k * kk_ref[...]
    kk = kk / jnp.maximum(jnp.sqrt(seg_sum(kk * kk)), 1e-12)
    k = k * (1.0 + (a - 1.0) * ka_ref[...])

    own = iota((N_RWKV_HEADS, RWKV_WIDTH), 0) == iota((N_RWKV_HEADS, RWKV_WIDTH), 1) // HEAD_DIM
    spread = lambda x: jnp.where(own, jnp.broadcast_to(x[0:1], (N_RWKV_HEADS, RWKV_WIDTH)), 0.0)
    stacked = jnp.concatenate([spread(decay), spread(-kk), spread(kk * a), spread(k), spread(r)], axis=0)
    per_head = sum(_dot(part, fold_ref[...]) for part in _split(stacked, 3))
    w_h, a_h, b_h, k_h, r_h = (per_head[i * N_RWKV_HEADS:(i + 1) * N_RWKV_HEADS] for i in range(5))
    rows = lambda x: jnp.concatenate(
        [jnp.broadcast_to(x[h:h + 1, :], (HEAD_DIM, HEAD_DIM)) for h in range(N_RWKV_HEADS)], axis=0)

    eye = iota((LANES, LANES), 0) == iota((LANES, LANES), 1)
    ones_cols = jnp.ones((LANES, HEAD_DIM), BF16)
    v_col = jnp.concatenate(
        [sum(_dot(part, ones_cols) for part in
             _split(jnp.where(eye, jnp.broadcast_to(v[0:1, t * LANES:(t + 1) * LANES], (LANES, LANES)), 0.0), 3))
         for t in range(RWKV_WIDTH // LANES)], axis=0)

    s = s_ref[0]
    sa = jnp.sum(s * rows(a_h), axis=-1, keepdims=True)
    s_new = s * rows(w_h) + sa * rows(b_h) + v_col * rows(k_h)
    sout_ref[0] = s_new
    y_heads = _dot_nt(r_h.astype(BF16), s_new.astype(BF16))
    y = jnp.broadcast_to(jnp.sum(jnp.where(own, y_heads, 0.0), axis=0, keepdims=True), (SUBLANES, RWKV_WIDTH))

    inv_n = 1.0 / HEAD_DIM
    dev = y - seg_sum(y) * inv_n
    var = seg_sum(dev * dev) * inv_n
    y = dev * lax.rsqrt(var + GN_EPS) * lnw_ref[...] + lnb_ref[...]
    bonus = seg_sum(r * k * rk_ref[...]) * v
    rw_ref[0] = ((y + bonus) * gate)[0:1].astype(BF16)


def _wkv_step(prw_rows, shift_rows, state, wp, layer):
    s = prw_rows.shape[0]
    lay = lambda *blk: pl.BlockSpec((None,) + blk, lambda i: (layer,) + (0,) * len(blk))
    vec = lay(1, RWKV_WIDTH)
    fold = np.tile(np.eye(HEAD_DIM, dtype=np.float32), (N_RWKV_HEADS, 1))
    return pl.pallas_call(
        _wkv_step_kernel,
        grid=(s,),
        in_specs=[pl.BlockSpec((1, 1, RWKV_PROJ_PAD), lambda i: (i, 0, 0)),
                  pl.BlockSpec((None, 1, 1, RWKV_PROJ_PAD), lambda i: (layer, i, 0, 0)),
                  pl.BlockSpec((None, 1, RWKV_WIDTH, HEAD_DIM), lambda i: (layer, i, 0, 0)),
                  lay(1, RWKV_PROJ_PAD), vec, lay(LANES, RWKV_WIDTH), vec, lay(LANES, RWKV_WIDTH),
                  lay(GATE_PAD, RWKV_WIDTH), vec, vec, vec, vec, vec,
                  pl.BlockSpec((RWKV_WIDTH, HEAD_DIM), lambda i: (0, 0))],
        out_specs=[pl.BlockSpec((1, 1, RWKV_WIDTH), lambda i: (i, 0, 0)),
                   pl.BlockSpec((1, RWKV_WIDTH, HEAD_DIM), lambda i: (i, 0, 0))],
        out_shape=[jax.ShapeDtypeStruct((s, 1, RWKV_WIDTH), BF16),
                   jax.ShapeDtypeStruct((s, RWKV_WIDTH, HEAD_DIM), F32)],
        compiler_params=_params(1),
        name="wkv7_step",
    )(prw_rows, shift_rows, state, wp["mix"], wp["w0"], wp["decay_up"], wp["a0"], wp["iclr_up"], wp["gate_up"],
      wp["k_k"], wp["k_a"], wp["r_k"], wp["ln_w"], wp["ln_b"], jnp.asarray(fold, BF16))


def _out_proj_kernel(att_ref, rw_ref, x_ref, gt_ref, sh_ref, sc_ref, g_ref, w_ref, x1_ref, h_ref):
    y = _dot(att_ref[0], w_ref[0:ATTN_WIDTH, :]) + _dot(rw_ref[0], w_ref[ATTN_WIDTH:, :])
    x1 = x_ref[0] + gt_ref[0] * y
    x1_ref[0] = x1
    h_ref[0] = _rmsnorm_mod(x1, g_ref[...], sh_ref[0], sc_ref[0]).astype(BF16)


def _out_proj(att, rw, x, gate, shift, scale, g, w, layer, *, tm):
    b, t, d = x.shape
    r = gate.shape[1]
    row = lambda width: pl.BlockSpec((1, tm, width), lambda i, m: (i, m, 0))
    mod = pl.BlockSpec((1, r, d), lambda i, m: (i, 0, 0))
    return pl.pallas_call(
        _out_proj_kernel,
        grid=(b, t // tm),
        in_specs=[row(ATTN_WIDTH), row(RWKV_WIDTH), row(d), mod, mod, mod,
                  pl.BlockSpec((None, 1, d), lambda i, m: (layer, 0, 0)),
                  pl.BlockSpec((None, d, d), lambda i, m: (layer, 0, 0))],
        out_specs=[row(d), row(d)],
        out_shape=[jax.ShapeDtypeStruct((b, t, d), F32), jax.ShapeDtypeStruct((b, t, d), BF16)],
        compiler_params=_params(2),
        name="out_proj",
    )(att, rw, x, gate, shift, scale, g, w)


def _ffn_kernel(h_ref, x_ref, gt_ref, wg_ref, wu_ref, wo_ref, gf_ref, o_ref, acc_ref, *, final_norm):
    j = pl.program_id(2)

    @pl.when(j == 0)
    def _():
        acc_ref[...] = jnp.zeros_like(acc_ref)

    h = h_ref[0]
    gate = _dot(h, wg_ref[...])
    up = _dot(h, wu_ref[...])
    act = (gate * jax.nn.sigmoid(gate) * up).astype(BF16)
    acc_ref[...] += _dot(act, wo_ref[...])

    @pl.when(j == pl.num_programs(2) - 1)
    def _():
        x2 = x_ref[0] + gt_ref[0] * acc_ref[...]
        if final_norm:
            ms = jnp.mean(x2 * x2, axis=-1, keepdims=True)
            x2 = x2 * lax.rsqrt(ms + RMS_EPS) * gf_ref[...]
        o_ref[0] = x2


def _ffn(h, x, gate, w_in, w_out, g_final, layer, *, tm, th, final_norm):
    b, t, d = x.shape
    r = gate.shape[1]
    n_h = FFN_HIDDEN // th
    return pl.pallas_call(
        functools.partial(_ffn_kernel, final_norm=final_norm),
        grid=(b, t // tm, n_h),
        in_specs=[pl.BlockSpec((1, tm, d), lambda i, m, j: (i, m, 0)),
                  pl.BlockSpec((1, tm, d), lambda i, m, j: (i, m, 0)),
                  pl.BlockSpec((1, r, d), lambda i, m, j: (i, 0, 0)),
                  pl.BlockSpec((None, d, th), lambda i, m, j: (layer, 0, j)),
                  pl.BlockSpec((None, d, th), lambda i, m, j: (layer, 0, j + n_h)),
                  pl.BlockSpec((None, th, d), lambda i, m, j: (layer, j, 0)),
                  pl.BlockSpec((1, d), lambda i, m, j: (0, 0))],
        out_specs=pl.BlockSpec((1, tm, d), lambda i, m, j: (i, m, 0)),
        out_shape=jax.ShapeDtypeStruct((b, t, d), F32),
        scratch_shapes=[pltpu.VMEM((tm, d), F32)],
        compiler_params=_params(3),
        name="ffn",
    )(h, x, gate, w_in, w_in, w_out, g_final)


def _tiles(t_prompt, n_sample):
    tiles = dict(proj_tm=1024, proj_tn=512, out_tm=512, ffn_tm=512, ffn_th=512, wkv_chunk=64)
    assert t_prompt % tiles["proj_tm"] == 0 and t_prompt % tiles["out_tm"] == 0 and t_prompt % tiles["ffn_tm"] == 0
    assert t_prompt % tiles["wkv_chunk"] == 0 and t_prompt % WINDOW == 0
    assert QKV_COLS % tiles["proj_tn"] == 0 and RWKV_PROJ_PAD % tiles["proj_tn"] == 0
    assert FFN_HIDDEN % tiles["ffn_th"] == 0 and n_sample % SUBLANES == 0
    return tiles


def _pad_cols(x, width):
    return jnp.pad(x, [(0, 0)] * (x.ndim - 1) + [(0, width - x.shape[-1])])


def _pad_rows(x, rows, offset=0):
    return jnp.pad(x, [(0, 0), (offset, rows - offset - x.shape[1]), (0, 0)])


def kernel(x_prompt, x_sample, cache_k, cache_v, state_wkv, state_shift, c_prompt, c_sample, w_ada, b_ada, g_norm_mix, g_norm_ffn, w_in, w_out, attn_sinks, mix_shift, decay_w0, decay_up, iclr_a0, iclr_up, gate_up, k_k, k_a, r_k, ln_x_w, ln_x_b, w_ffn_in, w_ffn_out, g_norm_final):
    n_layers = w_in.shape[0]
    bp, tp, d = x_prompt.shape
    bs = x_sample.shape[0]
    tiles = _tiles(tp, bs)

    w_in_t = jnp.pad(jnp.swapaxes(w_in, 1, 2).astype(BF16),
                     [(0, 0), (0, QKV_COLS + RWKV_PROJ_PAD - w_in.shape[2]), (0, 0)])
    w_out_b = w_out.astype(BF16)
    w_ffn_in_b = w_ffn_in.astype(BF16)
    w_ffn_out_b = w_ffn_out.astype(BF16)
    vec = lambda x: x.reshape(n_layers, 1, -1)
    wp = dict(
        mix=vec(_pad_cols(mix_shift, RWKV_PROJ_PAD)), w0=vec(decay_w0), a0=vec(iclr_a0),
        decay_up=_pad_rows(decay_up, LANES).astype(BF16),
        iclr_up=_pad_rows(iclr_up, LANES, D_DECAY_LORA).astype(BF16),
        gate_up=_pad_rows(gate_up, GATE_PAD).astype(BF16),
        k_k=vec(k_k), k_a=vec(k_a), r_k=vec(r_k), ln_w=vec(ln_x_w), ln_b=vec(ln_x_b))
    g_mix, g_ffn = vec(g_norm_mix), vec(g_norm_ffn)
    g_final = g_norm_final.reshape(1, d)
    sinks_flat = attn_sinks.reshape(-1)
    sinks_col = attn_sinks.reshape(n_layers, N_Q_HEADS, 1)
    cache_k2 = cache_k.reshape(n_layers, bs, WINDOW, KV_WIDTH)
    cache_v2 = cache_v.reshape(n_layers, bs, WINDOW, KV_WIDTH)
    shift_rows = _pad_cols(state_shift, RWKV_PROJ_PAD).reshape(n_layers, bs, 1, RWKV_PROJ_PAD)
    state_rows = state_wkv.reshape(n_layers, bs, RWKV_WIDTH, HEAD_DIM)
    zero_shift = jnp.zeros((bp, 1, RWKV_PROJ_PAD), F32)
    zero_state = jnp.zeros((bp, RWKV_WIDTH, HEAD_DIM), F32)

    mod = _ada(jnp.concatenate([c_prompt, c_sample], axis=0), w_ada, b_ada)

    def mods(layer):
        parts = [mod[layer, :, i * d:(i + 1) * d] for i in range(6)]
        return ([p[:bp].reshape(bp, 1, d) for p in parts], [p[bp:].reshape(1, bs, d) for p in parts])

    xp = x_prompt
    xs = x_sample.reshape(1, bs, d)
    outs_p = [[] for _ in range(4)]
    outs_s = [[] for _ in range(4)]
    for layer in range(n_layers):
        last = layer == n_layers - 1
        (sh1, sc1, gt1, sh2, sc2, gt2), (ssh1, ssc1, sgt1, ssh2, ssc2, sgt2) = mods(layer)

        qkv, prw = _norm_proj(xp, sh1, sc1, g_mix, w_in_t, layer, tm=tiles["proj_tm"], tn=tiles["proj_tn"])
        att = _attn_prompt(qkv, sinks_flat, layer)
        rw, state = _wkv(prw, zero_shift, zero_state, wp, layer, chunk=tiles["wkv_chunk"])
        x1, h2 = _out_proj(att, rw, xp, gt1, sh2, sc2, g_ffn, w_out_b, layer, tm=tiles["out_tm"])
        xp = _ffn(h2, x1, gt2, w_ffn_in_b, w_ffn_out_b, g_final, layer, tm=tiles["ffn_tm"], th=tiles["ffn_th"],
                  final_norm=last)
        outs_p[0].append(qkv[:, tp - WINDOW:, ATTN_WIDTH:ATTN_WIDTH + KV_WIDTH].reshape(bp, WINDOW, N_KV_HEADS, HEAD_DIM))
        outs_p[1].append(qkv[:, tp - WINDOW:, ATTN_WIDTH + KV_WIDTH:].reshape(bp, WINDOW, N_KV_HEADS, HEAD_DIM))
        outs_p[2].append(state.reshape(bp, N_RWKV_HEADS, HEAD_DIM, HEAD_DIM))
        outs_p[3].append(prw[:, tp - 1, :RWKV_PROJ])

        qkv_s, prw_s = _norm_proj(xs, ssh1, ssc1, g_mix, w_in_t, layer, tm=bs, tn=tiles["proj_tn"])
        att_s, nk, nv = _attn_sample(qkv_s.reshape(bs, 1, QKV_COLS), cache_k2, cache_v2, sinks_col, layer)
        prw_tok = prw_s.reshape(bs, 1, RWKV_PROJ_PAD)
        rw_s, state_s = _wkv_step(prw_tok, shift_rows, state_rows, wp, layer)
        x1_s, h2_s = _out_proj(att_s.reshape(1, bs, ATTN_WIDTH), rw_s.reshape(1, bs, RWKV_WIDTH), xs,
                               sgt1, ssh2, ssc2, g_ffn, w_out_b, layer, tm=bs)
        xs = _ffn(h2_s, x1_s, sgt2, w_ffn_in_b, w_ffn_out_b, g_final, layer, tm=bs, th=tiles["ffn_th"],
                  final_norm=last)
        outs_s[0].append(nk.reshape(bs, WINDOW, N_KV_HEADS, HEAD_DIM))
        outs_s[1].append(nv.reshape(bs, WINDOW, N_KV_HEADS, HEAD_DIM))
        outs_s[2].append(state_s.reshape(bs, N_RWKV_HEADS, HEAD_DIM, HEAD_DIM))
        outs_s[3].append(prw_tok[:, 0, :RWKV_PROJ])

    stack = lambda xs_: jnp.stack(xs_, axis=0)
    return (xp, xs.reshape(bs, 1, d),
            stack(outs_p[0]), stack(outs_p[1]), stack(outs_p[2]), stack(outs_p[3]),
            stack(outs_s[0]), stack(outs_s[1]), stack(outs_s[2]), stack(outs_s[3]))
```

```python
import functools
import math

import numpy as np
import jax
import jax.numpy as jnp
from jax import lax
from jax.experimental import pallas as pl
from jax.experimental.pallas import tpu as pltpu

F32 = jnp.float32
BF16 = jnp.bfloat16

D_MODEL = 2048
HEAD_DIM = 64
ATTN_WIDTH = D_MODEL // 2
RWKV_WIDTH = D_MODEL - ATTN_WIDTH
N_Q_HEADS = ATTN_WIDTH // HEAD_DIM
N_KV_HEADS = N_Q_HEADS // 4
GQA_REP = N_Q_HEADS // N_KV_HEADS
KV_WIDTH = N_KV_HEADS * HEAD_DIM
N_RWKV_HEADS = RWKV_WIDTH // HEAD_DIM
WINDOW = 128
D_DECAY_LORA = 64
D_ICLR_LORA = 64
D_GATE_LORA = 160
RWKV_PROJ = 3 * RWKV_WIDTH + D_DECAY_LORA + D_ICLR_LORA + D_GATE_LORA
QKV_COLS = ATTN_WIDTH + 2 * KV_WIDTH
FFN_HIDDEN = 5632
RMS_EPS = 1e-5
GN_EPS = 64e-5
NEG_INF = -1e30

LANES = 128
MXU_DIM = 256
VMEM_LIMIT_BYTES = 56 * 1024 * 1024

LORA_OFF = 3 * RWKV_WIDTH
GATE_OFF = LORA_OFF + D_DECAY_LORA + D_ICLR_LORA
GATE_PAD = 2 * LANES
RWKV_PROJ_PAD = 3584
HEADS_PER_GROUP = MXU_DIM // HEAD_DIM
N_GROUPS = N_RWKV_HEADS // HEADS_PER_GROUP

ALIBI_SLOPES = tuple(float(s) for s in np.exp2(-8.0 * np.arange(1, N_Q_HEADS + 1, dtype=np.float32) / N_Q_HEADS))


def _params(n_axes):
    return pltpu.CompilerParams(dimension_semantics=("arbitrary",) * n_axes,
                                vmem_limit_bytes=VMEM_LIMIT_BYTES)


def _dot(a, b):
    return jnp.dot(a, b, preferred_element_type=F32)


def _dot_nt(a, b):
    return lax.dot_general(a, b, (((1,), (1,)), ((), ())), preferred_element_type=F32)


def _dot_tn(a, b):
    return lax.dot_general(a, b, (((0,), (0,)), ((), ())), preferred_element_type=F32)


def _split(x, parts):
    out = []
    for _ in range(parts):
        p = x.astype(BF16)
        out.append(p)
        x = x - p.astype(F32)
    return out


def _rmsnorm_mod(x, g, shift, scale):
    ms = jnp.mean(x * x, axis=-1, keepdims=True)
    y = x * lax.rsqrt(ms + RMS_EPS) * g
    return y * (1.0 + scale) + shift


def _ada_kernel(c_ref, w_ref, b_ref, o_ref):
    c = c_ref[...]
    s = (c * jax.nn.sigmoid(c)).astype(BF16)
    o_ref[...] = _dot(s, w_ref[...].astype(BF16)) + b_ref[...]


def _ada(c_all, w_ada, b_ada):
    n_layers, d, n = w_ada.shape
    rows = c_all.shape[0]
    tn = 1024
    return pl.pallas_call(
        _ada_kernel,
        grid=(n_layers, n // tn),
        in_specs=[pl.BlockSpec((rows, d), lambda l, j: (0, 0)),
                  pl.BlockSpec((None, d, tn), lambda l, j: (l, 0, j)),
                  pl.BlockSpec((None, 1, tn), lambda l, j: (l, 0, j))],
        out_specs=pl.BlockSpec((None, rows, tn), lambda l, j: (l, 0, j)),
        out_shape=jax.ShapeDtypeStruct((n_layers, rows, n), F32),
        compiler_params=_params(2),
        name="ada_mod",
    )(c_all, w_ada, b_ada.reshape(n_layers, 1, n))


NORM_STRIP = 32


MOD_SHIFT1, MOD_SCALE1, MOD_GATE1, MOD_SHIFT2, MOD_SCALE2, MOD_GATE2 = range(6)


def _mod_spec(mod, layer, part):
    rows, d = mod.shape[1], mod.shape[2] // 6
    return pl.BlockSpec((None, rows, d), lambda *_: (layer, 0, part))


def _mod_rows(ref, n_sample, is_sample):
    if is_sample:
        return ref[0:n_sample, :]
    return ref[pl.ds(n_sample + pl.program_id(0), 1), :]


def _norm_proj_kernel(x_ref, sh_ref, sc_ref, g_ref, w_ref, qkv_ref, prw_ref, h_scr, *, n_qkv_tiles, n_sample,
                      is_sample):
    j = pl.program_id(2)
    tm = h_scr.shape[0]

    @pl.when(j == 0)
    def _():
        gain = g_ref[...] * (1.0 + _mod_rows(sc_ref, n_sample, is_sample))
        shift = _mod_rows(sh_ref, n_sample, is_sample)
        if gain.shape[0] == 1:
            def strip(i, carry):
                rows = pl.ds(pl.multiple_of(i * NORM_STRIP, NORM_STRIP), NORM_STRIP)
                x = x_ref[0, rows, :]
                ms = jnp.mean(x * x, axis=-1, keepdims=True)
                h_scr[rows, :] = (x * lax.rsqrt(ms + RMS_EPS) * gain + shift).astype(BF16)
                return carry
            lax.fori_loop(0, tm // NORM_STRIP, strip, 0, unroll=4)
        else:
            x = x_ref[0]
            ms = jnp.mean(x * x, axis=-1, keepdims=True)
            h_scr[...] = (x * lax.rsqrt(ms + RMS_EPS) * gain + shift).astype(BF16)

    @pl.when(j < n_qkv_tiles)
    def _():
        qkv_ref[0] = _dot_nt(h_scr[...], w_ref[...])

    @pl.when(j >= n_qkv_tiles)
    def _():
        prw_ref[0] = _dot_nt(h_scr[...], w_ref[...])


def _norm_proj(x, mod, g, w, layer, *, tm, tn, n_sample, is_sample):
    b, t, d = x.shape
    n_qkv_tiles = QKV_COLS // tn
    n_tiles = (QKV_COLS + RWKV_PROJ_PAD) // tn
    return pl.pallas_call(
        functools.partial(_norm_proj_kernel, n_qkv_tiles=n_qkv_tiles, n_sample=n_sample, is_sample=is_sample),
        grid=(b, t // tm, n_tiles),
        in_specs=[pl.BlockSpec((1, tm, d), lambda i, m, j: (i, m, 0)),
                  _mod_spec(mod, layer, MOD_SHIFT1), _mod_spec(mod, layer, MOD_SCALE1),
                  pl.BlockSpec((None, 1, d), lambda i, m, j: (layer, 0, 0)),
                  pl.BlockSpec((None, tn, d), lambda i, m, j: (layer, j, 0))],
        out_specs=[pl.BlockSpec((1, tm, tn), lambda i, m, j: (i, m, jnp.minimum(j, n_qkv_tiles - 1))),
                   pl.BlockSpec((1, tm, tn), lambda i, m, j: (i, m, jnp.maximum(j - n_qkv_tiles, 0)))],
        out_shape=[jax.ShapeDtypeStruct((b, t, QKV_COLS), F32),
                   jax.ShapeDtypeStruct((b, t, RWKV_PROJ_PAD), F32)],
        scratch_shapes=[pltpu.VMEM((tm, d), BF16)],
        compiler_params=_params(3),
        name="norm_proj",
    )(x, mod, mod, g, w)


LOG2E = math.log2(math.e)


def _attn_bias_table():
    t = np.arange(WINDOW)[:, None]
    j = np.arange(2 * WINDOW)[None, :]
    dist = t + WINDOW - j
    valid = (dist >= 0) & (dist <= WINDOW)
    slopes = np.asarray(ALIBI_SLOPES, np.float32)[:, None, None]
    bias = np.where(valid[None], -slopes * dist[None].astype(np.float32) * np.float32(LOG2E), np.float32(NEG_INF))
    first = np.where((j >= WINDOW)[None], bias, np.float32(NEG_INF))
    return np.stack([first, bias]).astype(np.float32)


def _attn_prompt_kernel(sink_ref, bias_ref, q_ref, kc_ref, vc_ref, kp_ref, vp_ref, o_ref, *, layer):
    q = q_ref[0] * (HEAD_DIM ** -0.5 * LOG2E)
    kc, vc, kp, vp = kc_ref[0], vc_ref[0], kp_ref[0], vp_ref[0]
    kv_heads, heads = range(N_KV_HEADS), range(N_Q_HEADS)
    ksl = [slice(g * HEAD_DIM, (g + 1) * HEAD_DIM) for g in kv_heads]
    k_band = [jnp.concatenate([kp[:, ksl[g]], kc[:, ksl[g]]], axis=0).astype(BF16) for g in kv_heads]
    v_band = [jnp.concatenate([vp[:, ksl[g]], vc[:, ksl[g]]], axis=0).astype(BF16) for g in kv_heads]
    lg = [_dot_nt(q[:, h * HEAD_DIM:(h + 1) * HEAD_DIM].astype(BF16), k_band[h // GQA_REP]) + bias_ref[0, h]
          for h in heads]
    sink = [sink_ref[layer * N_Q_HEADS + h] * LOG2E for h in heads]
    m = [jnp.maximum(jnp.max(lg[h], axis=-1, keepdims=True), sink[h]) for h in heads]
    p = [jnp.exp2(lg[h] - m[h]).astype(BF16) for h in heads]
    ones = jnp.ones((2 * WINDOW, HEAD_DIM), BF16)
    den = [_dot(p[h], ones) + jnp.exp2(sink[h] - m[h]) for h in heads]
    outs = [_dot(p[h], v_band[h // GQA_REP]) * (1.0 / den[h]) for h in heads]
    o_ref[0] = jnp.concatenate(outs, axis=1).astype(BF16)


def _attn_prompt(qkv, sinks_flat, layer):
    b, t, _ = qkv.shape
    kcol = ATTN_WIDTH // KV_WIDTH
    prev = lambda i, m: (i, jnp.maximum(m - 1, 0), kcol)
    prev_v = lambda i, m: (i, jnp.maximum(m - 1, 0), kcol + 1)
    return pl.pallas_call(
        functools.partial(_attn_prompt_kernel, layer=layer),
        grid=(b, t // WINDOW),
        in_specs=[pl.BlockSpec(memory_space=pltpu.SMEM),
                  pl.BlockSpec((1, N_Q_HEADS, WINDOW, 2 * WINDOW), lambda i, m: (jnp.minimum(m, 1), 0, 0, 0)),
                  pl.BlockSpec((1, WINDOW, ATTN_WIDTH), lambda i, m: (i, m, 0)),
                  pl.BlockSpec((1, WINDOW, KV_WIDTH), lambda i, m: (i, m, kcol)),
                  pl.BlockSpec((1, WINDOW, KV_WIDTH), lambda i, m: (i, m, kcol + 1)),
                  pl.BlockSpec((1, WINDOW, KV_WIDTH), prev),
                  pl.BlockSpec((1, WINDOW, KV_WIDTH), prev_v)],
        out_specs=pl.BlockSpec((1, WINDOW, ATTN_WIDTH), lambda i, m: (i, m, 0)),
        out_shape=jax.ShapeDtypeStruct((b, t, ATTN_WIDTH), BF16),
        compiler_params=_params(2),
        name="attn_prompt",
    )(sinks_flat, jnp.asarray(_attn_bias_table()), qkv, qkv, qkv, qkv, qkv)


def _attn_sample_kernel(row_ref, kc_ref, vc_ref, sink_ref, slope_ref, spread_ref, gather_ref,
                        o_ref, nk_ref, nv_ref):
    seqs = range(row_ref.shape[0])
    head_row = lax.broadcasted_iota(jnp.int32, (N_Q_HEADS, ATTN_WIDTH), 0)
    head_lane = lax.broadcasted_iota(jnp.int32, (N_Q_HEADS, ATTN_WIDTH), 1) // HEAD_DIM
    own = head_row == head_lane
    grp_row = lax.broadcasted_iota(jnp.int32, (N_Q_HEADS, KV_WIDTH), 0) // GQA_REP
    grp_lane = lax.broadcasted_iota(jnp.int32, (N_Q_HEADS, KV_WIDTH), 1) // HEAD_DIM
    bias = slope_ref[...] * (WINDOW - lax.broadcasted_iota(jnp.int32, (N_Q_HEADS, WINDOW), 1)).astype(F32)
    sink = sink_ref[...]
    last = lax.broadcasted_iota(jnp.int32, (WINDOW, KV_WIDTH), 0) == WINDOW - 1
    rows = [row_ref[s] for s in seqs]
    k_new = [r[:, ATTN_WIDTH:ATTN_WIDTH + KV_WIDTH] for r in rows]
    v_new = [r[:, ATTN_WIDTH + KV_WIDTH:] for r in rows]
    kc = [kc_ref[s] for s in seqs]
    vc = [vc_ref[s] for s in seqs]
    q_rows = [jnp.where(own, jnp.broadcast_to(r[:, :ATTN_WIDTH] * (HEAD_DIM ** -0.5), (N_Q_HEADS, ATTN_WIDTH)), 0.0)
              .astype(BF16) for r in rows]
    q_grp = [_dot(x, spread_ref[...]) for x in q_rows]
    lg = [_dot_nt(q_grp[s].astype(BF16), kc[s].astype(BF16)) - bias for s in seqs]
    lg_new = [jnp.sum(q_grp[s] * k_new[s].astype(BF16).astype(F32), axis=-1, keepdims=True) for s in seqs]
    m = [jnp.maximum(jnp.maximum(jnp.max(lg[s], axis=-1, keepdims=True), lg_new[s]), sink) for s in seqs]
    p = [jnp.exp(lg[s] - m[s]) for s in seqs]
    p_new = [jnp.exp(lg_new[s] - m[s]) for s in seqs]
    den = [jnp.sum(p[s], axis=-1, keepdims=True) + p_new[s] + jnp.exp(sink - m[s]) for s in seqs]
    out = [_dot((p[s] / den[s]).astype(BF16), vc[s].astype(BF16))
           + (p_new[s] / den[s]).astype(BF16).astype(F32) * v_new[s].astype(BF16).astype(F32) for s in seqs]
    full = [_dot(jnp.where(grp_row == grp_lane, x, 0.0).astype(BF16), gather_ref[...]) for x in out]
    for s in seqs:
        o_ref[s] = jnp.sum(jnp.where(own, full[s], 0.0), axis=0, keepdims=True).astype(BF16)
        nk_ref[s] = jnp.where(last, k_new[s], pltpu.roll(kc[s], WINDOW - 1, axis=0))
        nv_ref[s] = jnp.where(last, v_new[s], pltpu.roll(vc[s], WINDOW - 1, axis=0))


def _attn_sample(qkv_rows, cache_k, cache_v, sinks_col, layer):
    s = qkv_rows.shape[0]
    spread = np.zeros((ATTN_WIDTH, KV_WIDTH), np.float32)
    for h in range(N_Q_HEADS):
        for d in range(HEAD_DIM):
            spread[h * HEAD_DIM + d, (h // GQA_REP) * HEAD_DIM + d] = 1.0
    slopes = np.asarray(ALIBI_SLOPES, np.float32).reshape(N_Q_HEADS, 1)
    per_step = SUBLANES
    win = (per_step, WINDOW, KV_WIDTH)
    return pl.pallas_call(
        _attn_sample_kernel,
        grid=(s // per_step,),
        in_specs=[pl.BlockSpec((per_step, 1, QKV_COLS), lambda i: (i, 0, 0)),
                  pl.BlockSpec((None,) + win, lambda i: (layer, i, 0, 0)),
                  pl.BlockSpec((None,) + win, lambda i: (layer, i, 0, 0)),
                  pl.BlockSpec((None, N_Q_HEADS, 1), lambda i: (layer, 0, 0)),
                  pl.BlockSpec((N_Q_HEADS, 1), lambda i: (0, 0)),
                  pl.BlockSpec((ATTN_WIDTH, KV_WIDTH), lambda i: (0, 0)),
                  pl.BlockSpec((KV_WIDTH, ATTN_WIDTH), lambda i: (0, 0))],
        out_specs=[pl.BlockSpec((per_step, 1, ATTN_WIDTH), lambda i: (i, 0, 0)),
                   pl.BlockSpec(win, lambda i: (i, 0, 0)),
                   pl.BlockSpec(win, lambda i: (i, 0, 0))],
        out_shape=[jax.ShapeDtypeStruct((s, 1, ATTN_WIDTH), BF16),
                   jax.ShapeDtypeStruct((s, WINDOW, KV_WIDTH), F32),
                   jax.ShapeDtypeStruct((s, WINDOW, KV_WIDTH), F32)],
        compiler_params=_params(1),
        name="attn_sample",
    )(qkv_rows, cache_k, cache_v, sinks_col, jnp.asarray(slopes),
      jnp.asarray(spread, BF16), jnp.asarray(spread.T, BF16))


def _wkv_kernel(p_ref, shift0_ref, s0_ref, mix_ref, w0_ref, dup_ref, a0_ref, iup_ref, gup_ref,
                kk_ref, ka_ref, rk_ref, lnw_ref, lnb_ref, rw_ref, sout_ref, s_scr, prev_scr,
                arb_scr, aak_scr, ar_scr, vbd_scr, vb_scr, bk_scr, total_scr, gate_scr, bonus_scr, ncat_scr, ecat_scr,
                *, chunk):
    step = pl.program_id(1)
    n_steps = pl.num_programs(1)
    staged = (arb_scr, aak_scr, ar_scr, vbd_scr, vb_scr, bk_scr, total_scr, gate_scr, bonus_scr)
    bd_rows = HEADS_PER_GROUP * chunk
    groups = range(N_GROUPS)
    cols = [slice(g * MXU_DIM, (g + 1) * MXU_DIM) for g in groups]
    iota = lambda shape, dim: lax.broadcasted_iota(jnp.int32, shape, dim)
    same_head = iota((MXU_DIM, MXU_DIM), 0) // HEAD_DIM == iota((MXU_DIM, MXU_DIM), 1) // HEAD_DIM
    ones_bd = jnp.where(same_head, 1.0, 0.0).astype(BF16)
    tile_sel = jnp.where(iota((HEAD_DIM, MXU_DIM), 0) == iota((HEAD_DIM, MXU_DIM), 1) % HEAD_DIM, 1.0, 0.0).astype(BF16)
    fold_sel = jnp.where(iota((MXU_DIM, HEAD_DIM), 0) % HEAD_DIM == iota((MXU_DIM, HEAD_DIM), 1), 1.0, 0.0).astype(BF16)

    @pl.when(step == 0)
    def _():
        prev_scr[...] = shift0_ref[0]
        for g in groups:
            rows = s0_ref[0, cols[g], :]
            tiled = sum(_dot(part, tile_sel) for part in _split(rows, 3))
            s_scr[g] = jnp.where(same_head, tiled, 0.0)
        for ref in staged + (ncat_scr, ecat_scr):
            ref[...] = jnp.zeros_like(ref)
        total_scr[...] = jnp.ones_like(total_scr)

    row = iota((chunk, 1), 0)

    def shifted(lo, width):
        cur = p_ref[0, :, lo:lo + width]
        prev = jnp.where(row == 0, prev_scr[:, lo:lo + width], pltpu.roll(cur, 1, axis=0))
        return cur + (prev - cur) * mix_ref[:, lo:lo + width]

    def seg_sum(x):
        return _dot(x.astype(BF16), ones_bd)

    head_mask = [jnp.where(iota((chunk, MXU_DIM), 1) // HEAD_DIM == h, 1.0, 0.0).astype(BF16)
                 for h in range(HEADS_PER_GROUP)]

    def head_blocks(x):
        return jnp.concatenate([x * m for m in head_mask], axis=0)

    slot = step % 2
    inv_n = 1.0 / HEAD_DIM
    t_idx = iota((chunk, 2 * bd_rows), 0)
    s_idx = iota((chunk, 2 * bd_rows), 1) % chunk
    strict = (s_idx < t_idx)[:, :bd_rows]
    same_block = jnp.where(iota((bd_rows, bd_rows), 0) // chunk == iota((bd_rows, bd_rows), 1) // chunk,
                           1.0, 0.0).astype(BF16)
    eye_cat = jnp.where(t_idx == s_idx, 1.0, 0.0)[:, :bd_rows].astype(BF16)
    block_diag = lambda x: jnp.concatenate([x] * HEADS_PER_GROUP, axis=0) * same_block


    st_ar, st_ak, st_r, st_vbd, st_v, st_bk, st_total, st_gate, st_bonus = (
        [ref[slot, g] for g in groups] for ref in staged)
    st_e = [ecat_scr[g] for g in groups]
    s_b = [s_scr[g].astype(BF16) for g in groups]
    uy = [_dot_nt(st_ar[g], s_b[g]) for g in groups]

    e_cat = [ncat_scr[g] for g in groups]
    m_b = [x.astype(BF16) for x in e_cat]
    m_b = [_dot(x, block_diag(x)).astype(BF16) for x in m_b]
    levels = int(math.log2(chunk)) - 1

    def inverse_level(lvl, e_cat, m_b):
        t_bd = [block_diag(x.astype(BF16) + eye_cat) for x in e_cat]
        if lvl < levels - 1:
            both = [_dot(m_b[g], jnp.concatenate([t_bd[g], block_diag(m_b[g])], axis=1)) for g in groups]
            return ([e_cat[g] + both[g][:, :bd_rows] for g in groups], [x[:, bd_rows:].astype(BF16) for x in both])
        return [e_cat[g] + _dot(m_b[g], t_bd[g]) for g in groups], None

    lora_in = shifted(LORA_OFF, LANES)
    lora_tanh = jnp.tanh(lora_in).astype(BF16)
    lora_lin = lora_in.astype(BF16)
    gate_sig = jax.nn.sigmoid(shifted(GATE_OFF, GATE_PAD)).astype(BF16)
    r = [shifted(g * MXU_DIM, MXU_DIM) for g in groups]

    rhs = [uy[g][:chunk] + _dot(st_ak[g], st_vbd[g]) for g in groups]
    e_cat, m_b = inverse_level(0, e_cat, m_b)

    k = [shifted(RWKV_WIDTH + g * MXU_DIM, MXU_DIM) for g in groups]
    v = [shifted(2 * RWKV_WIDTH + g * MXU_DIM, MXU_DIM) for g in groups]
    prev_scr[...] = p_ref[0, chunk - 1:chunk, :]

    p = [rhs[g] + _dot(st_e[g], head_blocks(rhs[g].astype(BF16))) for g in groups]
    p_b = [x.astype(BF16) for x in p]
    if levels > 1:
        e_cat, m_b = inverse_level(1, e_cat, m_b)

    logw = [-math.exp(-0.5) * jax.nn.sigmoid(w0_ref[:, cols[g]] + _dot(lora_tanh, dup_ref[:, cols[g]])) for g in groups]
    a = [jax.nn.sigmoid(a0_ref[:, cols[g]] + _dot(lora_lin, iup_ref[:, cols[g]])) for g in groups]
    gate = [_dot(gate_sig, gup_ref[:, cols[g]]) for g in groups]

    y = [uy[g][chunk:] + _dot(st_r[g], jnp.concatenate([head_blocks(p_b[g]), st_vbd[g]], axis=0)) for g in groups]
    for g in groups:
        upd = _dot_tn(jnp.concatenate([p_b[g], st_v[g]], axis=0), st_bk[g])
        s_scr[g] = s_scr[g] * st_total[g] + jnp.where(same_head, upd, 0.0)
    if levels > 2:
        e_cat, m_b = inverse_level(2, e_cat, m_b)

    kk = [k[g] * kk_ref[:, cols[g]] for g in groups]
    kk = [kk[g] / jnp.maximum(jnp.sqrt(seg_sum(kk[g] * kk[g])), 1e-12) for g in groups]
    k = [k[g] * (1.0 + (a[g] - 1.0) * ka_ref[:, cols[g]]) for g in groups]

    dev = [y[g] - seg_sum(y[g]) * inv_n for g in groups]
    if levels > 3:
        e_cat, m_b = inverse_level(3, e_cat, m_b)

    tri = jnp.where(iota((chunk, chunk), 0) >= iota((chunk, chunk), 1), 1.0, 0.0).astype(BF16)
    cum = [sum(_dot(tri, part) for part in _split(logw[g], 3)) for g in groups]

    var = [seg_sum(x * x) * inv_n for x in dev]
    for lvl in range(4, levels):
        e_cat, m_b = inverse_level(lvl, e_cat, m_b)

    e_pos = [jnp.exp(x) for x in cum]
    e_neg = [jnp.exp(-x) for x in cum]
    a_t = [-kk[g] * jnp.exp(cum[g] - logw[g]) for g in groups]
    b_t = [kk[g] * a[g] * e_neg[g] for g in groups]

    for g in groups:
        yn = dev[g] * lax.rsqrt(var[g] + GN_EPS) * lnw_ref[:, cols[g]] + lnb_ref[:, cols[g]]
        rw_ref[0, :, cols[g]] = ((yn + st_bonus[g]) * st_gate[g]).astype(BF16)

    k_t = [k[g] * e_neg[g] for g in groups]
    r_t = [r[g] * e_pos[g] for g in groups]
    total = [x[chunk - 1:chunk, :] for x in e_pos]
    ar_b = [jnp.concatenate([a_t[g], r_t[g]], axis=0).astype(BF16) for g in groups]
    xb =[head_blocks(x.astype(BF16)) for x in b_t]
    xk = [head_blocks(x.astype(BF16)) for x in k_t]
    v_bd = [head_blocks(x.astype(BF16)) for x in v]

    scores = [_dot_nt(ar_b[g], jnp.concatenate([xb[g], xk[g]], axis=0)) for g in groups]
    bonus = [seg_sum(r[g] * k[g] * rk_ref[:, cols[g]]) * v[g] for g in groups]
    for g in groups:
        ecat_scr[g] = e_cat[g].astype(BF16)
        ncat_scr[g] = jnp.where(strict, scores[g][:chunk, :bd_rows], 0.0)
        arb_scr[slot, g] = ar_b[g]
        aak_scr[slot, g] = jnp.where(strict, scores[g][:chunk, bd_rows:], 0.0).astype(BF16)
        ar_scr[slot, g] = jnp.where(s_idx <= t_idx, scores[g][chunk:], 0.0).astype(BF16)
        vbd_scr[slot, g] = v_bd[g]
        vb_scr[slot, g] = v[g].astype(BF16)
        bk_scr[slot, g] = jnp.concatenate([b_t[g] * total[g], k_t[g] * total[g]], axis=0).astype(BF16)
        total_scr[slot, g], gate_scr[slot, g], bonus_scr[slot, g] = total[g], gate[g], bonus[g]

    @pl.when(step == n_steps - 1)
    def _():
        for g in groups:
            folded = sum(_dot(part, fold_sel) for part in _split(s_scr[g], 3))
            sout_ref[0, cols[g], :] = folded


def _wkv(prw, shift0, s0, wp, layer, *, chunk):
    b, t, _ = prw.shape
    n_chunks = t // chunk
    bd_rows = HEADS_PER_GROUP * chunk
    lay = lambda *blk: pl.BlockSpec((None,) + blk, lambda i, c: (layer,) + (0,) * len(blk))
    vec = lay(1, RWKV_WIDTH)
    grp = lambda rows, width, dtype: pltpu.VMEM((N_GROUPS, rows, width), dtype)
    two = lambda rows, width, dtype: pltpu.VMEM((2, N_GROUPS, rows, width), dtype)
    return pl.pallas_call(
        functools.partial(_wkv_kernel, chunk=chunk),
        grid=(b, n_chunks + 2),
        in_specs=[pl.BlockSpec((1, chunk, RWKV_PROJ_PAD), lambda i, c: (i, jnp.minimum(c, n_chunks - 1), 0)),
                  pl.BlockSpec((1, 1, RWKV_PROJ_PAD), lambda i, c: (i, 0, 0)),
                  pl.BlockSpec((1, RWKV_WIDTH, HEAD_DIM), lambda i, c: (i, 0, 0)),
                  lay(1, RWKV_PROJ_PAD), vec, lay(LANES, RWKV_WIDTH), vec, lay(LANES, RWKV_WIDTH),
                  lay(GATE_PAD, RWKV_WIDTH), vec, vec, vec, vec, vec],
        out_specs=[pl.BlockSpec((1, chunk, RWKV_WIDTH), lambda i, c: (i, jnp.maximum(c - 2, 0), 0)),
                   pl.BlockSpec((1, RWKV_WIDTH, HEAD_DIM), lambda i, c: (i, 0, 0))],
        out_shape=[jax.ShapeDtypeStruct((b, t, RWKV_WIDTH), BF16),
                   jax.ShapeDtypeStruct((b, RWKV_WIDTH, HEAD_DIM), F32)],
        scratch_shapes=[pltpu.VMEM((N_GROUPS, MXU_DIM, MXU_DIM), F32),
                        pltpu.VMEM((1, RWKV_PROJ_PAD), F32),
                        two(2 * chunk, MXU_DIM, BF16), two(chunk, bd_rows, BF16), two(chunk, 2 * bd_rows, BF16),
                        two(bd_rows, MXU_DIM, BF16), two(chunk, MXU_DIM, BF16), two(2 * chunk, MXU_DIM, BF16),
                        two(1, MXU_DIM, F32), two(chunk, MXU_DIM, F32), two(chunk, MXU_DIM, F32),
                        grp(chunk, bd_rows, F32), grp(chunk, bd_rows, BF16)],
        compiler_params=_params(2),
        name="wkv7",
    )(prw, shift0, s0, wp["mix"], wp["w0"], wp["decay_up"], wp["a0"], wp["iclr_up"], wp["gate_up"],
      wp["k_k"], wp["k_a"], wp["r_k"], wp["ln_w"], wp["ln_b"])


SUBLANES = 8


def _wkv_step_kernel(p_ref, shift_ref, s_ref, mix_ref, w0_ref, dup_ref, a0_ref, iup_ref, gup_ref,
                     kk_ref, ka_ref, rk_ref, lnw_ref, lnb_ref, fold_ref, rw_ref, sout_ref):
    iota = lambda shape, dim: lax.broadcasted_iota(jnp.int32, shape, dim)
    groups = range(N_GROUPS)
    cols = [slice(g * MXU_DIM, (g + 1) * MXU_DIM) for g in groups]
    ones_bd = jnp.where(iota((MXU_DIM, MXU_DIM), 0) // HEAD_DIM == iota((MXU_DIM, MXU_DIM), 1) // HEAD_DIM,
                        1.0, 0.0).astype(BF16)

    def seg_sum(x):
        return jnp.concatenate([_dot(x[:, cols[g]].astype(BF16), ones_bd) for g in groups], axis=1)

    cur = jnp.broadcast_to(p_ref[0], (SUBLANES, RWKV_PROJ_PAD))
    prev = jnp.broadcast_to(shift_ref[0], (SUBLANES, RWKV_PROJ_PAD))
    p = cur + (prev - cur) * mix_ref[...]
    r = p[:, 0:RWKV_WIDTH]
    k = p[:, RWKV_WIDTH:2 * RWKV_WIDTH]
    v = p[:, 2 * RWKV_WIDTH:3 * RWKV_WIDTH]
    lora_in = p[:, LORA_OFF:LORA_OFF + LANES]
    gate_in = p[:, GATE_OFF:GATE_OFF + GATE_PAD]
    w = w0_ref[...] + _dot(jnp.tanh(lora_in).astype(BF16), dup_ref[...])
    decay = jnp.exp(-math.exp(-0.5) * jax.nn.sigmoid(w))
    a = jax.nn.sigmoid(a0_ref[...] + _dot(lora_in.astype(BF16), iup_ref[...]))
    gate = _dot(jax.nn.sigmoid(gate_in).astype(BF16), gup_ref[...])
    kk = k * kk_ref[...]
    kk = kk / jnp.maximum(jnp.sqrt(seg_sum(kk * kk)), 1e-12)
    k = k * (1.0 + (a - 1.0) * ka_ref[...])

    own = iota((N_RWKV_HEADS, RWKV_WIDTH), 0) == iota((N_RWKV_HEADS, RWKV_WIDTH), 1) // HEAD_DIM
    spread = lambda x: jnp.where(own, jnp.broadcast_to(x[0:1], (N_RWKV_HEADS, RWKV_WIDTH)), 0.0)
    stacked = jnp.concatenate([spread(decay), spread(-kk), spread(kk * a), spread(k), spread(r)], axis=0)
    per_head = sum(_dot(part, fold_ref[...]) for part in _split(stacked, 3))
    w_h, a_h, b_h, k_h, r_h = (per_head[i * N_RWKV_HEADS:(i + 1) * N_RWKV_HEADS] for i in range(5))
    rows = lambda x: jnp.concatenate(
        [jnp.broadcast_to(x[h:h + 1, :], (HEAD_DIM, HEAD_DIM)) for h in range(N_RWKV_HEADS)], axis=0)

    eye = iota((LANES, LANES), 0) == iota((LANES, LANES), 1)
    ones_cols = jnp.ones((LANES, HEAD_DIM), BF16)
    v_col = jnp.concatenate(
        [sum(_dot(part, ones_cols) for part in
             _split(jnp.where(eye, jnp.broadcast_to(v[0:1, t * LANES:(t + 1) * LANES], (LANES, LANES)), 0.0), 3))
         for t in range(RWKV_WIDTH // LANES)], axis=0)

    s = s_ref[0]
    sa = jnp.sum(s * rows(a_h), axis=-1, keepdims=True)
    s_new = s * rows(w_h) + sa * rows(b_h) + v_col * rows(k_h)
    sout_ref[0] = s_new
    y_heads = _dot_nt(r_h.astype(BF16), s_new.astype(BF16))
    y = jnp.broadcast_to(jnp.sum(jnp.where(own, y_heads, 0.0), axis=0, keepdims=True), (SUBLANES, RWKV_WIDTH))

    inv_n = 1.0 / HEAD_DIM
    dev = y - seg_sum(y) * inv_n
    var = seg_sum(dev * dev) * inv_n
    y = dev * lax.rsqrt(var + GN_EPS) * lnw_ref[...] + lnb_ref[...]
    bonus = seg_sum(r * k * rk_ref[...]) * v
    rw_ref[0] = ((y + bonus) * gate)[0:1].astype(BF16)


def _wkv_step(prw_rows, shift_rows, state, wp, layer):
    s = prw_rows.shape[0]
    lay = lambda *blk: pl.BlockSpec((None,) + blk, lambda i: (layer,) + (0,) * len(blk))
    vec = lay(1, RWKV_WIDTH)
    fold = np.tile(np.eye(HEAD_DIM, dtype=np.float32), (N_RWKV_HEADS, 1))
    return pl.pallas_call(
        _wkv_step_kernel,
        grid=(s,),
        in_specs=[pl.BlockSpec((1, 1, RWKV_PROJ_PAD), lambda i: (i, 0, 0)),
                  pl.BlockSpec((None, 1, 1, RWKV_PROJ_PAD), lambda i: (layer, i, 0, 0)),
                  pl.BlockSpec((None, 1, RWKV_WIDTH, HEAD_DIM), lambda i: (layer, i, 0, 0)),
                  lay(1, RWKV_PROJ_PAD), vec, lay(LANES, RWKV_WIDTH), vec, lay(LANES, RWKV_WIDTH),
                  lay(GATE_PAD, RWKV_WIDTH), vec, vec, vec, vec, vec,
                  pl.BlockSpec((RWKV_WIDTH, HEAD_DIM), lambda i: (0, 0))],
        out_specs=[pl.BlockSpec((1, 1, RWKV_WIDTH), lambda i: (i, 0, 0)),
                   pl.BlockSpec((1, RWKV_WIDTH, HEAD_DIM), lambda i: (i, 0, 0))],
        out_shape=[jax.ShapeDtypeStruct((s, 1, RWKV_WIDTH), BF16),
                   jax.ShapeDtypeStruct((s, RWKV_WIDTH, HEAD_DIM), F32)],
        compiler_params=_params(1),
        name="wkv7_step",
    )(prw_rows, shift_rows, state, wp["mix"], wp["w0"], wp["decay_up"], wp["a0"], wp["iclr_up"], wp["gate_up"],
      wp["k_k"], wp["k_a"], wp["r_k"], wp["ln_w"], wp["ln_b"], jnp.asarray(fold, BF16))


def _out_proj_kernel(att_ref, rw_ref, x_ref, gt_ref, sh_ref, sc_ref, g_ref, w_ref, x1_ref, h_ref, *, n_sample,
                     is_sample):
    rows = functools.partial(_mod_rows, n_sample=n_sample, is_sample=is_sample)
    y = _dot(att_ref[0], w_ref[0:ATTN_WIDTH, :]) + _dot(rw_ref[0], w_ref[ATTN_WIDTH:, :])
    x1 = x_ref[0] + rows(gt_ref) * y
    x1_ref[0] = x1
    h_ref[0] = _rmsnorm_mod(x1, g_ref[...], rows(sh_ref), rows(sc_ref)).astype(BF16)


def _out_proj(att, rw, x, mod, g, w, layer, *, tm, n_sample, is_sample):
    b, t, d = x.shape
    row = lambda width: pl.BlockSpec((1, tm, width), lambda i, m: (i, m, 0))
    return pl.pallas_call(
        functools.partial(_out_proj_kernel, n_sample=n_sample, is_sample=is_sample),
        grid=(b, t // tm),
        in_specs=[row(ATTN_WIDTH), row(RWKV_WIDTH), row(d),
                  _mod_spec(mod, layer, MOD_GATE1), _mod_spec(mod, layer, MOD_SHIFT2), _mod_spec(mod, layer, MOD_SCALE2),
                  pl.BlockSpec((None, 1, d), lambda i, m: (layer, 0, 0)),
                  pl.BlockSpec((None, d, d), lambda i, m: (layer, 0, 0))],
        out_specs=[row(d), row(d)],
        out_shape=[jax.ShapeDtypeStruct((b, t, d), F32), jax.ShapeDtypeStruct((b, t, d), BF16)],
        compiler_params=_params(2),
        name="out_proj",
    )(att, rw, x, mod, mod, mod, g, w)


def _ffn_kernel(h_ref, x_ref, gt_ref, wg_ref, wu_ref, wo_ref, gf_ref, o_ref, acc_ref, *, final_norm, n_sample,
                is_sample):
    j = pl.program_id(2)

    @pl.when(j == 0)
    def _():
        acc_ref[...] = jnp.zeros_like(acc_ref)

    h = h_ref[0]
    gate = _dot(h, wg_ref[...])
    up = _dot(h, wu_ref[...])
    act = (gate * jax.nn.sigmoid(gate) * up).astype(BF16)
    acc_ref[...] += _dot(act, wo_ref[...])

    @pl.when(j == pl.num_programs(2) - 1)
    def _():
        x2 = x_ref[0] + _mod_rows(gt_ref, n_sample, is_sample) * acc_ref[...]
        if final_norm:
            ms = jnp.mean(x2 * x2, axis=-1, keepdims=True)
            x2 = x2 * lax.rsqrt(ms + RMS_EPS) * gf_ref[...]
        o_ref[0] = x2


def _ffn(h, x, mod, w_in, w_out, g_final, layer, *, tm, th, final_norm, n_sample, is_sample):
    b, t, d = x.shape
    n_h = FFN_HIDDEN // th
    return pl.pallas_call(
        functools.partial(_ffn_kernel, final_norm=final_norm, n_sample=n_sample, is_sample=is_sample),
        grid=(b, t // tm, n_h),
        in_specs=[pl.BlockSpec((1, tm, d), lambda i, m, j: (i, m, 0)),
                  pl.BlockSpec((1, tm, d), lambda i, m, j: (i, m, 0)),
                  _mod_spec(mod, layer, MOD_GATE2),
                  pl.BlockSpec((None, d, th), lambda i, m, j: (layer, 0, j)),
                  pl.BlockSpec((None, d, th), lambda i, m, j: (layer, 0, j + n_h)),
                  pl.BlockSpec((None, th, d), lambda i, m, j: (layer, j, 0)),
                  pl.BlockSpec((1, d), lambda i, m, j: (0, 0))],
        out_specs=pl.BlockSpec((1, tm, d), lambda i, m, j: (i, m, 0)),
        out_shape=jax.ShapeDtypeStruct((b, t, d), F32),
        scratch_shapes=[pltpu.VMEM((tm, d), F32)],
        compiler_params=_params(3),
        name="ffn",
    )(h, x, mod, w_in, w_in, w_out, g_final)


def _tiles(t_prompt, n_sample):
    tiles = dict(proj_tm=1024, proj_tn=512, out_tm=512, ffn_tm=512, ffn_th=512, wkv_chunk=64)
    assert t_prompt % tiles["proj_tm"] == 0 and t_prompt % tiles["out_tm"] == 0 and t_prompt % tiles["ffn_tm"] == 0
    assert t_prompt % tiles["wkv_chunk"] == 0 and t_prompt % WINDOW == 0
    assert QKV_COLS % tiles["proj_tn"] == 0 and RWKV_PROJ_PAD % tiles["proj_tn"] == 0
    assert FFN_HIDDEN % tiles["ffn_th"] == 0 and n_sample % SUBLANES == 0
    return tiles


def _pad_cols(x, width):
    return jnp.pad(x, [(0, 0)] * (x.ndim - 1) + [(0, width - x.shape[-1])])


def _pad_rows(x, rows, offset=0):
    return jnp.pad(x, [(0, 0), (offset, rows - offset - x.shape[1]), (0, 0)])


def kernel(x_prompt, x_sample, cache_k, cache_v, state_wkv, state_shift, c_prompt, c_sample, w_ada, b_ada, g_norm_mix, g_norm_ffn, w_in, w_out, attn_sinks, mix_shift, decay_w0, decay_up, iclr_a0, iclr_up, gate_up, k_k, k_a, r_k, ln_x_w, ln_x_b, w_ffn_in, w_ffn_out, g_norm_final):
    n_layers = w_in.shape[0]
    bp, tp, d = x_prompt.shape
    bs = x_sample.shape[0]
    tiles = _tiles(tp, bs)

    w_in_t = jnp.pad(jnp.swapaxes(w_in, 1, 2).astype(BF16),
                     [(0, 0), (0, QKV_COLS + RWKV_PROJ_PAD - w_in.shape[2]), (0, 0)])
    w_out_b = w_out.astype(BF16)
    w_ffn_in_b = w_ffn_in.astype(BF16)
    w_ffn_out_b = w_ffn_out.astype(BF16)
    vec = lambda x: x.reshape(n_layers, 1, -1)
    wp = dict(
        mix=vec(_pad_cols(mix_shift, RWKV_PROJ_PAD)), w0=vec(decay_w0), a0=vec(iclr_a0),
        decay_up=_pad_rows(decay_up, LANES).astype(BF16),
        iclr_up=_pad_rows(iclr_up, LANES, D_DECAY_LORA).astype(BF16),
        gate_up=_pad_rows(gate_up, GATE_PAD).astype(BF16),
        k_k=vec(k_k), k_a=vec(k_a), r_k=vec(r_k), ln_w=vec(ln_x_w), ln_b=vec(ln_x_b))
    g_mix, g_ffn = vec(g_norm_mix), vec(g_norm_ffn)
    g_final = g_norm_final.reshape(1, d)
    sinks_flat = attn_sinks.reshape(-1)
    sinks_col = attn_sinks.reshape(n_layers, N_Q_HEADS, 1)
    cache_k2 = cache_k.reshape(n_layers, bs, WINDOW, KV_WIDTH)
    cache_v2 = cache_v.reshape(n_layers, bs, WINDOW, KV_WIDTH)
    shift_rows = _pad_cols(state_shift, RWKV_PROJ_PAD).reshape(n_layers, bs, 1, RWKV_PROJ_PAD)
    state_rows = state_wkv.reshape(n_layers, bs, RWKV_WIDTH, HEAD_DIM)
    zero_shift = jnp.zeros((bp, 1, RWKV_PROJ_PAD), F32)
    zero_state = jnp.zeros((bp, RWKV_WIDTH, HEAD_DIM), F32)

    mod = _ada(jnp.concatenate([c_sample, c_prompt], axis=0), w_ada, b_ada)
    prompt = dict(n_sample=bs, is_sample=False)
    sample = dict(n_sample=bs, is_sample=True)

    xp = x_prompt
    xs = x_sample.reshape(1, bs, d)
    outs_p = [[] for _ in range(4)]
    outs_s = [[] for _ in range(4)]
    for layer in range(n_layers):
        last = layer == n_layers - 1

        qkv, prw = _norm_proj(xp, mod, g_mix, w_in_t, layer, tm=tiles["proj_tm"], tn=tiles["proj_tn"], **prompt)
        att = _attn_prompt(qkv, sinks_flat, layer)
        rw, state = _wkv(prw, zero_shift, zero_state, wp, layer, chunk=tiles["wkv_chunk"])
        x1, h2 = _out_proj(att, rw, xp, mod, g_ffn, w_out_b, layer, tm=tiles["out_tm"], **prompt)
        xp = _ffn(h2, x1, mod, w_ffn_in_b, w_ffn_out_b, g_final, layer, tm=tiles["ffn_tm"], th=tiles["ffn_th"],
                  final_norm=last, **prompt)
        outs_p[0].append(qkv[:, tp - WINDOW:, ATTN_WIDTH:ATTN_WIDTH + KV_WIDTH].reshape(bp, WINDOW, N_KV_HEADS, HEAD_DIM))
        outs_p[1].append(qkv[:, tp - WINDOW:, ATTN_WIDTH + KV_WIDTH:].reshape(bp, WINDOW, N_KV_HEADS, HEAD_DIM))
        outs_p[2].append(state.reshape(bp, N_RWKV_HEADS, HEAD_DIM, HEAD_DIM))
        outs_p[3].append(prw[:, tp - 1, :RWKV_PROJ])

        qkv_s, prw_s = _norm_proj(xs, mod, g_mix, w_in_t, layer, tm=bs, tn=tiles["proj_tn"], **sample)
        att_s, nk, nv = _attn_sample(qkv_s.reshape(bs, 1, QKV_COLS), cache_k2, cache_v2, sinks_col, layer)
        prw_tok = prw_s.reshape(bs, 1, RWKV_PROJ_PAD)
        rw_s, state_s = _wkv_step(prw_tok, shift_rows, state_rows, wp, layer)
        x1_s, h2_s = _out_proj(att_s.reshape(1, bs, ATTN_WIDTH), rw_s.reshape(1, bs, RWKV_WIDTH), xs,
                               mod, g_ffn, w_out_b, layer, tm=bs, **sample)
        xs = _ffn(h2_s, x1_s, mod, w_ffn_in_b, w_ffn_out_b, g_final, layer, tm=bs, th=tiles["ffn_th"],
                  final_norm=last, **sample)
        outs_s[0].append(nk.reshape(bs, WINDOW, N_KV_HEADS, HEAD_DIM))
        outs_s[1].append(nv.reshape(bs, WINDOW, N_KV_HEADS, HEAD_DIM))
        outs_s[2].append(state_s.reshape(bs, N_RWKV_HEADS, HEAD_DIM, HEAD_DIM))
        outs_s[3].append(prw_tok[:, 0, :RWKV_PROJ])

    stack = lambda xs_: jnp.stack(xs_, axis=0)
    return (xp, xs.reshape(bs, 1, d),
            stack(outs_p[0]), stack(outs_p[1]), stack(outs_p[2]), stack(outs_p[3]),
            stack(outs_s[0]), stack(outs_s[1]), stack(outs_s[2]), stack(outs_s[3]))
```

```python
import functools
import math

import numpy as np
import jax
import jax.numpy as jnp
from jax import lax
from jax.experimental import pallas as pl
from jax.experimental.pallas import tpu as pltpu

F32 = jnp.float32
BF16 = jnp.bfloat16

D_MODEL = 2048
HEAD_DIM = 64
ATTN_WIDTH = D_MODEL // 2
RWKV_WIDTH = D_MODEL - ATTN_WIDTH
N_Q_HEADS = ATTN_WIDTH // HEAD_DIM
N_KV_HEADS = N_Q_HEADS // 4
GQA_REP = N_Q_HEADS // N_KV_HEADS
KV_WIDTH = N_KV_HEADS * HEAD_DIM
N_RWKV_HEADS = RWKV_WIDTH // HEAD_DIM
WINDOW = 128
D_DECAY_LORA = 64
D_ICLR_LORA = 64
D_GATE_LORA = 160
RWKV_PROJ = 3 * RWKV_WIDTH + D_DECAY_LORA + D_ICLR_LORA + D_GATE_LORA
QKV_COLS = ATTN_WIDTH + 2 * KV_WIDTH
FFN_HIDDEN = 5632
RMS_EPS = 1e-5
GN_EPS = 64e-5
NEG_INF = -1e30

LANES = 128
MXU_DIM = 256
VMEM_LIMIT_BYTES = 56 * 1024 * 1024

LORA_OFF = 3 * RWKV_WIDTH
GATE_OFF = LORA_OFF + D_DECAY_LORA + D_ICLR_LORA
GATE_PAD = 2 * LANES
RWKV_PROJ_PAD = 3584
HEADS_PER_GROUP = MXU_DIM // HEAD_DIM
N_GROUPS = N_RWKV_HEADS // HEADS_PER_GROUP

ALIBI_SLOPES = tuple(float(s) for s in np.exp2(-8.0 * np.arange(1, N_Q_HEADS + 1, dtype=np.float32) / N_Q_HEADS))


def _params(n_axes):
    return pltpu.CompilerParams(dimension_semantics=("arbitrary",) * n_axes,
                                vmem_limit_bytes=VMEM_LIMIT_BYTES)


def _dot(a, b):
    return jnp.dot(a, b, preferred_element_type=F32)


def _dot_nt(a, b):
    return lax.dot_general(a, b, (((1,), (1,)), ((), ())), preferred_element_type=F32)


def _dot_tn(a, b):
    return lax.dot_general(a, b, (((0,), (0,)), ((), ())), preferred_element_type=F32)


def _split(x, parts):
    out = []
    for _ in range(parts):
        p = x.astype(BF16)
        out.append(p)
        x = x - p.astype(F32)
    return out


def _rmsnorm_mod(x, g, shift, scale):
    ms = jnp.mean(x * x, axis=-1, keepdims=True)
    y = x * lax.rsqrt(ms + RMS_EPS) * g
    return y * (1.0 + scale) + shift


def _ada_kernel(c_ref, w_ref, b_ref, o_ref):
    c = c_ref[...]
    s = (c * jax.nn.sigmoid(c)).astype(BF16)
    o_ref[...] = _dot(s, w_ref[...].astype(BF16)) + b_ref[...]


def _ada(c_all, w_ada, b_ada):
    n_layers, d, n = w_ada.shape
    rows = c_all.shape[0]
    tn = 1024
    return pl.pallas_call(
        _ada_kernel,
        grid=(n_layers, n // tn),
        in_specs=[pl.BlockSpec((rows, d), lambda l, j: (0, 0)),
                  pl.BlockSpec((None, d, tn), lambda l, j: (l, 0, j)),
                  pl.BlockSpec((None, 1, tn), lambda l, j: (l, 0, j))],
        out_specs=pl.BlockSpec((None, rows, tn), lambda l, j: (l, 0, j)),
        out_shape=jax.ShapeDtypeStruct((n_layers, rows, n), F32),
        compiler_params=_params(2),
        name="ada_mod",
    )(c_all, w_ada, b_ada.reshape(n_layers, 1, n))


NORM_STRIP = 32


MOD_SHIFT1, MOD_SCALE1, MOD_GATE1, MOD_SHIFT2, MOD_SCALE2, MOD_GATE2 = range(6)


def _mod_spec(mod, layer, part):
    rows, d = mod.shape[1], mod.shape[2] // 6
    return pl.BlockSpec((None, rows, d), lambda *_: (layer, 0, part))


def _mod_rows(ref, n_sample, is_sample):
    if is_sample:
        return ref[0:n_sample, :]
    return ref[pl.ds(n_sample + pl.program_id(0), 1), :]


def _norm_proj_kernel(x_ref, sh_ref, sc_ref, g_ref, w_ref, o_ref, h_scr, *, n_sample, is_sample):
    j = pl.program_id(2)
    tm = h_scr.shape[0]

    @pl.when(j == 0)
    def _():
        gain = g_ref[...] * (1.0 + _mod_rows(sc_ref, n_sample, is_sample))
        shift = _mod_rows(sh_ref, n_sample, is_sample)
        if gain.shape[0] == 1:
            def strip(i, carry):
                rows = pl.ds(pl.multiple_of(i * NORM_STRIP, NORM_STRIP), NORM_STRIP)
                x = x_ref[0, rows, :]
                ms = jnp.mean(x * x, axis=-1, keepdims=True)
                h_scr[rows, :] = (x * lax.rsqrt(ms + RMS_EPS) * gain + shift).astype(BF16)
                return carry
            lax.fori_loop(0, tm // NORM_STRIP, strip, 0, unroll=4)
        else:
            x = x_ref[0]
            ms = jnp.mean(x * x, axis=-1, keepdims=True)
            h_scr[...] = (x * lax.rsqrt(ms + RMS_EPS) * gain + shift).astype(BF16)

    o_ref[0] = _dot_nt(h_scr[...], w_ref[...])


def _norm_proj(x, mod, g, w, layer, *, tm, tn, n_sample, is_sample):
    b, t, d = x.shape
    n = QKV_COLS + RWKV_PROJ_PAD
    return pl.pallas_call(
        functools.partial(_norm_proj_kernel, n_sample=n_sample, is_sample=is_sample),
        grid=(b, t // tm, n // tn),
        in_specs=[pl.BlockSpec((1, tm, d), lambda i, m, j: (i, m, 0)),
                  _mod_spec(mod, layer, MOD_SHIFT1), _mod_spec(mod, layer, MOD_SCALE1),
                  pl.BlockSpec((None, 1, d), lambda i, m, j: (layer, 0, 0)),
                  pl.BlockSpec((None, tn, d), lambda i, m, j: (layer, j, 0))],
        out_specs=pl.BlockSpec((1, tm, tn), lambda i, m, j: (i, m, j)),
        out_shape=jax.ShapeDtypeStruct((b, t, n), F32),
        scratch_shapes=[pltpu.VMEM((tm, d), BF16)],
        compiler_params=_params(3),
        name="norm_proj",
    )(x, mod, mod, g, w)


LOG2E = math.log2(math.e)


def _attn_bias_table():
    t = np.arange(WINDOW)[:, None]
    j = np.arange(2 * WINDOW)[None, :]
    dist = t + WINDOW - j
    valid = (dist >= 0) & (dist <= WINDOW)
    slopes = np.asarray(ALIBI_SLOPES, np.float32)[:, None, None]
    bias = np.where(valid[None], -slopes * dist[None].astype(np.float32) * np.float32(LOG2E), np.float32(NEG_INF))
    first = np.where((j >= WINDOW)[None], bias, np.float32(NEG_INF))
    return np.stack([first, bias]).astype(np.float32)


def _attn_prompt_kernel(sink_ref, bias_ref, q_ref, kc_ref, vc_ref, kp_ref, vp_ref, o_ref, *, layer):
    q = q_ref[0] * (HEAD_DIM ** -0.5 * LOG2E)
    kc, vc, kp, vp = kc_ref[0], vc_ref[0], kp_ref[0], vp_ref[0]
    kv_heads, heads = range(N_KV_HEADS), range(N_Q_HEADS)
    ksl = [slice(g * HEAD_DIM, (g + 1) * HEAD_DIM) for g in kv_heads]
    k_band = [jnp.concatenate([kp[:, ksl[g]], kc[:, ksl[g]]], axis=0).astype(BF16) for g in kv_heads]
    v_band = [jnp.concatenate([vp[:, ksl[g]], vc[:, ksl[g]]], axis=0).astype(BF16) for g in kv_heads]
    lg = [_dot_nt(q[:, h * HEAD_DIM:(h + 1) * HEAD_DIM].astype(BF16), k_band[h // GQA_REP]) + bias_ref[0, h]
          for h in heads]
    sink = [sink_ref[layer * N_Q_HEADS + h] * LOG2E for h in heads]
    m = [jnp.maximum(jnp.max(lg[h], axis=-1, keepdims=True), sink[h]) for h in heads]
    p = [jnp.exp2(lg[h] - m[h]).astype(BF16) for h in heads]
    ones = jnp.ones((2 * WINDOW, HEAD_DIM), BF16)
    den = [_dot(p[h], ones) + jnp.exp2(sink[h] - m[h]) for h in heads]
    outs = [_dot(p[h], v_band[h // GQA_REP]) * (1.0 / den[h]) for h in heads]
    o_ref[0] = jnp.concatenate(outs, axis=1).astype(BF16)


def _attn_prompt(qkv, sinks_flat, layer):
    b, t, _ = qkv.shape
    kcol = ATTN_WIDTH // KV_WIDTH
    prev = lambda i, m: (i, jnp.maximum(m - 1, 0), kcol)
    prev_v = lambda i, m: (i, jnp.maximum(m - 1, 0), kcol + 1)
    return pl.pallas_call(
        functools.partial(_attn_prompt_kernel, layer=layer),
        grid=(b, t // WINDOW),
        in_specs=[pl.BlockSpec(memory_space=pltpu.SMEM),
                  pl.BlockSpec((1, N_Q_HEADS, WINDOW, 2 * WINDOW), lambda i, m: (jnp.minimum(m, 1), 0, 0, 0)),
                  pl.BlockSpec((1, WINDOW, ATTN_WIDTH), lambda i, m: (i, m, 0)),
                  pl.BlockSpec((1, WINDOW, KV_WIDTH), lambda i, m: (i, m, kcol)),
                  pl.BlockSpec((1, WINDOW, KV_WIDTH), lambda i, m: (i, m, kcol + 1)),
                  pl.BlockSpec((1, WINDOW, KV_WIDTH), prev),
                  pl.BlockSpec((1, WINDOW, KV_WIDTH), prev_v)],
        out_specs=pl.BlockSpec((1, WINDOW, ATTN_WIDTH), lambda i, m: (i, m, 0)),
        out_shape=jax.ShapeDtypeStruct((b, t, ATTN_WIDTH), BF16),
        compiler_params=_params(2),
        name="attn_prompt",
    )(sinks_flat, jnp.asarray(_attn_bias_table()), qkv, qkv, qkv, qkv, qkv)


def _attn_sample_kernel(row_ref, kc_ref, vc_ref, sink_ref, slope_ref, spread_ref, gather_ref,
                        o_ref, nk_ref, nv_ref):
    seqs = range(row_ref.shape[0])
    head_row = lax.broadcasted_iota(jnp.int32, (N_Q_HEADS, ATTN_WIDTH), 0)
    head_lane = lax.broadcasted_iota(jnp.int32, (N_Q_HEADS, ATTN_WIDTH), 1) // HEAD_DIM
    own = head_row == head_lane
    grp_row = lax.broadcasted_iota(jnp.int32, (N_Q_HEADS, KV_WIDTH), 0) // GQA_REP
    grp_lane = lax.broadcasted_iota(jnp.int32, (N_Q_HEADS, KV_WIDTH), 1) // HEAD_DIM
    bias = slope_ref[...] * (WINDOW - lax.broadcasted_iota(jnp.int32, (N_Q_HEADS, WINDOW), 1)).astype(F32)
    sink = sink_ref[...]
    last = lax.broadcasted_iota(jnp.int32, (WINDOW, KV_WIDTH), 0) == WINDOW - 1
    rows = [row_ref[s] for s in seqs]
    k_new = [r[:, ATTN_WIDTH:ATTN_WIDTH + KV_WIDTH] for r in rows]
    v_new = [r[:, ATTN_WIDTH + KV_WIDTH:] for r in rows]
    kc = [kc_ref[s] for s in seqs]
    vc = [vc_ref[s] for s in seqs]
    q_rows = [jnp.where(own, jnp.broadcast_to(r[:, :ATTN_WIDTH] * (HEAD_DIM ** -0.5), (N_Q_HEADS, ATTN_WIDTH)), 0.0)
              .astype(BF16) for r in rows]
    q_grp = [_dot(x, spread_ref[...]) for x in q_rows]
    lg = [_dot_nt(q_grp[s].astype(BF16), kc[s].astype(BF16)) - bias for s in seqs]
    lg_new = [jnp.sum(q_grp[s] * k_new[s].astype(BF16).astype(F32), axis=-1, keepdims=True) for s in seqs]
    m = [jnp.maximum(jnp.maximum(jnp.max(lg[s], axis=-1, keepdims=True), lg_new[s]), sink) for s in seqs]
    p = [jnp.exp(lg[s] - m[s]) for s in seqs]
    p_new = [jnp.exp(lg_new[s] - m[s]) for s in seqs]
    den = [jnp.sum(p[s], axis=-1, keepdims=True) + p_new[s] + jnp.exp(sink - m[s]) for s in seqs]
    out = [_dot((p[s] / den[s]).astype(BF16), vc[s].astype(BF16))
           + (p_new[s] / den[s]).astype(BF16).astype(F32) * v_new[s].astype(BF16).astype(F32) for s in seqs]
    full = [_dot(jnp.where(grp_row == grp_lane, x, 0.0).astype(BF16), gather_ref[...]) for x in out]
    for s in seqs:
        o_ref[s] = jnp.sum(jnp.where(own, full[s], 0.0), axis=0, keepdims=True).astype(BF16)
        nk_ref[s] = jnp.where(last, k_new[s], pltpu.roll(kc[s], WINDOW - 1, axis=0))
        nv_ref[s] = jnp.where(last, v_new[s], pltpu.roll(vc[s], WINDOW - 1, axis=0))


def _attn_sample(qkv_rows, cache_k, cache_v, sinks_col, layer):
    s = qkv_rows.shape[0]
    spread = np.zeros((ATTN_WIDTH, KV_WIDTH), np.float32)
    for h in range(N_Q_HEADS):
        for d in range(HEAD_DIM):
            spread[h * HEAD_DIM + d, (h // GQA_REP) * HEAD_DIM + d] = 1.0
    slopes = np.asarray(ALIBI_SLOPES, np.float32).reshape(N_Q_HEADS, 1)
    per_step = SUBLANES
    win = (per_step, WINDOW, KV_WIDTH)
    return pl.pallas_call(
        _attn_sample_kernel,
        grid=(s // per_step,),
        in_specs=[pl.BlockSpec((per_step, 1, QKV_COLS), lambda i: (i, 0, 0)),
                  pl.BlockSpec((None,) + win, lambda i: (layer, i, 0, 0)),
                  pl.BlockSpec((None,) + win, lambda i: (layer, i, 0, 0)),
                  pl.BlockSpec((None, N_Q_HEADS, 1), lambda i: (layer, 0, 0)),
                  pl.BlockSpec((N_Q_HEADS, 1), lambda i: (0, 0)),
                  pl.BlockSpec((ATTN_WIDTH, KV_WIDTH), lambda i: (0, 0)),
                  pl.BlockSpec((KV_WIDTH, ATTN_WIDTH), lambda i: (0, 0))],
        out_specs=[pl.BlockSpec((per_step, 1, ATTN_WIDTH), lambda i: (i, 0, 0)),
                   pl.BlockSpec(win, lambda i: (i, 0, 0)),
                   pl.BlockSpec(win, lambda i: (i, 0, 0))],
        out_shape=[jax.ShapeDtypeStruct((s, 1, ATTN_WIDTH), BF16),
                   jax.ShapeDtypeStruct((s, WINDOW, KV_WIDTH), F32),
                   jax.ShapeDtypeStruct((s, WINDOW, KV_WIDTH), F32)],
        compiler_params=_params(1),
        name="attn_sample",
    )(qkv_rows, cache_k, cache_v, sinks_col, jnp.asarray(slopes),
      jnp.asarray(spread, BF16), jnp.asarray(spread.T, BF16))


def _wkv_kernel(p_ref, shift0_ref, s0_ref, mix_ref, w0_ref, dup_ref, a0_ref, iup_ref, gup_ref,
                kk_ref, ka_ref, rk_ref, lnw_ref, lnb_ref, rw_ref, sout_ref, s_scr, prev_scr,
                arb_scr, aak_scr, ar_scr, vbd_scr, vb_scr, bk_scr, total_scr, gate_scr, bonus_scr, ncat_scr, ecat_scr,
                *, chunk):
    step = pl.program_id(1)
    n_steps = pl.num_programs(1)
    staged = (arb_scr, aak_scr, ar_scr, vbd_scr, vb_scr, bk_scr, total_scr, gate_scr, bonus_scr)
    bd_rows = HEADS_PER_GROUP * chunk
    groups = range(N_GROUPS)
    cols = [slice(g * MXU_DIM, (g + 1) * MXU_DIM) for g in groups]
    iota = lambda shape, dim: lax.broadcasted_iota(jnp.int32, shape, dim)
    same_head = iota((MXU_DIM, MXU_DIM), 0) // HEAD_DIM == iota((MXU_DIM, MXU_DIM), 1) // HEAD_DIM
    ones_bd = jnp.where(same_head, 1.0, 0.0).astype(BF16)
    tile_sel = jnp.where(iota((HEAD_DIM, MXU_DIM), 0) == iota((HEAD_DIM, MXU_DIM), 1) % HEAD_DIM, 1.0, 0.0).astype(BF16)
    fold_sel = jnp.where(iota((MXU_DIM, HEAD_DIM), 0) % HEAD_DIM == iota((MXU_DIM, HEAD_DIM), 1), 1.0, 0.0).astype(BF16)

    @pl.when(step == 0)
    def _():
        prev_scr[...] = shift0_ref[0]
        for g in groups:
            rows = s0_ref[0, cols[g], :]
            tiled = sum(_dot(part, tile_sel) for part in _split(rows, 3))
            s_scr[g] = jnp.where(same_head, tiled, 0.0)
        for ref in staged + (ncat_scr, ecat_scr):
            ref[...] = jnp.zeros_like(ref)
        total_scr[...] = jnp.ones_like(total_scr)

    row = iota((chunk, 1), 0)

    def shifted(lo, width):
        cur = p_ref[0, :, QKV_COLS + lo:QKV_COLS + lo + width]
        prev = jnp.where(row == 0, prev_scr[:, lo:lo + width], pltpu.roll(cur, 1, axis=0))
        return cur + (prev - cur) * mix_ref[:, lo:lo + width]

    def seg_sum(x):
        return _dot(x.astype(BF16), ones_bd)

    head_mask = [jnp.where(iota((chunk, MXU_DIM), 1) // HEAD_DIM == h, 1.0, 0.0).astype(BF16)
                 for h in range(HEADS_PER_GROUP)]

    def head_blocks(x):
        return jnp.concatenate([x * m for m in head_mask], axis=0)

    slot = step % 2
    inv_n = 1.0 / HEAD_DIM
    t_idx = iota((chunk, 2 * bd_rows), 0)
    s_idx = iota((chunk, 2 * bd_rows), 1) % chunk
    strict = (s_idx < t_idx)[:, :bd_rows]
    same_block = jnp.where(iota((bd_rows, bd_rows), 0) // chunk == iota((bd_rows, bd_rows), 1) // chunk,
                           1.0, 0.0).astype(BF16)
    eye_cat = jnp.where(t_idx == s_idx, 1.0, 0.0)[:, :bd_rows].astype(BF16)
    block_diag = lambda x: jnp.concatenate([x] * HEADS_PER_GROUP, axis=0) * same_block


    st_ar, st_ak, st_r, st_vbd, st_v, st_bk, st_total, st_gate, st_bonus = (
        [ref[slot, g] for g in groups] for ref in staged)
    st_e = [ecat_scr[g] for g in groups]
    s_b = [s_scr[g].astype(BF16) for g in groups]
    uy = [_dot_nt(st_ar[g], s_b[g]) for g in groups]

    e_cat = [ncat_scr[g] for g in groups]
    m_b = [x.astype(BF16) for x in e_cat]
    m_b = [_dot(x, block_diag(x)).astype(BF16) for x in m_b]
    levels = int(math.log2(chunk)) - 1

    def inverse_level(lvl, e_cat, m_b):
        t_bd = [block_diag(x.astype(BF16) + eye_cat) for x in e_cat]
        if lvl < levels - 1:
            both = [_dot(m_b[g], jnp.concatenate([t_bd[g], block_diag(m_b[g])], axis=1)) for g in groups]
            return ([e_cat[g] + both[g][:, :bd_rows] for g in groups], [x[:, bd_rows:].astype(BF16) for x in both])
        return [e_cat[g] + _dot(m_b[g], t_bd[g]) for g in groups], None

    lora_in = shifted(LORA_OFF, LANES)
    lora_tanh = jnp.tanh(lora_in).astype(BF16)
    lora_lin = lora_in.astype(BF16)
    gate_sig = jax.nn.sigmoid(shifted(GATE_OFF, GATE_PAD)).astype(BF16)
    r = [shifted(g * MXU_DIM, MXU_DIM) for g in groups]

    rhs = [uy[g][:chunk] + _dot(st_ak[g], st_vbd[g]) for g in groups]
    e_cat, m_b = inverse_level(0, e_cat, m_b)

    k = [shifted(RWKV_WIDTH + g * MXU_DIM, MXU_DIM) for g in groups]
    v = [shifted(2 * RWKV_WIDTH + g * MXU_DIM, MXU_DIM) for g in groups]
    prev_scr[...] = p_ref[0, chunk - 1:chunk, QKV_COLS:]

    p = [rhs[g] + _dot(st_e[g], head_blocks(rhs[g].astype(BF16))) for g in groups]
    p_b = [x.astype(BF16) for x in p]
    if levels > 1:
        e_cat, m_b = inverse_level(1, e_cat, m_b)

    logw = [-math.exp(-0.5) * jax.nn.sigmoid(w0_ref[:, cols[g]] + _dot(lora_tanh, dup_ref[:, cols[g]])) for g in groups]
    a = [jax.nn.sigmoid(a0_ref[:, cols[g]] + _dot(lora_lin, iup_ref[:, cols[g]])) for g in groups]
    gate = [_dot(gate_sig, gup_ref[:, cols[g]]) for g in groups]

    y = [uy[g][chunk:] + _dot(st_r[g], jnp.concatenate([head_blocks(p_b[g]), st_vbd[g]], axis=0)) for g in groups]
    for g in groups:
        upd = _dot_tn(jnp.concatenate([p_b[g], st_v[g]], axis=0), st_bk[g])
        s_scr[g] = s_scr[g] * st_total[g] + jnp.where(same_head, upd, 0.0)
    if levels > 2:
        e_cat, m_b = inverse_level(2, e_cat, m_b)

    kk = [k[g] * kk_ref[:, cols[g]] for g in groups]
    kk = [kk[g] / jnp.maximum(jnp.sqrt(seg_sum(kk[g] * kk[g])), 1e-12) for g in groups]
    k = [k[g] * (1.0 + (a[g] - 1.0) * ka_ref[:, cols[g]]) for g in groups]

    dev = [y[g] - seg_sum(y[g]) * inv_n for g in groups]
    if levels > 3:
        e_cat, m_b = inverse_level(3, e_cat, m_b)

    tri = jnp.where(iota((chunk, chunk), 0) >= iota((chunk, chunk), 1), 1.0, 0.0).astype(BF16)
    cum = [sum(_dot(tri, part) for part in _split(logw[g], 3)) for g in groups]

    var = [seg_sum(x * x) * inv_n for x in dev]
    for lvl in range(4, levels):
        e_cat, m_b = inverse_level(lvl, e_cat, m_b)

    e_pos = [jnp.exp(x) for x in cum]
    e_neg = [jnp.exp(-x) for x in cum]
    a_t = [-kk[g] * jnp.exp(cum[g] - logw[g]) for g in groups]
    b_t = [kk[g] * a[g] * e_neg[g] for g in groups]

    for g in groups:
        yn = dev[g] * lax.rsqrt(var[g] + GN_EPS) * lnw_ref[:, cols[g]] + lnb_ref[:, cols[g]]
        rw_ref[0, :, cols[g]] = ((yn + st_bonus[g]) * st_gate[g]).astype(BF16)

    k_t = [k[g] * e_neg[g] for g in groups]
    r_t = [r[g] * e_pos[g] for g in groups]
    total = [x[chunk - 1:chunk, :] for x in e_pos]
    ar_b = [jnp.concatenate([a_t[g], r_t[g]], axis=0).astype(BF16) for g in groups]
    xb =[head_blocks(x.astype(BF16)) for x in b_t]
    xk = [head_blocks(x.astype(BF16)) for x in k_t]
    v_bd = [head_blocks(x.astype(BF16)) for x in v]

    scores = [_dot_nt(ar_b[g], jnp.concatenate([xb[g], xk[g]], axis=0)) for g in groups]
    bonus = [seg_sum(r[g] * k[g] * rk_ref[:, cols[g]]) * v[g] for g in groups]
    for g in groups:
        ecat_scr[g] = e_cat[g].astype(BF16)
        ncat_scr[g] = jnp.where(strict, scores[g][:chunk, :bd_rows], 0.0)
        arb_scr[slot, g] = ar_b[g]
        aak_scr[slot, g] = jnp.where(strict, scores[g][:chunk, bd_rows:], 0.0).astype(BF16)
        ar_scr[slot, g] = jnp.where(s_idx <= t_idx, scores[g][chunk:], 0.0).astype(BF16)
        vbd_scr[slot, g] = v_bd[g]
        vb_scr[slot, g] = v[g].astype(BF16)
        bk_scr[slot, g] = jnp.concatenate([b_t[g] * total[g], k_t[g] * total[g]], axis=0).astype(BF16)
        total_scr[slot, g], gate_scr[slot, g], bonus_scr[slot, g] = total[g], gate[g], bonus[g]

    @pl.when(step == n_steps - 1)
    def _():
        for g in groups:
            folded = sum(_dot(part, fold_sel) for part in _split(s_scr[g], 3))
            sout_ref[0, cols[g], :] = folded


def _wkv(proj, shift0, s0, wp, layer, *, chunk):
    b, t, n_proj = proj.shape
    n_chunks = t // chunk
    bd_rows = HEADS_PER_GROUP * chunk
    lay = lambda *blk: pl.BlockSpec((None,) + blk, lambda i, c: (layer,) + (0,) * len(blk))
    vec = lay(1, RWKV_WIDTH)
    grp = lambda rows, width, dtype: pltpu.VMEM((N_GROUPS, rows, width), dtype)
    two = lambda rows, width, dtype: pltpu.VMEM((2, N_GROUPS, rows, width), dtype)
    return pl.pallas_call(
        functools.partial(_wkv_kernel, chunk=chunk),
        grid=(b, n_chunks + 2),
        in_specs=[pl.BlockSpec((1, chunk, n_proj), lambda i, c: (i, jnp.minimum(c, n_chunks - 1), 0)),
                  pl.BlockSpec((1, 1, RWKV_PROJ_PAD), lambda i, c: (i, 0, 0)),
                  pl.BlockSpec((1, RWKV_WIDTH, HEAD_DIM), lambda i, c: (i, 0, 0)),
                  lay(1, RWKV_PROJ_PAD), vec, lay(LANES, RWKV_WIDTH), vec, lay(LANES, RWKV_WIDTH),
                  lay(GATE_PAD, RWKV_WIDTH), vec, vec, vec, vec, vec],
        out_specs=[pl.BlockSpec((1, chunk, RWKV_WIDTH), lambda i, c: (i, jnp.maximum(c - 2, 0), 0)),
                   pl.BlockSpec((1, RWKV_WIDTH, HEAD_DIM), lambda i, c: (i, 0, 0))],
        out_shape=[jax.ShapeDtypeStruct((b, t, RWKV_WIDTH), BF16),
                   jax.ShapeDtypeStruct((b, RWKV_WIDTH, HEAD_DIM), F32)],
        scratch_shapes=[pltpu.VMEM((N_GROUPS, MXU_DIM, MXU_DIM), F32),
                        pltpu.VMEM((1, RWKV_PROJ_PAD), F32),
                        two(2 * chunk, MXU_DIM, BF16), two(chunk, bd_rows, BF16), two(chunk, 2 * bd_rows, BF16),
                        two(bd_rows, MXU_DIM, BF16), two(chunk, MXU_DIM, BF16), two(2 * chunk, MXU_DIM, BF16),
                        two(1, MXU_DIM, F32), two(chunk, MXU_DIM, F32), two(chunk, MXU_DIM, F32),
                        grp(chunk, bd_rows, F32), grp(chunk, bd_rows, BF16)],
        compiler_params=_params(2),
        name="wkv7",
    )(proj, shift0, s0, wp["mix"], wp["w0"], wp["decay_up"], wp["a0"], wp["iclr_up"], wp["gate_up"],
      wp["k_k"], wp["k_a"], wp["r_k"], wp["ln_w"], wp["ln_b"])


SUBLANES = 8


def _wkv_step_kernel(p_ref, shift_ref, s_ref, mix_ref, w0_ref, dup_ref, a0_ref, iup_ref, gup_ref,
                     kk_ref, ka_ref, rk_ref, lnw_ref, lnb_ref, fold_ref, rw_ref, sout_ref):
    iota = lambda shape, dim: lax.broadcasted_iota(jnp.int32, shape, dim)
    groups = range(N_GROUPS)
    cols = [slice(g * MXU_DIM, (g + 1) * MXU_DIM) for g in groups]
    ones_bd = jnp.where(iota((MXU_DIM, MXU_DIM), 0) // HEAD_DIM == iota((MXU_DIM, MXU_DIM), 1) // HEAD_DIM,
                        1.0, 0.0).astype(BF16)

    def seg_sum(x):
        return jnp.concatenate([_dot(x[:, cols[g]].astype(BF16), ones_bd) for g in groups], axis=1)

    cur = jnp.broadcast_to(p_ref[0], (SUBLANES, RWKV_PROJ_PAD))
    prev = jnp.broadcast_to(shift_ref[0], (SUBLANES, RWKV_PROJ_PAD))
    p = cur + (prev - cur) * mix_ref[...]
    r = p[:, 0:RWKV_WIDTH]
    k = p[:, RWKV_WIDTH:2 * RWKV_WIDTH]
    v = p[:, 2 * RWKV_WIDTH:3 * RWKV_WIDTH]
    lora_in = p[:, LORA_OFF:LORA_OFF + LANES]
    gate_in = p[:, GATE_OFF:GATE_OFF + GATE_PAD]
    w = w0_ref[...] + _dot(jnp.tanh(lora_in).astype(BF16), dup_ref[...])
    decay = jnp.exp(-math.exp(-0.5) * jax.nn.sigmoid(w))
    a = jax.nn.sigmoid(a0_ref[...] + _dot(lora_in.astype(BF16), iup_ref[...]))
    gate = _dot(jax.nn.sigmoid(gate_in).astype(BF16), gup_ref[...])
    kk = k * kk_ref[...]
    kk = kk / jnp.maximum(jnp.sqrt(seg_sum(kk * kk)), 1e-12)
    k = k * (1.0 + (a - 1.0) * ka_ref[...])

    own = iota((N_RWKV_HEADS, RWKV_WIDTH), 0) == iota((N_RWKV_HEADS, RWKV_WIDTH), 1) // HEAD_DIM
    spread = lambda x: jnp.where(own, jnp.broadcast_to(x[0:1], (N_RWKV_HEADS, RWKV_WIDTH)), 0.0)
    stacked = jnp.concatenate([spread(decay), spread(-kk), spread(kk * a), spread(k), spread(r)], axis=0)
    per_head = sum(_dot(part, fold_ref[...]) for part in _split(stacked, 3))
    w_h, a_h, b_h, k_h, r_h = (per_head[i * N_RWKV_HEADS:(i + 1) * N_RWKV_HEADS] for i in range(5))
    rows = lambda x: jnp.concatenate(
        [jnp.broadcast_to(x[h:h + 1, :], (HEAD_DIM, HEAD_DIM)) for h in range(N_RWKV_HEADS)], axis=0)

    eye = iota((LANES, LANES), 0) == iota((LANES, LANES), 1)
    ones_cols = jnp.ones((LANES, HEAD_DIM), BF16)
    v_col = jnp.concatenate(
        [sum(_dot(part, ones_cols) for part in
             _split(jnp.where(eye, jnp.broadcast_to(v[0:1, t * LANES:(t + 1) * LANES], (LANES, LANES)), 0.0), 3))
         for t in range(RWKV_WIDTH // LANES)], axis=0)

    s = s_ref[0]
    sa = jnp.sum(s * rows(a_h), axis=-1, keepdims=True)
    s_new = s * rows(w_h) + sa * rows(b_h) + v_col * rows(k_h)
    sout_ref[0] = s_new
    y_heads = _dot_nt(r_h.astype(BF16), s_new.astype(BF16))
    y = jnp.broadcast_to(jnp.sum(jnp.where(own, y_heads, 0.0), axis=0, keepdims=True), (SUBLANES, RWKV_WIDTH))

    inv_n = 1.0 / HEAD_DIM
    dev = y - seg_sum(y) * inv_n
    var = seg_sum(dev * dev) * inv_n
    y = dev * lax.rsqrt(var + GN_EPS) * lnw_ref[...] + lnb_ref[...]
    bonus = seg_sum(r * k * rk_ref[...]) * v
    rw_ref[0] = ((y + bonus) * gate)[0:1].astype(BF16)


def _wkv_step(prw_rows, shift_rows, state, wp, layer):
    s = prw_rows.shape[0]
    lay = lambda *blk: pl.BlockSpec((None,) + blk, lambda i: (layer,) + (0,) * len(blk))
    vec = lay(1, RWKV_WIDTH)
    fold = np.tile(np.eye(HEAD_DIM, dtype=np.float32), (N_RWKV_HEADS, 1))
    return pl.pallas_call(
        _wkv_step_kernel,
        grid=(s,),
        in_specs=[pl.BlockSpec((1, 1, RWKV_PROJ_PAD), lambda i: (i, 0, 0)),
                  pl.BlockSpec((None, 1, 1, RWKV_PROJ_PAD), lambda i: (layer, i, 0, 0)),
                  pl.BlockSpec((None, 1, RWKV_WIDTH, HEAD_DIM), lambda i: (layer, i, 0, 0)),
                  lay(1, RWKV_PROJ_PAD), vec, lay(LANES, RWKV_WIDTH), vec, lay(LANES, RWKV_WIDTH),
                  lay(GATE_PAD, RWKV_WIDTH), vec, vec, vec, vec, vec,
                  pl.BlockSpec((RWKV_WIDTH, HEAD_DIM), lambda i: (0, 0))],
        out_specs=[pl.BlockSpec((1, 1, RWKV_WIDTH), lambda i: (i, 0, 0)),
                   pl.BlockSpec((1, RWKV_WIDTH, HEAD_DIM), lambda i: (i, 0, 0))],
        out_shape=[jax.ShapeDtypeStruct((s, 1, RWKV_WIDTH), BF16),
                   jax.ShapeDtypeStruct((s, RWKV_WIDTH, HEAD_DIM), F32)],
        compiler_params=_params(1),
        name="wkv7_step",
    )(prw_rows, shift_rows, state, wp["mix"], wp["w0"], wp["decay_up"], wp["a0"], wp["iclr_up"], wp["gate_up"],
      wp["k_k"], wp["k_a"], wp["r_k"], wp["ln_w"], wp["ln_b"], jnp.asarray(fold, BF16))


def _out_proj_kernel(att_ref, rw_ref, x_ref, gt_ref, sh_ref, sc_ref, g_ref, w_ref, x1_ref, h_ref, *, n_sample,
                     is_sample):
    rows = functools.partial(_mod_rows, n_sample=n_sample, is_sample=is_sample)
    y = _dot(att_ref[0], w_ref[0:ATTN_WIDTH, :]) + _dot(rw_ref[0], w_ref[ATTN_WIDTH:, :])
    x1 = x_ref[0] + rows(gt_ref) * y
    x1_ref[0] = x1
    h_ref[0] = _rmsnorm_mod(x1, g_ref[...], rows(sh_ref), rows(sc_ref)).astype(BF16)


def _out_proj(att, rw, x, mod, g, w, layer, *, tm, n_sample, is_sample):
    b, t, d = x.shape
    row = lambda width: pl.BlockSpec((1, tm, width), lambda i, m: (i, m, 0))
    return pl.pallas_call(
        functools.partial(_out_proj_kernel, n_sample=n_sample, is_sample=is_sample),
        grid=(b, t // tm),
        in_specs=[row(ATTN_WIDTH), row(RWKV_WIDTH), row(d),
                  _mod_spec(mod, layer, MOD_GATE1), _mod_spec(mod, layer, MOD_SHIFT2), _mod_spec(mod, layer, MOD_SCALE2),
                  pl.BlockSpec((None, 1, d), lambda i, m: (layer, 0, 0)),
                  pl.BlockSpec((None, d, d), lambda i, m: (layer, 0, 0))],
        out_specs=[row(d), row(d)],
        out_shape=[jax.ShapeDtypeStruct((b, t, d), F32), jax.ShapeDtypeStruct((b, t, d), BF16)],
        compiler_params=_params(2),
        name="out_proj",
    )(att, rw, x, mod, mod, mod, g, w)


def _ffn_kernel(h_ref, x_ref, gt_ref, wg_ref, wu_ref, wo_ref, gf_ref, o_ref, acc_ref, *, final_norm, n_sample,
                is_sample):
    j = pl.program_id(2)

    @pl.when(j == 0)
    def _():
        acc_ref[...] = jnp.zeros_like(acc_ref)

    h = h_ref[0]
    gate = _dot(h, wg_ref[...])
    up = _dot(h, wu_ref[...])
    act = (gate * jax.nn.sigmoid(gate) * up).astype(BF16)
    acc_ref[...] += _dot(act, wo_ref[...])

    @pl.when(j == pl.num_programs(2) - 1)
    def _():
        x2 = x_ref[0] + _mod_rows(gt_ref, n_sample, is_sample) * acc_ref[...]
        if final_norm:
            ms = jnp.mean(x2 * x2, axis=-1, keepdims=True)
            x2 = x2 * lax.rsqrt(ms + RMS_EPS) * gf_ref[...]
        o_ref[0] = x2


def _ffn(h, x, mod, w_in, w_out, g_final, layer, *, tm, th, final_norm, n_sample, is_sample):
    b, t, d = x.shape
    n_h = FFN_HIDDEN // th
    return pl.pallas_call(
        functools.partial(_ffn_kernel, final_norm=final_norm, n_sample=n_sample, is_sample=is_sample),
        grid=(b, t // tm, n_h),
        in_specs=[pl.BlockSpec((1, tm, d), lambda i, m, j: (i, m, 0)),
                  pl.BlockSpec((1, tm, d), lambda i, m, j: (i, m, 0)),
                  _mod_spec(mod, layer, MOD_GATE2),
                  pl.BlockSpec((None, d, th), lambda i, m, j: (layer, 0, j)),
                  pl.BlockSpec((None, d, th), lambda i, m, j: (layer, 0, j + n_h)),
                  pl.BlockSpec((None, th, d), lambda i, m, j: (layer, j, 0)),
                  pl.BlockSpec((1, d), lambda i, m, j: (0, 0))],
        out_specs=pl.BlockSpec((1, tm, d), lambda i, m, j: (i, m, 0)),
        out_shape=jax.ShapeDtypeStruct((b, t, d), F32),
        scratch_shapes=[pltpu.VMEM((tm, d), F32)],
        compiler_params=_params(3),
        name="ffn",
    )(h, x, mod, w_in, w_in, w_out, g_final)


def _tiles(t_prompt, n_sample):
    tiles = dict(proj_tm=1024, proj_tn=1024, out_tm=512, ffn_tm=512, ffn_th=512, wkv_chunk=64)
    assert t_prompt % tiles["proj_tm"] == 0 and t_prompt % tiles["out_tm"] == 0 and t_prompt % tiles["ffn_tm"] == 0
    assert t_prompt % tiles["wkv_chunk"] == 0 and t_prompt % WINDOW == 0
    assert (QKV_COLS + RWKV_PROJ_PAD) % tiles["proj_tn"] == 0
    assert FFN_HIDDEN % tiles["ffn_th"] == 0 and n_sample % SUBLANES == 0
    return tiles


def _pad_cols(x, width):
    return jnp.pad(x, [(0, 0)] * (x.ndim - 1) + [(0, width - x.shape[-1])])


def _pad_rows(x, rows, offset=0):
    return jnp.pad(x, [(0, 0), (offset, rows - offset - x.shape[1]), (0, 0)])


def kernel(x_prompt, x_sample, cache_k, cache_v, state_wkv, state_shift, c_prompt, c_sample, w_ada, b_ada, g_norm_mix, g_norm_ffn, w_in, w_out, attn_sinks, mix_shift, decay_w0, decay_up, iclr_a0, iclr_up, gate_up, k_k, k_a, r_k, ln_x_w, ln_x_b, w_ffn_in, w_ffn_out, g_norm_final):
    n_layers = w_in.shape[0]
    bp, tp, d = x_prompt.shape
    bs = x_sample.shape[0]
    tiles = _tiles(tp, bs)

    w_in_t = jnp.pad(jnp.swapaxes(w_in, 1, 2).astype(BF16),
                     [(0, 0), (0, QKV_COLS + RWKV_PROJ_PAD - w_in.shape[2]), (0, 0)])
    w_out_b = w_out.astype(BF16)
    w_ffn_in_b = w_ffn_in.astype(BF16)
    w_ffn_out_b = w_ffn_out.astype(BF16)
    vec = lambda x: x.reshape(n_layers, 1, -1)
    wp = dict(
        mix=vec(_pad_cols(mix_shift, RWKV_PROJ_PAD)), w0=vec(decay_w0), a0=vec(iclr_a0),
        decay_up=_pad_rows(decay_up, LANES).astype(BF16),
        iclr_up=_pad_rows(iclr_up, LANES, D_DECAY_LORA).astype(BF16),
        gate_up=_pad_rows(gate_up, GATE_PAD).astype(BF16),
        k_k=vec(k_k), k_a=vec(k_a), r_k=vec(r_k), ln_w=vec(ln_x_w), ln_b=vec(ln_x_b))
    g_mix, g_ffn = vec(g_norm_mix), vec(g_norm_ffn)
    g_final = g_norm_final.reshape(1, d)
    sinks_flat = attn_sinks.reshape(-1)
    sinks_col = attn_sinks.reshape(n_layers, N_Q_HEADS, 1)
    cache_k2 = cache_k.reshape(n_layers, bs, WINDOW, KV_WIDTH)
    cache_v2 = cache_v.reshape(n_layers, bs, WINDOW, KV_WIDTH)
    shift_rows = _pad_cols(state_shift, RWKV_PROJ_PAD).reshape(n_layers, bs, 1, RWKV_PROJ_PAD)
    state_rows = state_wkv.reshape(n_layers, bs, RWKV_WIDTH, HEAD_DIM)
    zero_shift = jnp.zeros((bp, 1, RWKV_PROJ_PAD), F32)
    zero_state = jnp.zeros((bp, RWKV_WIDTH, HEAD_DIM), F32)

    mod = _ada(jnp.concatenate([c_sample, c_prompt], axis=0), w_ada, b_ada)
    prompt = dict(n_sample=bs, is_sample=False)
    sample = dict(n_sample=bs, is_sample=True)

    xp = x_prompt
    xs = x_sample.reshape(1, bs, d)
    outs_p = [[] for _ in range(4)]
    outs_s = [[] for _ in range(4)]
    for layer in range(n_layers):
        last = layer == n_layers - 1

        proj = _norm_proj(xp, mod, g_mix, w_in_t, layer, tm=tiles["proj_tm"], tn=tiles["proj_tn"], **prompt)
        att = _attn_prompt(proj, sinks_flat, layer)
        rw, state = _wkv(proj, zero_shift, zero_state, wp, layer, chunk=tiles["wkv_chunk"])
        x1, h2 = _out_proj(att, rw, xp, mod, g_ffn, w_out_b, layer, tm=tiles["out_tm"], **prompt)
        xp = _ffn(h2, x1, mod, w_ffn_in_b, w_ffn_out_b, g_final, layer, tm=tiles["ffn_tm"], th=tiles["ffn_th"],
                  final_norm=last, **prompt)
        new_kv = proj[:, tp - WINDOW:, ATTN_WIDTH:QKV_COLS]
        outs_p[0].append(new_kv[:, :, :KV_WIDTH].reshape(bp, WINDOW, N_KV_HEADS, HEAD_DIM))
        outs_p[1].append(new_kv[:, :, KV_WIDTH:].reshape(bp, WINDOW, N_KV_HEADS, HEAD_DIM))
        outs_p[2].append(state.reshape(bp, N_RWKV_HEADS, HEAD_DIM, HEAD_DIM))
        outs_p[3].append(proj[:, tp - 1, QKV_COLS:QKV_COLS + RWKV_PROJ])

        proj_s = _norm_proj(xs, mod, g_mix, w_in_t, layer, tm=bs, tn=tiles["proj_tn"], **sample)[0]
        att_s, nk, nv = _attn_sample(proj_s[:, :QKV_COLS].reshape(bs, 1, QKV_COLS), cache_k2, cache_v2, sinks_col, layer)
        prw_tok = proj_s[:, QKV_COLS:].reshape(bs, 1, RWKV_PROJ_PAD)
        rw_s, state_s = _wkv_step(prw_tok, shift_rows, state_rows, wp, layer)
        x1_s, h2_s = _out_proj(att_s.reshape(1, bs, ATTN_WIDTH), rw_s.reshape(1, bs, RWKV_WIDTH), xs,
                               mod, g_ffn, w_out_b, layer, tm=bs, **sample)
        xs = _ffn(h2_s, x1_s, mod, w_ffn_in_b, w_ffn_out_b, g_final, layer, tm=bs, th=tiles["ffn_th"],
                  final_norm=last, **sample)
        outs_s[0].append(nk.reshape(bs, WINDOW, N_KV_HEADS, HEAD_DIM))
        outs_s[1].append(nv.reshape(bs, WINDOW, N_KV_HEADS, HEAD_DIM))
        outs_s[2].append(state_s.reshape(bs, N_RWKV_HEADS, HEAD_DIM, HEAD_DIM))
        outs_s[3].append(prw_tok[:, 0, :RWKV_PROJ])

    stack = lambda xs_: jnp.stack(xs_, axis=0)
    return (xp, xs.reshape(bs, 1, d),
            stack(outs_p[0]), stack(outs_p[1]), stack(outs_p[2]), stack(outs_p[3]),
            stack(outs_s[0]), stack(outs_s[1]), stack(outs_s[2]), stack(outs_s[3]))
```

```python
import functools
import math

import numpy as np
import jax
import jax.numpy as jnp
from jax import lax
from jax.experimental import pallas as pl
from jax.experimental.pallas import tpu as pltpu

F32 = jnp.float32
BF16 = jnp.bfloat16

D_MODEL = 2048
HEAD_DIM = 64
ATTN_WIDTH = D_MODEL // 2
RWKV_WIDTH = D_MODEL - ATTN_WIDTH
N_Q_HEADS = ATTN_WIDTH // HEAD_DIM
N_KV_HEADS = N_Q_HEADS // 4
GQA_REP = N_Q_HEADS // N_KV_HEADS
KV_WIDTH = N_KV_HEADS * HEAD_DIM
N_RWKV_HEADS = RWKV_WIDTH // HEAD_DIM
WINDOW = 128
D_DECAY_LORA = 64
D_ICLR_LORA = 64
D_GATE_LORA = 160
RWKV_PROJ = 3 * RWKV_WIDTH + D_DECAY_LORA + D_ICLR_LORA + D_GATE_LORA
QKV_COLS = ATTN_WIDTH + 2 * KV_WIDTH
FFN_HIDDEN = 5632
RMS_EPS = 1e-5
GN_EPS = 64e-5
NEG_INF = -1e30

LANES = 128
MXU_DIM = 256
VMEM_LIMIT_BYTES = 56 * 1024 * 1024

LORA_OFF = 3 * RWKV_WIDTH
GATE_OFF = LORA_OFF + D_DECAY_LORA + D_ICLR_LORA
GATE_PAD = 2 * LANES
RWKV_PROJ_PAD = 3584
HEADS_PER_GROUP = MXU_DIM // HEAD_DIM
N_GROUPS = N_RWKV_HEADS // HEADS_PER_GROUP

ALIBI_SLOPES = tuple(float(s) for s in np.exp2(-8.0 * np.arange(1, N_Q_HEADS + 1, dtype=np.float32) / N_Q_HEADS))


def _params(n_axes):
    return pltpu.CompilerParams(dimension_semantics=("arbitrary",) * n_axes,
                                vmem_limit_bytes=VMEM_LIMIT_BYTES)


def _dot(a, b):
    return jnp.dot(a, b, preferred_element_type=F32)


def _dot_nt(a, b):
    return lax.dot_general(a, b, (((1,), (1,)), ((), ())), preferred_element_type=F32)


def _dot_tn(a, b):
    return lax.dot_general(a, b, (((0,), (0,)), ((), ())), preferred_element_type=F32)


def _split(x, parts):
    out = []
    for _ in range(parts):
        p = x.astype(BF16)
        out.append(p)
        x = x - p.astype(F32)
    return out


def _rmsnorm_mod(x, g, shift, scale):
    ms = jnp.mean(x * x, axis=-1, keepdims=True)
    y = x * lax.rsqrt(ms + RMS_EPS) * g
    return y * (1.0 + scale) + shift


def _ada_kernel(c_ref, w_ref, b_ref, o_ref):
    c = c_ref[...]
    s = (c * jax.nn.sigmoid(c)).astype(BF16)
    o_ref[...] = _dot(s, w_ref[...].astype(BF16)) + b_ref[...]


def _ada(c_all, w_ada, b_ada):
    n_layers, d, n = w_ada.shape
    rows = c_all.shape[0]
    tn = 1024
    return pl.pallas_call(
        _ada_kernel,
        grid=(n_layers, n // tn),
        in_specs=[pl.BlockSpec((rows, d), lambda l, j: (0, 0)),
                  pl.BlockSpec((None, d, tn), lambda l, j: (l, 0, j)),
                  pl.BlockSpec((None, 1, tn), lambda l, j: (l, 0, j))],
        out_specs=pl.BlockSpec((None, rows, tn), lambda l, j: (l, 0, j)),
        out_shape=jax.ShapeDtypeStruct((n_layers, rows, n), F32),
        compiler_params=_params(2),
        name="ada_mod",
    )(c_all, w_ada, b_ada.reshape(n_layers, 1, n))


NORM_STRIP = 32


MOD_SHIFT1, MOD_SCALE1, MOD_GATE1, MOD_SHIFT2, MOD_SCALE2, MOD_GATE2 = range(6)


def _mod_spec(mod, layer, part):
    rows, d = mod.shape[1], mod.shape[2] // 6
    return pl.BlockSpec((None, rows, d), lambda *_: (layer, 0, part))


def _mod_rows(ref, n_sample, is_sample):
    if is_sample:
        return ref[0:n_sample, :]
    return ref[pl.ds(n_sample + pl.program_id(0), 1), :]


def _norm_proj_kernel(x_ref, sh_ref, sc_ref, g_ref, w_ref, o_ref, h_scr, *, n_sample, is_sample):
    j = pl.program_id(2)
    tm = h_scr.shape[0]

    @pl.when(j == 0)
    def _():
        gain = g_ref[...] * (1.0 + _mod_rows(sc_ref, n_sample, is_sample))
        shift = _mod_rows(sh_ref, n_sample, is_sample)
        if gain.shape[0] == 1:
            def strip(i, carry):
                rows = pl.ds(pl.multiple_of(i * NORM_STRIP, NORM_STRIP), NORM_STRIP)
                x = x_ref[0, rows, :]
                ms = jnp.mean(x * x, axis=-1, keepdims=True)
                h_scr[rows, :] = (x * lax.rsqrt(ms + RMS_EPS) * gain + shift).astype(BF16)
                return carry
            lax.fori_loop(0, tm // NORM_STRIP, strip, 0, unroll=4)
        else:
            x = x_ref[0]
            ms = jnp.mean(x * x, axis=-1, keepdims=True)
            h_scr[...] = (x * lax.rsqrt(ms + RMS_EPS) * gain + shift).astype(BF16)

    o_ref[0] = _dot_nt(h_scr[...], w_ref[...])


def _norm_proj(x, mod, g, w, layer, *, tm, tn, n_sample, is_sample):
    b, t, d = x.shape
    n = QKV_COLS + RWKV_PROJ_PAD
    return pl.pallas_call(
        functools.partial(_norm_proj_kernel, n_sample=n_sample, is_sample=is_sample),
        grid=(b, t // tm, n // tn),
        in_specs=[pl.BlockSpec((1, tm, d), lambda i, m, j: (i, m, 0)),
                  _mod_spec(mod, layer, MOD_SHIFT1), _mod_spec(mod, layer, MOD_SCALE1),
                  pl.BlockSpec((None, 1, d), lambda i, m, j: (layer, 0, 0)),
                  pl.BlockSpec((None, tn, d), lambda i, m, j: (layer, j, 0))],
        out_specs=pl.BlockSpec((1, tm, tn), lambda i, m, j: (i, m, j)),
        out_shape=jax.ShapeDtypeStruct((b, t, n), F32),
        scratch_shapes=[pltpu.VMEM((tm, d), BF16)],
        compiler_params=_params(3),
        name="norm_proj",
    )(x, mod, mod, g, w)


LOG2E = math.log2(math.e)


def _attn_bias_table():
    t = np.arange(WINDOW)[:, None]
    j = np.arange(2 * WINDOW)[None, :]
    dist = t + WINDOW - j
    valid = (dist >= 0) & (dist <= WINDOW)
    slopes = np.asarray(ALIBI_SLOPES, np.float32)[:, None, None]
    bias = np.where(valid[None], -slopes * dist[None].astype(np.float32) * np.float32(LOG2E), np.float32(NEG_INF))
    first = np.where((j >= WINDOW)[None], bias, np.float32(NEG_INF))
    return np.stack([first, bias]).astype(np.float32)


def _attn_prompt_kernel(sink_ref, bias_ref, q_ref, kc_ref, vc_ref, kp_ref, vp_ref, o_ref, *, layer):
    q = q_ref[0] * (HEAD_DIM ** -0.5 * LOG2E)
    kc, vc, kp, vp = kc_ref[0], vc_ref[0], kp_ref[0], vp_ref[0]
    kv_heads, heads = range(N_KV_HEADS), range(N_Q_HEADS)
    ksl = [slice(g * HEAD_DIM, (g + 1) * HEAD_DIM) for g in kv_heads]
    k_band = [jnp.concatenate([kp[:, ksl[g]], kc[:, ksl[g]]], axis=0).astype(BF16) for g in kv_heads]
    v_band = [jnp.concatenate([vp[:, ksl[g]], vc[:, ksl[g]]], axis=0).astype(BF16) for g in kv_heads]
    lg = [_dot_nt(q[:, h * HEAD_DIM:(h + 1) * HEAD_DIM].astype(BF16), k_band[h // GQA_REP]) + bias_ref[0, h]
          for h in heads]
    sink = [sink_ref[layer * N_Q_HEADS + h] * LOG2E for h in heads]
    m = [jnp.maximum(jnp.max(lg[h], axis=-1, keepdims=True), sink[h]) for h in heads]
    p = [jnp.exp2(lg[h] - m[h]).astype(BF16) for h in heads]
    ones = jnp.ones((2 * WINDOW, HEAD_DIM), BF16)
    den = [_dot(p[h], ones) + jnp.exp2(sink[h] - m[h]) for h in heads]
    outs = [_dot(p[h], v_band[h // GQA_REP]) * (1.0 / den[h]) for h in heads]
    o_ref[0] = jnp.concatenate(outs, axis=1).astype(BF16)


def _attn_prompt(qkv, sinks_flat, layer):
    b, t, _ = qkv.shape
    kcol = ATTN_WIDTH // KV_WIDTH
    prev = lambda i, m: (i, jnp.maximum(m - 1, 0), kcol)
    prev_v = lambda i, m: (i, jnp.maximum(m - 1, 0), kcol + 1)
    return pl.pallas_call(
        functools.partial(_attn_prompt_kernel, layer=layer),
        grid=(b, t // WINDOW),
        in_specs=[pl.BlockSpec(memory_space=pltpu.SMEM),
                  pl.BlockSpec((1, N_Q_HEADS, WINDOW, 2 * WINDOW), lambda i, m: (jnp.minimum(m, 1), 0, 0, 0)),
                  pl.BlockSpec((1, WINDOW, ATTN_WIDTH), lambda i, m: (i, m, 0)),
                  pl.BlockSpec((1, WINDOW, KV_WIDTH), lambda i, m: (i, m, kcol)),
                  pl.BlockSpec((1, WINDOW, KV_WIDTH), lambda i, m: (i, m, kcol + 1)),
                  pl.BlockSpec((1, WINDOW, KV_WIDTH), prev),
                  pl.BlockSpec((1, WINDOW, KV_WIDTH), prev_v)],
        out_specs=pl.BlockSpec((1, WINDOW, ATTN_WIDTH), lambda i, m: (i, m, 0)),
        out_shape=jax.ShapeDtypeStruct((b, t, ATTN_WIDTH), BF16),
        compiler_params=_params(2),
        name="attn_prompt",
    )(sinks_flat, jnp.asarray(_attn_bias_table()), qkv, qkv, qkv, qkv, qkv)


def _attn_sample_kernel(row_ref, kc_ref, vc_ref, sink_ref, slope_ref, spread_ref, gather_ref,
                        o_ref, nk_ref, nv_ref):
    seqs = range(row_ref.shape[0])
    head_row = lax.broadcasted_iota(jnp.int32, (N_Q_HEADS, ATTN_WIDTH), 0)
    head_lane = lax.broadcasted_iota(jnp.int32, (N_Q_HEADS, ATTN_WIDTH), 1) // HEAD_DIM
    own = head_row == head_lane
    grp_row = lax.broadcasted_iota(jnp.int32, (N_Q_HEADS, KV_WIDTH), 0) // GQA_REP
    grp_lane = lax.broadcasted_iota(jnp.int32, (N_Q_HEADS, KV_WIDTH), 1) // HEAD_DIM
    bias = slope_ref[...] * (WINDOW - lax.broadcasted_iota(jnp.int32, (N_Q_HEADS, WINDOW), 1)).astype(F32)
    sink = sink_ref[...]
    last = lax.broadcasted_iota(jnp.int32, (WINDOW, KV_WIDTH), 0) == WINDOW - 1
    rows = [row_ref[s] for s in seqs]
    k_new = [r[:, ATTN_WIDTH:ATTN_WIDTH + KV_WIDTH] for r in rows]
    v_new = [r[:, ATTN_WIDTH + KV_WIDTH:] for r in rows]
    kc = [kc_ref[s] for s in seqs]
    vc = [vc_ref[s] for s in seqs]
    q_rows = [jnp.where(own, jnp.broadcast_to(r[:, :ATTN_WIDTH] * (HEAD_DIM ** -0.5), (N_Q_HEADS, ATTN_WIDTH)), 0.0)
              .astype(BF16) for r in rows]
    q_grp = [_dot(x, spread_ref[...]) for x in q_rows]
    lg = [_dot_nt(q_grp[s].astype(BF16), kc[s].astype(BF16)) - bias for s in seqs]
    lg_new = [jnp.sum(q_grp[s] * k_new[s].astype(BF16).astype(F32), axis=-1, keepdims=True) for s in seqs]
    m = [jnp.maximum(jnp.maximum(jnp.max(lg[s], axis=-1, keepdims=True), lg_new[s]), sink) for s in seqs]
    p = [jnp.exp(lg[s] - m[s]) for s in seqs]
    p_new = [jnp.exp(lg_new[s] - m[s]) for s in seqs]
    den = [jnp.sum(p[s], axis=-1, keepdims=True) + p_new[s] + jnp.exp(sink - m[s]) for s in seqs]
    out = [_dot((p[s] / den[s]).astype(BF16), vc[s].astype(BF16))
           + (p_new[s] / den[s]).astype(BF16).astype(F32) * v_new[s].astype(BF16).astype(F32) for s in seqs]
    full = [_dot(jnp.where(grp_row == grp_lane, x, 0.0).astype(BF16), gather_ref[...]) for x in out]
    for s in seqs:
        o_ref[s] = jnp.sum(jnp.where(own, full[s], 0.0), axis=0, keepdims=True).astype(BF16)
        nk_ref[s] = jnp.where(last, k_new[s], pltpu.roll(kc[s], WINDOW - 1, axis=0))
        nv_ref[s] = jnp.where(last, v_new[s], pltpu.roll(vc[s], WINDOW - 1, axis=0))


def _attn_sample(qkv_rows, cache_k, cache_v, sinks_col, layer):
    s = qkv_rows.shape[0]
    spread = np.zeros((ATTN_WIDTH, KV_WIDTH), np.float32)
    for h in range(N_Q_HEADS):
        for d in range(HEAD_DIM):
            spread[h * HEAD_DIM + d, (h // GQA_REP) * HEAD_DIM + d] = 1.0
    slopes = np.asarray(ALIBI_SLOPES, np.float32).reshape(N_Q_HEADS, 1)
    per_step = SUBLANES
    win = (per_step, WINDOW, KV_WIDTH)
    return pl.pallas_call(
        _attn_sample_kernel,
        grid=(s // per_step,),
        in_specs=[pl.BlockSpec((per_step, 1, QKV_COLS), lambda i: (i, 0, 0)),
                  pl.BlockSpec((None,) + win, lambda i: (layer, i, 0, 0)),
                  pl.BlockSpec((None,) + win, lambda i: (layer, i, 0, 0)),
                  pl.BlockSpec((None, N_Q_HEADS, 1), lambda i: (layer, 0, 0)),
                  pl.BlockSpec((N_Q_HEADS, 1), lambda i: (0, 0)),
                  pl.BlockSpec((ATTN_WIDTH, KV_WIDTH), lambda i: (0, 0)),
                  pl.BlockSpec((KV_WIDTH, ATTN_WIDTH), lambda i: (0, 0))],
        out_specs=[pl.BlockSpec((per_step, 1, ATTN_WIDTH), lambda i: (i, 0, 0)),
                   pl.BlockSpec(win, lambda i: (i, 0, 0)),
                   pl.BlockSpec(win, lambda i: (i, 0, 0))],
        out_shape=[jax.ShapeDtypeStruct((s, 1, ATTN_WIDTH), BF16),
                   jax.ShapeDtypeStruct((s, WINDOW, KV_WIDTH), F32),
                   jax.ShapeDtypeStruct((s, WINDOW, KV_WIDTH), F32)],
        compiler_params=_params(1),
        name="attn_sample",
    )(qkv_rows, cache_k, cache_v, sinks_col, jnp.asarray(slopes),
      jnp.asarray(spread, BF16), jnp.asarray(spread.T, BF16))


def _wkv_kernel(p_ref, shift0_ref, s0_ref, mix_ref, w0_ref, dup_ref, a0_ref, iup_ref, gup_ref,
                kk_ref, ka_ref, rk_ref, lnw_ref, lnb_ref, rw_ref, sout_ref, s_scr, prev_scr,
                arb_scr, aak_scr, ar_scr, vbd_scr, vb_scr, bk_scr, total_scr, gate_scr, bonus_scr, ncat_scr, ecat_scr,
                *, chunk, n_chunks):
    step = pl.program_id(0)
    staged = (arb_scr, aak_scr, ar_scr, vbd_scr, vb_scr, bk_scr, total_scr, gate_scr, bonus_scr)
    bd_rows = HEADS_PER_GROUP * chunk
    groups = range(N_GROUPS)
    cols = [slice(g * MXU_DIM, (g + 1) * MXU_DIM) for g in groups]
    iota = lambda shape, dim: lax.broadcasted_iota(jnp.int32, shape, dim)
    same_head = iota((MXU_DIM, MXU_DIM), 0) // HEAD_DIM == iota((MXU_DIM, MXU_DIM), 1) // HEAD_DIM
    ones_bd = jnp.where(same_head, 1.0, 0.0).astype(BF16)
    tile_sel = jnp.where(iota((HEAD_DIM, MXU_DIM), 0) == iota((HEAD_DIM, MXU_DIM), 1) % HEAD_DIM, 1.0, 0.0).astype(BF16)
    fold_sel = jnp.where(iota((MXU_DIM, HEAD_DIM), 0) % HEAD_DIM == iota((MXU_DIM, HEAD_DIM), 1), 1.0, 0.0).astype(BF16)

    first_prep = step % n_chunks == 0
    first_apply = (step >= 2) & ((step - 2) % n_chunks == 0)
    last_apply = (step >= 2) & ((step - 2) % n_chunks == n_chunks - 1)

    @pl.when(step == 0)
    def _():
        for ref in staged + (ncat_scr, ecat_scr, s_scr):
            ref[...] = jnp.zeros_like(ref)
        total_scr[...] = jnp.ones_like(total_scr)

    @pl.when(first_prep)
    def _():
        prev_scr[...] = shift0_ref[0]

    @pl.when(first_apply)
    def _():
        for g in groups:
            rows = s0_ref[0, cols[g], :]
            tiled = sum(_dot(part, tile_sel) for part in _split(rows, 3))
            s_scr[g] = jnp.where(same_head, tiled, 0.0)

    row = iota((chunk, 1), 0)

    def shifted(lo, width):
        cur = p_ref[0, :, QKV_COLS + lo:QKV_COLS + lo + width]
        prev = jnp.where(row == 0, prev_scr[:, lo:lo + width], pltpu.roll(cur, 1, axis=0))
        return cur + (prev - cur) * mix_ref[:, lo:lo + width]

    def seg_sum(x):
        return _dot(x.astype(BF16), ones_bd)

    head_mask = [jnp.where(iota((chunk, MXU_DIM), 1) // HEAD_DIM == h, 1.0, 0.0).astype(BF16)
                 for h in range(HEADS_PER_GROUP)]

    def head_blocks(x):
        return jnp.concatenate([x * m for m in head_mask], axis=0)

    slot = step % 2
    inv_n = 1.0 / HEAD_DIM
    t_idx = iota((chunk, 2 * bd_rows), 0)
    s_idx = iota((chunk, 2 * bd_rows), 1) % chunk
    strict = (s_idx < t_idx)[:, :bd_rows]
    same_block = jnp.where(iota((bd_rows, bd_rows), 0) // chunk == iota((bd_rows, bd_rows), 1) // chunk,
                           1.0, 0.0).astype(BF16)
    eye_cat = jnp.where(t_idx == s_idx, 1.0, 0.0)[:, :bd_rows].astype(BF16)
    block_diag = lambda x: jnp.concatenate([x] * HEADS_PER_GROUP, axis=0) * same_block


    st_ar, st_ak, st_r, st_vbd, st_v, st_bk, st_total, st_gate, st_bonus = (
        [ref[slot, g] for g in groups] for ref in staged)
    st_e = [ecat_scr[g] for g in groups]
    s_b = [s_scr[g].astype(BF16) for g in groups]
    uy = [_dot_nt(st_ar[g], s_b[g]) for g in groups]

    e_cat = [ncat_scr[g] for g in groups]
    m_b = [x.astype(BF16) for x in e_cat]
    m_b = [_dot(x, block_diag(x)).astype(BF16) for x in m_b]
    levels = int(math.log2(chunk)) - 1

    def inverse_level(lvl, e_cat, m_b):
        t_bd = [block_diag(x.astype(BF16) + eye_cat) for x in e_cat]
        if lvl < levels - 1:
            both = [_dot(m_b[g], jnp.concatenate([t_bd[g], block_diag(m_b[g])], axis=1)) for g in groups]
            return ([e_cat[g] + both[g][:, :bd_rows] for g in groups], [x[:, bd_rows:].astype(BF16) for x in both])
        return [e_cat[g] + _dot(m_b[g], t_bd[g]) for g in groups], None

    lora_in = shifted(LORA_OFF, LANES)
    lora_tanh = jnp.tanh(lora_in).astype(BF16)
    lora_lin = lora_in.astype(BF16)
    gate_sig = jax.nn.sigmoid(shifted(GATE_OFF, GATE_PAD)).astype(BF16)
    r = [shifted(g * MXU_DIM, MXU_DIM) for g in groups]

    rhs = [uy[g][:chunk] + _dot(st_ak[g], st_vbd[g]) for g in groups]
    e_cat, m_b = inverse_level(0, e_cat, m_b)

    k = [shifted(RWKV_WIDTH + g * MXU_DIM, MXU_DIM) for g in groups]
    v = [shifted(2 * RWKV_WIDTH + g * MXU_DIM, MXU_DIM) for g in groups]
    prev_scr[...] = p_ref[0, chunk - 1:chunk, QKV_COLS:]

    p = [rhs[g] + _dot(st_e[g], head_blocks(rhs[g].astype(BF16))) for g in groups]
    p_b = [x.astype(BF16) for x in p]
    if levels > 1:
        e_cat, m_b = inverse_level(1, e_cat, m_b)

    logw = [-math.exp(-0.5) * jax.nn.sigmoid(w0_ref[:, cols[g]] + _dot(lora_tanh, dup_ref[:, cols[g]])) for g in groups]
    a = [jax.nn.sigmoid(a0_ref[:, cols[g]] + _dot(lora_lin, iup_ref[:, cols[g]])) for g in groups]
    gate = [_dot(gate_sig, gup_ref[:, cols[g]]) for g in groups]

    y = [uy[g][chunk:] + _dot(st_r[g], jnp.concatenate([head_blocks(p_b[g]), st_vbd[g]], axis=0)) for g in groups]
    for g in groups:
        upd = _dot_tn(jnp.concatenate([p_b[g], st_v[g]], axis=0), st_bk[g])
        s_scr[g] = s_scr[g] * st_total[g] + jnp.where(same_head, upd, 0.0)
    if levels > 2:
        e_cat, m_b = inverse_level(2, e_cat, m_b)

    kk = [k[g] * kk_ref[:, cols[g]] for g in groups]
    kk = [kk[g] / jnp.maximum(jnp.sqrt(seg_sum(kk[g] * kk[g])), 1e-12) for g in groups]
    k = [k[g] * (1.0 + (a[g] - 1.0) * ka_ref[:, cols[g]]) for g in groups]

    dev = [y[g] - seg_sum(y[g]) * inv_n for g in groups]
    if levels > 3:
        e_cat, m_b = inverse_level(3, e_cat, m_b)

    tri = jnp.where(iota((chunk, chunk), 0) >= iota((chunk, chunk), 1), 1.0, 0.0).astype(BF16)
    cum = [sum(_dot(tri, part) for part in _split(logw[g], 3)) for g in groups]

    var = [seg_sum(x * x) * inv_n for x in dev]
    for lvl in range(4, levels):
        e_cat, m_b = inverse_level(lvl, e_cat, m_b)

    e_pos = [jnp.exp(x) for x in cum]
    e_neg = [jnp.exp(-x) for x in cum]
    a_t = [-kk[g] * jnp.exp(cum[g] - logw[g]) for g in groups]
    b_t = [kk[g] * a[g] * e_neg[g] for g in groups]

    for g in groups:
        yn = dev[g] * lax.rsqrt(var[g] + GN_EPS) * lnw_ref[:, cols[g]] + lnb_ref[:, cols[g]]
        rw_ref[0, :, cols[g]] = ((yn + st_bonus[g]) * st_gate[g]).astype(BF16)

    k_t = [k[g] * e_neg[g] for g in groups]
    r_t = [r[g] * e_pos[g] for g in groups]
    total = [x[chunk - 1:chunk, :] for x in e_pos]
    ar_b = [jnp.concatenate([a_t[g], r_t[g]], axis=0).astype(BF16) for g in groups]
    xb =[head_blocks(x.astype(BF16)) for x in b_t]
    xk = [head_blocks(x.astype(BF16)) for x in k_t]
    v_bd = [head_blocks(x.astype(BF16)) for x in v]

    scores = [_dot_nt(ar_b[g], jnp.concatenate([xb[g], xk[g]], axis=0)) for g in groups]
    bonus = [seg_sum(r[g] * k[g] * rk_ref[:, cols[g]]) * v[g] for g in groups]
    for g in groups:
        ecat_scr[g] = e_cat[g].astype(BF16)
        ncat_scr[g] = jnp.where(strict, scores[g][:chunk, :bd_rows], 0.0)
        arb_scr[slot, g] = ar_b[g]
        aak_scr[slot, g] = jnp.where(strict, scores[g][:chunk, bd_rows:], 0.0).astype(BF16)
        ar_scr[slot, g] = jnp.where(s_idx <= t_idx, scores[g][chunk:], 0.0).astype(BF16)
        vbd_scr[slot, g] = v_bd[g]
        vb_scr[slot, g] = v[g].astype(BF16)
        bk_scr[slot, g] = jnp.concatenate([b_t[g] * total[g], k_t[g] * total[g]], axis=0).astype(BF16)
        total_scr[slot, g], gate_scr[slot, g], bonus_scr[slot, g] = total[g], gate[g], bonus[g]

    @pl.when(last_apply)
    def _():
        for g in groups:
            folded = sum(_dot(part, fold_sel) for part in _split(s_scr[g], 3))
            sout_ref[0, cols[g], :] = folded


def _wkv(proj, shift0, s0, wp, layer, *, chunk):
    b, t, n_proj = proj.shape
    n_chunks = t // chunk
    bd_rows = HEADS_PER_GROUP * chunk
    lay = lambda *blk: pl.BlockSpec((None,) + blk, lambda s: (layer,) + (0,) * len(blk))
    vec = lay(1, RWKV_WIDTH)
    grp = lambda rows, width, dtype: pltpu.VMEM((N_GROUPS, rows, width), dtype)
    two = lambda rows, width, dtype: pltpu.VMEM((2, N_GROUPS, rows, width), dtype)
    total = b * n_chunks
    prep = lambda s: jnp.minimum(s, total - 1)
    apply = lambda s: jnp.clip(s - 2, 0, total - 1)
    return pl.pallas_call(
        functools.partial(_wkv_kernel, chunk=chunk, n_chunks=n_chunks),
        grid=(total + 2,),
        in_specs=[pl.BlockSpec((1, chunk, n_proj), lambda s: (prep(s) // n_chunks, prep(s) % n_chunks, 0)),
                  pl.BlockSpec((1, 1, RWKV_PROJ_PAD), lambda s: (prep(s) // n_chunks, 0, 0)),
                  pl.BlockSpec((1, RWKV_WIDTH, HEAD_DIM), lambda s: (apply(s) // n_chunks, 0, 0)),
                  lay(1, RWKV_PROJ_PAD), vec, lay(LANES, RWKV_WIDTH), vec, lay(LANES, RWKV_WIDTH),
                  lay(GATE_PAD, RWKV_WIDTH), vec, vec, vec, vec, vec],
        out_specs=[pl.BlockSpec((1, chunk, RWKV_WIDTH), lambda s: (apply(s) // n_chunks, apply(s) % n_chunks, 0)),
                   pl.BlockSpec((1, RWKV_WIDTH, HEAD_DIM), lambda s: (apply(s) // n_chunks, 0, 0))],
        out_shape=[jax.ShapeDtypeStruct((b, t, RWKV_WIDTH), BF16),
                   jax.ShapeDtypeStruct((b, RWKV_WIDTH, HEAD_DIM), F32)],
        scratch_shapes=[pltpu.VMEM((N_GROUPS, MXU_DIM, MXU_DIM), F32),
                        pltpu.VMEM((1, RWKV_PROJ_PAD), F32),
                        two(2 * chunk, MXU_DIM, BF16), two(chunk, bd_rows, BF16), two(chunk, 2 * bd_rows, BF16),
                        two(bd_rows, MXU_DIM, BF16), two(chunk, MXU_DIM, BF16), two(2 * chunk, MXU_DIM, BF16),
                        two(1, MXU_DIM, F32), two(chunk, MXU_DIM, F32), two(chunk, MXU_DIM, F32),
                        grp(chunk, bd_rows, F32), grp(chunk, bd_rows, BF16)],
        compiler_params=_params(1),
        name="wkv7",
    )(proj, shift0, s0, wp["mix"], wp["w0"], wp["decay_up"], wp["a0"], wp["iclr_up"], wp["gate_up"],
      wp["k_k"], wp["k_a"], wp["r_k"], wp["ln_w"], wp["ln_b"])


SUBLANES = 8


def _wkv_step_kernel(p_ref, shift_ref, s_ref, mix_ref, w0_ref, dup_ref, a0_ref, iup_ref, gup_ref,
                     kk_ref, ka_ref, rk_ref, lnw_ref, lnb_ref, fold_ref, rw_ref, sout_ref):
    iota = lambda shape, dim: lax.broadcasted_iota(jnp.int32, shape, dim)
    groups = range(N_GROUPS)
    cols = [slice(g * MXU_DIM, (g + 1) * MXU_DIM) for g in groups]
    ones_bd = jnp.where(iota((MXU_DIM, MXU_DIM), 0) // HEAD_DIM == iota((MXU_DIM, MXU_DIM), 1) // HEAD_DIM,
                        1.0, 0.0).astype(BF16)

    def seg_sum(x):
        return jnp.concatenate([_dot(x[:, cols[g]].astype(BF16), ones_bd) for g in groups], axis=1)

    cur = jnp.broadcast_to(p_ref[0], (SUBLANES, RWKV_PROJ_PAD))
    prev = jnp.broadcast_to(shift_ref[0], (SUBLANES, RWKV_PROJ_PAD))
    p = cur + (prev - cur) * mix_ref[...]
    r = p[:, 0:RWKV_WIDTH]
    k = p[:, RWKV_WIDTH:2 * RWKV_WIDTH]
    v = p[:, 2 * RWKV_WIDTH:3 * RWKV_WIDTH]
    lora_in = p[:, LORA_OFF:LORA_OFF + LANES]
    gate_in = p[:, GATE_OFF:GATE_OFF + GATE_PAD]
    w = w0_ref[...] + _dot(jnp.tanh(lora_in).astype(BF16), dup_ref[...])
    decay = jnp.exp(-math.exp(-0.5) * jax.nn.sigmoid(w))
    a = jax.nn.sigmoid(a0_ref[...] + _dot(lora_in.astype(BF16), iup_ref[...]))
    gate = _dot(jax.nn.sigmoid(gate_in).astype(BF16), gup_ref[...])
    kk = k * kk_ref[...]
    kk = kk / jnp.maximum(jnp.sqrt(seg_sum(kk * kk)), 1e-12)
    k = k * (1.0 + (a - 1.0) * ka_ref[...])

    own = iota((N_RWKV_HEADS, RWKV_WIDTH), 0) == iota((N_RWKV_HEADS, RWKV_WIDTH), 1) // HEAD_DIM
    spread = lambda x: jnp.where(own, jnp.broadcast_to(x[0:1], (N_RWKV_HEADS, RWKV_WIDTH)), 0.0)
    stacked = jnp.concatenate([spread(decay), spread(-kk), spread(kk * a), spread(k), spread(r)], axis=0)
    per_head = sum(_dot(part, fold_ref[...]) for part in _split(stacked, 3))
    w_h, a_h, b_h, k_h, r_h = (per_head[i * N_RWKV_HEADS:(i + 1) * N_RWKV_HEADS] for i in range(5))
    rows = lambda x: jnp.concatenate(
        [jnp.broadcast_to(x[h:h + 1, :], (HEAD_DIM, HEAD_DIM)) for h in range(N_RWKV_HEADS)], axis=0)

    eye = iota((LANES, LANES), 0) == iota((LANES, LANES), 1)
    ones_cols = jnp.ones((LANES, HEAD_DIM), BF16)
    v_col = jnp.concatenate(
        [sum(_dot(part, ones_cols) for part in
             _split(jnp.where(eye, jnp.broadcast_to(v[0:1, t * LANES:(t + 1) * LANES], (LANES, LANES)), 0.0), 3))
         for t in range(RWKV_WIDTH // LANES)], axis=0)

    s = s_ref[0]
    sa = jnp.sum(s * rows(a_h), axis=-1, keepdims=True)
    s_new = s * rows(w_h) + sa * rows(b_h) + v_col * rows(k_h)
    sout_ref[0] = s_new
    y_heads = _dot_nt(r_h.astype(BF16), s_new.astype(BF16))
    y = jnp.broadcast_to(jnp.sum(jnp.where(own, y_heads, 0.0), axis=0, keepdims=True), (SUBLANES, RWKV_WIDTH))

    inv_n = 1.0 / HEAD_DIM
    dev = y - seg_sum(y) * inv_n
    var = seg_sum(dev * dev) * inv_n
    y = dev * lax.rsqrt(var + GN_EPS) * lnw_ref[...] + lnb_ref[...]
    bonus = seg_sum(r * k * rk_ref[...]) * v
    rw_ref[0] = ((y + bonus) * gate)[0:1].astype(BF16)


def _wkv_step(prw_rows, shift_rows, state, wp, layer):
    s = prw_rows.shape[0]
    lay = lambda *blk: pl.BlockSpec((None,) + blk, lambda i: (layer,) + (0,) * len(blk))
    vec = lay(1, RWKV_WIDTH)
    fold = np.tile(np.eye(HEAD_DIM, dtype=np.float32), (N_RWKV_HEADS, 1))
    return pl.pallas_call(
        _wkv_step_kernel,
        grid=(s,),
        in_specs=[pl.BlockSpec((1, 1, RWKV_PROJ_PAD), lambda i: (i, 0, 0)),
                  pl.BlockSpec((None, 1, 1, RWKV_PROJ_PAD), lambda i: (layer, i, 0, 0)),
                  pl.BlockSpec((None, 1, RWKV_WIDTH, HEAD_DIM), lambda i: (layer, i, 0, 0)),
                  lay(1, RWKV_PROJ_PAD), vec, lay(LANES, RWKV_WIDTH), vec, lay(LANES, RWKV_WIDTH),
                  lay(GATE_PAD, RWKV_WIDTH), vec, vec, vec, vec, vec,
                  pl.BlockSpec((RWKV_WIDTH, HEAD_DIM), lambda i: (0, 0))],
        out_specs=[pl.BlockSpec((1, 1, RWKV_WIDTH), lambda i: (i, 0, 0)),
                   pl.BlockSpec((1, RWKV_WIDTH, HEAD_DIM), lambda i: (i, 0, 0))],
        out_shape=[jax.ShapeDtypeStruct((s, 1, RWKV_WIDTH), BF16),
                   jax.ShapeDtypeStruct((s, RWKV_WIDTH, HEAD_DIM), F32)],
        compiler_params=_params(1),
        name="wkv7_step",
    )(prw_rows, shift_rows, state, wp["mix"], wp["w0"], wp["decay_up"], wp["a0"], wp["iclr_up"], wp["gate_up"],
      wp["k_k"], wp["k_a"], wp["r_k"], wp["ln_w"], wp["ln_b"], jnp.asarray(fold, BF16))


def _out_proj_kernel(att_ref, rw_ref, x_ref, gt_ref, sh_ref, sc_ref, g_ref, w_ref, x1_ref, h_ref, *, n_sample,
                     is_sample):
    rows = functools.partial(_mod_rows, n_sample=n_sample, is_sample=is_sample)
    y = _dot(att_ref[0], w_ref[0:ATTN_WIDTH, :]) + _dot(rw_ref[0], w_ref[ATTN_WIDTH:, :])
    x1 = x_ref[0] + rows(gt_ref) * y
    x1_ref[0] = x1
    h_ref[0] = _rmsnorm_mod(x1, g_ref[...], rows(sh_ref), rows(sc_ref)).astype(BF16)


def _out_proj(att, rw, x, mod, g, w, layer, *, tm, n_sample, is_sample):
    b, t, d = x.shape
    row = lambda width: pl.BlockSpec((1, tm, width), lambda i, m: (i, m, 0))
    return pl.pallas_call(
        functools.partial(_out_proj_kernel, n_sample=n_sample, is_sample=is_sample),
        grid=(b, t // tm),
        in_specs=[row(ATTN_WIDTH), row(RWKV_WIDTH), row(d),
                  _mod_spec(mod, layer, MOD_GATE1), _mod_spec(mod, layer, MOD_SHIFT2), _mod_spec(mod, layer, MOD_SCALE2),
                  pl.BlockSpec((None, 1, d), lambda i, m: (layer, 0, 0)),
                  pl.BlockSpec((None, d, d), lambda i, m: (layer, 0, 0))],
        out_specs=[row(d), row(d)],
        out_shape=[jax.ShapeDtypeStruct((b, t, d), F32), jax.ShapeDtypeStruct((b, t, d), BF16)],
        compiler_params=_params(2),
        name="out_proj",
    )(att, rw, x, mod, mod, mod, g, w)


def _ffn_kernel(h_ref, x_ref, gt_ref, wg_ref, wu_ref, wo_ref, gf_ref, o_ref, acc_ref, *, final_norm, n_sample,
                is_sample):
    j = pl.program_id(2)

    @pl.when(j == 0)
    def _():
        acc_ref[...] = jnp.zeros_like(acc_ref)

    h = h_ref[0]
    gate = _dot(h, wg_ref[...])
    up = _dot(h, wu_ref[...])
    act = (gate * jax.nn.sigmoid(gate) * up).astype(BF16)
    acc_ref[...] += _dot(act, wo_ref[...])

    @pl.when(j == pl.num_programs(2) - 1)
    def _():
        x2 = x_ref[0] + _mod_rows(gt_ref, n_sample, is_sample) * acc_ref[...]
        if final_norm:
            ms = jnp.mean(x2 * x2, axis=-1, keepdims=True)
            x2 = x2 * lax.rsqrt(ms + RMS_EPS) * gf_ref[...]
        o_ref[0] = x2


def _ffn(h, x, mod, w_in, w_out, g_final, layer, *, tm, th, final_norm, n_sample, is_sample):
    b, t, d = x.shape
    n_h = FFN_HIDDEN // th
    return pl.pallas_call(
        functools.partial(_ffn_kernel, final_norm=final_norm, n_sample=n_sample, is_sample=is_sample),
        grid=(b, t // tm, n_h),
        in_specs=[pl.BlockSpec((1, tm, d), lambda i, m, j: (i, m, 0)),
                  pl.BlockSpec((1, tm, d), lambda i, m, j: (i, m, 0)),
                  _mod_spec(mod, layer, MOD_GATE2),
                  pl.BlockSpec((None, d, th), lambda i, m, j: (layer, 0, j)),
                  pl.BlockSpec((None, d, th), lambda i, m, j: (layer, 0, j + n_h)),
                  pl.BlockSpec((None, th, d), lambda i, m, j: (layer, j, 0)),
                  pl.BlockSpec((1, d), lambda i, m, j: (0, 0))],
        out_specs=pl.BlockSpec((1, tm, d), lambda i, m, j: (i, m, 0)),
        out_shape=jax.ShapeDtypeStruct((b, t, d), F32),
        scratch_shapes=[pltpu.VMEM((tm, d), F32)],
        compiler_params=_params(3),
        name="ffn",
    )(h, x, mod, w_in, w_in, w_out, g_final)


def _tiles(t_prompt, n_sample):
    tiles = dict(proj_tm=1024, proj_tn=1024, out_tm=512, ffn_tm=512, ffn_th=512, wkv_chunk=64)
    assert t_prompt % tiles["proj_tm"] == 0 and t_prompt % tiles["out_tm"] == 0 and t_prompt % tiles["ffn_tm"] == 0
    assert t_prompt % tiles["wkv_chunk"] == 0 and t_prompt % WINDOW == 0
    assert (QKV_COLS + RWKV_PROJ_PAD) % tiles["proj_tn"] == 0
    assert FFN_HIDDEN % tiles["ffn_th"] == 0 and n_sample % SUBLANES == 0
    return tiles


def _pad_cols(x, width):
    return jnp.pad(x, [(0, 0)] * (x.ndim - 1) + [(0, width - x.shape[-1])])


def _pad_rows(x, rows, offset=0):
    return jnp.pad(x, [(0, 0), (offset, rows - offset - x.shape[1]), (0, 0)])


def kernel(x_prompt, x_sample, cache_k, cache_v, state_wkv, state_shift, c_prompt, c_sample, w_ada, b_ada, g_norm_mix, g_norm_ffn, w_in, w_out, attn_sinks, mix_shift, decay_w0, decay_up, iclr_a0, iclr_up, gate_up, k_k, k_a, r_k, ln_x_w, ln_x_b, w_ffn_in, w_ffn_out, g_norm_final):
    n_layers = w_in.shape[0]
    bp, tp, d = x_prompt.shape
    bs = x_sample.shape[0]
    tiles = _tiles(tp, bs)

    w_in_t = jnp.pad(jnp.swapaxes(w_in, 1, 2).astype(BF16),
                     [(0, 0), (0, QKV_COLS + RWKV_PROJ_PAD - w_in.shape[2]), (0, 0)])
    w_out_b = w_out.astype(BF16)
    w_ffn_in_b = w_ffn_in.astype(BF16)
    w_ffn_out_b = w_ffn_out.astype(BF16)
    vec = lambda x: x.reshape(n_layers, 1, -1)
    wp = dict(
        mix=vec(_pad_cols(mix_shift, RWKV_PROJ_PAD)), w0=vec(decay_w0), a0=vec(iclr_a0),
        decay_up=_pad_rows(decay_up, LANES).astype(BF16),
        iclr_up=_pad_rows(iclr_up, LANES, D_DECAY_LORA).astype(BF16),
        gate_up=_pad_rows(gate_up, GATE_PAD).astype(BF16),
        k_k=vec(k_k), k_a=vec(k_a), r_k=vec(r_k), ln_w=vec(ln_x_w), ln_b=vec(ln_x_b))
    g_mix, g_ffn = vec(g_norm_mix), vec(g_norm_ffn)
    g_final = g_norm_final.reshape(1, d)
    sinks_flat = attn_sinks.reshape(-1)
    sinks_col = attn_sinks.reshape(n_layers, N_Q_HEADS, 1)
    cache_k2 = cache_k.reshape(n_layers, bs, WINDOW, KV_WIDTH)
    cache_v2 = cache_v.reshape(n_layers, bs, WINDOW, KV_WIDTH)
    shift_rows = _pad_cols(state_shift, RWKV_PROJ_PAD).reshape(n_layers, bs, 1, RWKV_PROJ_PAD)
    state_rows = state_wkv.reshape(n_layers, bs, RWKV_WIDTH, HEAD_DIM)
    zero_shift = jnp.zeros((bp, 1, RWKV_PROJ_PAD), F32)
    zero_state = jnp.zeros((bp, RWKV_WIDTH, HEAD_DIM), F32)

    mod = _ada(jnp.concatenate([c_sample, c_prompt], axis=0), w_ada, b_ada)
    prompt = dict(n_sample=bs, is_sample=False)
    sample = dict(n_sample=bs, is_sample=True)

    xp = x_prompt
    xs = x_sample.reshape(1, bs, d)
    outs_p = [[] for _ in range(4)]
    outs_s = [[] for _ in range(4)]
    for layer in range(n_layers):
        last = layer == n_layers - 1

        proj = _norm_proj(xp, mod, g_mix, w_in_t, layer, tm=tiles["proj_tm"], tn=tiles["proj_tn"], **prompt)
        att = _attn_prompt(proj, sinks_flat, layer)
        rw, state = _wkv(proj, zero_shift, zero_state, wp, layer, chunk=tiles["wkv_chunk"])
        x1, h2 = _out_proj(att, rw, xp, mod, g_ffn, w_out_b, layer, tm=tiles["out_tm"], **prompt)
        xp = _ffn(h2, x1, mod, w_ffn_in_b, w_ffn_out_b, g_final, layer, tm=tiles["ffn_tm"], th=tiles["ffn_th"],
                  final_norm=last, **prompt)
        new_kv = proj[:, tp - WINDOW:, ATTN_WIDTH:QKV_COLS]
        outs_p[0].append(new_kv[:, :, :KV_WIDTH].reshape(bp, WINDOW, N_KV_HEADS, HEAD_DIM))
        outs_p[1].append(new_kv[:, :, KV_WIDTH:].reshape(bp, WINDOW, N_KV_HEADS, HEAD_DIM))
        outs_p[2].append(state.reshape(bp, N_RWKV_HEADS, HEAD_DIM, HEAD_DIM))
        outs_p[3].append(proj[:, tp - 1, QKV_COLS:QKV_COLS + RWKV_PROJ])

        proj_s = _norm_proj(xs, mod, g_mix, w_in_t, layer, tm=bs, tn=tiles["proj_tn"], **sample)[0]
        att_s, nk, nv = _attn_sample(proj_s[:, :QKV_COLS].reshape(bs, 1, QKV_COLS), cache_k2, cache_v2, sinks_col, layer)
        prw_tok = proj_s[:, QKV_COLS:].reshape(bs, 1, RWKV_PROJ_PAD)
        rw_s, state_s = _wkv_step(prw_tok, shift_rows, state_rows, wp, layer)
        x1_s, h2_s = _out_proj(att_s.reshape(1, bs, ATTN_WIDTH), rw_s.reshape(1, bs, RWKV_WIDTH), xs,
                               mod, g_ffn, w_out_b, layer, tm=bs, **sample)
        xs = _ffn(h2_s, x1_s, mod, w_ffn_in_b, w_ffn_out_b, g_final, layer, tm=bs, th=tiles["ffn_th"],
                  final_norm=last, **sample)
        outs_s[0].append(nk.reshape(bs, WINDOW, N_KV_HEADS, HEAD_DIM))
        outs_s[1].append(nv.reshape(bs, WINDOW, N_KV_HEADS, HEAD_DIM))
        outs_s[2].append(state_s.reshape(bs, N_RWKV_HEADS, HEAD_DIM, HEAD_DIM))
        outs_s[3].append(prw_tok[:, 0, :RWKV_PROJ])

    stack = lambda xs_: jnp.stack(xs_, axis=0)
    return (xp, xs.reshape(bs, 1, d),
            stack(outs_p[0]), stack(outs_p[1]), stack(outs_p[2]), stack(outs_p[3]),
            stack(outs_s[0]), stack(outs_s[1]), stack(outs_s[2]), stack(outs_s[3]))
```

```python
import functools
import math

import numpy as np
import jax
import jax.numpy as jnp
from jax import lax
from jax.experimental import pallas as pl
from jax.experimental.pallas import tpu as pltpu

F32 = jnp.float32
BF16 = jnp.bfloat16

D_MODEL = 2048
HEAD_DIM = 64
ATTN_WIDTH = D_MODEL // 2
RWKV_WIDTH = D_MODEL - ATTN_WIDTH
N_Q_HEADS = ATTN_WIDTH // HEAD_DIM
N_KV_HEADS = N_Q_HEADS // 4
GQA_REP = N_Q_HEADS // N_KV_HEADS
KV_WIDTH = N_KV_HEADS * HEAD_DIM
N_RWKV_HEADS = RWKV_WIDTH // HEAD_DIM
WINDOW = 128
D_DECAY_LORA = 64
D_ICLR_LORA = 64
D_GATE_LORA = 160
RWKV_PROJ = 3 * RWKV_WIDTH + D_DECAY_LORA + D_ICLR_LORA + D_GATE_LORA
QKV_COLS = ATTN_WIDTH + 2 * KV_WIDTH
FFN_HIDDEN = 5632
RMS_EPS = 1e-5
GN_EPS = 64e-5
NEG_INF = -1e30

LANES = 128
MXU_DIM = 256
VMEM_LIMIT_BYTES = 56 * 1024 * 1024

LORA_OFF = 3 * RWKV_WIDTH
GATE_OFF = LORA_OFF + D_DECAY_LORA + D_ICLR_LORA
GATE_PAD = 2 * LANES
RWKV_PROJ_PAD = 3584
HEADS_PER_GROUP = MXU_DIM // HEAD_DIM
N_GROUPS = N_RWKV_HEADS // HEADS_PER_GROUP

ALIBI_SLOPES = tuple(float(s) for s in np.exp2(-8.0 * np.arange(1, N_Q_HEADS + 1, dtype=np.float32) / N_Q_HEADS))


def _params(n_axes):
    return pltpu.CompilerParams(dimension_semantics=("arbitrary",) * n_axes,
                                vmem_limit_bytes=VMEM_LIMIT_BYTES)


def _dot(a, b):
    return jnp.dot(a, b, preferred_element_type=F32)


def _dot_nt(a, b):
    return lax.dot_general(a, b, (((1,), (1,)), ((), ())), preferred_element_type=F32)


def _dot_tn(a, b):
    return lax.dot_general(a, b, (((0,), (0,)), ((), ())), preferred_element_type=F32)


def _split(x, parts):
    out = []
    for _ in range(parts):
        p = x.astype(BF16)
        out.append(p)
        x = x - p.astype(F32)
    return out


def _rmsnorm_mod(x, g, shift, scale):
    ms = jnp.mean(x * x, axis=-1, keepdims=True)
    y = x * lax.rsqrt(ms + RMS_EPS) * g
    return y * (1.0 + scale) + shift


def _ada_kernel(c_ref, w_ref, b_ref, o_ref):
    c = c_ref[...]
    s = (c * jax.nn.sigmoid(c)).astype(BF16)
    o_ref[...] = _dot(s, w_ref[...].astype(BF16)) + b_ref[...]


def _ada(c_all, w_ada, b_ada):
    n_layers, d, n = w_ada.shape
    rows = c_all.shape[0]
    tn = 1024
    return pl.pallas_call(
        _ada_kernel,
        grid=(n_layers, n // tn),
        in_specs=[pl.BlockSpec((rows, d), lambda l, j: (0, 0)),
                  pl.BlockSpec((None, d, tn), lambda l, j: (l, 0, j)),
                  pl.BlockSpec((None, 1, tn), lambda l, j: (l, 0, j))],
        out_specs=pl.BlockSpec((None, rows, tn), lambda l, j: (l, 0, j)),
        out_shape=jax.ShapeDtypeStruct((n_layers, rows, n), F32),
        compiler_params=_params(2),
        name="ada_mod",
    )(c_all, w_ada, b_ada.reshape(n_layers, 1, n))


NORM_STRIP = 32


MOD_SHIFT1, MOD_SCALE1, MOD_GATE1, MOD_SHIFT2, MOD_SCALE2, MOD_GATE2 = range(6)


def _mod_spec(mod, layer, part):
    rows, d = mod.shape[1], mod.shape[2] // 6
    return pl.BlockSpec((None, rows, d), lambda *_: (layer, 0, part))


def _mod_rows(ref, n_sample, is_sample):
    if is_sample:
        return ref[0:n_sample, :]
    return ref[pl.ds(n_sample + pl.program_id(0), 1), :]


def _norm_proj_kernel(x_ref, sh_ref, sc_ref, g_ref, w_ref, o_ref, h_scr, *, n_sample, is_sample):
    j = pl.program_id(2)
    tm = h_scr.shape[0]

    @pl.when(j == 0)
    def _():
        gain = g_ref[...] * (1.0 + _mod_rows(sc_ref, n_sample, is_sample))
        shift = _mod_rows(sh_ref, n_sample, is_sample)
        if gain.shape[0] == 1:
            def strip(i, carry):
                rows = pl.ds(pl.multiple_of(i * NORM_STRIP, NORM_STRIP), NORM_STRIP)
                x = x_ref[0, rows, :]
                ms = jnp.mean(x * x, axis=-1, keepdims=True)
                h_scr[rows, :] = (x * lax.rsqrt(ms + RMS_EPS) * gain + shift).astype(BF16)
                return carry
            lax.fori_loop(0, tm // NORM_STRIP, strip, 0, unroll=4)
        else:
            x = x_ref[0]
            ms = jnp.mean(x * x, axis=-1, keepdims=True)
            h_scr[...] = (x * lax.rsqrt(ms + RMS_EPS) * gain + shift).astype(BF16)

    o_ref[0] = _dot_nt(h_scr[...], w_ref[...])


def _norm_proj(x, mod, g, w, layer, *, tm, tn, n_sample, is_sample):
    b, t, d = x.shape
    n = QKV_COLS + RWKV_PROJ_PAD
    return pl.pallas_call(
        functools.partial(_norm_proj_kernel, n_sample=n_sample, is_sample=is_sample),
        grid=(b, t // tm, n // tn),
        in_specs=[pl.BlockSpec((1, tm, d), lambda i, m, j: (i, m, 0)),
                  _mod_spec(mod, layer, MOD_SHIFT1), _mod_spec(mod, layer, MOD_SCALE1),
                  pl.BlockSpec((None, 1, d), lambda i, m, j: (layer, 0, 0)),
                  pl.BlockSpec((None, tn, d), lambda i, m, j: (layer, j, 0))],
        out_specs=pl.BlockSpec((1, tm, tn), lambda i, m, j: (i, m, j)),
        out_shape=jax.ShapeDtypeStruct((b, t, n), F32),
        scratch_shapes=[pltpu.VMEM((tm, d), BF16)],
        compiler_params=_params(3),
        name="norm_proj",
    )(x, mod, mod, g, w)


LOG2E = math.log2(math.e)


def _attn_bias_table():
    t = np.arange(WINDOW)[:, None]
    j = np.arange(2 * WINDOW)[None, :]
    dist = t + WINDOW - j
    valid = (dist >= 0) & (dist <= WINDOW)
    slopes = np.asarray(ALIBI_SLOPES, np.float32)[:, None, None]
    bias = np.where(valid[None], -slopes * dist[None].astype(np.float32) * np.float32(LOG2E), np.float32(NEG_INF))
    first = np.where((j >= WINDOW)[None], bias, np.float32(NEG_INF))
    return np.stack([first, bias]).astype(np.float32)


def _attn_prompt_kernel(sink_ref, bias_ref, q_ref, kc_ref, vc_ref, kp_ref, vp_ref, o_ref, *, layer):
    q = q_ref[0] * (HEAD_DIM ** -0.5 * LOG2E)
    kc, vc, kp, vp = kc_ref[0], vc_ref[0], kp_ref[0], vp_ref[0]
    kv_heads, heads = range(N_KV_HEADS), range(N_Q_HEADS)
    ksl = [slice(g * HEAD_DIM, (g + 1) * HEAD_DIM) for g in kv_heads]
    k_band = [jnp.concatenate([kp[:, ksl[g]], kc[:, ksl[g]]], axis=0).astype(BF16) for g in kv_heads]
    v_band = [jnp.concatenate([vp[:, ksl[g]], vc[:, ksl[g]]], axis=0).astype(BF16) for g in kv_heads]
    lg = [_dot_nt(q[:, h * HEAD_DIM:(h + 1) * HEAD_DIM].astype(BF16), k_band[h // GQA_REP]) + bias_ref[0, h]
          for h in heads]
    sink = [sink_ref[layer * N_Q_HEADS + h] * LOG2E for h in heads]
    m = [jnp.maximum(jnp.max(lg[h], axis=-1, keepdims=True), sink[h]) for h in heads]
    p = [jnp.exp2(lg[h] - m[h]).astype(BF16) for h in heads]
    ones = jnp.ones((2 * WINDOW, HEAD_DIM), BF16)
    den = [_dot(p[h], ones) + jnp.exp2(sink[h] - m[h]) for h in heads]
    outs = [_dot(p[h], v_band[h // GQA_REP]) * (1.0 / den[h]) for h in heads]
    o_ref[0] = jnp.concatenate(outs, axis=1).astype(BF16)


def _attn_prompt(qkv, sinks_flat, layer):
    b, t, _ = qkv.shape
    kcol = ATTN_WIDTH // KV_WIDTH
    prev = lambda i, m: (i, jnp.maximum(m - 1, 0), kcol)
    prev_v = lambda i, m: (i, jnp.maximum(m - 1, 0), kcol + 1)
    return pl.pallas_call(
        functools.partial(_attn_prompt_kernel, layer=layer),
        grid=(b, t // WINDOW),
        in_specs=[pl.BlockSpec(memory_space=pltpu.SMEM),
                  pl.BlockSpec((1, N_Q_HEADS, WINDOW, 2 * WINDOW), lambda i, m: (jnp.minimum(m, 1), 0, 0, 0)),
                  pl.BlockSpec((1, WINDOW, ATTN_WIDTH), lambda i, m: (i, m, 0)),
                  pl.BlockSpec((1, WINDOW, KV_WIDTH), lambda i, m: (i, m, kcol)),
                  pl.BlockSpec((1, WINDOW, KV_WIDTH), lambda i, m: (i, m, kcol + 1)),
                  pl.BlockSpec((1, WINDOW, KV_WIDTH), prev),
                  pl.BlockSpec((1, WINDOW, KV_WIDTH), prev_v)],
        out_specs=pl.BlockSpec((1, WINDOW, ATTN_WIDTH), lambda i, m: (i, m, 0)),
        out_shape=jax.ShapeDtypeStruct((b, t, ATTN_WIDTH), BF16),
        compiler_params=_params(2),
        name="attn_prompt",
    )(sinks_flat, jnp.asarray(_attn_bias_table()), qkv, qkv, qkv, qkv, qkv)


def _attn_sample_kernel(row_ref, kc_ref, vc_ref, sink_ref, slope_ref, spread_ref, gather_ref,
                        o_ref, nk_ref, nv_ref):
    seqs = range(row_ref.shape[0])
    head_row = lax.broadcasted_iota(jnp.int32, (N_Q_HEADS, ATTN_WIDTH), 0)
    head_lane = lax.broadcasted_iota(jnp.int32, (N_Q_HEADS, ATTN_WIDTH), 1) // HEAD_DIM
    own = head_row == head_lane
    grp_row = lax.broadcasted_iota(jnp.int32, (N_Q_HEADS, KV_WIDTH), 0) // GQA_REP
    grp_lane = lax.broadcasted_iota(jnp.int32, (N_Q_HEADS, KV_WIDTH), 1) // HEAD_DIM
    bias = slope_ref[...] * (WINDOW - lax.broadcasted_iota(jnp.int32, (N_Q_HEADS, WINDOW), 1)).astype(F32)
    sink = sink_ref[...]
    last = lax.broadcasted_iota(jnp.int32, (WINDOW, KV_WIDTH), 0) == WINDOW - 1
    rows = [row_ref[s] for s in seqs]
    k_new = [r[:, ATTN_WIDTH:ATTN_WIDTH + KV_WIDTH] for r in rows]
    v_new = [r[:, ATTN_WIDTH + KV_WIDTH:] for r in rows]
    kc = [kc_ref[s] for s in seqs]
    vc = [vc_ref[s] for s in seqs]
    q_rows = [jnp.where(own, jnp.broadcast_to(r[:, :ATTN_WIDTH] * (HEAD_DIM ** -0.5), (N_Q_HEADS, ATTN_WIDTH)), 0.0)
              .astype(BF16) for r in rows]
    q_grp = [_dot(x, spread_ref[...]) for x in q_rows]
    lg = [_dot_nt(q_grp[s].astype(BF16), kc[s].astype(BF16)) - bias for s in seqs]
    lg_new = [jnp.sum(q_grp[s] * k_new[s].astype(BF16).astype(F32), axis=-1, keepdims=True) for s in seqs]
    m = [jnp.maximum(jnp.maximum(jnp.max(lg[s], axis=-1, keepdims=True), lg_new[s]), sink) for s in seqs]
    p = [jnp.exp(lg[s] - m[s]) for s in seqs]
    p_new = [jnp.exp(lg_new[s] - m[s]) for s in seqs]
    den = [jnp.sum(p[s], axis=-1, keepdims=True) + p_new[s] + jnp.exp(sink - m[s]) for s in seqs]
    out = [_dot((p[s] / den[s]).astype(BF16), vc[s].astype(BF16))
           + (p_new[s] / den[s]).astype(BF16).astype(F32) * v_new[s].astype(BF16).astype(F32) for s in seqs]
    full = [_dot(jnp.where(grp_row == grp_lane, x, 0.0).astype(BF16), gather_ref[...]) for x in out]
    for s in seqs:
        o_ref[s] = jnp.sum(jnp.where(own, full[s], 0.0), axis=0, keepdims=True).astype(BF16)
        nk_ref[s] = jnp.where(last, k_new[s], pltpu.roll(kc[s], WINDOW - 1, axis=0))
        nv_ref[s] = jnp.where(last, v_new[s], pltpu.roll(vc[s], WINDOW - 1, axis=0))


def _attn_sample(qkv_rows, cache_k, cache_v, sinks_col, layer):
    s = qkv_rows.shape[0]
    spread = np.zeros((ATTN_WIDTH, KV_WIDTH), np.float32)
    for h in range(N_Q_HEADS):
        for d in range(HEAD_DIM):
            spread[h * HEAD_DIM + d, (h // GQA_REP) * HEAD_DIM + d] = 1.0
    slopes = np.asarray(ALIBI_SLOPES, np.float32).reshape(N_Q_HEADS, 1)
    per_step = SUBLANES
    win = (per_step, WINDOW, KV_WIDTH)
    return pl.pallas_call(
        _attn_sample_kernel,
        grid=(s // per_step,),
        in_specs=[pl.BlockSpec((per_step, 1, QKV_COLS), lambda i: (i, 0, 0)),
                  pl.BlockSpec((None,) + win, lambda i: (layer, i, 0, 0)),
                  pl.BlockSpec((None,) + win, lambda i: (layer, i, 0, 0)),
                  pl.BlockSpec((None, N_Q_HEADS, 1), lambda i: (layer, 0, 0)),
                  pl.BlockSpec((N_Q_HEADS, 1), lambda i: (0, 0)),
                  pl.BlockSpec((ATTN_WIDTH, KV_WIDTH), lambda i: (0, 0)),
                  pl.BlockSpec((KV_WIDTH, ATTN_WIDTH), lambda i: (0, 0))],
        out_specs=[pl.BlockSpec((per_step, 1, ATTN_WIDTH), lambda i: (i, 0, 0)),
                   pl.BlockSpec(win, lambda i: (i, 0, 0)),
                   pl.BlockSpec(win, lambda i: (i, 0, 0))],
        out_shape=[jax.ShapeDtypeStruct((s, 1, ATTN_WIDTH), BF16),
                   jax.ShapeDtypeStruct((s, WINDOW, KV_WIDTH), F32),
                   jax.ShapeDtypeStruct((s, WINDOW, KV_WIDTH), F32)],
        compiler_params=_params(1),
        name="attn_sample",
    )(qkv_rows, cache_k, cache_v, sinks_col, jnp.asarray(slopes),
      jnp.asarray(spread, BF16), jnp.asarray(spread.T, BF16))


def _wkv_kernel(p_ref, shift0_ref, s0_ref, mix_ref, w0_ref, dup_ref, a0_ref, iup_ref, gup_ref,
                kk_ref, ka_ref, rk_ref, lnw_ref, lnb_ref, rw_ref, sout_ref, s_scr, prev_scr,
                arb_scr, aak_scr, ar_scr, vbd_scr, vb_scr, bk_scr, total_scr, gate_scr, bonus_scr, ncat_scr, ecat_scr,
                *, chunk, n_chunks):
    step = pl.program_id(0)
    staged = (arb_scr, aak_scr, ar_scr, vbd_scr, vb_scr, bk_scr, total_scr, gate_scr, bonus_scr)
    bd_rows = HEADS_PER_GROUP * chunk
    groups = range(N_GROUPS)
    cols = [slice(g * MXU_DIM, (g + 1) * MXU_DIM) for g in groups]
    iota = lambda shape, dim: lax.broadcasted_iota(jnp.int32, shape, dim)
    same_head = iota((MXU_DIM, MXU_DIM), 0) // HEAD_DIM == iota((MXU_DIM, MXU_DIM), 1) // HEAD_DIM
    ones_bd = jnp.where(same_head, 1.0, 0.0).astype(BF16)
    tile_sel = jnp.where(iota((HEAD_DIM, MXU_DIM), 0) == iota((HEAD_DIM, MXU_DIM), 1) % HEAD_DIM, 1.0, 0.0).astype(BF16)
    fold_sel = jnp.where(iota((MXU_DIM, HEAD_DIM), 0) % HEAD_DIM == iota((MXU_DIM, HEAD_DIM), 1), 1.0, 0.0).astype(BF16)

    first_prep = step % n_chunks == 0
    first_apply = (step >= 2) & ((step - 2) % n_chunks == 0)
    last_apply = (step >= 2) & ((step - 2) % n_chunks == n_chunks - 1)

    @pl.when(step == 0)
    def _():
        for ref in staged + (ncat_scr, ecat_scr, s_scr):
            ref[...] = jnp.zeros_like(ref)
        total_scr[...] = jnp.ones_like(total_scr)

    @pl.when(first_prep)
    def _():
        prev_scr[...] = shift0_ref[0]

    @pl.when(first_apply)
    def _():
        for g in groups:
            rows = s0_ref[0, cols[g], :]
            tiled = sum(_dot(part, tile_sel) for part in _split(rows, 3))
            s_scr[g] = jnp.where(same_head, tiled, 0.0)

    row = iota((chunk, 1), 0)

    def shifted(lo, width):
        cur = p_ref[0, :, QKV_COLS + lo:QKV_COLS + lo + width]
        prev = jnp.where(row == 0, prev_scr[:, lo:lo + width], pltpu.roll(cur, 1, axis=0))
        return cur + (prev - cur) * mix_ref[:, lo:lo + width]

    def seg_sum(x):
        return _dot(x.astype(BF16), ones_bd)

    head_mask = [jnp.where(iota((chunk, MXU_DIM), 1) // HEAD_DIM == h, 1.0, 0.0).astype(BF16)
                 for h in range(HEADS_PER_GROUP)]

    def head_blocks(x):
        return jnp.concatenate([x * m for m in head_mask], axis=0)

    slot = step % 2
    inv_n = 1.0 / HEAD_DIM
    t_idx = iota((chunk, 2 * bd_rows), 0)
    s_idx = iota((chunk, 2 * bd_rows), 1) % chunk
    strict = (s_idx < t_idx)[:, :bd_rows]
    same_block = jnp.where(iota((bd_rows, bd_rows), 0) // chunk == iota((bd_rows, bd_rows), 1) // chunk,
                           1.0, 0.0).astype(BF16)
    eye_cat = jnp.where(t_idx == s_idx, 1.0, 0.0)[:, :bd_rows].astype(BF16)
    block_diag = lambda x: jnp.concatenate([x] * HEADS_PER_GROUP, axis=0) * same_block


    st_ar, st_ak, st_r, st_vbd, st_v, st_bk, st_total, st_gate, st_bonus = (
        [ref[slot, g] for g in groups] for ref in staged)
    st_e = [ecat_scr[g] for g in groups]
    s_b = [s_scr[g].astype(BF16) for g in groups]
    uy = [_dot_nt(st_ar[g], s_b[g]) for g in groups]

    e_cat = [ncat_scr[g] for g in groups]
    m_b = [x.astype(BF16) for x in e_cat]
    m_b = [_dot(x, block_diag(x)).astype(BF16) for x in m_b]
    levels = int(math.log2(chunk)) - 1

    def inverse_level(lvl, e_cat, m_b):
        t_bd = [block_diag(x.astype(BF16) + eye_cat) for x in e_cat]
        if lvl < levels - 1:
            both = [_dot(m_b[g], jnp.concatenate([t_bd[g], block_diag(m_b[g])], axis=1)) for g in groups]
            return ([e_cat[g] + both[g][:, :bd_rows] for g in groups], [x[:, bd_rows:].astype(BF16) for x in both])
        return [e_cat[g] + _dot(m_b[g], t_bd[g]) for g in groups], None

    lora_in = shifted(LORA_OFF, LANES)
    lora_tanh = jnp.tanh(lora_in).astype(BF16)
    lora_lin = lora_in.astype(BF16)
    gate_sig = jax.nn.sigmoid(shifted(GATE_OFF, GATE_PAD)).astype(BF16)
    r = [shifted(g * MXU_DIM, MXU_DIM) for g in groups]

    rhs = [uy[g][:chunk] + _dot(st_ak[g], st_vbd[g]) for g in groups]
    e_cat, m_b = inverse_level(0, e_cat, m_b)

    k = [shifted(RWKV_WIDTH + g * MXU_DIM, MXU_DIM) for g in groups]
    v = [shifted(2 * RWKV_WIDTH + g * MXU_DIM, MXU_DIM) for g in groups]
    prev_scr[...] = p_ref[0, chunk - 1:chunk, QKV_COLS:]

    p = [rhs[g] + _dot(st_e[g], head_blocks(rhs[g].astype(BF16))) for g in groups]
    p_b = [x.astype(BF16) for x in p]
    if levels > 1:
        e_cat, m_b = inverse_level(1, e_cat, m_b)

    logw = [-math.exp(-0.5) * jax.nn.sigmoid(w0_ref[:, cols[g]] + _dot(lora_tanh, dup_ref[:, cols[g]])) for g in groups]
    a = [jax.nn.sigmoid(a0_ref[:, cols[g]] + _dot(lora_lin, iup_ref[:, cols[g]])) for g in groups]
    gate = [_dot(gate_sig, gup_ref[:, cols[g]]) for g in groups]

    y = [uy[g][chunk:] + _dot(st_r[g], jnp.concatenate([head_blocks(p_b[g]), st_vbd[g]], axis=0)) for g in groups]
    for g in groups:
        upd = _dot_tn(jnp.concatenate([p_b[g], st_v[g]], axis=0), st_bk[g])
        s_scr[g] = s_scr[g] * st_total[g] + jnp.where(same_head, upd, 0.0)
    if levels > 2:
        e_cat, m_b = inverse_level(2, e_cat, m_b)

    kk = [k[g] * kk_ref[:, cols[g]] for g in groups]
    kk = [kk[g] / jnp.maximum(jnp.sqrt(seg_sum(kk[g] * kk[g])), 1e-12) for g in groups]
    k = [k[g] * (1.0 + (a[g] - 1.0) * ka_ref[:, cols[g]]) for g in groups]

    dev = [y[g] - seg_sum(y[g]) * inv_n for g in groups]
    if levels > 3:
        e_cat, m_b = inverse_level(3, e_cat, m_b)

    tri = jnp.where(iota((chunk, chunk), 0) >= iota((chunk, chunk), 1), 1.0, 0.0).astype(BF16)
    cum = [sum(_dot(tri, part) for part in _split(logw[g], 3)) for g in groups]

    var = [seg_sum(x * x) * inv_n for x in dev]
    for lvl in range(4, levels):
        e_cat, m_b = inverse_level(lvl, e_cat, m_b)

    e_pos = [jnp.exp(x) for x in cum]
    e_neg = [jnp.exp(-x) for x in cum]
    a_t = [-kk[g] * jnp.exp(cum[g] - logw[g]) for g in groups]
    b_t = [kk[g] * a[g] * e_neg[g] for g in groups]

    for g in groups:
        yn = dev[g] * lax.rsqrt(var[g] + GN_EPS) * lnw_ref[:, cols[g]] + lnb_ref[:, cols[g]]
        rw_ref[0, :, cols[g]] = ((yn + st_bonus[g]) * st_gate[g]).astype(BF16)

    k_t = [k[g] * e_neg[g] for g in groups]
    r_t = [r[g] * e_pos[g] for g in groups]
    total = [x[chunk - 1:chunk, :] for x in e_pos]
    ar_b = [jnp.concatenate([a_t[g], r_t[g]], axis=0).astype(BF16) for g in groups]
    xb =[head_blocks(x.astype(BF16)) for x in b_t]
    xk = [head_blocks(x.astype(BF16)) for x in k_t]
    v_bd = [head_blocks(x.astype(BF16)) for x in v]

    scores = [_dot_nt(ar_b[g], jnp.concatenate([xb[g], xk[g]], axis=0)) for g in groups]
    bonus = [seg_sum(r[g] * k[g] * rk_ref[:, cols[g]]) * v[g] for g in groups]
    for g in groups:
        ecat_scr[g] = e_cat[g].astype(BF16)
        ncat_scr[g] = jnp.where(strict, scores[g][:chunk, :bd_rows], 0.0)
        arb_scr[slot, g] = ar_b[g]
        aak_scr[slot, g] = jnp.where(strict, scores[g][:chunk, bd_rows:], 0.0).astype(BF16)
        ar_scr[slot, g] = jnp.where(s_idx <= t_idx, scores[g][chunk:], 0.0).astype(BF16)
        vbd_scr[slot, g] = v_bd[g]
        vb_scr[slot, g] = v[g].astype(BF16)
        bk_scr[slot, g] = jnp.concatenate([b_t[g] * total[g], k_t[g] * total[g]], axis=0).astype(BF16)
        total_scr[slot, g], gate_scr[slot, g], bonus_scr[slot, g] = total[g], gate[g], bonus[g]

    @pl.when(last_apply)
    def _():
        for g in groups:
            folded = sum(_dot(part, fold_sel) for part in _split(s_scr[g], 3))
            sout_ref[0, cols[g], :] = folded


def _wkv(proj, shift0, s0, wp, layer, *, chunk):
    b, t, n_proj = proj.shape
    n_chunks = t // chunk
    bd_rows = HEADS_PER_GROUP * chunk
    lay = lambda *blk: pl.BlockSpec((None,) + blk, lambda s: (layer,) + (0,) * len(blk))
    vec = lay(1, RWKV_WIDTH)
    grp = lambda rows, width, dtype: pltpu.VMEM((N_GROUPS, rows, width), dtype)
    two = lambda rows, width, dtype: pltpu.VMEM((2, N_GROUPS, rows, width), dtype)
    total = b * n_chunks
    prep = lambda s: jnp.minimum(s, total - 1)
    apply = lambda s: jnp.clip(s - 2, 0, total - 1)
    return pl.pallas_call(
        functools.partial(_wkv_kernel, chunk=chunk, n_chunks=n_chunks),
        grid=(total + 2,),
        in_specs=[pl.BlockSpec((1, chunk, n_proj), lambda s: (prep(s) // n_chunks, prep(s) % n_chunks, 0)),
                  pl.BlockSpec((1, 1, RWKV_PROJ_PAD), lambda s: (prep(s) // n_chunks, 0, 0)),
                  pl.BlockSpec((1, RWKV_WIDTH, HEAD_DIM), lambda s: (apply(s) // n_chunks, 0, 0)),
                  lay(1, RWKV_PROJ_PAD), vec, lay(LANES, RWKV_WIDTH), vec, lay(LANES, RWKV_WIDTH),
                  lay(GATE_PAD, RWKV_WIDTH), vec, vec, vec, vec, vec],
        out_specs=[pl.BlockSpec((1, chunk, RWKV_WIDTH), lambda s: (apply(s) // n_chunks, apply(s) % n_chunks, 0)),
                   pl.BlockSpec((1, RWKV_WIDTH, HEAD_DIM), lambda s: (apply(s) // n_chunks, 0, 0))],
        out_shape=[jax.ShapeDtypeStruct((b, t, RWKV_WIDTH), BF16),
                   jax.ShapeDtypeStruct((b, RWKV_WIDTH, HEAD_DIM), F32)],
        scratch_shapes=[pltpu.VMEM((N_GROUPS, MXU_DIM, MXU_DIM), F32),
                        pltpu.VMEM((1, RWKV_PROJ_PAD), F32),
                        two(2 * chunk, MXU_DIM, BF16), two(chunk, bd_rows, BF16), two(chunk, 2 * bd_rows, BF16),
                        two(bd_rows, MXU_DIM, BF16), two(chunk, MXU_DIM, BF16), two(2 * chunk, MXU_DIM, BF16),
                        two(1, MXU_DIM, F32), two(chunk, MXU_DIM, F32), two(chunk, MXU_DIM, F32),
                        grp(chunk, bd_rows, F32), grp(chunk, bd_rows, BF16)],
        compiler_params=_params(1),
        name="wkv7",
    )(proj, shift0, s0, wp["mix"], wp["w0"], wp["decay_up"], wp["a0"], wp["iclr_up"], wp["gate_up"],
      wp["k_k"], wp["k_a"], wp["r_k"], wp["ln_w"], wp["ln_b"])


SUBLANES = 8


def _wkv_step_kernel(p_ref, shift_ref, s_ref, mix_ref, w0_ref, dup_ref, a0_ref, iup_ref, gup_ref,
                     kk_ref, ka_ref, rk_ref, lnw_ref, lnb_ref, fold_ref, rw_ref, sout_ref):
    iota = lambda shape, dim: lax.broadcasted_iota(jnp.int32, shape, dim)
    groups = range(N_GROUPS)
    cols = [slice(g * MXU_DIM, (g + 1) * MXU_DIM) for g in groups]
    ones_bd = jnp.where(iota((MXU_DIM, MXU_DIM), 0) // HEAD_DIM == iota((MXU_DIM, MXU_DIM), 1) // HEAD_DIM,
                        1.0, 0.0).astype(BF16)

    def seg_sum(x):
        return jnp.concatenate([_dot(x[:, cols[g]].astype(BF16), ones_bd) for g in groups], axis=1)

    cur = jnp.broadcast_to(p_ref[0], (SUBLANES, RWKV_PROJ_PAD))
    prev = jnp.broadcast_to(shift_ref[0], (SUBLANES, RWKV_PROJ_PAD))
    p = cur + (prev - cur) * mix_ref[...]
    r = p[:, 0:RWKV_WIDTH]
    k = p[:, RWKV_WIDTH:2 * RWKV_WIDTH]
    v = p[:, 2 * RWKV_WIDTH:3 * RWKV_WIDTH]
    lora_in = p[:, LORA_OFF:LORA_OFF + LANES]
    gate_in = p[:, GATE_OFF:GATE_OFF + GATE_PAD]
    w = w0_ref[...] + _dot(jnp.tanh(lora_in).astype(BF16), dup_ref[...])
    decay = jnp.exp(-math.exp(-0.5) * jax.nn.sigmoid(w))
    a = jax.nn.sigmoid(a0_ref[...] + _dot(lora_in.astype(BF16), iup_ref[...]))
    gate = _dot(jax.nn.sigmoid(gate_in).astype(BF16), gup_ref[...])
    kk = k * kk_ref[...]
    kk = kk / jnp.maximum(jnp.sqrt(seg_sum(kk * kk)), 1e-12)
    k = k * (1.0 + (a - 1.0) * ka_ref[...])

    own = iota((N_RWKV_HEADS, RWKV_WIDTH), 0) == iota((N_RWKV_HEADS, RWKV_WIDTH), 1) // HEAD_DIM
    spread = lambda x: jnp.where(own, jnp.broadcast_to(x[0:1], (N_RWKV_HEADS, RWKV_WIDTH)), 0.0)
    stacked = jnp.concatenate([spread(decay), spread(-kk), spread(kk * a), spread(k), spread(r)], axis=0)
    per_head = sum(_dot(part, fold_ref[...]) for part in _split(stacked, 3))
    w_h, a_h, b_h, k_h, r_h = (per_head[i * N_RWKV_HEADS:(i + 1) * N_RWKV_HEADS] for i in range(5))
    rows = lambda x: jnp.concatenate(
        [jnp.broadcast_to(x[h:h + 1, :], (HEAD_DIM, HEAD_DIM)) for h in range(N_RWKV_HEADS)], axis=0)

    eye = iota((LANES, LANES), 0) == iota((LANES, LANES), 1)
    ones_cols = jnp.ones((LANES, HEAD_DIM), BF16)
    v_col = jnp.concatenate(
        [sum(_dot(part, ones_cols) for part in
             _split(jnp.where(eye, jnp.broadcast_to(v[0:1, t * LANES:(t + 1) * LANES], (LANES, LANES)), 0.0), 3))
         for t in range(RWKV_WIDTH // LANES)], axis=0)

    s = s_ref[0]
    sa = jnp.sum(s * rows(a_h), axis=-1, keepdims=True)
    s_new = s * rows(w_h) + sa * rows(b_h) + v_col * rows(k_h)
    sout_ref[0] = s_new
    y_heads = _dot_nt(r_h.astype(BF16), s_new.astype(BF16))
    y = jnp.broadcast_to(jnp.sum(jnp.where(own, y_heads, 0.0), axis=0, keepdims=True), (SUBLANES, RWKV_WIDTH))

    inv_n = 1.0 / HEAD_DIM
    dev = y - seg_sum(y) * inv_n
    var = seg_sum(dev * dev) * inv_n
    y = dev * lax.rsqrt(var + GN_EPS) * lnw_ref[...] + lnb_ref[...]
    bonus = seg_sum(r * k * rk_ref[...]) * v
    rw_ref[0] = ((y + bonus) * gate)[0:1].astype(BF16)


def _wkv_step(prw_rows, shift_rows, state, wp, layer):
    s = prw_rows.shape[0]
    lay = lambda *blk: pl.BlockSpec((None,) + blk, lambda i: (layer,) + (0,) * len(blk))
    vec = lay(1, RWKV_WIDTH)
    fold = np.tile(np.eye(HEAD_DIM, dtype=np.float32), (N_RWKV_HEADS, 1))
    return pl.pallas_call(
        _wkv_step_kernel,
        grid=(s,),
        in_specs=[pl.BlockSpec((1, 1, RWKV_PROJ_PAD), lambda i: (i, 0, 0)),
                  pl.BlockSpec((None, 1, 1, RWKV_PROJ_PAD), lambda i: (layer, i, 0, 0)),
                  pl.BlockSpec((None, 1, RWKV_WIDTH, HEAD_DIM), lambda i: (layer, i, 0, 0)),
                  lay(1, RWKV_PROJ_PAD), vec, lay(LANES, RWKV_WIDTH), vec, lay(LANES, RWKV_WIDTH),
                  lay(GATE_PAD, RWKV_WIDTH), vec, vec, vec, vec, vec,
                  pl.BlockSpec((RWKV_WIDTH, HEAD_DIM), lambda i: (0, 0))],
        out_specs=[pl.BlockSpec((1, 1, RWKV_WIDTH), lambda i: (i, 0, 0)),
                   pl.BlockSpec((1, RWKV_WIDTH, HEAD_DIM), lambda i: (i, 0, 0))],
        out_shape=[jax.ShapeDtypeStruct((s, 1, RWKV_WIDTH), BF16),
                   jax.ShapeDtypeStruct((s, RWKV_WIDTH, HEAD_DIM), F32)],
        compiler_params=_params(1),
        name="wkv7_step",
    )(prw_rows, shift_rows, state, wp["mix"], wp["w0"], wp["decay_up"], wp["a0"], wp["iclr_up"], wp["gate_up"],
      wp["k_k"], wp["k_a"], wp["r_k"], wp["ln_w"], wp["ln_b"], jnp.asarray(fold, BF16))


def _out_proj_kernel(att_ref, rw_ref, x_ref, gt_ref, sh_ref, sc_ref, g_ref, w_ref, x1_ref, h_ref, *, n_sample,
                     is_sample):
    rows = functools.partial(_mod_rows, n_sample=n_sample, is_sample=is_sample)
    y = _dot(att_ref[0], w_ref[0:ATTN_WIDTH, :]) + _dot(rw_ref[0], w_ref[ATTN_WIDTH:, :])
    x1 = x_ref[0] + rows(gt_ref) * y
    x1_ref[0] = x1
    h_ref[0] = _rmsnorm_mod(x1, g_ref[...], rows(sh_ref), rows(sc_ref)).astype(BF16)


def _out_proj(att, rw, x, mod, g, w, layer, *, tm, n_sample, is_sample):
    b, t, d = x.shape
    row = lambda width: pl.BlockSpec((1, tm, width), lambda i, m: (i, m, 0))
    return pl.pallas_call(
        functools.partial(_out_proj_kernel, n_sample=n_sample, is_sample=is_sample),
        grid=(b, t // tm),
        in_specs=[row(ATTN_WIDTH), row(RWKV_WIDTH), row(d),
                  _mod_spec(mod, layer, MOD_GATE1), _mod_spec(mod, layer, MOD_SHIFT2), _mod_spec(mod, layer, MOD_SCALE2),
                  pl.BlockSpec((None, 1, d), lambda i, m: (layer, 0, 0)),
                  pl.BlockSpec((None, d, d), lambda i, m: (layer, 0, 0))],
        out_specs=[row(d), row(d)],
        out_shape=[jax.ShapeDtypeStruct((b, t, d), F32), jax.ShapeDtypeStruct((b, t, d), BF16)],
        compiler_params=_params(2),
        name="out_proj",
    )(att, rw, x, mod, mod, mod, g, w)


def _ffn_kernel(h_ref, x_ref, gt_ref, wg_ref, wu_ref, wo_ref, gf_ref, o_ref, *rest, final_norm, n_sample,
                is_sample, cast_weights):
    j = pl.program_id(2)
    acc_ref = rest[-1]

    @pl.when(j == 0)
    def _():
        acc_ref[...] = jnp.zeros_like(acc_ref)

    wg, wu, wo = wg_ref[...], wu_ref[...], wo_ref[...]
    if cast_weights:
        wg, wu, wo = wg.astype(BF16), wu.astype(BF16), wo.astype(BF16)
        rest[0][...], rest[1][...], rest[2][...] = wg, wu, wo

    h = h_ref[0]
    gate = _dot(h, wg)
    up = _dot(h, wu)
    act = (gate * jax.nn.sigmoid(gate) * up).astype(BF16)
    acc_ref[...] += _dot(act, wo)

    @pl.when(j == pl.num_programs(2) - 1)
    def _():
        x2 = x_ref[0] + _mod_rows(gt_ref, n_sample, is_sample) * acc_ref[...]
        if final_norm:
            ms = jnp.mean(x2 * x2, axis=-1, keepdims=True)
            x2 = x2 * lax.rsqrt(ms + RMS_EPS) * gf_ref[...]
        o_ref[0] = x2


def _ffn(h, x, mod, w_gate, w_up, w_out, g_final, layer, *, tm, th, final_norm, n_sample, is_sample, cast_weights):
    b, t, d = x.shape
    n_h = FFN_HIDDEN // th
    w_layer, up_off = (layer, n_h) if cast_weights else (0, 0)
    col_tile = lambda off: pl.BlockSpec((None, d, th), lambda i, m, j: (w_layer, 0, j + off))
    row_tile = pl.BlockSpec((None, th, d), lambda i, m, j: (w_layer, j, 0))
    out_specs = [pl.BlockSpec((1, tm, d), lambda i, m, j: (i, m, 0))]
    out_shape = [jax.ShapeDtypeStruct((b, t, d), F32)]
    if cast_weights:
        assert t == tm and b == 1, "every weight tile must be visited exactly once"
        out_specs += [pl.BlockSpec((None, d, th), lambda i, m, j: (0, 0, j))] * 2
        out_specs += [pl.BlockSpec((None, th, d), lambda i, m, j: (0, j, 0))]
        out_shape += [jax.ShapeDtypeStruct((1, d, FFN_HIDDEN), BF16)] * 2 + [jax.ShapeDtypeStruct((1, FFN_HIDDEN, d), BF16)]
    return pl.pallas_call(
        functools.partial(_ffn_kernel, final_norm=final_norm, n_sample=n_sample, is_sample=is_sample,
                          cast_weights=cast_weights),
        grid=(b, t // tm, n_h),
        in_specs=[pl.BlockSpec((1, tm, d), lambda i, m, j: (i, m, 0)),
                  pl.BlockSpec((1, tm, d), lambda i, m, j: (i, m, 0)),
                  _mod_spec(mod, layer, MOD_GATE2),
                  col_tile(0), col_tile(up_off), row_tile,
                  pl.BlockSpec((1, d), lambda i, m, j: (0, 0))],
        out_specs=out_specs,
        out_shape=out_shape,
        scratch_shapes=[pltpu.VMEM((tm, d), F32)],
        compiler_params=_params(3),
        name="ffn",
    )(h, x, mod, w_gate, w_up, w_out, g_final)


def _tiles(t_prompt, n_sample):
    tiles = dict(proj_tm=1024, proj_tn=1024, out_tm=512, ffn_tm=512, ffn_th=512, wkv_chunk=64)
    assert t_prompt % tiles["proj_tm"] == 0 and t_prompt % tiles["out_tm"] == 0 and t_prompt % tiles["ffn_tm"] == 0
    assert t_prompt % tiles["wkv_chunk"] == 0 and t_prompt % WINDOW == 0
    assert (QKV_COLS + RWKV_PROJ_PAD) % tiles["proj_tn"] == 0
    assert FFN_HIDDEN % tiles["ffn_th"] == 0 and n_sample % SUBLANES == 0
    return tiles


def _pad_cols(x, width):
    return jnp.pad(x, [(0, 0)] * (x.ndim - 1) + [(0, width - x.shape[-1])])


def _pad_rows(x, rows, offset=0):
    return jnp.pad(x, [(0, 0), (offset, rows - offset - x.shape[1]), (0, 0)])


def kernel(x_prompt, x_sample, cache_k, cache_v, state_wkv, state_shift, c_prompt, c_sample, w_ada, b_ada, g_norm_mix, g_norm_ffn, w_in, w_out, attn_sinks, mix_shift, decay_w0, decay_up, iclr_a0, iclr_up, gate_up, k_k, k_a, r_k, ln_x_w, ln_x_b, w_ffn_in, w_ffn_out, g_norm_final):
    n_layers = w_in.shape[0]
    bp, tp, d = x_prompt.shape
    bs = x_sample.shape[0]
    tiles = _tiles(tp, bs)

    w_in_t = jnp.pad(jnp.swapaxes(w_in, 1, 2).astype(BF16),
                     [(0, 0), (0, QKV_COLS + RWKV_PROJ_PAD - w_in.shape[2]), (0, 0)])
    w_out_b = w_out.astype(BF16)
    vec = lambda x: x.reshape(n_layers, 1, -1)
    wp = dict(
        mix=vec(_pad_cols(mix_shift, RWKV_PROJ_PAD)), w0=vec(decay_w0), a0=vec(iclr_a0),
        decay_up=_pad_rows(decay_up, LANES).astype(BF16),
        iclr_up=_pad_rows(iclr_up, LANES, D_DECAY_LORA).astype(BF16),
        gate_up=_pad_rows(gate_up, GATE_PAD).astype(BF16),
        k_k=vec(k_k), k_a=vec(k_a), r_k=vec(r_k), ln_w=vec(ln_x_w), ln_b=vec(ln_x_b))
    g_mix, g_ffn = vec(g_norm_mix), vec(g_norm_ffn)
    g_final = g_norm_final.reshape(1, d)
    sinks_flat = attn_sinks.reshape(-1)
    sinks_col = attn_sinks.reshape(n_layers, N_Q_HEADS, 1)
    cache_k2 = cache_k.reshape(n_layers, bs, WINDOW, KV_WIDTH)
    cache_v2 = cache_v.reshape(n_layers, bs, WINDOW, KV_WIDTH)
    shift_rows = _pad_cols(state_shift, RWKV_PROJ_PAD).reshape(n_layers, bs, 1, RWKV_PROJ_PAD)
    state_rows = state_wkv.reshape(n_layers, bs, RWKV_WIDTH, HEAD_DIM)
    zero_shift = jnp.zeros((bp, 1, RWKV_PROJ_PAD), F32)
    zero_state = jnp.zeros((bp, RWKV_WIDTH, HEAD_DIM), F32)

    mod = _ada(jnp.concatenate([c_sample, c_prompt], axis=0), w_ada, b_ada)
    prompt = dict(n_sample=bs, is_sample=False)
    sample = dict(n_sample=bs, is_sample=True)

    xp = x_prompt
    xs = x_sample.reshape(1, bs, d)
    outs_p = [[] for _ in range(4)]
    outs_s = [[] for _ in range(4)]
    for layer in range(n_layers):
        last = layer == n_layers - 1

        proj_s = _norm_proj(xs, mod, g_mix, w_in_t, layer, tm=bs, tn=tiles["proj_tn"], **sample)[0]
        att_s, nk, nv = _attn_sample(proj_s[:, :QKV_COLS].reshape(bs, 1, QKV_COLS), cache_k2, cache_v2, sinks_col, layer)
        prw_tok = proj_s[:, QKV_COLS:].reshape(bs, 1, RWKV_PROJ_PAD)
        rw_s, state_s = _wkv_step(prw_tok, shift_rows, state_rows, wp, layer)
        x1_s, h2_s = _out_proj(att_s.reshape(1, bs, ATTN_WIDTH), rw_s.reshape(1, bs, RWKV_WIDTH), xs,
                               mod, g_ffn, w_out_b, layer, tm=bs, **sample)
        xs, w_gate_b, w_up_b, w_down_b = _ffn(h2_s, x1_s, mod, w_ffn_in, w_ffn_in, w_ffn_out, g_final, layer, tm=bs,
                                              th=tiles["ffn_th"], final_norm=last, cast_weights=True, **sample)
        outs_s[0].append(nk.reshape(bs, WINDOW, N_KV_HEADS, HEAD_DIM))
        outs_s[1].append(nv.reshape(bs, WINDOW, N_KV_HEADS, HEAD_DIM))
        outs_s[2].append(state_s.reshape(bs, N_RWKV_HEADS, HEAD_DIM, HEAD_DIM))
        outs_s[3].append(prw_tok[:, 0, :RWKV_PROJ])

        proj = _norm_proj(xp, mod, g_mix, w_in_t, layer, tm=tiles["proj_tm"], tn=tiles["proj_tn"], **prompt)
        att = _attn_prompt(proj, sinks_flat, layer)
        rw, state = _wkv(proj, zero_shift, zero_state, wp, layer, chunk=tiles["wkv_chunk"])
        x1, h2 = _out_proj(att, rw, xp, mod, g_ffn, w_out_b, layer, tm=tiles["out_tm"], **prompt)
        xp = _ffn(h2, x1, mod, w_gate_b, w_up_b, w_down_b, g_final, layer, tm=tiles["ffn_tm"], th=tiles["ffn_th"],
                  final_norm=last, cast_weights=False, **prompt)[0]
        new_kv = proj[:, tp - WINDOW:, ATTN_WIDTH:QKV_COLS]
        outs_p[0].append(new_kv[:, :, :KV_WIDTH].reshape(bp, WINDOW, N_KV_HEADS, HEAD_DIM))
        outs_p[1].append(new_kv[:, :, KV_WIDTH:].reshape(bp, WINDOW, N_KV_HEADS, HEAD_DIM))
        outs_p[2].append(state.reshape(bp, N_RWKV_HEADS, HEAD_DIM, HEAD_DIM))
        outs_p[3].append(proj[:, tp - 1, QKV_COLS:QKV_COLS + RWKV_PROJ])

    stack = lambda xs_: jnp.stack(xs_, axis=0)
    return (xp, xs.reshape(bs, 1, d),
            stack(outs_p[0]), stack(outs_p[1]), stack(outs_p[2]), stack(outs_p[3]),
            stack(outs_s[0]), stack(outs_s[1]), stack(outs_s[2]), stack(outs_s[3]))
```

```python
import functools
import math

import numpy as np
import jax
import jax.numpy as jnp
from jax import lax
from jax.experimental import pallas as pl
from jax.experimental.pallas import tpu as pltpu

F32 = jnp.float32
BF16 = jnp.bfloat16

D_MODEL = 2048
HEAD_DIM = 64
ATTN_WIDTH = D_MODEL // 2
RWKV_WIDTH = D_MODEL - ATTN_WIDTH
N_Q_HEADS = ATTN_WIDTH // HEAD_DIM
N_KV_HEADS = N_Q_HEADS // 4
GQA_REP = N_Q_HEADS // N_KV_HEADS
KV_WIDTH = N_KV_HEADS * HEAD_DIM
N_RWKV_HEADS = RWKV_WIDTH // HEAD_DIM
WINDOW = 128
D_DECAY_LORA = 64
D_ICLR_LORA = 64
D_GATE_LORA = 160
RWKV_PROJ = 3 * RWKV_WIDTH + D_DECAY_LORA + D_ICLR_LORA + D_GATE_LORA
QKV_COLS = ATTN_WIDTH + 2 * KV_WIDTH
FFN_HIDDEN = 5632
RMS_EPS = 1e-5
GN_EPS = 64e-5
NEG_INF = -1e30

LANES = 128
MXU_DIM = 256
VMEM_LIMIT_BYTES = 56 * 1024 * 1024

LORA_OFF = 3 * RWKV_WIDTH
GATE_OFF = LORA_OFF + D_DECAY_LORA + D_ICLR_LORA
GATE_PAD = 2 * LANES
RWKV_PROJ_PAD = 3584
HEADS_PER_GROUP = MXU_DIM // HEAD_DIM
N_GROUPS = N_RWKV_HEADS // HEADS_PER_GROUP

ALIBI_SLOPES = tuple(float(s) for s in np.exp2(-8.0 * np.arange(1, N_Q_HEADS + 1, dtype=np.float32) / N_Q_HEADS))


def _params(n_axes):
    return pltpu.CompilerParams(dimension_semantics=("arbitrary",) * n_axes,
                                vmem_limit_bytes=VMEM_LIMIT_BYTES)


def _dot(a, b):
    return jnp.dot(a, b, preferred_element_type=F32)


def _dot_nt(a, b):
    return lax.dot_general(a, b, (((1,), (1,)), ((), ())), preferred_element_type=F32)


def _dot_tn(a, b):
    return lax.dot_general(a, b, (((0,), (0,)), ((), ())), preferred_element_type=F32)


def _split(x, parts):
    out = []
    for _ in range(parts):
        p = x.astype(BF16)
        out.append(p)
        x = x - p.astype(F32)
    return out


def _rmsnorm_mod(x, g, shift, scale):
    ms = jnp.mean(x * x, axis=-1, keepdims=True)
    y = x * lax.rsqrt(ms + RMS_EPS) * g
    return y * (1.0 + scale) + shift


def _ada_kernel(c_ref, w_ref, b_ref, o_ref):
    c = c_ref[...]
    s = (c * jax.nn.sigmoid(c)).astype(BF16)
    o_ref[...] = _dot(s, w_ref[...].astype(BF16)) + b_ref[...]


def _ada(c_all, w_ada, b_ada):
    n_layers, d, n = w_ada.shape
    rows = c_all.shape[0]
    tn = 1024
    return pl.pallas_call(
        _ada_kernel,
        grid=(n_layers, n // tn),
        in_specs=[pl.BlockSpec((rows, d), lambda l, j: (0, 0)),
                  pl.BlockSpec((None, d, tn), lambda l, j: (l, 0, j)),
                  pl.BlockSpec((None, 1, tn), lambda l, j: (l, 0, j))],
        out_specs=pl.BlockSpec((None, rows, tn), lambda l, j: (l, 0, j)),
        out_shape=jax.ShapeDtypeStruct((n_layers, rows, n), F32),
        compiler_params=_params(2),
        name="ada_mod",
    )(c_all, w_ada, b_ada.reshape(n_layers, 1, n))


NORM_STRIP = 32


MOD_SHIFT1, MOD_SCALE1, MOD_GATE1, MOD_SHIFT2, MOD_SCALE2, MOD_GATE2 = range(6)


def _mod_spec(mod, layer, part):
    rows, d = mod.shape[1], mod.shape[2] // 6
    return pl.BlockSpec((None, rows, d), lambda *_: (layer, 0, part))


def _mod_rows(ref, n_sample, is_sample):
    if is_sample:
        return ref[0:n_sample, :]
    return ref[pl.ds(n_sample + pl.program_id(0), 1), :]


def _norm_proj_kernel(x_ref, sh_ref, sc_ref, g_ref, w_ref, o_ref, h_scr, *, n_sample, is_sample):
    j = pl.program_id(2)
    tm = h_scr.shape[0]

    @pl.when(j == 0)
    def _():
        gain = g_ref[...] * (1.0 + _mod_rows(sc_ref, n_sample, is_sample))
        shift = _mod_rows(sh_ref, n_sample, is_sample)
        if gain.shape[0] == 1:
            def strip(i, carry):
                rows = pl.ds(pl.multiple_of(i * NORM_STRIP, NORM_STRIP), NORM_STRIP)
                x = x_ref[0, rows, :]
                ms = jnp.mean(x * x, axis=-1, keepdims=True)
                h_scr[rows, :] = (x * lax.rsqrt(ms + RMS_EPS) * gain + shift).astype(BF16)
                return carry
            lax.fori_loop(0, tm // NORM_STRIP, strip, 0, unroll=4)
        else:
            x = x_ref[0]
            ms = jnp.mean(x * x, axis=-1, keepdims=True)
            h_scr[...] = (x * lax.rsqrt(ms + RMS_EPS) * gain + shift).astype(BF16)

    o_ref[0] = _dot_nt(h_scr[...], w_ref[...])


def _norm_proj(x, mod, g, w, layer, *, tm, tn, n_sample, is_sample):
    b, t, d = x.shape
    n = QKV_COLS + RWKV_PROJ_PAD
    return pl.pallas_call(
        functools.partial(_norm_proj_kernel, n_sample=n_sample, is_sample=is_sample),
        grid=(b, t // tm, n // tn),
        in_specs=[pl.BlockSpec((1, tm, d), lambda i, m, j: (i, m, 0)),
                  _mod_spec(mod, layer, MOD_SHIFT1), _mod_spec(mod, layer, MOD_SCALE1),
                  pl.BlockSpec((None, 1, d), lambda i, m, j: (layer, 0, 0)),
                  pl.BlockSpec((None, tn, d), lambda i, m, j: (layer, j, 0))],
        out_specs=pl.BlockSpec((1, tm, tn), lambda i, m, j: (i, m, j)),
        out_shape=jax.ShapeDtypeStruct((b, t, n), F32),
        scratch_shapes=[pltpu.VMEM((tm, d), BF16)],
        compiler_params=_params(3),
        name="norm_proj",
    )(x, mod, mod, g, w)


LOG2E = math.log2(math.e)


def _attn_bias_table():
    t = np.arange(WINDOW)[:, None]
    j = np.arange(2 * WINDOW)[None, :]
    dist = t + WINDOW - j
    valid = (dist >= 0) & (dist <= WINDOW)
    slopes = np.asarray(ALIBI_SLOPES, np.float32)[:, None, None]
    bias = np.where(valid[None], -slopes * dist[None].astype(np.float32) * np.float32(LOG2E), np.float32(NEG_INF))
    first = np.where((j >= WINDOW)[None], bias, np.float32(NEG_INF))
    return np.stack([first, bias]).astype(np.float32)


def _attn_prompt_kernel(sink_ref, bias_ref, q_ref, kc_ref, vc_ref, kp_ref, vp_ref, o_ref, *, layer):
    q = q_ref[0] * (HEAD_DIM ** -0.5 * LOG2E)
    kc, vc, kp, vp = kc_ref[0], vc_ref[0], kp_ref[0], vp_ref[0]
    kv_heads, heads = range(N_KV_HEADS), range(N_Q_HEADS)
    ksl = [slice(g * HEAD_DIM, (g + 1) * HEAD_DIM) for g in kv_heads]
    k_band = [jnp.concatenate([kp[:, ksl[g]], kc[:, ksl[g]]], axis=0).astype(BF16) for g in kv_heads]
    v_band = [jnp.concatenate([vp[:, ksl[g]], vc[:, ksl[g]]], axis=0).astype(BF16) for g in kv_heads]
    lg = [_dot_nt(q[:, h * HEAD_DIM:(h + 1) * HEAD_DIM].astype(BF16), k_band[h // GQA_REP]) + bias_ref[0, h]
          for h in heads]
    sink = [sink_ref[layer * N_Q_HEADS + h] * LOG2E for h in heads]
    m = [jnp.maximum(jnp.max(lg[h], axis=-1, keepdims=True), sink[h]) for h in heads]
    p = [jnp.exp2(lg[h] - m[h]).astype(BF16) for h in heads]
    ones = jnp.ones((2 * WINDOW, HEAD_DIM), BF16)
    den = [_dot(p[h], ones) + jnp.exp2(sink[h] - m[h]) for h in heads]
    outs = [_dot(p[h], v_band[h // GQA_REP]) * (1.0 / den[h]) for h in heads]
    o_ref[0] = jnp.concatenate(outs, axis=1).astype(BF16)


def _attn_prompt(qkv, sinks_flat, layer):
    b, t, _ = qkv.shape
    kcol = ATTN_WIDTH // KV_WIDTH
    prev = lambda i, m: (i, jnp.maximum(m - 1, 0), kcol)
    prev_v = lambda i, m: (i, jnp.maximum(m - 1, 0), kcol + 1)
    return pl.pallas_call(
        functools.partial(_attn_prompt_kernel, layer=layer),
        grid=(b, t // WINDOW),
        in_specs=[pl.BlockSpec(memory_space=pltpu.SMEM),
                  pl.BlockSpec((1, N_Q_HEADS, WINDOW, 2 * WINDOW), lambda i, m: (jnp.minimum(m, 1), 0, 0, 0)),
                  pl.BlockSpec((1, WINDOW, ATTN_WIDTH), lambda i, m: (i, m, 0)),
                  pl.BlockSpec((1, WINDOW, KV_WIDTH), lambda i, m: (i, m, kcol)),
                  pl.BlockSpec((1, WINDOW, KV_WIDTH), lambda i, m: (i, m, kcol + 1)),
                  pl.BlockSpec((1, WINDOW, KV_WIDTH), prev),
                  pl.BlockSpec((1, WINDOW, KV_WIDTH), prev_v)],
        out_specs=pl.BlockSpec((1, WINDOW, ATTN_WIDTH), lambda i, m: (i, m, 0)),
        out_shape=jax.ShapeDtypeStruct((b, t, ATTN_WIDTH), BF16),
        compiler_params=_params(2),
        name="attn_prompt",
    )(sinks_flat, jnp.asarray(_attn_bias_table()), qkv, qkv, qkv, qkv, qkv)


def _attn_sample_kernel(row_ref, kc_ref, vc_ref, sink_ref, slope_ref, spread_ref, gather_ref,
                        o_ref, nk_ref, nv_ref):
    seqs = range(row_ref.shape[0])
    head_row = lax.broadcasted_iota(jnp.int32, (N_Q_HEADS, ATTN_WIDTH), 0)
    head_lane = lax.broadcasted_iota(jnp.int32, (N_Q_HEADS, ATTN_WIDTH), 1) // HEAD_DIM
    own = head_row == head_lane
    grp_row = lax.broadcasted_iota(jnp.int32, (N_Q_HEADS, KV_WIDTH), 0) // GQA_REP
    grp_lane = lax.broadcasted_iota(jnp.int32, (N_Q_HEADS, KV_WIDTH), 1) // HEAD_DIM
    bias = slope_ref[...] * (WINDOW - lax.broadcasted_iota(jnp.int32, (N_Q_HEADS, WINDOW), 1)).astype(F32)
    sink = sink_ref[...]
    last = lax.broadcasted_iota(jnp.int32, (WINDOW, KV_WIDTH), 0) == WINDOW - 1
    rows = [row_ref[s] for s in seqs]
    k_new = [r[:, ATTN_WIDTH:ATTN_WIDTH + KV_WIDTH] for r in rows]
    v_new = [r[:, ATTN_WIDTH + KV_WIDTH:] for r in rows]
    kc = [kc_ref[s] for s in seqs]
    vc = [vc_ref[s] for s in seqs]
    q_rows = [jnp.where(own, jnp.broadcast_to(r[:, :ATTN_WIDTH] * (HEAD_DIM ** -0.5), (N_Q_HEADS, ATTN_WIDTH)), 0.0)
              .astype(BF16) for r in rows]
    q_grp = [_dot(x, spread_ref[...]) for x in q_rows]
    lg = [_dot_nt(q_grp[s].astype(BF16), kc[s].astype(BF16)) - bias for s in seqs]
    lg_new = [jnp.sum(q_grp[s] * k_new[s].astype(BF16).astype(F32), axis=-1, keepdims=True) for s in seqs]
    m = [jnp.maximum(jnp.maximum(jnp.max(lg[s], axis=-1, keepdims=True), lg_new[s]), sink) for s in seqs]
    p = [jnp.exp(lg[s] - m[s]) for s in seqs]
    p_new = [jnp.exp(lg_new[s] - m[s]) for s in seqs]
    den = [jnp.sum(p[s], axis=-1, keepdims=True) + p_new[s] + jnp.exp(sink - m[s]) for s in seqs]
    out = [_dot((p[s] / den[s]).astype(BF16), vc[s].astype(BF16))
           + (p_new[s] / den[s]).astype(BF16).astype(F32) * v_new[s].astype(BF16).astype(F32) for s in seqs]
    full = [_dot(jnp.where(grp_row == grp_lane, x, 0.0).astype(BF16), gather_ref[...]) for x in out]
    for s in seqs:
        o_ref[s] = jnp.sum(jnp.where(own, full[s], 0.0), axis=0, keepdims=True).astype(BF16)
        nk_ref[s] = jnp.where(last, k_new[s], pltpu.roll(kc[s], WINDOW - 1, axis=0))
        nv_ref[s] = jnp.where(last, v_new[s], pltpu.roll(vc[s], WINDOW - 1, axis=0))


def _attn_sample(qkv_rows, cache_k, cache_v, sinks_col, layer):
    s = qkv_rows.shape[0]
    spread = np.zeros((ATTN_WIDTH, KV_WIDTH), np.float32)
    for h in range(N_Q_HEADS):
        for d in range(HEAD_DIM):
            spread[h * HEAD_DIM + d, (h // GQA_REP) * HEAD_DIM + d] = 1.0
    slopes = np.asarray(ALIBI_SLOPES, np.float32).reshape(N_Q_HEADS, 1)
    per_step = SUBLANES
    win = (per_step, WINDOW, KV_WIDTH)
    return pl.pallas_call(
        _attn_sample_kernel,
        grid=(s // per_step,),
        in_specs=[pl.BlockSpec((per_step, 1, QKV_COLS), lambda i: (i, 0, 0)),
                  pl.BlockSpec((None,) + win, lambda i: (layer, i, 0, 0)),
                  pl.BlockSpec((None,) + win, lambda i: (layer, i, 0, 0)),
                  pl.BlockSpec((None, N_Q_HEADS, 1), lambda i: (layer, 0, 0)),
                  pl.BlockSpec((N_Q_HEADS, 1), lambda i: (0, 0)),
                  pl.BlockSpec((ATTN_WIDTH, KV_WIDTH), lambda i: (0, 0)),
                  pl.BlockSpec((KV_WIDTH, ATTN_WIDTH), lambda i: (0, 0))],
        out_specs=[pl.BlockSpec((per_step, 1, ATTN_WIDTH), lambda i: (i, 0, 0)),
                   pl.BlockSpec(win, lambda i: (i, 0, 0)),
                   pl.BlockSpec(win, lambda i: (i, 0, 0))],
        out_shape=[jax.ShapeDtypeStruct((s, 1, ATTN_WIDTH), BF16),
                   jax.ShapeDtypeStruct((s, WINDOW, KV_WIDTH), F32),
                   jax.ShapeDtypeStruct((s, WINDOW, KV_WIDTH), F32)],
        compiler_params=_params(1),
        name="attn_sample",
    )(qkv_rows, cache_k, cache_v, sinks_col, jnp.asarray(slopes),
      jnp.asarray(spread, BF16), jnp.asarray(spread.T, BF16))


def _wkv_kernel(p_ref, shift0_ref, s0_ref, mix_ref, w0_ref, dup_ref, a0_ref, iup_ref, gup_ref,
                kk_ref, ka_ref, rk_ref, lnw_ref, lnb_ref, rw_ref, sout_ref, s_scr, prev_scr,
                arb_scr, aak_scr, ar_scr, vbd_scr, vb_scr, bk_scr, total_scr, gate_scr, bonus_scr, ncat_scr, ecat_scr,
                *, chunk, n_chunks):
    step = pl.program_id(0)
    staged = (arb_scr, aak_scr, ar_scr, vbd_scr, vb_scr, bk_scr, total_scr, gate_scr, bonus_scr)
    bd_rows = HEADS_PER_GROUP * chunk
    groups = range(N_GROUPS)
    cols = [slice(g * MXU_DIM, (g + 1) * MXU_DIM) for g in groups]
    iota = lambda shape, dim: lax.broadcasted_iota(jnp.int32, shape, dim)
    same_head = iota((MXU_DIM, MXU_DIM), 0) // HEAD_DIM == iota((MXU_DIM, MXU_DIM), 1) // HEAD_DIM
    ones_bd = jnp.where(same_head, 1.0, 0.0).astype(BF16)
    tile_sel = jnp.where(iota((HEAD_DIM, MXU_DIM), 0) == iota((HEAD_DIM, MXU_DIM), 1) % HEAD_DIM, 1.0, 0.0).astype(BF16)
    fold_sel = jnp.where(iota((MXU_DIM, HEAD_DIM), 0) % HEAD_DIM == iota((MXU_DIM, HEAD_DIM), 1), 1.0, 0.0).astype(BF16)

    first_prep = step % n_chunks == 0
    first_apply = (step >= 2) & ((step - 2) % n_chunks == 0)
    last_apply = (step >= 2) & ((step - 2) % n_chunks == n_chunks - 1)

    @pl.when(step == 0)
    def _():
        for ref in staged + (ncat_scr, ecat_scr, s_scr):
            ref[...] = jnp.zeros_like(ref)
        total_scr[...] = jnp.ones_like(total_scr)

    @pl.when(first_prep)
    def _():
        prev_scr[...] = shift0_ref[0]

    @pl.when(first_apply)
    def _():
        for g in groups:
            rows = s0_ref[0, cols[g], :]
            tiled = sum(_dot(part, tile_sel) for part in _split(rows, 3))
            s_scr[g] = jnp.where(same_head, tiled, 0.0)

    row = iota((chunk, 1), 0)

    def shifted(lo, width):
        cur = p_ref[0, :, QKV_COLS + lo:QKV_COLS + lo + width]
        prev = jnp.where(row == 0, prev_scr[:, lo:lo + width], pltpu.roll(cur, 1, axis=0))
        return cur + (prev - cur) * mix_ref[:, lo:lo + width]

    def seg_sum(x):
        return _dot(x.astype(BF16), ones_bd)

    head_mask = [jnp.where(iota((chunk, MXU_DIM), 1) // HEAD_DIM == h, 1.0, 0.0).astype(BF16)
                 for h in range(HEADS_PER_GROUP)]

    def head_blocks(x):
        return jnp.concatenate([x * m for m in head_mask], axis=0)

    slot = step % 2
    inv_n = 1.0 / HEAD_DIM
    t_idx = iota((chunk, 2 * bd_rows), 0)
    s_idx = iota((chunk, 2 * bd_rows), 1) % chunk
    strict = (s_idx < t_idx)[:, :bd_rows]
    same_block = jnp.where(iota((bd_rows, bd_rows), 0) // chunk == iota((bd_rows, bd_rows), 1) // chunk,
                           1.0, 0.0).astype(BF16)
    eye_cat = jnp.where(t_idx == s_idx, 1.0, 0.0)[:, :bd_rows].astype(BF16)
    block_diag = lambda x: jnp.concatenate([x] * HEADS_PER_GROUP, axis=0) * same_block


    st_ar, st_ak, st_r, st_vbd, st_v, st_bk, st_total, st_gate, st_bonus = (
        [ref[slot, g] for g in groups] for ref in staged)
    st_e = [ecat_scr[g] for g in groups]
    s_b = [s_scr[g].astype(BF16) for g in groups]
    uy = [_dot_nt(st_ar[g], s_b[g]) for g in groups]

    e_cat = [ncat_scr[g] for g in groups]
    m_b = [x.astype(BF16) for x in e_cat]
    m_b = [_dot(x, block_diag(x)).astype(BF16) for x in m_b]
    levels = int(math.log2(chunk)) - 1

    def inverse_level(lvl, e_cat, m_b):
        t_bd = [block_diag(x.astype(BF16) + eye_cat) for x in e_cat]
        if lvl < levels - 1:
            both = [_dot(m_b[g], jnp.concatenate([t_bd[g], block_diag(m_b[g])], axis=1)) for g in groups]
            return ([e_cat[g] + both[g][:, :bd_rows] for g in groups], [x[:, bd_rows:].astype(BF16) for x in both])
        return [e_cat[g] + _dot(m_b[g], t_bd[g]) for g in groups], None

    lora_in = shifted(LORA_OFF, LANES)
    lora_tanh = jnp.tanh(lora_in).astype(BF16)
    lora_lin = lora_in.astype(BF16)
    gate_sig = jax.nn.sigmoid(shifted(GATE_OFF, GATE_PAD)).astype(BF16)
    r = [shifted(g * MXU_DIM, MXU_DIM) for g in groups]

    rhs = [uy[g][:chunk] + _dot(st_ak[g], st_vbd[g]) for g in groups]
    e_cat, m_b = inverse_level(0, e_cat, m_b)

    k = [shifted(RWKV_WIDTH + g * MXU_DIM, MXU_DIM) for g in groups]
    v = [shifted(2 * RWKV_WIDTH + g * MXU_DIM, MXU_DIM) for g in groups]
    prev_scr[...] = p_ref[0, chunk - 1:chunk, QKV_COLS:]

    p = [rhs[g] + _dot(st_e[g], head_blocks(rhs[g].astype(BF16))) for g in groups]
    p_b = [x.astype(BF16) for x in p]
    if levels > 1:
        e_cat, m_b = inverse_level(1, e_cat, m_b)

    logw = [-math.exp(-0.5) * jax.nn.sigmoid(w0_ref[:, cols[g]] + _dot(lora_tanh, dup_ref[:, cols[g]])) for g in groups]
    a = [jax.nn.sigmoid(a0_ref[:, cols[g]] + _dot(lora_lin, iup_ref[:, cols[g]])) for g in groups]
    gate = [_dot(gate_sig, gup_ref[:, cols[g]]) for g in groups]

    y = [uy[g][chunk:] + _dot(st_r[g], jnp.concatenate([head_blocks(p_b[g]), st_vbd[g]], axis=0)) for g in groups]
    for g in groups:
        upd = _dot_tn(jnp.concatenate([p_b[g], st_v[g]], axis=0), st_bk[g])
        s_scr[g] = s_scr[g] * st_total[g] + jnp.where(same_head, upd, 0.0)
    if levels > 2:
        e_cat, m_b = inverse_level(2, e_cat, m_b)

    kk = [k[g] * kk_ref[:, cols[g]] for g in groups]
    kk = [kk[g] / jnp.maximum(jnp.sqrt(seg_sum(kk[g] * kk[g])), 1e-12) for g in groups]
    k = [k[g] * (1.0 + (a[g] - 1.0) * ka_ref[:, cols[g]]) for g in groups]

    dev = [y[g] - seg_sum(y[g]) * inv_n for g in groups]
    if levels > 3:
        e_cat, m_b = inverse_level(3, e_cat, m_b)

    tri = jnp.where(iota((chunk, chunk), 0) >= iota((chunk, chunk), 1), 1.0, 0.0).astype(BF16)
    cum = [sum(_dot(tri, part) for part in _split(logw[g], 3)) for g in groups]

    var = [seg_sum(x * x) * inv_n for x in dev]
    for lvl in range(4, levels):
        e_cat, m_b = inverse_level(lvl, e_cat, m_b)

    e_pos = [jnp.exp(x) for x in cum]
    e_neg = [jnp.exp(-x) for x in cum]
    a_t = [-kk[g] * jnp.exp(cum[g] - logw[g]) for g in groups]
    b_t = [kk[g] * a[g] * e_neg[g] for g in groups]

    for g in groups:
        yn = dev[g] * lax.rsqrt(var[g] + GN_EPS) * lnw_ref[:, cols[g]] + lnb_ref[:, cols[g]]
        rw_ref[0, :, cols[g]] = ((yn + st_bonus[g]) * st_gate[g]).astype(BF16)

    k_t = [k[g] * e_neg[g] for g in groups]
    r_t = [r[g] * e_pos[g] for g in groups]
    total = [x[chunk - 1:chunk, :] for x in e_pos]
    ar_b = [jnp.concatenate([a_t[g], r_t[g]], axis=0).astype(BF16) for g in groups]
    xb =[head_blocks(x.astype(BF16)) for x in b_t]
    xk = [head_blocks(x.astype(BF16)) for x in k_t]
    v_bd = [head_blocks(x.astype(BF16)) for x in v]

    scores = [_dot_nt(ar_b[g], jnp.concatenate([xb[g], xk[g]], axis=0)) for g in groups]
    bonus = [seg_sum(r[g] * k[g] * rk_ref[:, cols[g]]) * v[g] for g in groups]
    for g in groups:
        ecat_scr[g] = e_cat[g].astype(BF16)
        ncat_scr[g] = jnp.where(strict, scores[g][:chunk, :bd_rows], 0.0)
        arb_scr[slot, g] = ar_b[g]
        aak_scr[slot, g] = jnp.where(strict, scores[g][:chunk, bd_rows:], 0.0).astype(BF16)
        ar_scr[slot, g] = jnp.where(s_idx <= t_idx, scores[g][chunk:], 0.0).astype(BF16)
        vbd_scr[slot, g] = v_bd[g]
        vb_scr[slot, g] = v[g].astype(BF16)
        bk_scr[slot, g] = jnp.concatenate([b_t[g] * total[g], k_t[g] * total[g]], axis=0).astype(BF16)
        total_scr[slot, g], gate_scr[slot, g], bonus_scr[slot, g] = total[g], gate[g], bonus[g]

    @pl.when(last_apply)
    def _():
        for g in groups:
            folded = sum(_dot(part, fold_sel) for part in _split(s_scr[g], 3))
            sout_ref[0, cols[g], :] = folded


def _wkv(proj, shift0, s0, wp, layer, *, chunk):
    b, t, n_proj = proj.shape
    n_chunks = t // chunk
    bd_rows = HEADS_PER_GROUP * chunk
    lay = lambda *blk: pl.BlockSpec((None,) + blk, lambda s: (layer,) + (0,) * len(blk))
    vec = lay(1, RWKV_WIDTH)
    grp = lambda rows, width, dtype: pltpu.VMEM((N_GROUPS, rows, width), dtype)
    two = lambda rows, width, dtype: pltpu.VMEM((2, N_GROUPS, rows, width), dtype)
    total = b * n_chunks
    prep = lambda s: jnp.minimum(s, total - 1)
    apply = lambda s: jnp.clip(s - 2, 0, total - 1)
    return pl.pallas_call(
        functools.partial(_wkv_kernel, chunk=chunk, n_chunks=n_chunks),
        grid=(total + 2,),
        in_specs=[pl.BlockSpec((1, chunk, n_proj), lambda s: (prep(s) // n_chunks, prep(s) % n_chunks, 0)),
                  pl.BlockSpec((1, 1, RWKV_PROJ_PAD), lambda s: (prep(s) // n_chunks, 0, 0)),
                  pl.BlockSpec((1, RWKV_WIDTH, HEAD_DIM), lambda s: (apply(s) // n_chunks, 0, 0)),
                  lay(1, RWKV_PROJ_PAD), vec, lay(LANES, RWKV_WIDTH), vec, lay(LANES, RWKV_WIDTH),
                  lay(GATE_PAD, RWKV_WIDTH), vec, vec, vec, vec, vec],
        out_specs=[pl.BlockSpec((1, chunk, RWKV_WIDTH), lambda s: (apply(s) // n_chunks, apply(s) % n_chunks, 0)),
                   pl.BlockSpec((1, RWKV_WIDTH, HEAD_DIM), lambda s: (apply(s) // n_chunks, 0, 0))],
        out_shape=[jax.ShapeDtypeStruct((b, t, RWKV_WIDTH), BF16),
                   jax.ShapeDtypeStruct((b, RWKV_WIDTH, HEAD_DIM), F32)],
        scratch_shapes=[pltpu.VMEM((N_GROUPS, MXU_DIM, MXU_DIM), F32),
                        pltpu.VMEM((1, RWKV_PROJ_PAD), F32),
                        two(2 * chunk, MXU_DIM, BF16), two(chunk, bd_rows, BF16), two(chunk, 2 * bd_rows, BF16),
                        two(bd_rows, MXU_DIM, BF16), two(chunk, MXU_DIM, BF16), two(2 * chunk, MXU_DIM, BF16),
                        two(1, MXU_DIM, F32), two(chunk, MXU_DIM, F32), two(chunk, MXU_DIM, F32),
                        grp(chunk, bd_rows, F32), grp(chunk, bd_rows, BF16)],
        compiler_params=_params(1),
        name="wkv7",
    )(proj, shift0, s0, wp["mix"], wp["w0"], wp["decay_up"], wp["a0"], wp["iclr_up"], wp["gate_up"],
      wp["k_k"], wp["k_a"], wp["r_k"], wp["ln_w"], wp["ln_b"])


SUBLANES = 8


def _wkv_step_kernel(p_ref, shift_ref, s_ref, mix_ref, w0_ref, dup_ref, a0_ref, iup_ref, gup_ref,
                     kk_ref, ka_ref, rk_ref, lnw_ref, lnb_ref, fold_ref, rw_ref, sout_ref):
    iota = lambda shape, dim: lax.broadcasted_iota(jnp.int32, shape, dim)
    groups = range(N_GROUPS)
    seqs = range(p_ref.shape[0])
    cols = [slice(g * MXU_DIM, (g + 1) * MXU_DIM) for g in groups]
    ones_bd = jnp.where(iota((MXU_DIM, MXU_DIM), 0) // HEAD_DIM == iota((MXU_DIM, MXU_DIM), 1) // HEAD_DIM,
                        1.0, 0.0).astype(BF16)

    def seg_sum(x):
        return jnp.concatenate([_dot(x[:, cols[g]].astype(BF16), ones_bd) for g in groups], axis=1)

    cur = p_ref[...]
    p = cur + (shift_ref[...] - cur) * mix_ref[...]
    r = p[:, 0:RWKV_WIDTH]
    k = p[:, RWKV_WIDTH:2 * RWKV_WIDTH]
    v = p[:, 2 * RWKV_WIDTH:3 * RWKV_WIDTH]
    lora_in = p[:, LORA_OFF:LORA_OFF + LANES]
    gate_in = p[:, GATE_OFF:GATE_OFF + GATE_PAD]
    w = w0_ref[...] + _dot(jnp.tanh(lora_in).astype(BF16), dup_ref[...])
    decay = jnp.exp(-math.exp(-0.5) * jax.nn.sigmoid(w))
    a = jax.nn.sigmoid(a0_ref[...] + _dot(lora_in.astype(BF16), iup_ref[...]))
    gate = _dot(jax.nn.sigmoid(gate_in).astype(BF16), gup_ref[...])
    kk = k * kk_ref[...]
    kk = kk / jnp.maximum(jnp.sqrt(seg_sum(kk * kk)), 1e-12)
    k = k * (1.0 + (a - 1.0) * ka_ref[...])
    neg_kk, kk_a = -kk, kk * a

    own = iota((N_RWKV_HEADS, RWKV_WIDTH), 0) == iota((N_RWKV_HEADS, RWKV_WIDTH), 1) // HEAD_DIM
    spread = lambda x, s: jnp.where(own, jnp.broadcast_to(x[s:s + 1], (N_RWKV_HEADS, RWKV_WIDTH)), 0.0)
    stacked = [jnp.concatenate([spread(x, s) for x in (decay, neg_kk, kk_a, k, r)], axis=0) for s in seqs]
    per_head = [sum(_dot(part, fold_ref[...]) for part in _split(x, 3)) for x in stacked]
    head = lambda s, i: per_head[s][i * N_RWKV_HEADS:(i + 1) * N_RWKV_HEADS]
    rows = lambda x: jnp.concatenate(
        [jnp.broadcast_to(x[h:h + 1, :], (HEAD_DIM, HEAD_DIM)) for h in range(N_RWKV_HEADS)], axis=0)

    eye = iota((LANES, LANES), 0) == iota((LANES, LANES), 1)
    ones_cols = jnp.ones((LANES, HEAD_DIM), BF16)
    v_col = [jnp.concatenate(
        [sum(_dot(part, ones_cols) for part in
             _split(jnp.where(eye, jnp.broadcast_to(v[s:s + 1, t * LANES:(t + 1) * LANES], (LANES, LANES)), 0.0), 3))
         for t in range(RWKV_WIDTH // LANES)], axis=0) for s in seqs]

    state = [s_ref[s] for s in seqs]
    sa = [jnp.sum(state[s] * rows(head(s, 1)), axis=-1, keepdims=True) for s in seqs]
    s_new = [state[s] * rows(head(s, 0)) + sa[s] * rows(head(s, 2)) + v_col[s] * rows(head(s, 3)) for s in seqs]
    for s in seqs:
        sout_ref[s] = s_new[s]
    y_heads = [_dot_nt(head(s, 4).astype(BF16), s_new[s].astype(BF16)) for s in seqs]
    y = jnp.concatenate([jnp.sum(jnp.where(own, x, 0.0), axis=0, keepdims=True) for x in y_heads], axis=0)

    inv_n = 1.0 / HEAD_DIM
    dev = y - seg_sum(y) * inv_n
    var = seg_sum(dev * dev) * inv_n
    y = dev * lax.rsqrt(var + GN_EPS) * lnw_ref[...] + lnb_ref[...]
    bonus = seg_sum(r * k * rk_ref[...]) * v
    rw_ref[...] = (y + bonus) * gate


def _wkv_step(prw_rows, shift_rows, state, wp, layer):
    s = prw_rows.shape[0]
    per_step = SUBLANES
    lay = lambda *blk: pl.BlockSpec((None,) + blk, lambda i: (layer,) + (0,) * len(blk))
    vec = lay(1, RWKV_WIDTH)
    fold = np.tile(np.eye(HEAD_DIM, dtype=np.float32), (N_RWKV_HEADS, 1))
    return pl.pallas_call(
        _wkv_step_kernel,
        grid=(s // per_step,),
        in_specs=[pl.BlockSpec((per_step, RWKV_PROJ_PAD), lambda i: (i, 0)),
                  pl.BlockSpec((None, per_step, RWKV_PROJ_PAD), lambda i: (layer, i, 0)),
                  pl.BlockSpec((None, per_step, RWKV_WIDTH, HEAD_DIM), lambda i: (layer, i, 0, 0)),
                  lay(1, RWKV_PROJ_PAD), vec, lay(LANES, RWKV_WIDTH), vec, lay(LANES, RWKV_WIDTH),
                  lay(GATE_PAD, RWKV_WIDTH), vec, vec, vec, vec, vec,
                  pl.BlockSpec((RWKV_WIDTH, HEAD_DIM), lambda i: (0, 0))],
        out_specs=[pl.BlockSpec((per_step, RWKV_WIDTH), lambda i: (i, 0)),
                   pl.BlockSpec((per_step, RWKV_WIDTH, HEAD_DIM), lambda i: (i, 0, 0))],
        out_shape=[jax.ShapeDtypeStruct((s, RWKV_WIDTH), F32),
                   jax.ShapeDtypeStruct((s, RWKV_WIDTH, HEAD_DIM), F32)],
        compiler_params=_params(1),
        name="wkv7_step",
    )(prw_rows, shift_rows, state, wp["mix"], wp["w0"], wp["decay_up"], wp["a0"], wp["iclr_up"], wp["gate_up"],
      wp["k_k"], wp["k_a"], wp["r_k"], wp["ln_w"], wp["ln_b"], jnp.asarray(fold, BF16))


def _out_proj_kernel(att_ref, rw_ref, x_ref, gt_ref, sh_ref, sc_ref, g_ref, w_ref, x1_ref, h_ref, *, n_sample,
                     is_sample):
    rows = functools.partial(_mod_rows, n_sample=n_sample, is_sample=is_sample)
    y = _dot(att_ref[0], w_ref[0:ATTN_WIDTH, :]) + _dot(rw_ref[0], w_ref[ATTN_WIDTH:, :])
    x1 = x_ref[0] + rows(gt_ref) * y
    x1_ref[0] = x1
    h_ref[0] = _rmsnorm_mod(x1, g_ref[...], rows(sh_ref), rows(sc_ref)).astype(BF16)


def _out_proj(att, rw, x, mod, g, w, layer, *, tm, n_sample, is_sample):
    b, t, d = x.shape
    row = lambda width: pl.BlockSpec((1, tm, width), lambda i, m: (i, m, 0))
    return pl.pallas_call(
        functools.partial(_out_proj_kernel, n_sample=n_sample, is_sample=is_sample),
        grid=(b, t // tm),
        in_specs=[row(ATTN_WIDTH), row(RWKV_WIDTH), row(d),
                  _mod_spec(mod, layer, MOD_GATE1), _mod_spec(mod, layer, MOD_SHIFT2), _mod_spec(mod, layer, MOD_SCALE2),
                  pl.BlockSpec((None, 1, d), lambda i, m: (layer, 0, 0)),
                  pl.BlockSpec((None, d, d), lambda i, m: (layer, 0, 0))],
        out_specs=[row(d), row(d)],
        out_shape=[jax.ShapeDtypeStruct((b, t, d), F32), jax.ShapeDtypeStruct((b, t, d), BF16)],
        compiler_params=_params(2),
        name="out_proj",
    )(att, rw, x, mod, mod, mod, g, w)


def _ffn_kernel(h_ref, x_ref, gt_ref, wg_ref, wu_ref, wo_ref, gf_ref, o_ref, *rest, final_norm, n_sample,
                is_sample, cast_weights):
    j = pl.program_id(2)
    acc_ref = rest[-1]

    @pl.when(j == 0)
    def _():
        acc_ref[...] = jnp.zeros_like(acc_ref)

    wg, wu, wo = wg_ref[...], wu_ref[...], wo_ref[...]
    if cast_weights:
        wg, wu, wo = wg.astype(BF16), wu.astype(BF16), wo.astype(BF16)
        rest[0][...], rest[1][...], rest[2][...] = wg, wu, wo

    h = h_ref[0]
    gate = _dot(h, wg)
    up = _dot(h, wu)
    act = (gate * jax.nn.sigmoid(gate) * up).astype(BF16)
    acc_ref[...] += _dot(act, wo)

    @pl.when(j == pl.num_programs(2) - 1)
    def _():
        x2 = x_ref[0] + _mod_rows(gt_ref, n_sample, is_sample) * acc_ref[...]
        if final_norm:
            ms = jnp.mean(x2 * x2, axis=-1, keepdims=True)
            x2 = x2 * lax.rsqrt(ms + RMS_EPS) * gf_ref[...]
        o_ref[0] = x2


def _ffn(h, x, mod, w_gate, w_up, w_out, g_final, layer, *, tm, th, final_norm, n_sample, is_sample, cast_weights):
    b, t, d = x.shape
    n_h = FFN_HIDDEN // th
    w_layer, up_off = (layer, n_h) if cast_weights else (0, 0)
    col_tile = lambda off: pl.BlockSpec((None, d, th), lambda i, m, j: (w_layer, 0, j + off))
    row_tile = pl.BlockSpec((None, th, d), lambda i, m, j: (w_layer, j, 0))
    out_specs = [pl.BlockSpec((1, tm, d), lambda i, m, j: (i, m, 0))]
    out_shape = [jax.ShapeDtypeStruct((b, t, d), F32)]
    if cast_weights:
        assert t == tm and b == 1, "every weight tile must be visited exactly once"
        out_specs += [pl.BlockSpec((None, d, th), lambda i, m, j: (0, 0, j))] * 2
        out_specs += [pl.BlockSpec((None, th, d), lambda i, m, j: (0, j, 0))]
        out_shape += [jax.ShapeDtypeStruct((1, d, FFN_HIDDEN), BF16)] * 2 + [jax.ShapeDtypeStruct((1, FFN_HIDDEN, d), BF16)]
    return pl.pallas_call(
        functools.partial(_ffn_kernel, final_norm=final_norm, n_sample=n_sample, is_sample=is_sample,
                          cast_weights=cast_weights),
        grid=(b, t // tm, n_h),
        in_specs=[pl.BlockSpec((1, tm, d), lambda i, m, j: (i, m, 0)),
                  pl.BlockSpec((1, tm, d), lambda i, m, j: (i, m, 0)),
                  _mod_spec(mod, layer, MOD_GATE2),
                  col_tile(0), col_tile(up_off), row_tile,
                  pl.BlockSpec((1, d), lambda i, m, j: (0, 0))],
        out_specs=out_specs,
        out_shape=out_shape,
        scratch_shapes=[pltpu.VMEM((tm, d), F32)],
        compiler_params=_params(3),
        name="ffn",
    )(h, x, mod, w_gate, w_up, w_out, g_final)


def _tiles(t_prompt, n_sample):
    tiles = dict(proj_tm=1024, proj_tn=1024, out_tm=512, ffn_tm=512, ffn_th=512, wkv_chunk=64)
    assert t_prompt % tiles["proj_tm"] == 0 and t_prompt % tiles["out_tm"] == 0 and t_prompt % tiles["ffn_tm"] == 0
    assert t_prompt % tiles["wkv_chunk"] == 0 and t_prompt % WINDOW == 0
    assert (QKV_COLS + RWKV_PROJ_PAD) % tiles["proj_tn"] == 0
    assert FFN_HIDDEN % tiles["ffn_th"] == 0 and n_sample % SUBLANES == 0
    return tiles


def _pad_cols(x, width):
    return jnp.pad(x, [(0, 0)] * (x.ndim - 1) + [(0, width - x.shape[-1])])


def _pad_rows(x, rows, offset=0):
    return jnp.pad(x, [(0, 0), (offset, rows - offset - x.shape[1]), (0, 0)])


def kernel(x_prompt, x_sample, cache_k, cache_v, state_wkv, state_shift, c_prompt, c_sample, w_ada, b_ada, g_norm_mix, g_norm_ffn, w_in, w_out, attn_sinks, mix_shift, decay_w0, decay_up, iclr_a0, iclr_up, gate_up, k_k, k_a, r_k, ln_x_w, ln_x_b, w_ffn_in, w_ffn_out, g_norm_final):
    n_layers = w_in.shape[0]
    bp, tp, d = x_prompt.shape
    bs = x_sample.shape[0]
    tiles = _tiles(tp, bs)

    w_in_t = jnp.pad(jnp.swapaxes(w_in, 1, 2).astype(BF16),
                     [(0, 0), (0, QKV_COLS + RWKV_PROJ_PAD - w_in.shape[2]), (0, 0)])
    w_out_b = w_out.astype(BF16)
    vec = lambda x: x.reshape(n_layers, 1, -1)
    wp = dict(
        mix=vec(_pad_cols(mix_shift, RWKV_PROJ_PAD)), w0=vec(decay_w0), a0=vec(iclr_a0),
        decay_up=_pad_rows(decay_up, LANES).astype(BF16),
        iclr_up=_pad_rows(iclr_up, LANES, D_DECAY_LORA).astype(BF16),
        gate_up=_pad_rows(gate_up, GATE_PAD).astype(BF16),
        k_k=vec(k_k), k_a=vec(k_a), r_k=vec(r_k), ln_w=vec(ln_x_w), ln_b=vec(ln_x_b))
    g_mix, g_ffn = vec(g_norm_mix), vec(g_norm_ffn)
    g_final = g_norm_final.reshape(1, d)
    sinks_flat = attn_sinks.reshape(-1)
    sinks_col = attn_sinks.reshape(n_layers, N_Q_HEADS, 1)
    cache_k2 = cache_k.reshape(n_layers, bs, WINDOW, KV_WIDTH)
    cache_v2 = cache_v.reshape(n_layers, bs, WINDOW, KV_WIDTH)
    shift_rows = _pad_cols(state_shift, RWKV_PROJ_PAD)
    state_rows = state_wkv.reshape(n_layers, bs, RWKV_WIDTH, HEAD_DIM)
    zero_shift = jnp.zeros((bp, 1, RWKV_PROJ_PAD), F32)
    zero_state = jnp.zeros((bp, RWKV_WIDTH, HEAD_DIM), F32)

    mod = _ada(jnp.concatenate([c_sample, c_prompt], axis=0), w_ada, b_ada)
    prompt = dict(n_sample=bs, is_sample=False)
    sample = dict(n_sample=bs, is_sample=True)

    xp = x_prompt
    xs = x_sample.reshape(1, bs, d)
    outs_p = [[] for _ in range(4)]
    outs_s = [[] for _ in range(4)]
    for layer in range(n_layers):
        last = layer == n_layers - 1

        proj_s = _norm_proj(xs, mod, g_mix, w_in_t, layer, tm=bs, tn=tiles["proj_tn"], **sample)[0]
        att_s, nk, nv = _attn_sample(proj_s[:, :QKV_COLS].reshape(bs, 1, QKV_COLS), cache_k2, cache_v2, sinks_col, layer)
        prw_tok = proj_s[:, QKV_COLS:]
        rw_s, state_s = _wkv_step(prw_tok, shift_rows, state_rows, wp, layer)
        x1_s, h2_s = _out_proj(att_s.reshape(1, bs, ATTN_WIDTH), rw_s.astype(BF16).reshape(1, bs, RWKV_WIDTH), xs,
                               mod, g_ffn, w_out_b, layer, tm=bs, **sample)
        xs, w_gate_b, w_up_b, w_down_b = _ffn(h2_s, x1_s, mod, w_ffn_in, w_ffn_in, w_ffn_out, g_final, layer, tm=bs,
                                              th=tiles["ffn_th"], final_norm=last, cast_weights=True, **sample)
        outs_s[0].append(nk.reshape(bs, WINDOW, N_KV_HEADS, HEAD_DIM))
        outs_s[1].append(nv.reshape(bs, WINDOW, N_KV_HEADS, HEAD_DIM))
        outs_s[2].append(state_s.reshape(bs, N_RWKV_HEADS, HEAD_DIM, HEAD_DIM))
        outs_s[3].append(prw_tok[:, :RWKV_PROJ])

        proj = _norm_proj(xp, mod, g_mix, w_in_t, layer, tm=tiles["proj_tm"], tn=tiles["proj_tn"], **prompt)
        att = _attn_prompt(proj, sinks_flat, layer)
        rw, state = _wkv(proj, zero_shift, zero_state, wp, layer, chunk=tiles["wkv_chunk"])
        x1, h2 = _out_proj(att, rw, xp, mod, g_ffn, w_out_b, layer, tm=tiles["out_tm"], **prompt)
        xp = _ffn(h2, x1, mod, w_gate_b, w_up_b, w_down_b, g_final, layer, tm=tiles["ffn_tm"], th=tiles["ffn_th"],
                  final_norm=last, cast_weights=False, **prompt)[0]
        new_kv = proj[:, tp - WINDOW:, ATTN_WIDTH:QKV_COLS]
        outs_p[0].append(new_kv[:, :, :KV_WIDTH].reshape(bp, WINDOW, N_KV_HEADS, HEAD_DIM))
        outs_p[1].append(new_kv[:, :, KV_WIDTH:].reshape(bp, WINDOW, N_KV_HEADS, HEAD_DIM))
        outs_p[2].append(state.reshape(bp, N_RWKV_HEADS, HEAD_DIM, HEAD_DIM))
        outs_p[3].append(proj[:, tp - 1, QKV_COLS:QKV_COLS + RWKV_PROJ])

    stack = lambda xs_: jnp.stack(xs_, axis=0)
    return (xp, xs.reshape(bs, 1, d),
            stack(outs_p[0]), stack(outs_p[1]), stack(outs_p[2]), stack(outs_p[3]),
            stack(outs_s[0]), stack(outs_s[1]), stack(outs_s[2]), stack(outs_s[3]))
```

```python
import functools
import math

import numpy as np
import jax
import jax.numpy as jnp
from jax import lax
from jax.experimental import pallas as pl
from jax.experimental.pallas import tpu as pltpu

F32 = jnp.float32
BF16 = jnp.bfloat16

D_MODEL = 2048
HEAD_DIM = 64
ATTN_WIDTH = D_MODEL // 2
RWKV_WIDTH = D_MODEL - ATTN_WIDTH
N_Q_HEADS = ATTN_WIDTH // HEAD_DIM
N_KV_HEADS = N_Q_HEADS // 4
GQA_REP = N_Q_HEADS // N_KV_HEADS
KV_WIDTH = N_KV_HEADS * HEAD_DIM
N_RWKV_HEADS = RWKV_WIDTH // HEAD_DIM
WINDOW = 128
D_DECAY_LORA = 64
D_ICLR_LORA = 64
D_GATE_LORA = 160
RWKV_PROJ = 3 * RWKV_WIDTH + D_DECAY_LORA + D_ICLR_LORA + D_GATE_LORA
QKV_COLS = ATTN_WIDTH + 2 * KV_WIDTH
FFN_HIDDEN = 5632
RMS_EPS = 1e-5
GN_EPS = 64e-5
NEG_INF = -1e30

LANES = 128
SUBLANES = 8
MXU_DIM = 256
VMEM_LIMIT_BYTES = 56 * 1024 * 1024

LORA_OFF = 3 * RWKV_WIDTH
GATE_OFF = LORA_OFF + D_DECAY_LORA + D_ICLR_LORA
GATE_PAD = 2 * LANES
RWKV_PROJ_PAD = 3584
HEADS_PER_GROUP = MXU_DIM // HEAD_DIM
N_GROUPS = N_RWKV_HEADS // HEADS_PER_GROUP

ALIBI_SLOPES = tuple(float(s) for s in np.exp2(-8.0 * np.arange(1, N_Q_HEADS + 1, dtype=np.float32) / N_Q_HEADS))


def _params(n_axes):
    return pltpu.CompilerParams(dimension_semantics=("arbitrary",) * n_axes,
                                vmem_limit_bytes=VMEM_LIMIT_BYTES)


def _dot(a, b):
    return jnp.dot(a, b, preferred_element_type=F32)


def _dot_nt(a, b):
    return lax.dot_general(a, b, (((1,), (1,)), ((), ())), preferred_element_type=F32)


def _dot_tn(a, b):
    return lax.dot_general(a, b, (((0,), (0,)), ((), ())), preferred_element_type=F32)


def _split(x, parts):
    out = []
    for _ in range(parts):
        p = x.astype(BF16)
        out.append(p)
        x = x - p.astype(F32)
    return out


def _rmsnorm_mod(x, g, shift, scale):
    ms = jnp.mean(x * x, axis=-1, keepdims=True)
    y = x * lax.rsqrt(ms + RMS_EPS) * g
    return y * (1.0 + scale) + shift


def _ada_kernel(c_ref, w_ref, b_ref, o_ref):
    c = c_ref[...]
    s = (c * jax.nn.sigmoid(c)).astype(BF16)
    o_ref[...] = _dot(s, w_ref[...].astype(BF16)) + b_ref[...]


def _ada(c_all, w_ada, b_ada):
    n_layers, d, n = w_ada.shape
    rows = c_all.shape[0]
    tn = 1024
    return pl.pallas_call(
        _ada_kernel,
        grid=(n_layers, n // tn),
        in_specs=[pl.BlockSpec((rows, d), lambda l, j: (0, 0)),
                  pl.BlockSpec((None, d, tn), lambda l, j: (l, 0, j)),
                  pl.BlockSpec((None, 1, tn), lambda l, j: (l, 0, j))],
        out_specs=pl.BlockSpec((None, rows, tn), lambda l, j: (l, 0, j)),
        out_shape=jax.ShapeDtypeStruct((n_layers, rows, n), F32),
        compiler_params=_params(2),
        name="ada_mod",
    )(c_all, w_ada, b_ada.reshape(n_layers, 1, n))


NORM_STRIP = 32


MOD_SHIFT1, MOD_SCALE1, MOD_GATE1, MOD_SHIFT2, MOD_SCALE2, MOD_GATE2 = range(6)


def _mod_spec(mod, layer, part):
    rows, d = mod.shape[1], mod.shape[2] // 6
    return pl.BlockSpec((None, rows, d), lambda *_: (layer, 0, part))


def _mod_rows(ref, n_sample, is_sample):
    if is_sample:
        return ref[0:n_sample, :]
    return ref[pl.ds(n_sample + pl.program_id(0), 1), :]


def _norm_proj_kernel(x_ref, sh_ref, sc_ref, g_ref, w_ref, o_ref, h_scr, *, n_sample, is_sample):
    j = pl.program_id(2)
    tm = h_scr.shape[0]

    @pl.when(j == 0)
    def _():
        gain = g_ref[...] * (1.0 + _mod_rows(sc_ref, n_sample, is_sample))
        shift = _mod_rows(sh_ref, n_sample, is_sample)
        if gain.shape[0] == 1:
            def strip(i, carry):
                rows = pl.ds(pl.multiple_of(i * NORM_STRIP, NORM_STRIP), NORM_STRIP)
                x = x_ref[0, rows, :]
                ms = jnp.mean(x * x, axis=-1, keepdims=True)
                h_scr[rows, :] = (x * lax.rsqrt(ms + RMS_EPS) * gain + shift).astype(BF16)
                return carry
            lax.fori_loop(0, tm // NORM_STRIP, strip, 0, unroll=4)
        else:
            x = x_ref[0]
            ms = jnp.mean(x * x, axis=-1, keepdims=True)
            h_scr[...] = (x * lax.rsqrt(ms + RMS_EPS) * gain + shift).astype(BF16)

    o_ref[0] = _dot_nt(h_scr[...], w_ref[...])


def _norm_proj(x, mod, g, w, layer, *, tm, tn, n_sample, is_sample):
    b, t, d = x.shape
    n = QKV_COLS + RWKV_PROJ_PAD
    return pl.pallas_call(
        functools.partial(_norm_proj_kernel, n_sample=n_sample, is_sample=is_sample),
        grid=(b, t // tm, n // tn),
        in_specs=[pl.BlockSpec((1, tm, d), lambda i, m, j: (i, m, 0)),
                  _mod_spec(mod, layer, MOD_SHIFT1), _mod_spec(mod, layer, MOD_SCALE1),
                  pl.BlockSpec((None, 1, d), lambda i, m, j: (layer, 0, 0)),
                  pl.BlockSpec((None, tn, d), lambda i, m, j: (layer, j, 0))],
        out_specs=pl.BlockSpec((1, tm, tn), lambda i, m, j: (i, m, j)),
        out_shape=jax.ShapeDtypeStruct((b, t, n), F32),
        scratch_shapes=[pltpu.VMEM((tm, d), BF16)],
        compiler_params=_params(3),
        name="norm_proj",
    )(x, mod, mod, g, w)


LOG2E = math.log2(math.e)


def _attn_bias_table():
    t = np.arange(WINDOW)[:, None]
    j = np.arange(2 * WINDOW)[None, :]
    dist = t + WINDOW - j
    valid = (dist >= 0) & (dist <= WINDOW)
    slopes = np.asarray(ALIBI_SLOPES, np.float32)[:, None, None]
    bias = np.where(valid[None], -slopes * dist[None].astype(np.float32) * np.float32(LOG2E), np.float32(NEG_INF))
    first = np.where((j >= WINDOW)[None], bias, np.float32(NEG_INF))
    return np.stack([first, bias]).astype(np.float32)


def _attn_prompt_kernel(sink_ref, bias_ref, q_ref, kc_ref, vc_ref, kp_ref, vp_ref, o_ref, *, layer):
    q = q_ref[0] * (HEAD_DIM ** -0.5 * LOG2E)
    kc, vc, kp, vp = kc_ref[0], vc_ref[0], kp_ref[0], vp_ref[0]
    kv_heads, heads = range(N_KV_HEADS), range(N_Q_HEADS)
    ksl = [slice(g * HEAD_DIM, (g + 1) * HEAD_DIM) for g in kv_heads]
    k_band = [jnp.concatenate([kp[:, ksl[g]], kc[:, ksl[g]]], axis=0).astype(BF16) for g in kv_heads]
    v_band = [jnp.concatenate([vp[:, ksl[g]], vc[:, ksl[g]]], axis=0).astype(BF16) for g in kv_heads]
    lg = [_dot_nt(q[:, h * HEAD_DIM:(h + 1) * HEAD_DIM].astype(BF16), k_band[h // GQA_REP]) + bias_ref[0, h]
          for h in heads]
    sink = [sink_ref[layer * N_Q_HEADS + h] * LOG2E for h in heads]
    m = [jnp.maximum(jnp.max(lg[h], axis=-1, keepdims=True), sink[h]) for h in heads]
    p = [jnp.exp2(lg[h] - m[h]).astype(BF16) for h in heads]
    ones = jnp.ones((2 * WINDOW, HEAD_DIM), BF16)
    den = [_dot(p[h], ones) + jnp.exp2(sink[h] - m[h]) for h in heads]
    outs = [_dot(p[h], v_band[h // GQA_REP]) * (1.0 / den[h]) for h in heads]
    o_ref[0] = jnp.concatenate(outs, axis=1).astype(BF16)


def _attn_prompt(qkv, sinks_flat, layer):
    b, t, _ = qkv.shape
    kcol = ATTN_WIDTH // KV_WIDTH
    prev = lambda i, m: (i, jnp.maximum(m - 1, 0), kcol)
    prev_v = lambda i, m: (i, jnp.maximum(m - 1, 0), kcol + 1)
    return pl.pallas_call(
        functools.partial(_attn_prompt_kernel, layer=layer),
        grid=(b, t // WINDOW),
        in_specs=[pl.BlockSpec(memory_space=pltpu.SMEM),
                  pl.BlockSpec((1, N_Q_HEADS, WINDOW, 2 * WINDOW), lambda i, m: (jnp.minimum(m, 1), 0, 0, 0)),
                  pl.BlockSpec((1, WINDOW, ATTN_WIDTH), lambda i, m: (i, m, 0)),
                  pl.BlockSpec((1, WINDOW, KV_WIDTH), lambda i, m: (i, m, kcol)),
                  pl.BlockSpec((1, WINDOW, KV_WIDTH), lambda i, m: (i, m, kcol + 1)),
                  pl.BlockSpec((1, WINDOW, KV_WIDTH), prev),
                  pl.BlockSpec((1, WINDOW, KV_WIDTH), prev_v)],
        out_specs=pl.BlockSpec((1, WINDOW, ATTN_WIDTH), lambda i, m: (i, m, 0)),
        out_shape=jax.ShapeDtypeStruct((b, t, ATTN_WIDTH), BF16),
        compiler_params=_params(2),
        name="attn_prompt",
    )(sinks_flat, jnp.asarray(_attn_bias_table()), qkv, qkv, qkv, qkv, qkv)


def _attn_sample_kernel(row_ref, kc_ref, vc_ref, sink_ref, slope_ref, spread_ref, gather_ref,
                        o_ref, nk_ref, nv_ref):
    seqs = range(row_ref.shape[0])
    head_row = lax.broadcasted_iota(jnp.int32, (N_Q_HEADS, ATTN_WIDTH), 0)
    head_lane = lax.broadcasted_iota(jnp.int32, (N_Q_HEADS, ATTN_WIDTH), 1) // HEAD_DIM
    own = head_row == head_lane
    grp_row = lax.broadcasted_iota(jnp.int32, (N_Q_HEADS, KV_WIDTH), 0) // GQA_REP
    grp_lane = lax.broadcasted_iota(jnp.int32, (N_Q_HEADS, KV_WIDTH), 1) // HEAD_DIM
    bias = slope_ref[...] * (WINDOW - lax.broadcasted_iota(jnp.int32, (N_Q_HEADS, WINDOW), 1)).astype(F32)
    sink = sink_ref[...]
    last = lax.broadcasted_iota(jnp.int32, (WINDOW, KV_WIDTH), 0) == WINDOW - 1
    rows = [row_ref[s] for s in seqs]
    k_new = [r[:, ATTN_WIDTH:ATTN_WIDTH + KV_WIDTH] for r in rows]
    v_new = [r[:, ATTN_WIDTH + KV_WIDTH:] for r in rows]
    kc = [kc_ref[s] for s in seqs]
    vc = [vc_ref[s] for s in seqs]
    q_rows = [jnp.where(own, jnp.broadcast_to(r[:, :ATTN_WIDTH] * (HEAD_DIM ** -0.5), (N_Q_HEADS, ATTN_WIDTH)), 0.0)
              .astype(BF16) for r in rows]
    q_grp = [_dot(x, spread_ref[...]) for x in q_rows]
    lg = [_dot_nt(q_grp[s].astype(BF16), kc[s].astype(BF16)) - bias for s in seqs]
    lg_new = [jnp.sum(q_grp[s] * k_new[s].astype(BF16).astype(F32), axis=-1, keepdims=True) for s in seqs]
    m = [jnp.maximum(jnp.maximum(jnp.max(lg[s], axis=-1, keepdims=True), lg_new[s]), sink) for s in seqs]
    p = [jnp.exp(lg[s] - m[s]) for s in seqs]
    p_new = [jnp.exp(lg_new[s] - m[s]) for s in seqs]
    den = [jnp.sum(p[s], axis=-1, keepdims=True) + p_new[s] + jnp.exp(sink - m[s]) for s in seqs]
    out = [_dot((p[s] / den[s]).astype(BF16), vc[s].astype(BF16))
           + (p_new[s] / den[s]).astype(BF16).astype(F32) * v_new[s].astype(BF16).astype(F32) for s in seqs]
    full = [_dot(jnp.where(grp_row == grp_lane, x, 0.0).astype(BF16), gather_ref[...]) for x in out]
    for s in seqs:
        o_ref[s] = jnp.sum(jnp.where(own, full[s], 0.0), axis=0, keepdims=True).astype(BF16)
        nk_ref[s] = jnp.where(last, k_new[s], pltpu.roll(kc[s], WINDOW - 1, axis=0))
        nv_ref[s] = jnp.where(last, v_new[s], pltpu.roll(vc[s], WINDOW - 1, axis=0))


def _attn_sample(qkv_rows, cache_k, cache_v, sinks_col, layer):
    s = qkv_rows.shape[0]
    spread = np.zeros((ATTN_WIDTH, KV_WIDTH), np.float32)
    for h in range(N_Q_HEADS):
        for d in range(HEAD_DIM):
            spread[h * HEAD_DIM + d, (h // GQA_REP) * HEAD_DIM + d] = 1.0
    slopes = np.asarray(ALIBI_SLOPES, np.float32).reshape(N_Q_HEADS, 1)
    per_step = SUBLANES
    win = (per_step, WINDOW, KV_WIDTH)
    return pl.pallas_call(
        _attn_sample_kernel,
        grid=(s // per_step,),
        in_specs=[pl.BlockSpec((per_step, 1, QKV_COLS), lambda i: (i, 0, 0)),
                  pl.BlockSpec((None,) + win, lambda i: (layer, i, 0, 0)),
                  pl.BlockSpec((None,) + win, lambda i: (layer, i, 0, 0)),
                  pl.BlockSpec((None, N_Q_HEADS, 1), lambda i: (layer, 0, 0)),
                  pl.BlockSpec((N_Q_HEADS, 1), lambda i: (0, 0)),
                  pl.BlockSpec((ATTN_WIDTH, KV_WIDTH), lambda i: (0, 0)),
                  pl.BlockSpec((KV_WIDTH, ATTN_WIDTH), lambda i: (0, 0))],
        out_specs=[pl.BlockSpec((per_step, 1, ATTN_WIDTH), lambda i: (i, 0, 0)),
                   pl.BlockSpec(win, lambda i: (i, 0, 0)),
                   pl.BlockSpec(win, lambda i: (i, 0, 0))],
        out_shape=[jax.ShapeDtypeStruct((s, 1, ATTN_WIDTH), BF16),
                   jax.ShapeDtypeStruct((s, WINDOW, KV_WIDTH), F32),
                   jax.ShapeDtypeStruct((s, WINDOW, KV_WIDTH), F32)],
        compiler_params=_params(1),
        name="attn_sample",
    )(qkv_rows, cache_k, cache_v, sinks_col, jnp.asarray(slopes),
      jnp.asarray(spread, BF16), jnp.asarray(spread.T, BF16))


def _wkv_kernel(p_ref, shift0_ref, s0_ref, mix_ref, w0_ref, dup_ref, a0_ref, iup_ref, gup_ref,
                kk_ref, ka_ref, rk_ref, lnw_ref, lnb_ref, rw_ref, sout_ref, s_scr, prev_scr,
                arb_scr, aak_scr, ar_scr, vbd_scr, vb_scr, bk_scr, total_scr, gate_scr, bonus_scr, ncat_scr, ecat_scr,
                *, chunk, n_chunks):
    step = pl.program_id(0)
    staged = (arb_scr, aak_scr, ar_scr, vbd_scr, vb_scr, bk_scr, total_scr, gate_scr, bonus_scr)
    bd_rows = HEADS_PER_GROUP * chunk
    groups = range(N_GROUPS)
    cols = [slice(g * MXU_DIM, (g + 1) * MXU_DIM) for g in groups]
    iota = lambda shape, dim: lax.broadcasted_iota(jnp.int32, shape, dim)
    same_head = iota((MXU_DIM, MXU_DIM), 0) // HEAD_DIM == iota((MXU_DIM, MXU_DIM), 1) // HEAD_DIM
    ones_bd = jnp.where(same_head, 1.0, 0.0).astype(BF16)
    tile_sel = jnp.where(iota((HEAD_DIM, MXU_DIM), 0) == iota((HEAD_DIM, MXU_DIM), 1) % HEAD_DIM, 1.0, 0.0).astype(BF16)
    fold_sel = jnp.where(iota((MXU_DIM, HEAD_DIM), 0) % HEAD_DIM == iota((MXU_DIM, HEAD_DIM), 1), 1.0, 0.0).astype(BF16)

    first_prep = step % n_chunks == 0
    first_apply = (step >= 2) & ((step - 2) % n_chunks == 0)
    last_apply = (step >= 2) & ((step - 2) % n_chunks == n_chunks - 1)

    @pl.when(step == 0)
    def _():
        for ref in staged + (ncat_scr, ecat_scr, s_scr):
            ref[...] = jnp.zeros_like(ref)
        total_scr[...] = jnp.ones_like(total_scr)

    @pl.when(first_prep)
    def _():
        prev_scr[...] = shift0_ref[0]

    @pl.when(first_apply)
    def _():
        for g in groups:
            rows = s0_ref[0, cols[g], :]
            tiled = sum(_dot(part, tile_sel) for part in _split(rows, 3))
            s_scr[g] = jnp.where(same_head, tiled, 0.0)

    row = iota((chunk, 1), 0)

    def shifted(lo, width):
        cur = p_ref[0, :, QKV_COLS + lo:QKV_COLS + lo + width]
        prev = jnp.where(row == 0, prev_scr[:, lo:lo + width], pltpu.roll(cur, 1, axis=0))
        return cur + (prev - cur) * mix_ref[:, lo:lo + width]

    def seg_sum(x):
        return _dot(x.astype(BF16), ones_bd)

    head_mask = [jnp.where(iota((chunk, MXU_DIM), 1) // HEAD_DIM == h, 1.0, 0.0).astype(BF16)
                 for h in range(HEADS_PER_GROUP)]

    def head_blocks(x):
        return jnp.concatenate([x * m for m in head_mask], axis=0)

    slot = step % 2
    inv_n = 1.0 / HEAD_DIM
    t_idx = iota((chunk, 2 * bd_rows), 0)
    s_idx = iota((chunk, 2 * bd_rows), 1) % chunk
    strict = (s_idx < t_idx)[:, :bd_rows]
    same_block = jnp.where(iota((bd_rows, bd_rows), 0) // chunk == iota((bd_rows, bd_rows), 1) // chunk,
                           1.0, 0.0).astype(BF16)
    eye_cat = jnp.where(t_idx == s_idx, 1.0, 0.0)[:, :bd_rows].astype(BF16)
    block_diag = lambda x: jnp.concatenate([x] * HEADS_PER_GROUP, axis=0) * same_block


    st_ar, st_ak, st_r, st_vbd, st_v, st_bk, st_total, st_gate, st_bonus = (
        [ref[slot, g] for g in groups] for ref in staged)
    st_e = [ecat_scr[g] for g in groups]
    s_b = [s_scr[g].astype(BF16) for g in groups]
    uy = [_dot_nt(st_ar[g], s_b[g]) for g in groups]

    e_cat = [ncat_scr[g] for g in groups]
    m_b = [x.astype(BF16) for x in e_cat]
    m_b = [_dot(x, block_diag(x)).astype(BF16) for x in m_b]
    levels = int(math.log2(chunk)) - 1

    def inverse_level(lvl, e_cat, m_b):
        t_bd = [block_diag(x.astype(BF16) + eye_cat) for x in e_cat]
        if lvl < levels - 1:
            both = [_dot(m_b[g], jnp.concatenate([t_bd[g], block_diag(m_b[g])], axis=1)) for g in groups]
            return ([e_cat[g] + both[g][:, :bd_rows] for g in groups], [x[:, bd_rows:].astype(BF16) for x in both])
        return [e_cat[g] + _dot(m_b[g], t_bd[g]) for g in groups], None

    lora_in = shifted(LORA_OFF, LANES)
    lora_tanh = jnp.tanh(lora_in).astype(BF16)
    lora_lin = lora_in.astype(BF16)
    gate_sig = jax.nn.sigmoid(shifted(GATE_OFF, GATE_PAD)).astype(BF16)
    r = [shifted(g * MXU_DIM, MXU_DIM) for g in groups]

    rhs = [uy[g][:chunk] + _dot(st_ak[g], st_vbd[g]) for g in groups]
    e_cat, m_b = inverse_level(0, e_cat, m_b)

    k = [shifted(RWKV_WIDTH + g * MXU_DIM, MXU_DIM) for g in groups]
    v = [shifted(2 * RWKV_WIDTH + g * MXU_DIM, MXU_DIM) for g in groups]
    prev_scr[...] = p_ref[0, chunk - 1:chunk, QKV_COLS:]

    p = [rhs[g] + _dot(st_e[g], head_blocks(rhs[g].astype(BF16))) for g in groups]
    p_b = [x.astype(BF16) for x in p]
    if levels > 1:
        e_cat, m_b = inverse_level(1, e_cat, m_b)

    logw = [-math.exp(-0.5) * jax.nn.sigmoid(w0_ref[:, cols[g]] + _dot(lora_tanh, dup_ref[:, cols[g]])) for g in groups]
    a = [jax.nn.sigmoid(a0_ref[:, cols[g]] + _dot(lora_lin, iup_ref[:, cols[g]])) for g in groups]
    gate = [_dot(gate_sig, gup_ref[:, cols[g]]) for g in groups]

    y = [uy[g][chunk:] + _dot(st_r[g], jnp.concatenate([head_blocks(p_b[g]), st_vbd[g]], axis=0)) for g in groups]
    for g in groups:
        upd = _dot_tn(jnp.concatenate([p_b[g], st_v[g]], axis=0), st_bk[g])
        s_scr[g] = s_scr[g] * st_total[g] + jnp.where(same_head, upd, 0.0)
    if levels > 2:
        e_cat, m_b = inverse_level(2, e_cat, m_b)

    kk = [k[g] * kk_ref[:, cols[g]] for g in groups]
    kk = [kk[g] / jnp.maximum(jnp.sqrt(seg_sum(kk[g] * kk[g])), 1e-12) for g in groups]
    k = [k[g] * (1.0 + (a[g] - 1.0) * ka_ref[:, cols[g]]) for g in groups]

    dev = [y[g] - seg_sum(y[g]) * inv_n for g in groups]
    if levels > 3:
        e_cat, m_b = inverse_level(3, e_cat, m_b)

    tri = jnp.where(iota((chunk, chunk), 0) >= iota((chunk, chunk), 1), 1.0, 0.0).astype(BF16)
    cum = [sum(_dot(tri, part) for part in _split(logw[g], 3)) for g in groups]

    var = [seg_sum(x * x) * inv_n for x in dev]
    for lvl in range(4, levels):
        e_cat, m_b = inverse_level(lvl, e_cat, m_b)

    e_pos = [jnp.exp(x) for x in cum]
    e_neg = [jnp.exp(-x) for x in cum]
    a_t = [-kk[g] * jnp.exp(cum[g] - logw[g]) for g in groups]
    b_t = [kk[g] * a[g] * e_neg[g] for g in groups]

    for g in groups:
        yn = dev[g] * lax.rsqrt(var[g] + GN_EPS) * lnw_ref[:, cols[g]] + lnb_ref[:, cols[g]]
        rw_ref[0, :, cols[g]] = ((yn + st_bonus[g]) * st_gate[g]).astype(BF16)

    k_t = [k[g] * e_neg[g] for g in groups]
    r_t = [r[g] * e_pos[g] for g in groups]
    total = [x[chunk - 1:chunk, :] for x in e_pos]
    ar_b = [jnp.concatenate([a_t[g], r_t[g]], axis=0).astype(BF16) for g in groups]
    xb =[head_blocks(x.astype(BF16)) for x in b_t]
    xk = [head_blocks(x.astype(BF16)) for x in k_t]
    v_bd = [head_blocks(x.astype(BF16)) for x in v]

    scores = [_dot_nt(ar_b[g], jnp.concatenate([xb[g], xk[g]], axis=0)) for g in groups]
    bonus = [seg_sum(r[g] * k[g] * rk_ref[:, cols[g]]) * v[g] for g in groups]
    for g in groups:
        ecat_scr[g] = e_cat[g].astype(BF16)
        ncat_scr[g] = jnp.where(strict, scores[g][:chunk, :bd_rows], 0.0)
        arb_scr[slot, g] = ar_b[g]
        aak_scr[slot, g] = jnp.where(strict, scores[g][:chunk, bd_rows:], 0.0).astype(BF16)
        ar_scr[slot, g] = jnp.where(s_idx <= t_idx, scores[g][chunk:], 0.0).astype(BF16)
        vbd_scr[slot, g] = v_bd[g]
        vb_scr[slot, g] = v[g].astype(BF16)
        bk_scr[slot, g] = jnp.concatenate([b_t[g] * total[g], k_t[g] * total[g]], axis=0).astype(BF16)
        total_scr[slot, g], gate_scr[slot, g], bonus_scr[slot, g] = total[g], gate[g], bonus[g]

    @pl.when(last_apply)
    def _():
        for g in groups:
            folded = sum(_dot(part, fold_sel) for part in _split(s_scr[g], 3))
            sout_ref[0, cols[g], :] = folded


def _wkv(proj, shift0, s0, wp, layer, *, chunk):
    b, t, n_proj = proj.shape
    n_chunks = t // chunk
    bd_rows = HEADS_PER_GROUP * chunk
    lay = lambda *blk: pl.BlockSpec((None,) + blk, lambda s: (layer,) + (0,) * len(blk))
    vec = lay(1, RWKV_WIDTH)
    grp = lambda rows, width, dtype: pltpu.VMEM((N_GROUPS, rows, width), dtype)
    two = lambda rows, width, dtype: pltpu.VMEM((2, N_GROUPS, rows, width), dtype)
    total = b * n_chunks
    prep = lambda s: jnp.minimum(s, total - 1)
    apply = lambda s: jnp.clip(s - 2, 0, total - 1)
    return pl.pallas_call(
        functools.partial(_wkv_kernel, chunk=chunk, n_chunks=n_chunks),
        grid=(total + 2,),
        in_specs=[pl.BlockSpec((1, chunk, n_proj), lambda s: (prep(s) // n_chunks, prep(s) % n_chunks, 0)),
                  pl.BlockSpec((1, 1, RWKV_PROJ_PAD), lambda s: (prep(s) // n_chunks, 0, 0)),
                  pl.BlockSpec((1, RWKV_WIDTH, HEAD_DIM), lambda s: (apply(s) // n_chunks, 0, 0)),
                  lay(1, RWKV_PROJ_PAD), vec, lay(LANES, RWKV_WIDTH), vec, lay(LANES, RWKV_WIDTH),
                  lay(GATE_PAD, RWKV_WIDTH), vec, vec, vec, vec, vec],
        out_specs=[pl.BlockSpec((1, chunk, RWKV_WIDTH), lambda s: (apply(s) // n_chunks, apply(s) % n_chunks, 0)),
                   pl.BlockSpec((1, RWKV_WIDTH, HEAD_DIM), lambda s: (apply(s) // n_chunks, 0, 0))],
        out_shape=[jax.ShapeDtypeStruct((b, t, RWKV_WIDTH), BF16),
                   jax.ShapeDtypeStruct((b, RWKV_WIDTH, HEAD_DIM), F32)],
        scratch_shapes=[pltpu.VMEM((N_GROUPS, MXU_DIM, MXU_DIM), F32),
                        pltpu.VMEM((1, RWKV_PROJ_PAD), F32),
                        two(2 * chunk, MXU_DIM, BF16), two(chunk, bd_rows, BF16), two(chunk, 2 * bd_rows, BF16),
                        two(bd_rows, MXU_DIM, BF16), two(chunk, MXU_DIM, BF16), two(2 * chunk, MXU_DIM, BF16),
                        two(1, MXU_DIM, F32), two(chunk, MXU_DIM, F32), two(chunk, MXU_DIM, F32),
                        grp(chunk, bd_rows, F32), grp(chunk, bd_rows, BF16)],
        compiler_params=_params(1),
        name="wkv7",
    )(proj, shift0, s0, wp["mix"], wp["w0"], wp["decay_up"], wp["a0"], wp["iclr_up"], wp["gate_up"],
      wp["k_k"], wp["k_a"], wp["r_k"], wp["ln_w"], wp["ln_b"])


def _wkv_step_kernel(p_ref, shift_ref, s_ref, mix_ref, w0_ref, dup_ref, a0_ref, iup_ref, gup_ref,
                     kk_ref, ka_ref, rk_ref, lnw_ref, lnb_ref, fold_ref, rw_ref, sout_ref):
    iota = lambda shape, dim: lax.broadcasted_iota(jnp.int32, shape, dim)
    groups = range(N_GROUPS)
    seqs = range(p_ref.shape[0])
    cols = [slice(g * MXU_DIM, (g + 1) * MXU_DIM) for g in groups]
    ones_bd = jnp.where(iota((MXU_DIM, MXU_DIM), 0) // HEAD_DIM == iota((MXU_DIM, MXU_DIM), 1) // HEAD_DIM,
                        1.0, 0.0).astype(BF16)

    def seg_sum(x):
        return jnp.concatenate([_dot(x[:, cols[g]].astype(BF16), ones_bd) for g in groups], axis=1)

    cur = p_ref[...]
    p = cur + (shift_ref[...] - cur) * mix_ref[...]
    r = p[:, 0:RWKV_WIDTH]
    k = p[:, RWKV_WIDTH:2 * RWKV_WIDTH]
    v = p[:, 2 * RWKV_WIDTH:3 * RWKV_WIDTH]
    lora_in = p[:, LORA_OFF:LORA_OFF + LANES]
    gate_in = p[:, GATE_OFF:GATE_OFF + GATE_PAD]
    w = w0_ref[...] + _dot(jnp.tanh(lora_in).astype(BF16), dup_ref[...])
    decay = jnp.exp(-math.exp(-0.5) * jax.nn.sigmoid(w))
    a = jax.nn.sigmoid(a0_ref[...] + _dot(lora_in.astype(BF16), iup_ref[...]))
    gate = _dot(jax.nn.sigmoid(gate_in).astype(BF16), gup_ref[...])
    kk = k * kk_ref[...]
    kk = kk / jnp.maximum(jnp.sqrt(seg_sum(kk * kk)), 1e-12)
    k = k * (1.0 + (a - 1.0) * ka_ref[...])
    neg_kk, kk_a = -kk, kk * a

    own = iota((N_RWKV_HEADS, RWKV_WIDTH), 0) == iota((N_RWKV_HEADS, RWKV_WIDTH), 1) // HEAD_DIM
    spread = lambda x, s: jnp.where(own, jnp.broadcast_to(x[s:s + 1], (N_RWKV_HEADS, RWKV_WIDTH)), 0.0)
    stacked = [jnp.concatenate([spread(x, s) for x in (decay, neg_kk, kk_a, k, r)], axis=0) for s in seqs]
    per_head = [sum(_dot(part, fold_ref[...]) for part in _split(x, 3)) for x in stacked]
    head = lambda s, i: per_head[s][i * N_RWKV_HEADS:(i + 1) * N_RWKV_HEADS]
    rows = lambda x: jnp.concatenate(
        [jnp.broadcast_to(x[h:h + 1, :], (HEAD_DIM, HEAD_DIM)) for h in range(N_RWKV_HEADS)], axis=0)

    eye = iota((LANES, LANES), 0) == iota((LANES, LANES), 1)
    ones_cols = jnp.ones((LANES, HEAD_DIM), BF16)
    v_col = [jnp.concatenate(
        [sum(_dot(part, ones_cols) for part in
             _split(jnp.where(eye, jnp.broadcast_to(v[s:s + 1, t * LANES:(t + 1) * LANES], (LANES, LANES)), 0.0), 3))
         for t in range(RWKV_WIDTH // LANES)], axis=0) for s in seqs]

    state = [s_ref[s] for s in seqs]
    sa = [jnp.sum(state[s] * rows(head(s, 1)), axis=-1, keepdims=True) for s in seqs]
    s_new = [state[s] * rows(head(s, 0)) + sa[s] * rows(head(s, 2)) + v_col[s] * rows(head(s, 3)) for s in seqs]
    for s in seqs:
        sout_ref[s] = s_new[s]
    y_heads = [_dot_nt(head(s, 4).astype(BF16), s_new[s].astype(BF16)) for s in seqs]
    y = jnp.concatenate([jnp.sum(jnp.where(own, x, 0.0), axis=0, keepdims=True) for x in y_heads], axis=0)

    inv_n = 1.0 / HEAD_DIM
    dev = y - seg_sum(y) * inv_n
    var = seg_sum(dev * dev) * inv_n
    y = dev * lax.rsqrt(var + GN_EPS) * lnw_ref[...] + lnb_ref[...]
    bonus = seg_sum(r * k * rk_ref[...]) * v
    rw_ref[...] = (y + bonus) * gate


def _wkv_step(prw_rows, shift_rows, state, wp, layer):
    s = prw_rows.shape[0]
    per_step = SUBLANES
    lay = lambda *blk: pl.BlockSpec((None,) + blk, lambda i: (layer,) + (0,) * len(blk))
    vec = lay(1, RWKV_WIDTH)
    fold = np.tile(np.eye(HEAD_DIM, dtype=np.float32), (N_RWKV_HEADS, 1))
    return pl.pallas_call(
        _wkv_step_kernel,
        grid=(s // per_step,),
        in_specs=[pl.BlockSpec((per_step, RWKV_PROJ_PAD), lambda i: (i, 0)),
                  pl.BlockSpec((None, per_step, RWKV_PROJ_PAD), lambda i: (layer, i, 0)),
                  pl.BlockSpec((None, per_step, RWKV_WIDTH, HEAD_DIM), lambda i: (layer, i, 0, 0)),
                  lay(1, RWKV_PROJ_PAD), vec, lay(LANES, RWKV_WIDTH), vec, lay(LANES, RWKV_WIDTH),
                  lay(GATE_PAD, RWKV_WIDTH), vec, vec, vec, vec, vec,
                  pl.BlockSpec((RWKV_WIDTH, HEAD_DIM), lambda i: (0, 0))],
        out_specs=[pl.BlockSpec((per_step, RWKV_WIDTH), lambda i: (i, 0)),
                   pl.BlockSpec((per_step, RWKV_WIDTH, HEAD_DIM), lambda i: (i, 0, 0))],
        out_shape=[jax.ShapeDtypeStruct((s, RWKV_WIDTH), F32),
                   jax.ShapeDtypeStruct((s, RWKV_WIDTH, HEAD_DIM), F32)],
        compiler_params=_params(1),
        name="wkv7_step",
    )(prw_rows, shift_rows, state, wp["mix"], wp["w0"], wp["decay_up"], wp["a0"], wp["iclr_up"], wp["gate_up"],
      wp["k_k"], wp["k_a"], wp["r_k"], wp["ln_w"], wp["ln_b"], jnp.asarray(fold, BF16))


def _out_proj_kernel(att_ref, rw_ref, x_ref, gt_ref, sh_ref, sc_ref, g_ref, w_ref, x1_ref, h_ref, *, n_sample,
                     is_sample):
    rows = functools.partial(_mod_rows, n_sample=n_sample, is_sample=is_sample)
    y = _dot(att_ref[0], w_ref[0:ATTN_WIDTH, :]) + _dot(rw_ref[0], w_ref[ATTN_WIDTH:, :])
    x1 = x_ref[0] + rows(gt_ref) * y
    x1_ref[0] = x1
    h_ref[0] = _rmsnorm_mod(x1, g_ref[...], rows(sh_ref), rows(sc_ref)).astype(BF16)


def _out_proj(att, rw, x, mod, g, w, layer, *, tm, n_sample, is_sample):
    b, t, d = x.shape
    row = lambda width: pl.BlockSpec((1, tm, width), lambda i, m: (i, m, 0))
    return pl.pallas_call(
        functools.partial(_out_proj_kernel, n_sample=n_sample, is_sample=is_sample),
        grid=(b, t // tm),
        in_specs=[row(ATTN_WIDTH), row(RWKV_WIDTH), row(d),
                  _mod_spec(mod, layer, MOD_GATE1), _mod_spec(mod, layer, MOD_SHIFT2), _mod_spec(mod, layer, MOD_SCALE2),
                  pl.BlockSpec((None, 1, d), lambda i, m: (layer, 0, 0)),
                  pl.BlockSpec((None, d, d), lambda i, m: (layer, 0, 0))],
        out_specs=[row(d), row(d)],
        out_shape=[jax.ShapeDtypeStruct((b, t, d), F32), jax.ShapeDtypeStruct((b, t, d), BF16)],
        compiler_params=_params(2),
        name="out_proj",
    )(att, rw, x, mod, mod, mod, g, w)


def _ffn_kernel(h_ref, x_ref, gt_ref, wg_ref, wu_ref, wo_ref, gf_ref, o_ref, *rest, final_norm, n_sample,
                is_sample, cast_weights):
    j = pl.program_id(2)
    acc_ref = rest[-1]

    @pl.when(j == 0)
    def _():
        acc_ref[...] = jnp.zeros_like(acc_ref)

    wg, wu, wo = wg_ref[...], wu_ref[...], wo_ref[...]
    if cast_weights:
        wg, wu, wo = wg.astype(BF16), wu.astype(BF16), wo.astype(BF16)
        rest[0][...], rest[1][...], rest[2][...] = wg, wu, wo

    h = h_ref[0]
    gate = _dot(h, wg)
    up = _dot(h, wu)
    act = (gate * jax.nn.sigmoid(gate) * up).astype(BF16)
    acc_ref[...] += _dot(act, wo)

    @pl.when(j == pl.num_programs(2) - 1)
    def _():
        x2 = x_ref[0] + _mod_rows(gt_ref, n_sample, is_sample) * acc_ref[...]
        if final_norm:
            ms = jnp.mean(x2 * x2, axis=-1, keepdims=True)
            x2 = x2 * lax.rsqrt(ms + RMS_EPS) * gf_ref[...]
        o_ref[0] = x2


def _ffn(h, x, mod, w_gate, w_up, w_out, g_final, layer, *, tm, th, final_norm, n_sample, is_sample, cast_weights):
    b, t, d = x.shape
    n_h = FFN_HIDDEN // th
    w_layer, up_off = (layer, n_h) if cast_weights else (0, 0)
    col_tile = lambda off: pl.BlockSpec((None, d, th), lambda i, m, j: (w_layer, 0, j + off))
    row_tile = pl.BlockSpec((None, th, d), lambda i, m, j: (w_layer, j, 0))
    out_specs = [pl.BlockSpec((1, tm, d), lambda i, m, j: (i, m, 0))]
    out_shape = [jax.ShapeDtypeStruct((b, t, d), F32)]
    if cast_weights:
        assert t == tm and b == 1, "every weight tile must be visited exactly once"
        out_specs += [pl.BlockSpec((None, d, th), lambda i, m, j: (0, 0, j))] * 2
        out_specs += [pl.BlockSpec((None, th, d), lambda i, m, j: (0, j, 0))]
        out_shape += [jax.ShapeDtypeStruct((1, d, FFN_HIDDEN), BF16)] * 2 + [jax.ShapeDtypeStruct((1, FFN_HIDDEN, d), BF16)]
    return pl.pallas_call(
        functools.partial(_ffn_kernel, final_norm=final_norm, n_sample=n_sample, is_sample=is_sample,
                          cast_weights=cast_weights),
        grid=(b, t // tm, n_h),
        in_specs=[pl.BlockSpec((1, tm, d), lambda i, m, j: (i, m, 0)),
                  pl.BlockSpec((1, tm, d), lambda i, m, j: (i, m, 0)),
                  _mod_spec(mod, layer, MOD_GATE2),
                  col_tile(0), col_tile(up_off), row_tile,
                  pl.BlockSpec((1, d), lambda i, m, j: (0, 0))],
        out_specs=out_specs,
        out_shape=out_shape,
        scratch_shapes=[pltpu.VMEM((tm, d), F32)],
        compiler_params=_params(3),
        name="ffn",
    )(h, x, mod, w_gate, w_up, w_out, g_final)


def _tiles(t_prompt, n_sample):
    tiles = dict(proj_tm=1024, proj_tn=1024, out_tm=512, ffn_tm=512, ffn_th=512, wkv_chunk=64)
    assert t_prompt % tiles["proj_tm"] == 0 and t_prompt % tiles["out_tm"] == 0 and t_prompt % tiles["ffn_tm"] == 0
    assert t_prompt % tiles["wkv_chunk"] == 0 and t_prompt % WINDOW == 0
    assert (QKV_COLS + RWKV_PROJ_PAD) % tiles["proj_tn"] == 0
    assert FFN_HIDDEN % tiles["ffn_th"] == 0 and n_sample % SUBLANES == 0
    return tiles


def _pad_cols(x, width):
    return jnp.pad(x, [(0, 0)] * (x.ndim - 1) + [(0, width - x.shape[-1])])


def _pad_rows(x, rows, offset=0):
    return jnp.pad(x, [(0, 0), (offset, rows - offset - x.shape[1]), (0, 0)])


def kernel(x_prompt, x_sample, cache_k, cache_v, state_wkv, state_shift, c_prompt, c_sample, w_ada, b_ada, g_norm_mix, g_norm_ffn, w_in, w_out, attn_sinks, mix_shift, decay_w0, decay_up, iclr_a0, iclr_up, gate_up, k_k, k_a, r_k, ln_x_w, ln_x_b, w_ffn_in, w_ffn_out, g_norm_final):
    n_layers = w_in.shape[0]
    bp, tp, d = x_prompt.shape
    bs = x_sample.shape[0]
    tiles = _tiles(tp, bs)

    w_in_t = jnp.pad(jnp.swapaxes(w_in, 1, 2).astype(BF16),
                     [(0, 0), (0, QKV_COLS + RWKV_PROJ_PAD - w_in.shape[2]), (0, 0)])
    w_out_b = w_out.astype(BF16)
    vec = lambda x: x.reshape(n_layers, 1, -1)
    wp = dict(
        mix=vec(_pad_cols(mix_shift, RWKV_PROJ_PAD)), w0=vec(decay_w0), a0=vec(iclr_a0),
        decay_up=_pad_rows(decay_up, LANES).astype(BF16),
        iclr_up=_pad_rows(iclr_up, LANES, D_DECAY_LORA).astype(BF16),
        gate_up=_pad_rows(gate_up, GATE_PAD).astype(BF16),
        k_k=vec(k_k), k_a=vec(k_a), r_k=vec(r_k), ln_w=vec(ln_x_w), ln_b=vec(ln_x_b))
    g_mix, g_ffn = vec(g_norm_mix), vec(g_norm_ffn)
    g_final = g_norm_final.reshape(1, d)
    sinks_flat = attn_sinks.reshape(-1)
    sinks_col = attn_sinks.reshape(n_layers, N_Q_HEADS, 1)
    cache_k2 = cache_k.reshape(n_layers, bs, WINDOW, KV_WIDTH)
    cache_v2 = cache_v.reshape(n_layers, bs, WINDOW, KV_WIDTH)
    shift_rows = _pad_cols(state_shift, RWKV_PROJ_PAD)
    state_rows = state_wkv.reshape(n_layers, bs, RWKV_WIDTH, HEAD_DIM)
    zero_shift = jnp.zeros((bp, 1, RWKV_PROJ_PAD), F32)
    zero_state = jnp.zeros((bp, RWKV_WIDTH, HEAD_DIM), F32)

    mod = _ada(jnp.concatenate([c_sample, c_prompt], axis=0), w_ada, b_ada)
    prompt = dict(n_sample=bs, is_sample=False)
    sample = dict(n_sample=bs, is_sample=True)

    xp = x_prompt
    xs = x_sample.reshape(1, bs, d)
    outs_p = [[] for _ in range(4)]
    outs_s = [[] for _ in range(4)]
    for layer in range(n_layers):
        last = layer == n_layers - 1

        proj_s = _norm_proj(xs, mod, g_mix, w_in_t, layer, tm=bs, tn=tiles["proj_tn"], **sample)[0]
        att_s, nk, nv = _attn_sample(proj_s[:, :QKV_COLS].reshape(bs, 1, QKV_COLS), cache_k2, cache_v2, sinks_col, layer)
        prw_tok = proj_s[:, QKV_COLS:]
        rw_s, state_s = _wkv_step(prw_tok, shift_rows, state_rows, wp, layer)
        x1_s, h2_s = _out_proj(att_s.reshape(1, bs, ATTN_WIDTH), rw_s.astype(BF16).reshape(1, bs, RWKV_WIDTH), xs,
                               mod, g_ffn, w_out_b, layer, tm=bs, **sample)
        xs, w_gate_b, w_up_b, w_down_b = _ffn(h2_s, x1_s, mod, w_ffn_in, w_ffn_in, w_ffn_out, g_final, layer, tm=bs,
                                              th=tiles["ffn_th"], final_norm=last, cast_weights=True, **sample)
        outs_s[0].append(nk.reshape(bs, WINDOW, N_KV_HEADS, HEAD_DIM))
        outs_s[1].append(nv.reshape(bs, WINDOW, N_KV_HEADS, HEAD_DIM))
        outs_s[2].append(state_s.reshape(bs, N_RWKV_HEADS, HEAD_DIM, HEAD_DIM))
        outs_s[3].append(prw_tok[:, :RWKV_PROJ])

        proj = _norm_proj(xp, mod, g_mix, w_in_t, layer, tm=tiles["proj_tm"], tn=tiles["proj_tn"], **prompt)
        att = _attn_prompt(proj, sinks_flat, layer)
        rw, state = _wkv(proj, zero_shift, zero_state, wp, layer, chunk=tiles["wkv_chunk"])
        x1, h2 = _out_proj(att, rw, xp, mod, g_ffn, w_out_b, layer, tm=tiles["out_tm"], **prompt)
        xp = _ffn(h2, x1, mod, w_gate_b, w_up_b, w_down_b, g_final, layer, tm=tiles["ffn_tm"], th=tiles["ffn_th"],
                  final_norm=last, cast_weights=False, **prompt)[0]
        new_kv = proj[:, tp - WINDOW:, ATTN_WIDTH:QKV_COLS]
        outs_p[0].append(new_kv[:, :, :KV_WIDTH].reshape(bp, WINDOW, N_KV_HEADS, HEAD_DIM))
        outs_p[1].append(new_kv[:, :, KV_WIDTH:].reshape(bp, WINDOW, N_KV_HEADS, HEAD_DIM))
        outs_p[2].append(state.reshape(bp, N_RWKV_HEADS, HEAD_DIM, HEAD_DIM))
        outs_p[3].append(proj[:, tp - 1, QKV_COLS:QKV_COLS + RWKV_PROJ])

    stack = lambda xs_: jnp.stack(xs_, axis=0)
    return (xp, xs.reshape(bs, 1, d),
            stack(outs_p[0]), stack(outs_p[1]), stack(outs_p[2]), stack(outs_p[3]),
            stack(outs_s[0]), stack(outs_s[1]), stack(outs_s[2]), stack(outs_s[3]))
```
